```python
import jax, jax.numpy as jnp
from jax import lax
import numpy as np

D_MODEL = 1024
BATCH = 4
SEQ = 4096
DEPTH = 1

HEAD_DIM = 64
D_MIX = D_MODEL
GDN_HEADS = (D_MIX // 2) // HEAD_DIM
GDN_WIDTH = GDN_HEADS * HEAD_DIM
SWA_Q_HEADS = (D_MIX - GDN_WIDTH) // HEAD_DIM
SWA_KV_HEADS = 2
SWA_GROUP = SWA_Q_HEADS // SWA_KV_HEADS
SWA_WIDTH = SWA_Q_HEADS * HEAD_DIM
SWA_KV_WIDTH = SWA_KV_HEADS * HEAD_DIM
WINDOW = 128
CONV_WIDTH = 4
CHUNK = 64
D_FF = ((8 * D_MODEL // 3 + 255) // 256) * 256
PROJ_WIDTH = 4 * GDN_WIDTH + 2 * GDN_HEADS + SWA_WIDTH + 2 * SWA_KV_WIDTH
EPS = 1e-6

kernel_name = 'hymba_gdn_swa_adaln'


def rms_norm(x, w):
    xf = x.astype(jnp.float32)
    y = xf * lax.rsqrt(jnp.mean(xf * xf, axis=-1, keepdims=True) + EPS)
    return (y * w.astype(jnp.float32)).astype(x.dtype)


def l2_norm(x):
    xf = x.astype(jnp.float32)
    return xf * lax.rsqrt(jnp.sum(xf * xf, axis=-1, keepdims=True) + EPS)


def causal_depthwise_conv(x, w):
    return lax.conv_general_dilated(
        x, w.astype(x.dtype), window_strides=(1,), padding=[(CONV_WIDTH - 1, 0)],
        dimension_numbers=('NWC', 'WIO', 'NWC'), feature_group_count=x.shape[-1])


def gated_delta_rule_chunked(q, k, v, g, beta):
    B, T, H, Dk = q.shape
    Dv = v.shape[-1]
    N = T // CHUNK

    def chunks(t):
        t = t.reshape((B, N, CHUNK, H) + t.shape[3:])
        return jnp.moveaxis(t, 3, 1)

    q = chunks(q) * (Dk ** -0.5)
    k, v, g, beta = chunks(k), chunks(v), chunks(g), chunks(beta)
    G = jnp.cumsum(g, axis=-1)
    idx = jnp.arange(CHUNK)
    causal = idx[:, None] >= idx[None, :]
    strict = idx[:, None] > idx[None, :]
    decay = jnp.exp(jnp.where(causal, G[..., :, None] - G[..., None, :], -jnp.inf))
    kb = k * beta[..., None]
    A = jnp.where(strict, jnp.einsum('bhncd,bhnsd->bhncs', kb, k) * decay, 0.0)
    L = A + jnp.eye(CHUNK, dtype=A.dtype)
    rhs = jnp.concatenate([v * beta[..., None], kb * jnp.exp(G)[..., None]], axis=-1)
    sol = lax.linalg.triangular_solve(L, rhs, left_side=True, lower=True, unit_diagonal=True)
    u, w = sol[..., :Dv], sol[..., Dv:]
    qk = jnp.where(causal, jnp.einsum('bhncd,bhnsd->bhncs', q, k) * decay, 0.0)
    q_dec = q * jnp.exp(G)[..., None]
    k_dec = k * jnp.exp(G[..., -1:] - G)[..., None]
    chunk_decay = jnp.exp(G[..., -1])

    xs = tuple(jnp.moveaxis(t, 2, 0) for t in (u, w, qk, q_dec, k_dec, chunk_decay))

    def step(S, inp):
        u_c, w_c, qk_c, qd_c, kd_c, dec_c = inp
        v_new = u_c - jnp.einsum('bhcd,bhde->bhce', w_c, S)
        o = jnp.einsum('bhcd,bhde->bhce', qd_c, S) + jnp.einsum('bhcs,bhse->bhce', qk_c, v_new)
        S = S * dec_c[..., None, None] + jnp.einsum('bhcd,bhce->bhde', kd_c, v_new)
        return S, o

    S0 = jnp.zeros((B, H, Dk, Dv), jnp.float32)
    _, o = lax.scan(step, S0, xs)
    return jnp.transpose(o, (1, 0, 3, 2, 4)).reshape(B, T, H, Dv)


def sliding_window_attention(q, k, v, sinks):
    B, T, Hq, D = q.shape
    NB = T // WINDOW
    qb = q.reshape(B, NB, WINDOW, SWA_KV_HEADS, SWA_GROUP, D)

    def banded(t):
        t = t.reshape(B, NB, WINDOW, SWA_KV_HEADS, D)
        prev = jnp.pad(t, ((0, 0), (1, 0), (0, 0), (0, 0), (0, 0)))[:, :-1]
        return jnp.concatenate([prev, t], axis=2)

    kw, vw = banded(k), banded(v)
    s = jnp.einsum('bnqhgd,bnkhd->bhgnqk', qb, kw).astype(jnp.float32) * (D ** -0.5)
    qi = jnp.arange(WINDOW)[:, None] + WINDOW
    ki = jnp.arange(2 * WINDOW)[None, :]
    dist = (qi - ki).astype(jnp.float32)
    in_window = (qi - ki >= 0) & (qi - ki < WINDOW)
    key_exists = (jnp.arange(NB)[:, None] * WINDOW + ki - WINDOW) >= 0
    mask = in_window[None] & key_exists[:, None, :]
    slopes = 2.0 ** (-8.0 * (jnp.arange(Hq, dtype=jnp.float32) + 1.0) / Hq)
    slopes = slopes.reshape(SWA_KV_HEADS, SWA_GROUP)
    s = s - slopes[:, :, None, None, None] * dist
    s = jnp.where(mask, s, -jnp.inf)
    sink = sinks.astype(jnp.float32).reshape(SWA_KV_HEADS, SWA_GROUP)[None, :, :, None, None, None]
    m = jnp.maximum(jnp.max(s, axis=-1, keepdims=True), sink)
    p = jnp.exp(s - m)
    p = p / (jnp.sum(p, axis=-1, keepdims=True) + jnp.exp(sink - m))
    o = jnp.einsum('bhgnqk,bnkhd->bnqhgd', p.astype(v.dtype), vw)
    return o.reshape(B, T, Hq * D)


def setup_inputs(seed: int = 0) -> dict:
    key = jax.random.key(seed)
    ks = jax.random.split(key, 20)
    f32 = jnp.float32
    nrm = lambda k, shape, scale: jax.random.normal(k, shape, f32) * scale
    gain = lambda k, shape: 1.0 + 0.1 * jax.random.normal(k, shape, f32)
    a_init = jax.random.uniform(ks[6], (DEPTH, GDN_HEADS), f32, 1.0, 16.0)
    dt = jnp.exp(jax.random.uniform(ks[7], (DEPTH, GDN_HEADS), f32, np.log(1e-3), np.log(1e-1)))
    return {
        'x': nrm(ks[0], (BATCH, SEQ, D_MODEL), 1.0),
        'c': nrm(ks[1], (BATCH, D_MODEL), 1.0),
        'w_ada': nrm(ks[2], (DEPTH, D_MODEL, 6 * D_MODEL), D_MODEL ** -0.5),
        'b_ada': nrm(ks[3], (DEPTH, 6 * D_MODEL), 0.1),
        'norm1_w': gain(ks[4], (DEPTH, D_MODEL)),
        'w_in': nrm(ks[5], (DEPTH, D_MODEL, PROJ_WIDTH), D_MODEL ** -0.5),
        'conv_w': nrm(ks[8], (DEPTH, CONV_WIDTH, 1, 3 * GDN_WIDTH), CONV_WIDTH ** -0.5),
        'a_log': jnp.log(a_init),
        'dt_bias': dt + jnp.log(-jnp.expm1(-dt)),
        'gdn_norm_w': gain(ks[9], (DEPTH, HEAD_DIM)),
        'q_norm_w': gain(ks[10], (DEPTH, HEAD_DIM)),
        'k_norm_w': gain(ks[11], (DEPTH, HEAD_DIM)),
        'sinks': nrm(ks[12], (DEPTH, SWA_Q_HEADS), 1.0),
        'w_out': nrm(ks[13], (DEPTH, D_MIX, D_MODEL), D_MIX ** -0.5),
        'norm2_w': gain(ks[14], (DEPTH, D_MODEL)),
        'w_gate': nrm(ks[15], (DEPTH, D_MODEL, D_FF), D_MODEL ** -0.5),
        'w_up': nrm(ks[16], (DEPTH, D_MODEL, D_FF), D_MODEL ** -0.5),
        'w_down': nrm(ks[17], (DEPTH, D_FF, D_MODEL), D_FF ** -0.5),
    }


def reference(x, c, w_ada, b_ada, norm1_w, w_in, conv_w, a_log, dt_bias, gdn_norm_w,
              q_norm_w, k_norm_w, sinks, w_out, norm2_w, w_gate, w_up, w_down):
    B, T, _ = x.shape
    split_sizes = (GDN_WIDTH,) * 4 + (GDN_HEADS,) * 2 + (SWA_WIDTH, SWA_KV_WIDTH, SWA_KV_WIDTH)
    split_points = []
    acc = 0
    for sz in split_sizes[:-1]:
        acc += sz
        split_points.append(acc)
    c_act = jax.nn.silu(c)
    for l in range(DEPTH):
        mod = (c_act @ w_ada[l] + b_ada[l])[:, None, :]
        shift1, scale1, gate1, shift2, scale2, gate2 = jnp.split(mod, 6, axis=-1)

        h = rms_norm(x, norm1_w[l]) * (1.0 + scale1) + shift1
        proj = h @ w_in[l]
        gq, gk, gv, gz, ga, gb, sq, sk, sv = jnp.split(proj, split_points, axis=-1)

        qkv = jax.nn.silu(causal_depthwise_conv(jnp.concatenate([gq, gk, gv], axis=-1), conv_w[l]))
        gq, gk, gv = jnp.split(qkv, 3, axis=-1)
        heads = lambda t: t.reshape(B, T, -1, HEAD_DIM)
        q_g = l2_norm(heads(gq))
        k_g = l2_norm(heads(gk))
        v_g = heads(gv).astype(jnp.float32)
        beta = jax.nn.sigmoid(gb.astype(jnp.float32))
        g = -jnp.exp(a_log[l].astype(jnp.float32)) * jax.nn.softplus(
            ga.astype(jnp.float32) + dt_bias[l].astype(jnp.float32))
        o_g = gated_delta_rule_chunked(q_g, k_g, v_g, g, beta).astype(x.dtype)
        o_g = rms_norm(o_g, gdn_norm_w[l]) * jax.nn.silu(heads(gz))
        o_g = o_g.reshape(B, T, GDN_WIDTH)

        q_s = rms_norm(heads(sq), q_norm_w[l])
        k_s = rms_norm(heads(sk), k_norm_w[l])
        v_s = heads(sv)
        o_s = sliding_window_attention(q_s, k_s, v_s, sinks[l])

        mixed = jnp.concatenate([o_g, o_s], axis=-1) @ w_out[l]
        x = x + gate1 * mixed

        h2 = rms_norm(x, norm2_w[l]) * (1.0 + scale2) + shift2
        ffn = (jax.nn.silu(h2 @ w_gate[l]) * (h2 @ w_up[l])) @ w_down[l]
        x = x + gate2 * ffn
    return x
```

```python
import functools

import numpy as np
import jax
import jax.numpy as jnp
from jax import lax
from jax.experimental import pallas as pl
from jax.experimental.pallas import tpu as pltpu

F32 = jnp.float32
BF16 = jnp.bfloat16

D_MODEL = 1024
HEAD_DIM = 64
GDN_HEADS = 8
GDN_WIDTH = GDN_HEADS * HEAD_DIM
SWA_Q_HEADS = 8
SWA_KV_HEADS = 2
SWA_GROUP = SWA_Q_HEADS // SWA_KV_HEADS
SWA_WIDTH = SWA_Q_HEADS * HEAD_DIM
SWA_KV_WIDTH = SWA_KV_HEADS * HEAD_DIM
WINDOW = 128
CONV_WIDTH = 4
CHUNK = 64
D_FF = 2816
EPS = 1e-6
LANES = 128
GATE_PAD = LANES
HEADS_PER_GROUP = 4
GROUP_W = HEADS_PER_GROUP * HEAD_DIM
NEG_BIG = -1e30
VMEM_LIMIT = 56 * 1024 * 1024

PROJ_TM = 512
GDN_TT = 256
SWA_TQ = 512
FFN_TM = 256
ADA_TN = 1536


def _sigmoid(x):
    return 1.0 / (1.0 + jnp.exp(-x))


def _dot(a, b):
    return jnp.dot(a, b, preferred_element_type=F32)


def _dot_nt(a, b):
    return lax.dot_general(a, b, (((1,), (1,)), ((), ())), preferred_element_type=F32)


def _dot_tn(a, b):
    return lax.dot_general(a, b, (((0,), (0,)), ((), ())), preferred_element_type=F32)


def _split3(x):
    x1 = x.astype(BF16)
    r1 = x - x1.astype(F32)
    x2 = r1.astype(BF16)
    r2 = r1 - x2.astype(F32)
    x3 = r2.astype(BF16)
    return x1, x2, x3


def _ada_kernel(c_ref, w_ref, b_ref, o_ref):
    c = c_ref[...]
    ca = c * _sigmoid(c)
    o_ref[...] = _dot(ca.astype(BF16), w_ref[...].astype(BF16)) + b_ref[...]


def _ada_call(c_pad, w_ada, b_ada):
    n = w_ada.shape[1]
    return pl.pallas_call(
        _ada_kernel,
        grid=(n // ADA_TN,),
        in_specs=[
            pl.BlockSpec((8, D_MODEL), lambda j: (0, 0)),
            pl.BlockSpec((D_MODEL, ADA_TN), lambda j: (0, j)),
            pl.BlockSpec((1, ADA_TN), lambda j: (0, j)),
        ],
        out_specs=pl.BlockSpec((8, ADA_TN), lambda j: (0, j)),
        out_shape=jax.ShapeDtypeStruct((8, n), F32),
        compiler_params=pltpu.CompilerParams(
            dimension_semantics=("arbitrary",), vmem_limit_bytes=VMEM_LIMIT),
        name="ada",
    )(c_pad, w_ada, b_ada)


QKV_W = 3 * GDN_WIDTH
PROJ_SPLITS = (QKV_W, GDN_WIDTH, SWA_WIDTH, 2 * SWA_KV_WIDTH, GATE_PAD)
PROJ_COLS = sum(PROJ_SPLITS)


def _proj_kernel(x_ref, mod_ref, nw_ref, w_ref, qkv_ref, z_ref, sq_ref, skv_ref, ab_ref):
    x = x_ref[0]
    ms = jnp.mean(x * x, axis=-1, keepdims=True)
    y = x * lax.rsqrt(ms + EPS) * nw_ref[...]
    mod = mod_ref[0]
    shift = mod[:, 0:D_MODEL]
    scale = mod[:, D_MODEL:2 * D_MODEL]
    hb = (y * (1.0 + scale) + shift).astype(BF16)
    col = 0
    for ref, width in zip((qkv_ref, z_ref, sq_ref, skv_ref, ab_ref), PROJ_SPLITS):
        ref[0] = _dot(hb, w_ref[:, col:col + width])
        col += width


def _proj_call(x, mod3, norm_w, w_cat):
    b, t, _ = x.shape
    tm = PROJ_TM
    row = lambda width: pl.BlockSpec((1, tm, width), lambda bi, ti: (bi, ti, 0))
    const = lambda shape: pl.BlockSpec(shape, lambda bi, ti: (0,) * len(shape))
    return pl.pallas_call(
        _proj_kernel,
        grid=(b, t // tm),
        in_specs=[
            row(D_MODEL),
            pl.BlockSpec((1, 1, 6 * D_MODEL), lambda bi, ti: (bi, 0, 0)),
            const((1, D_MODEL)),
            const((D_MODEL, PROJ_COLS)),
        ],
        out_specs=[row(w) for w in PROJ_SPLITS],
        out_shape=[jax.ShapeDtypeStruct((b, t, w), F32) for w in PROJ_SPLITS],
        compiler_params=pltpu.CompilerParams(
            dimension_semantics=("arbitrary", "arbitrary"), vmem_limit_bytes=VMEM_LIMIT),
        name="proj",
    )(x, mod3, norm_w, w_cat)


def _block_diag_rows(p, mask_bd):
    return jnp.concatenate([p] * HEADS_PER_GROUP, axis=0) * mask_bd


def _split2(x):
    hi = x.astype(BF16)
    lo = (x - hi.astype(F32)).astype(BF16)
    return hi, lo


def _bd_pair(pair, mask_bd16):
    return tuple(_block_diag_rows(t, mask_bd16) for t in pair)


def _dot_hilo(a, b):
    return _dot(a[0], b[0]) + _dot(a[0], b[1]) + _dot(a[1], b[0])


def _gdn_kernel(qkv_ref, z_ref, ab_ref, cw_ref, alog_ref, dtb_ref, gnw_ref, bd_ref, eg_ref, eb_ref,
                ltri_ref, o_ref, xbuf, s_ref, obuf):
    tt = GDN_TT
    ti = pl.program_id(1)

    @pl.when(ti == 0)
    def _():
        xbuf[0:8, :] = jnp.zeros((8, QKV_W), F32)
        s_ref[...] = jnp.zeros_like(s_ref)

    xbuf[8:8 + tt, :] = qkv_ref[0]
    acc = cw_ref[CONV_WIDTH - 1:CONV_WIDTH, :] * xbuf[8:8 + tt, :]
    for j in range(CONV_WIDTH - 1):
        off = 8 - (CONV_WIDTH - 1) + j
        acc = acc + cw_ref[j:j + 1, :] * xbuf[off:off + tt, :]
    xbuf[0:8, :] = xbuf[tt:tt + 8, :]
    y = acc * _sigmoid(acc)

    bd = bd_ref[...]
    q = y[:, 0:GDN_WIDTH]
    k = y[:, GDN_WIDTH:2 * GDN_WIDTH]
    v = y[:, 2 * GDN_WIDTH:]
    q = q * lax.rsqrt(_dot((q * q).astype(BF16), bd) + EPS) * (HEAD_DIM ** -0.5)
    k = k * lax.rsqrt(_dot((k * k).astype(BF16), bd) + EPS)

    ab = ab_ref[0]
    lane = lax.broadcasted_iota(jnp.int32, (1, GATE_PAD), 1)
    xs = ab + dtb_ref[...]
    softplus = jnp.maximum(xs, 0.0) + jnp.log1p(jnp.exp(-jnp.abs(xs)))
    g = jnp.where(lane < GDN_HEADS, -jnp.exp(alog_ref[...]) * softplus, 0.0)
    beta = _sigmoid(ab)
    ltri = ltri_ref[...]
    g1, g2, g3 = _split3(g)
    gcum = _dot(ltri, g1) + _dot(ltri, g2) + _dot(ltri, g3)
    eg = eg_ref[...]
    c1, c2, c3 = _split3(gcum)
    g_x = _dot(c1, eg) + _dot(c2, eg) + _dot(c3, eg)
    beta_x = _dot(beta.astype(BF16), eb_ref[...])

    r64 = lax.broadcasted_iota(jnp.int32, (CHUNK, GROUP_W), 0)
    l64 = lax.broadcasted_iota(jnp.int32, (CHUNK, GROUP_W), 1) % CHUNK
    causal = r64 >= l64
    eye_x = (r64 == l64).astype(F32)
    strict_x = (r64 > l64).astype(F32)
    rb = lax.broadcasted_iota(jnp.int32, (GROUP_W, GROUP_W), 0) // HEAD_DIM
    cb = lax.broadcasted_iota(jnp.int32, (GROUP_W, GROUP_W), 1) // HEAD_DIM
    mask_bd = (rb == cb).astype(F32)
    mask_bd16 = mask_bd.astype(BF16)

    for gi in range(GDN_HEADS // HEADS_PER_GROUP):
        l0 = gi * GROUP_W
        s_state = s_ref[gi]
        for c in range(tt // CHUNK):
            r0 = c * CHUNK
            kc = k[r0:r0 + CHUNK, l0:l0 + GROUP_W]
            qc = q[r0:r0 + CHUNK, l0:l0 + GROUP_W]
            vc = v[r0:r0 + CHUNK, l0:l0 + GROUP_W]
            bx = beta_x[r0:r0 + CHUNK, l0:l0 + GROUP_W]
            gx = g_x[r0:r0 + CHUNK, l0:l0 + GROUP_W]
            eg_c = jnp.exp(gx)
            glast = gx[CHUNK - 1:CHUNK, :]
            kb = kc * bx
            vb = vc * bx
            wr = kb * eg_c
            qd = qc * eg_c
            kd = kc * jnp.exp(glast - gx)
            dec_row = jnp.exp(glast)
            grow = jnp.sum(gx * eye_x, axis=0, keepdims=True)
            dm = jnp.exp(jnp.where(causal, gx - grow, NEG_BIG))
            kbd = _block_diag_rows(kc, mask_bd)
            aq = _dot_nt(jnp.concatenate([kb, qc], axis=0), kbd)
            xm = -(aq[0:CHUNK] * dm * strict_x)
            qk = aq[CHUNK:] * dm
            sinv = eye_x + xm
            p = _dot_hilo(_split2(xm), _bd_pair(_split2(xm), mask_bd16))
            for _ in range(4):
                ph, pl_ = _split2(p)
                sh, sl = _split2(sinv)
                lhs = (jnp.concatenate([sh, ph], axis=0), jnp.concatenate([sl, pl_], axis=0))
                r = _dot_hilo(lhs, _bd_pair((ph, pl_), mask_bd16))
                sinv = sinv + r[0:CHUNK]
                p = r[CHUNK:]
            sinv = sinv + _dot_hilo(_split2(sinv), _bd_pair(_split2(p), mask_bd16))
            u = _dot(sinv, _block_diag_rows(vb, mask_bd))
            w = _dot(sinv, _block_diag_rows(wr, mask_bd))
            r2 = _dot(jnp.concatenate([w, qd], axis=0), s_state)
            vn = u - r2[0:CHUNK]
            o = r2[CHUNK:] + _dot(qk, _block_diag_rows(vn, mask_bd))
            s_state = s_state * dec_row + mask_bd * _dot_tn(kd, vn)
            obuf[r0:r0 + CHUNK, l0:l0 + GROUP_W] = o
        s_ref[gi] = s_state

    o = obuf[...]
    ms = _dot((o * o).astype(BF16), bd) * (1.0 / HEAD_DIM)
    zz = z_ref[0]
    o_ref[0] = (o * lax.rsqrt(ms + EPS) * gnw_ref[...] * (zz * _sigmoid(zz))).astype(BF16)


def _gdn_consts():
    h = np.arange(GDN_WIDTH) // HEAD_DIM
    bd = (h[:, None] == h[None, :]).astype(np.float32)
    eg = np.zeros((GATE_PAD, GDN_WIDTH), np.float32)
    eb = np.zeros((GATE_PAD, GDN_WIDTH), np.float32)
    eg[h, np.arange(GDN_WIDTH)] = 1.0
    eb[GDN_HEADS + h, np.arange(GDN_WIDTH)] = 1.0
    t = np.arange(GDN_TT)
    ltri = ((t[:, None] // CHUNK == t[None, :] // CHUNK) & (t[:, None] >= t[None, :])).astype(np.float32)
    return (jnp.asarray(bd, BF16), jnp.asarray(eg, BF16), jnp.asarray(eb, BF16), jnp.asarray(ltri, BF16))


def _gdn_call(qkv, z, ab, conv_w, alog_pad, dtb_pad, gnw_x):
    b, t, _ = qkv.shape
    tt = GDN_TT
    bd, eg, eb, ltri = _gdn_consts()
    row = lambda width: pl.BlockSpec((1, tt, width), lambda bi, ti: (bi, ti, 0))
    const = lambda shape: pl.BlockSpec(shape, lambda bi, ti: (0,) * len(shape))
    return pl.pallas_call(
        _gdn_kernel,
        grid=(b, t // tt),
        in_specs=[
            row(QKV_W), row(GDN_WIDTH), row(GATE_PAD),
            const((CONV_WIDTH, QKV_W)), const((1, GATE_PAD)), const((1, GATE_PAD)), const((1, GDN_WIDTH)),
            const((GDN_WIDTH, GDN_WIDTH)), const((GATE_PAD, GDN_WIDTH)), const((GATE_PAD, GDN_WIDTH)),
            const((tt, tt)),
        ],
        out_specs=row(GDN_WIDTH),
        out_shape=jax.ShapeDtypeStruct((b, t, GDN_WIDTH), BF16),
        scratch_shapes=[
            pltpu.VMEM((8 + tt, QKV_W), F32),
            pltpu.VMEM((GDN_HEADS // HEADS_PER_GROUP, GROUP_W, GROUP_W), F32),
            pltpu.VMEM((tt, GDN_WIDTH), F32),
        ],
        compiler_params=pltpu.CompilerParams(
            dimension_semantics=("arbitrary", "arbitrary"), vmem_limit_bytes=VMEM_LIMIT),
        name="gdn",
    )(qkv, z, ab, conv_w, alog_pad, dtb_pad, gnw_x, bd, eg, eb, ltri)


SWA_PAIRS = SWA_Q_HEADS // 2


def _swa_kernel(q_ref, kvc_ref, kvp_ref, qw_ref, kw_ref, bdq_ref, bdk_ref, bias_ref, sink_ref, o_ref):
    tq = SWA_TQ
    ti = pl.program_id(1)
    q = q_ref[0]
    q = q * lax.rsqrt(_dot((q * q).astype(BF16), bdq_ref[...]) * (1.0 / HEAD_DIM) + EPS)
    q = q * qw_ref[...] * (HEAD_DIM ** -0.5)
    kv = jnp.concatenate([kvp_ref[0], kvc_ref[0]], axis=0)
    k = kv[:, 0:SWA_KV_WIDTH]
    v = kv[:, SWA_KV_WIDTH:]
    k = k * lax.rsqrt(_dot((k * k).astype(BF16), bdk_ref[...]) * (1.0 / HEAD_DIM) + EPS) * kw_ref[...]
    lo = lax.broadcasted_iota(jnp.int32, (1, LANES), 1) < HEAD_DIM
    first = jnp.where(ti == 0, 1, 0)
    for j in range(tq // WINDOW):
        kb = k[j * WINDOW:(j + 2) * WINDOW]
        vb = v[j * WINDOW:(j + 2) * WINDOW]
        for p in range(SWA_PAIRS):
            qp = q[j * WINDOW:(j + 1) * WINDOW, p * LANES:(p + 1) * LANES]
            lhs = jnp.concatenate([jnp.where(lo, qp, 0.0), jnp.where(lo, 0.0, qp)], axis=0)
            bias = bias_ref[first, p] if j == 0 else bias_ref[0, p]
            s = _dot_nt(lhs, kb) + bias
            sink = sink_ref[p]
            m = jnp.maximum(jnp.max(s, axis=-1, keepdims=True), sink)
            pe = jnp.exp(s - m)
            den = jnp.sum(pe, axis=-1, keepdims=True) + jnp.exp(sink - m)
            o2 = _dot(pe, vb) / den
            op = jnp.where(lo, o2[0:WINDOW], o2[WINDOW:])
            o_ref[0, j * WINDOW:(j + 1) * WINDOW, p * LANES:(p + 1) * LANES] = op.astype(BF16)


def _swa_consts():
    qi = np.arange(WINDOW)[:, None] + WINDOW
    ki = np.arange(2 * WINDOW)[None, :]
    dist = (qi - ki).astype(np.float32)
    in_window = (qi - ki >= 0) & (qi - ki < WINDOW)
    slopes = 2.0 ** (-8.0 * (np.arange(SWA_Q_HEADS, dtype=np.float32) + 1.0) / SWA_Q_HEADS)
    bias = np.zeros((2, SWA_PAIRS, 2 * WINDOW, 2 * WINDOW), np.float32)
    for first in range(2):
        mask = in_window & ((ki >= WINDOW) if first else True)
        for p in range(SWA_PAIRS):
            for half, head in enumerate((p, SWA_GROUP + p)):
                bias[first, p, half * WINDOW:(half + 1) * WINDOW] = np.where(
                    mask, -slopes[head].astype(np.float32) * dist, np.float32(NEG_BIG))
    h = np.arange(SWA_WIDTH) // HEAD_DIM
    bdq = (h[:, None] == h[None, :]).astype(np.float32)
    hk = np.arange(SWA_KV_WIDTH) // HEAD_DIM
    bdk = (hk[:, None] == hk[None, :]).astype(np.float32)
    return jnp.asarray(bias), jnp.asarray(bdq, BF16), jnp.asarray(bdk, BF16)


def _swa_call(sq, skv, qw_x, kw_x, sink_col):
    b, t, _ = sq.shape
    tq = SWA_TQ
    nb = tq // WINDOW
    bias, bdq, bdk = _swa_consts()
    const = lambda shape: pl.BlockSpec(shape, lambda bi, ti: (0,) * len(shape))
    return pl.pallas_call(
        _swa_kernel,
        grid=(b, t // tq),
        in_specs=[
            pl.BlockSpec((1, tq, SWA_WIDTH), lambda bi, ti: (bi, ti, 0)),
            pl.BlockSpec((1, tq, 2 * SWA_KV_WIDTH), lambda bi, ti: (bi, ti, 0)),
            pl.BlockSpec((1, WINDOW, 2 * SWA_KV_WIDTH), lambda bi, ti: (bi, jnp.maximum(ti * nb - 1, 0), 0)),
            const((1, SWA_WIDTH)), const((1, SWA_KV_WIDTH)),
            const((SWA_WIDTH, SWA_WIDTH)), const((SWA_KV_WIDTH, SWA_KV_WIDTH)),
            const((2, SWA_PAIRS, 2 * WINDOW, 2 * WINDOW)),
            const((SWA_PAIRS, 2 * WINDOW, 1)),
        ],
        out_specs=pl.BlockSpec((1, tq, SWA_WIDTH), lambda bi, ti: (bi, ti, 0)),
        out_shape=jax.ShapeDtypeStruct((b, t, SWA_WIDTH), BF16),
        compiler_params=pltpu.CompilerParams(
            dimension_semantics=("arbitrary", "arbitrary"), vmem_limit_bytes=VMEM_LIMIT),
        name="swa",
    )(sq, skv, skv, qw_x, kw_x, bdq, bdk, bias, sink_col)


def _ffn_kernel(x_ref, og_ref, os_ref, mod_ref, nw_ref, wog_ref, wos_ref, wg_ref, wu_ref, wd_ref, o_ref):
    x = x_ref[0]
    mod = mod_ref[0]
    gate1 = mod[:, 2 * D_MODEL:3 * D_MODEL]
    shift2 = mod[:, 3 * D_MODEL:4 * D_MODEL]
    scale2 = mod[:, 4 * D_MODEL:5 * D_MODEL]
    gate2 = mod[:, 5 * D_MODEL:]
    mixed = _dot(og_ref[0], wog_ref[...]) + _dot(os_ref[0], wos_ref[...])
    x1 = x + gate1 * mixed
    ms = jnp.mean(x1 * x1, axis=-1, keepdims=True)
    hb = ((x1 * lax.rsqrt(ms + EPS) * nw_ref[...]) * (1.0 + scale2) + shift2).astype(BF16)
    gt = _dot(hb, wg_ref[...])
    up = _dot(hb, wu_ref[...])
    act = ((gt * _sigmoid(gt)) * up).astype(BF16)
    o_ref[0] = x1 + gate2 * _dot(act, wd_ref[...])


def _ffn_call(x, og, osw, mod3, norm_w, wog, wos, wg, wu, wd):
    b, t, _ = x.shape
    tm = FFN_TM
    row = lambda width: pl.BlockSpec((1, tm, width), lambda bi, ti: (bi, ti, 0))
    const = lambda shape: pl.BlockSpec(shape, lambda bi, ti: (0,) * len(shape), pipeline_mode=pl.Buffered(1))
    return pl.pallas_call(
        _ffn_kernel,
        grid=(b, t // tm),
        in_specs=[
            row(D_MODEL), row(GDN_WIDTH), row(SWA_WIDTH),
            pl.BlockSpec((1, 1, 6 * D_MODEL), lambda bi, ti: (bi, 0, 0)),
            const((1, D_MODEL)),
            const((GDN_WIDTH, D_MODEL)), const((SWA_WIDTH, D_MODEL)),
            const((D_MODEL, D_FF)), const((D_MODEL, D_FF)), const((D_FF, D_MODEL)),
        ],
        out_specs=row(D_MODEL),
        out_shape=jax.ShapeDtypeStruct((b, t, D_MODEL), F32),
        compiler_params=pltpu.CompilerParams(
            dimension_semantics=("arbitrary", "arbitrary"), vmem_limit_bytes=VMEM_LIMIT),
        name="ffn",
    )(x, og, osw, mod3, norm_w, wog, wos, wg, wu, wd)


def _pair_perm():
    cols = []
    for p in range(SWA_PAIRS):
        for head in (p, SWA_GROUP + p):
            cols.extend(range(head * HEAD_DIM, (head + 1) * HEAD_DIM))
    return np.asarray(cols, np.int32)


def _layer(x, mod, norm1_w, w_in, conv_w, a_log, dt_bias, gdn_norm_w, q_norm_w, k_norm_w, sinks, w_out,
           norm2_w, w_gate, w_up, w_down):
    b = x.shape[0]
    perm = _pair_perm()
    o0 = 4 * GDN_WIDTH
    o1 = o0 + 2 * GDN_HEADS
    o2 = o1 + SWA_WIDTH
    w_cat = jnp.concatenate([
        w_in[:, 0:QKV_W],
        w_in[:, QKV_W:o0],
        w_in[:, o1:o2][:, perm],
        w_in[:, o2:],
        jnp.pad(w_in[:, o0:o1], ((0, 0), (0, GATE_PAD - 2 * GDN_HEADS))),
    ], axis=1).astype(BF16)
    mod3 = mod[:b].reshape(b, 1, 6 * D_MODEL)
    qkv, z, sq, skv, ab = _proj_call(x, mod3, norm1_w.reshape(1, D_MODEL), w_cat)

    alog_pad = jnp.pad(a_log.reshape(1, GDN_HEADS), ((0, 0), (0, GATE_PAD - GDN_HEADS)))
    dtb_pad = jnp.pad(dt_bias.reshape(1, GDN_HEADS), ((0, 0), (0, GATE_PAD - GDN_HEADS)))
    gnw_x = jnp.tile(gdn_norm_w.reshape(1, HEAD_DIM), (1, GDN_HEADS))
    og = _gdn_call(qkv, z, ab, conv_w.reshape(CONV_WIDTH, QKV_W), alog_pad, dtb_pad, gnw_x)

    qw_x = jnp.tile(q_norm_w.reshape(1, HEAD_DIM), (1, SWA_Q_HEADS))
    kw_x = jnp.tile(k_norm_w.reshape(1, HEAD_DIM), (1, SWA_KV_HEADS))
    sink_pairs = jnp.stack([sinks[:SWA_GROUP], sinks[SWA_GROUP:]], axis=1)
    sink_col = jnp.repeat(sink_pairs, WINDOW, axis=1).reshape(SWA_PAIRS, 2 * WINDOW, 1)
    osw = _swa_call(sq, skv, qw_x, kw_x, sink_col)

    wog = w_out[:GDN_WIDTH].astype(BF16)
    wos = w_out[GDN_WIDTH:][perm].astype(BF16)
    return _ffn_call(x, og, osw, mod3, norm2_w.reshape(1, D_MODEL), wog, wos,
                     w_gate.astype(BF16), w_up.astype(BF16), w_down.astype(BF16))


def kernel(x, c, w_ada, b_ada, norm1_w, w_in, conv_w, a_log, dt_bias, gdn_norm_w, q_norm_w, k_norm_w, sinks,
           w_out, norm2_w, w_gate, w_up, w_down):
    depth = w_ada.shape[0]
    b = c.shape[0]
    c_pad = jnp.pad(c, ((0, 8 - b), (0, 0)))
    for l in range(depth):
        mod = _ada_call(c_pad, w_ada[l], b_ada[l].reshape(1, -1))
        x = _layer(x, mod, norm1_w[l], w_in[l], conv_w[l], a_log[l], dt_bias[l], gdn_norm_w[l], q_norm_w[l],
                   k_norm_w[l], sinks[l], w_out[l], norm2_w[l], w_gate[l], w_up[l], w_down[l])
    return x
```

```python
import functools

import numpy as np
import jax
import jax.numpy as jnp
from jax import lax
from jax.experimental import pallas as pl
from jax.experimental.pallas import tpu as pltpu

F32 = jnp.float32
BF16 = jnp.bfloat16

D_MODEL = 1024
HEAD_DIM = 64
GDN_HEADS = 8
GDN_WIDTH = GDN_HEADS * HEAD_DIM
SWA_Q_HEADS = 8
SWA_KV_HEADS = 2
SWA_GROUP = SWA_Q_HEADS // SWA_KV_HEADS
SWA_WIDTH = SWA_Q_HEADS * HEAD_DIM
SWA_KV_WIDTH = SWA_KV_HEADS * HEAD_DIM
WINDOW = 128
CONV_WIDTH = 4
CHUNK = 64
D_FF = 2816
EPS = 1e-6
LANES = 128
GATE_PAD = LANES
HEADS_PER_GROUP = 4
GROUP_W = HEADS_PER_GROUP * HEAD_DIM
INV_BASE = 8
INV_LEVELS = (8, 16, 32)
NEG_BIG = -1e30
VMEM_LIMIT = 56 * 1024 * 1024

PROJ_TM = 512
GDN_TT = 256
SWA_TQ = 512
FFN_TM = 256
ADA_TN = 1536


def _sigmoid(x):
    return 1.0 / (1.0 + jnp.exp(-x))


def _dot(a, b):
    return jnp.dot(a, b, preferred_element_type=F32)


def _dot_nt(a, b):
    return lax.dot_general(a, b, (((1,), (1,)), ((), ())), preferred_element_type=F32)


def _dot_tn(a, b):
    return lax.dot_general(a, b, (((0,), (0,)), ((), ())), preferred_element_type=F32)


def _split3(x):
    x1 = x.astype(BF16)
    r1 = x - x1.astype(F32)
    x2 = r1.astype(BF16)
    r2 = r1 - x2.astype(F32)
    x3 = r2.astype(BF16)
    return x1, x2, x3


def _ada_kernel(c_ref, w_ref, b_ref, o_ref):
    c = c_ref[...]
    ca = c * _sigmoid(c)
    o_ref[...] = _dot(ca.astype(BF16), w_ref[...].astype(BF16)) + b_ref[...]


def _ada_call(c_pad, w_ada, b_ada):
    n = w_ada.shape[1]
    return pl.pallas_call(
        _ada_kernel,
        grid=(n // ADA_TN,),
        in_specs=[
            pl.BlockSpec((8, D_MODEL), lambda j: (0, 0)),
            pl.BlockSpec((D_MODEL, ADA_TN), lambda j: (0, j)),
            pl.BlockSpec((1, ADA_TN), lambda j: (0, j)),
        ],
        out_specs=pl.BlockSpec((8, ADA_TN), lambda j: (0, j)),
        out_shape=jax.ShapeDtypeStruct((8, n), F32),
        compiler_params=pltpu.CompilerParams(
            dimension_semantics=("arbitrary",), vmem_limit_bytes=VMEM_LIMIT),
        name="ada",
    )(c_pad, w_ada, b_ada)


QKV_W = 3 * GDN_WIDTH
PROJ_SPLITS = (QKV_W, GDN_WIDTH, SWA_WIDTH, 2 * SWA_KV_WIDTH, GATE_PAD)
PROJ_COLS = sum(PROJ_SPLITS)


def _proj_kernel(x_ref, mod_ref, nw_ref, w_ref, qkv_ref, z_ref, sq_ref, skv_ref, ab_ref):
    x = x_ref[0]
    ms = jnp.mean(x * x, axis=-1, keepdims=True)
    y = x * lax.rsqrt(ms + EPS) * nw_ref[...]
    mod = mod_ref[0]
    shift = mod[:, 0:D_MODEL]
    scale = mod[:, D_MODEL:2 * D_MODEL]
    hb = (y * (1.0 + scale) + shift).astype(BF16)
    col = 0
    for ref, width in zip((qkv_ref, z_ref, sq_ref, skv_ref, ab_ref), PROJ_SPLITS):
        ref[0] = _dot(hb, w_ref[:, col:col + width])
        col += width


def _proj_call(x, mod3, norm_w, w_cat):
    b, t, _ = x.shape
    tm = PROJ_TM
    row = lambda width: pl.BlockSpec((1, tm, width), lambda bi, ti: (bi, ti, 0))
    const = lambda shape: pl.BlockSpec(shape, lambda bi, ti: (0,) * len(shape))
    return pl.pallas_call(
        _proj_kernel,
        grid=(b, t // tm),
        in_specs=[
            row(D_MODEL),
            pl.BlockSpec((1, 1, 6 * D_MODEL), lambda bi, ti: (bi, 0, 0)),
            const((1, D_MODEL)),
            const((D_MODEL, PROJ_COLS)),
        ],
        out_specs=[row(w) for w in PROJ_SPLITS],
        out_shape=[jax.ShapeDtypeStruct((b, t, w), F32) for w in PROJ_SPLITS],
        compiler_params=pltpu.CompilerParams(
            dimension_semantics=("arbitrary", "arbitrary"), vmem_limit_bytes=VMEM_LIMIT),
        name="proj",
    )(x, mod3, norm_w, w_cat)


def _block_diag_rows(p, mask_bd16):
    return jnp.concatenate([p.astype(BF16)] * HEADS_PER_GROUP, axis=0) * mask_bd16


def _gdn_kernel(qkv_ref, z_ref, ab_ref, cw_ref, alog_ref, dtb_ref, gnw_ref, bd_ref, eg_ref, eb_ref,
                ltri_ref, o_ref, xbuf, s_ref, obuf):
    tt = GDN_TT
    ti = pl.program_id(1)

    @pl.when(ti == 0)
    def _():
        xbuf[0:8, :] = jnp.zeros((8, QKV_W), F32)
        s_ref[...] = jnp.zeros_like(s_ref)

    xbuf[8:8 + tt, :] = qkv_ref[0]
    acc = cw_ref[CONV_WIDTH - 1:CONV_WIDTH, :] * xbuf[8:8 + tt, :]
    for j in range(CONV_WIDTH - 1):
        off = 8 - (CONV_WIDTH - 1) + j
        acc = acc + cw_ref[j:j + 1, :] * xbuf[off:off + tt, :]
    xbuf[0:8, :] = xbuf[tt:tt + 8, :]
    y = acc * _sigmoid(acc)

    bd = bd_ref[...]
    q = y[:, 0:GDN_WIDTH]
    k = y[:, GDN_WIDTH:2 * GDN_WIDTH]
    v = y[:, 2 * GDN_WIDTH:]
    q = q * lax.rsqrt(_dot((q * q).astype(BF16), bd) + EPS) * (HEAD_DIM ** -0.5)
    k = k * lax.rsqrt(_dot((k * k).astype(BF16), bd) + EPS)

    ab = ab_ref[0]
    lane = lax.broadcasted_iota(jnp.int32, (1, GATE_PAD), 1)
    xs = ab + dtb_ref[...]
    softplus = jnp.maximum(xs, 0.0) + jnp.log1p(jnp.exp(-jnp.abs(xs)))
    g = jnp.where(lane < GDN_HEADS, -jnp.exp(alog_ref[...]) * softplus, 0.0)
    beta = _sigmoid(ab)
    ltri = ltri_ref[...]
    g1, g2, g3 = _split3(g)
    gcum = _dot(ltri, g1) + _dot(ltri, g2) + _dot(ltri, g3)
    eg = eg_ref[...]
    c1, c2, c3 = _split3(gcum)
    g_x = _dot(c1, eg) + _dot(c2, eg) + _dot(c3, eg)
    beta_x = _dot(beta.astype(BF16), eb_ref[...])

    r64 = lax.broadcasted_iota(jnp.int32, (CHUNK, GROUP_W), 0)
    l64 = lax.broadcasted_iota(jnp.int32, (CHUNK, GROUP_W), 1) % CHUNK
    causal = r64 >= l64
    eye_x = (r64 == l64).astype(F32)
    strict_x = (r64 > l64).astype(F32)
    same_block = lambda size: (r64 // size) == (l64 // size)
    base_x = same_block(INV_BASE).astype(F32)
    level_x = [(same_block(2 * size) & ~same_block(size)).astype(F32) for size in INV_LEVELS]
    rb = lax.broadcasted_iota(jnp.int32, (GROUP_W, GROUP_W), 0) // HEAD_DIM
    cb = lax.broadcasted_iota(jnp.int32, (GROUP_W, GROUP_W), 1) // HEAD_DIM
    mask_bd = (rb == cb).astype(F32)
    mask_bd16 = mask_bd.astype(BF16)

    n_groups = GDN_HEADS // HEADS_PER_GROUP
    n_chunks = tt // CHUNK
    units = [(c, gi) for c in range(n_chunks) for gi in range(n_groups)]
    tile = lambda a: [a[c * CHUNK:(c + 1) * CHUNK, gi * GROUP_W:(gi + 1) * GROUP_W] for c, gi in units]
    each = lambda f, *lists: [f(*args) for args in zip(*lists)]
    bdr = lambda t: _block_diag_rows(t, mask_bd16)
    mm = lambda a, w16: _dot(a.astype(BF16), w16)
    stack = lambda a, b_: jnp.concatenate([a, b_], axis=0)

    kc, qc, vc, bx, gx = tile(k), tile(q), tile(v), tile(beta_x), tile(g_x)
    eg_c = each(jnp.exp, gx)
    glast = each(lambda g_: g_[CHUNK - 1:CHUNK, :], gx)
    kb = each(jnp.multiply, kc, bx)
    vb = each(jnp.multiply, vc, bx)
    wr = each(jnp.multiply, kb, eg_c)
    qd = each(jnp.multiply, qc, eg_c)
    kd = each(lambda k_, g_, gl: k_ * jnp.exp(gl - g_), kc, gx, glast)
    dec_row = each(jnp.exp, glast)
    dm = each(lambda g_: jnp.exp(jnp.where(causal, g_ - jnp.sum(g_ * eye_x, axis=0, keepdims=True), NEG_BIG)), gx)
    aq = each(lambda a, b_, k_: _dot_nt(stack(a, b_).astype(BF16), bdr(k_)), kb, qc, kc)
    xm = each(lambda a, d_: -(a[0:CHUNK] * d_ * strict_x), aq, dm)
    qk = each(lambda a, d_: a[CHUNK:] * d_, aq, dm)

    xd = each(lambda x_: x_ * base_x, xm)
    tinv = each(lambda x_: eye_x + x_, xd)
    pw = each(lambda x_: mm(x_, bdr(x_)), xd)
    r = each(lambda t_, p_: mm(stack(t_, p_), bdr(p_)), tinv, pw)
    tinv = each(lambda t_, r_: t_ + r_[0:CHUNK], tinv, r)
    pw = each(lambda r_: r_[CHUNK:], r)
    tinv = each(lambda t_, p_: t_ + mm(t_, bdr(p_)), tinv, pw)
    for lm in level_x:
        e = each(lambda x_, t_: mm(x_ * lm, bdr(t_)), xm, tinv)
        tinv = each(lambda t_, e_: t_ + mm(t_, bdr(e_)), tinv, e)
    u = each(lambda t_, v_: mm(t_, bdr(v_)), tinv, vb)
    w = each(lambda t_, w_: mm(t_, bdr(w_)), tinv, wr)

    s_state = [s_ref[gi] for gi in range(n_groups)]
    for c in range(n_chunks):
        ids = [c * n_groups + gi for gi in range(n_groups)]
        r2 = [mm(stack(w[i], qd[i]), s_state[gi].astype(BF16)) for gi, i in enumerate(ids)]
        vn = [u[i] - r2[gi][0:CHUNK] for gi, i in enumerate(ids)]
        o = [r2[gi][CHUNK:] + mm(qk[i], bdr(vn[gi])) for gi, i in enumerate(ids)]
        s_state = [s_state[gi] * dec_row[i] + mask_bd * _dot_tn(kd[i].astype(BF16), vn[gi].astype(BF16))
                   for gi, i in enumerate(ids)]
        for gi in range(n_groups):
            obuf[c * CHUNK:(c + 1) * CHUNK, gi * GROUP_W:(gi + 1) * GROUP_W] = o[gi]
    for gi in range(n_groups):
        s_ref[gi] = s_state[gi]

    o = obuf[...]
    ms = _dot((o * o).astype(BF16), bd) * (1.0 / HEAD_DIM)
    zz = z_ref[0]
    o_ref[0] = (o * lax.rsqrt(ms + EPS) * gnw_ref[...] * (zz * _sigmoid(zz))).astype(BF16)


def _gdn_consts():
    h = np.arange(GDN_WIDTH) // HEAD_DIM
    bd = (h[:, None] == h[None, :]).astype(np.float32)
    eg = np.zeros((GATE_PAD, GDN_WIDTH), np.float32)
    eb = np.zeros((GATE_PAD, GDN_WIDTH), np.float32)
    eg[h, np.arange(GDN_WIDTH)] = 1.0
    eb[GDN_HEADS + h, np.arange(GDN_WIDTH)] = 1.0
    t = np.arange(GDN_TT)
    ltri = ((t[:, None] // CHUNK == t[None, :] // CHUNK) & (t[:, None] >= t[None, :])).astype(np.float32)
    return (jnp.asarray(bd, BF16), jnp.asarray(eg, BF16), jnp.asarray(eb, BF16), jnp.asarray(ltri, BF16))


def _gdn_call(qkv, z, ab, conv_w, alog_pad, dtb_pad, gnw_x):
    b, t, _ = qkv.shape
    tt = GDN_TT
    bd, eg, eb, ltri = _gdn_consts()
    row = lambda width: pl.BlockSpec((1, tt, width), lambda bi, ti: (bi, ti, 0))
    const = lambda shape: pl.BlockSpec(shape, lambda bi, ti: (0,) * len(shape))
    return pl.pallas_call(
        _gdn_kernel,
        grid=(b, t // tt),
        in_specs=[
            row(QKV_W), row(GDN_WIDTH), row(GATE_PAD),
            const((CONV_WIDTH, QKV_W)), const((1, GATE_PAD)), const((1, GATE_PAD)), const((1, GDN_WIDTH)),
            const((GDN_WIDTH, GDN_WIDTH)), const((GATE_PAD, GDN_WIDTH)), const((GATE_PAD, GDN_WIDTH)),
            const((tt, tt)),
        ],
        out_specs=row(GDN_WIDTH),
        out_shape=jax.ShapeDtypeStruct((b, t, GDN_WIDTH), BF16),
        scratch_shapes=[
            pltpu.VMEM((8 + tt, QKV_W), F32),
            pltpu.VMEM((GDN_HEADS // HEADS_PER_GROUP, GROUP_W, GROUP_W), F32),
            pltpu.VMEM((tt, GDN_WIDTH), F32),
        ],
        compiler_params=pltpu.CompilerParams(
            dimension_semantics=("arbitrary", "arbitrary"), vmem_limit_bytes=VMEM_LIMIT),
        name="gdn",
    )(qkv, z, ab, conv_w, alog_pad, dtb_pad, gnw_x, bd, eg, eb, ltri)


SWA_PAIRS = SWA_Q_HEADS // 2


def _swa_kernel(q_ref, kvc_ref, kvp_ref, qw_ref, kw_ref, bdq_ref, bdk_ref, bias_ref, sink_ref, o_ref):
    tq = SWA_TQ
    ti = pl.program_id(1)
    q = q_ref[0]
    q = q * lax.rsqrt(_dot((q * q).astype(BF16), bdq_ref[...]) * (1.0 / HEAD_DIM) + EPS)
    q = q * qw_ref[...] * (HEAD_DIM ** -0.5)
    kv = jnp.concatenate([kvp_ref[0], kvc_ref[0]], axis=0)
    k = kv[:, 0:SWA_KV_WIDTH]
    v = kv[:, SWA_KV_WIDTH:]
    k = k * lax.rsqrt(_dot((k * k).astype(BF16), bdk_ref[...]) * (1.0 / HEAD_DIM) + EPS) * kw_ref[...]
    lo = lax.broadcasted_iota(jnp.int32, (1, LANES), 1) < HEAD_DIM
    first = jnp.where(ti == 0, 1, 0)
    for j in range(tq // WINDOW):
        kb = k[j * WINDOW:(j + 2) * WINDOW]
        vb = v[j * WINDOW:(j + 2) * WINDOW]
        for p in range(SWA_PAIRS):
            qp = q[j * WINDOW:(j + 1) * WINDOW, p * LANES:(p + 1) * LANES]
            lhs = jnp.concatenate([jnp.where(lo, qp, 0.0), jnp.where(lo, 0.0, qp)], axis=0)
            bias = bias_ref[first, p] if j == 0 else bias_ref[0, p]
            s = _dot_nt(lhs, kb) + bias
            sink = sink_ref[p]
            m = jnp.maximum(jnp.max(s, axis=-1, keepdims=True), sink)
            pe = jnp.exp(s - m)
            den = jnp.sum(pe, axis=-1, keepdims=True) + jnp.exp(sink - m)
            o2 = _dot(pe, vb) / den
            op = jnp.where(lo, o2[0:WINDOW], o2[WINDOW:])
            o_ref[0, j * WINDOW:(j + 1) * WINDOW, p * LANES:(p + 1) * LANES] = op.astype(BF16)


def _swa_consts():
    qi = np.arange(WINDOW)[:, None] + WINDOW
    ki = np.arange(2 * WINDOW)[None, :]
    dist = (qi - ki).astype(np.float32)
    in_window = (qi - ki >= 0) & (qi - ki < WINDOW)
    slopes = 2.0 ** (-8.0 * (np.arange(SWA_Q_HEADS, dtype=np.float32) + 1.0) / SWA_Q_HEADS)
    bias = np.zeros((2, SWA_PAIRS, 2 * WINDOW, 2 * WINDOW), np.float32)
    for first in range(2):
        mask = in_window & ((ki >= WINDOW) if first else True)
        for p in range(SWA_PAIRS):
            for half, head in enumerate((p, SWA_GROUP + p)):
                bias[first, p, half * WINDOW:(half + 1) * WINDOW] = np.where(
                    mask, -slopes[head].astype(np.float32) * dist, np.float32(NEG_BIG))
    h = np.arange(SWA_WIDTH) // HEAD_DIM
    bdq = (h[:, None] == h[None, :]).astype(np.float32)
    hk = np.arange(SWA_KV_WIDTH) // HEAD_DIM
    bdk = (hk[:, None] == hk[None, :]).astype(np.float32)
    return jnp.asarray(bias), jnp.asarray(bdq, BF16), jnp.asarray(bdk, BF16)


def _swa_call(sq, skv, qw_x, kw_x, sink_col):
    b, t, _ = sq.shape
    tq = SWA_TQ
    nb = tq // WINDOW
    bias, bdq, bdk = _swa_consts()
    const = lambda shape: pl.BlockSpec(shape, lambda bi, ti: (0,) * len(shape))
    return pl.pallas_call(
        _swa_kernel,
        grid=(b, t // tq),
        in_specs=[
            pl.BlockSpec((1, tq, SWA_WIDTH), lambda bi, ti: (bi, ti, 0)),
            pl.BlockSpec((1, tq, 2 * SWA_KV_WIDTH), lambda bi, ti: (bi, ti, 0)),
            pl.BlockSpec((1, WINDOW, 2 * SWA_KV_WIDTH), lambda bi, ti: (bi, jnp.maximum(ti * nb - 1, 0), 0)),
            const((1, SWA_WIDTH)), const((1, SWA_KV_WIDTH)),
            const((SWA_WIDTH, SWA_WIDTH)), const((SWA_KV_WIDTH, SWA_KV_WIDTH)),
            const((2, SWA_PAIRS, 2 * WINDOW, 2 * WINDOW)),
            const((SWA_PAIRS, 2 * WINDOW, 1)),
        ],
        out_specs=pl.BlockSpec((1, tq, SWA_WIDTH), lambda bi, ti: (bi, ti, 0)),
        out_shape=jax.ShapeDtypeStruct((b, t, SWA_WIDTH), BF16),
        compiler_params=pltpu.CompilerParams(
            dimension_semantics=("arbitrary", "arbitrary"), vmem_limit_bytes=VMEM_LIMIT),
        name="swa",
    )(sq, skv, skv, qw_x, kw_x, bdq, bdk, bias, sink_col)


def _ffn_kernel(x_ref, og_ref, os_ref, mod_ref, nw_ref, wog_ref, wos_ref, wg_ref, wu_ref, wd_ref, o_ref):
    x = x_ref[0]
    mod = mod_ref[0]
    gate1 = mod[:, 2 * D_MODEL:3 * D_MODEL]
    shift2 = mod[:, 3 * D_MODEL:4 * D_MODEL]
    scale2 = mod[:, 4 * D_MODEL:5 * D_MODEL]
    gate2 = mod[:, 5 * D_MODEL:]
    mixed = _dot(og_ref[0], wog_ref[...]) + _dot(os_ref[0], wos_ref[...])
    x1 = x + gate1 * mixed
    ms = jnp.mean(x1 * x1, axis=-1, keepdims=True)
    hb = ((x1 * lax.rsqrt(ms + EPS) * nw_ref[...]) * (1.0 + scale2) + shift2).astype(BF16)
    gt = _dot(hb, wg_ref[...])
    up = _dot(hb, wu_ref[...])
    act = ((gt * _sigmoid(gt)) * up).astype(BF16)
    o_ref[0] = x1 + gate2 * _dot(act, wd_ref[...])


def _ffn_call(x, og, osw, mod3, norm_w, wog, wos, wg, wu, wd):
    b, t, _ = x.shape
    tm = FFN_TM
    row = lambda width: pl.BlockSpec((1, tm, width), lambda bi, ti: (bi, ti, 0))
    const = lambda shape: pl.BlockSpec(shape, lambda bi, ti: (0,) * len(shape), pipeline_mode=pl.Buffered(1))
    return pl.pallas_call(
        _ffn_kernel,
        grid=(b, t // tm),
        in_specs=[
            row(D_MODEL), row(GDN_WIDTH), row(SWA_WIDTH),
            pl.BlockSpec((1, 1, 6 * D_MODEL), lambda bi, ti: (bi, 0, 0)),
            const((1, D_MODEL)),
            const((GDN_WIDTH, D_MODEL)), const((SWA_WIDTH, D_MODEL)),
            const((D_MODEL, D_FF)), const((D_MODEL, D_FF)), const((D_FF, D_MODEL)),
        ],
        out_specs=row(D_MODEL),
        out_shape=jax.ShapeDtypeStruct((b, t, D_MODEL), F32),
        compiler_params=pltpu.CompilerParams(
            dimension_semantics=("arbitrary", "arbitrary"), vmem_limit_bytes=VMEM_LIMIT),
        name="ffn",
    )(x, og, osw, mod3, norm_w, wog, wos, wg, wu, wd)


def _pair_perm():
    cols = []
    for p in range(SWA_PAIRS):
        for head in (p, SWA_GROUP + p):
            cols.extend(range(head * HEAD_DIM, (head + 1) * HEAD_DIM))
    return np.asarray(cols, np.int32)


def _layer(x, mod, norm1_w, w_in, conv_w, a_log, dt_bias, gdn_norm_w, q_norm_w, k_norm_w, sinks, w_out,
           norm2_w, w_gate, w_up, w_down):
    b = x.shape[0]
    perm = _pair_perm()
    o0 = 4 * GDN_WIDTH
    o1 = o0 + 2 * GDN_HEADS
    o2 = o1 + SWA_WIDTH
    w_cat = jnp.concatenate([
        w_in[:, 0:QKV_W],
        w_in[:, QKV_W:o0],
        w_in[:, o1:o2][:, perm],
        w_in[:, o2:],
        jnp.pad(w_in[:, o0:o1], ((0, 0), (0, GATE_PAD - 2 * GDN_HEADS))),
    ], axis=1).astype(BF16)
    mod3 = mod[:b].reshape(b, 1, 6 * D_MODEL)
    qkv, z, sq, skv, ab = _proj_call(x, mod3, norm1_w.reshape(1, D_MODEL), w_cat)

    alog_pad = jnp.pad(a_log.reshape(1, GDN_HEADS), ((0, 0), (0, GATE_PAD - GDN_HEADS)))
    dtb_pad = jnp.pad(dt_bias.reshape(1, GDN_HEADS), ((0, 0), (0, GATE_PAD - GDN_HEADS)))
    gnw_x = jnp.tile(gdn_norm_w.reshape(1, HEAD_DIM), (1, GDN_HEADS))
    og = _gdn_call(qkv, z, ab, conv_w.reshape(CONV_WIDTH, QKV_W), alog_pad, dtb_pad, gnw_x)

    qw_x = jnp.tile(q_norm_w.reshape(1, HEAD_DIM), (1, SWA_Q_HEADS))
    kw_x = jnp.tile(k_norm_w.reshape(1, HEAD_DIM), (1, SWA_KV_HEADS))
    sink_pairs = jnp.stack([sinks[:SWA_GROUP], sinks[SWA_GROUP:]], axis=1)
    sink_col = jnp.repeat(sink_pairs, WINDOW, axis=1).reshape(SWA_PAIRS, 2 * WINDOW, 1)
    osw = _swa_call(sq, skv, qw_x, kw_x, sink_col)

    wog = w_out[:GDN_WIDTH].astype(BF16)
    wos = w_out[GDN_WIDTH:][perm].astype(BF16)
    return _ffn_call(x, og, osw, mod3, norm2_w.reshape(1, D_MODEL), wog, wos,
                     w_gate.astype(BF16), w_up.astype(BF16), w_down.astype(BF16))


def kernel(x, c, w_ada, b_ada, norm1_w, w_in, conv_w, a_log, dt_bias, gdn_norm_w, q_norm_w, k_norm_w, sinks,
           w_out, norm2_w, w_gate, w_up, w_down):
    depth = w_ada.shape[0]
    b = c.shape[0]
    c_pad = jnp.pad(c, ((0, 8 - b), (0, 0)))
    for l in range(depth):
        mod = _ada_call(c_pad, w_ada[l], b_ada[l].reshape(1, -1))
        x = _layer(x, mod, norm1_w[l], w_in[l], conv_w[l], a_log[l], dt_bias[l], gdn_norm_w[l], q_norm_w[l],
                   k_norm_w[l], sinks[l], w_out[l], norm2_w[l], w_gate[l], w_up[l], w_down[l])
    return x
```

```python
import functools

import numpy as np
import jax
import jax.numpy as jnp
from jax import lax
from jax.experimental import pallas as pl
from jax.experimental.pallas import tpu as pltpu

F32 = jnp.float32
BF16 = jnp.bfloat16

D_MODEL = 1024
HEAD_DIM = 64
GDN_HEADS = 8
GDN_WIDTH = GDN_HEADS * HEAD_DIM
SWA_Q_HEADS = 8
SWA_KV_HEADS = 2
SWA_GROUP = SWA_Q_HEADS // SWA_KV_HEADS
SWA_WIDTH = SWA_Q_HEADS * HEAD_DIM
SWA_KV_WIDTH = SWA_KV_HEADS * HEAD_DIM
WINDOW = 128
CONV_WIDTH = 4
CHUNK = 64
D_FF = 2816
EPS = 1e-6
LANES = 128
GATE_PAD = LANES
HEADS_PER_GROUP = 4
GROUP_W = HEADS_PER_GROUP * HEAD_DIM
INV_BASE = 8
INV_LEVELS = (8, 16, 32)
NEG_BIG = -1e30
VMEM_LIMIT = 56 * 1024 * 1024

PROJ_TM = 512
GDN_TT = 256
SWA_TQ = 512
FFN_TM = 256
ADA_TN = 1536


def _sigmoid(x):
    return 1.0 / (1.0 + jnp.exp(-x))


def _dot(a, b):
    return jnp.dot(a, b, preferred_element_type=F32)


def _dot_nt(a, b):
    return lax.dot_general(a, b, (((1,), (1,)), ((), ())), preferred_element_type=F32)


def _dot_tn(a, b):
    return lax.dot_general(a, b, (((0,), (0,)), ((), ())), preferred_element_type=F32)


def _split3(x):
    x1 = x.astype(BF16)
    r1 = x - x1.astype(F32)
    x2 = r1.astype(BF16)
    r2 = r1 - x2.astype(F32)
    x3 = r2.astype(BF16)
    return x1, x2, x3


def _ada_kernel(c_ref, w_ref, b_ref, o_ref):
    c = c_ref[...]
    ca = c * _sigmoid(c)
    o_ref[...] = _dot(ca.astype(BF16), w_ref[...].astype(BF16)) + b_ref[...]


def _ada_call(c_pad, w_ada, b_ada):
    n = w_ada.shape[1]
    return pl.pallas_call(
        _ada_kernel,
        grid=(n // ADA_TN,),
        in_specs=[
            pl.BlockSpec((8, D_MODEL), lambda j: (0, 0)),
            pl.BlockSpec((D_MODEL, ADA_TN), lambda j: (0, j)),
            pl.BlockSpec((1, ADA_TN), lambda j: (0, j)),
        ],
        out_specs=pl.BlockSpec((8, ADA_TN), lambda j: (0, j)),
        out_shape=jax.ShapeDtypeStruct((8, n), F32),
        compiler_params=pltpu.CompilerParams(
            dimension_semantics=("arbitrary",), vmem_limit_bytes=VMEM_LIMIT),
        name="ada",
    )(c_pad, w_ada, b_ada)


QKV_W = 3 * GDN_WIDTH
PROJ_SPLITS = (QKV_W, GDN_WIDTH, SWA_WIDTH, 2 * SWA_KV_WIDTH, GATE_PAD)
PROJ_COLS = sum(PROJ_SPLITS)


def _proj_kernel(x_ref, mod_ref, nw_ref, w_ref, qkv_ref, z_ref, sq_ref, skv_ref, ab_ref):
    x = x_ref[0]
    ms = jnp.mean(x * x, axis=-1, keepdims=True)
    y = x * lax.rsqrt(ms + EPS) * nw_ref[...]
    mod = mod_ref[0]
    shift = mod[:, 0:D_MODEL]
    scale = mod[:, D_MODEL:2 * D_MODEL]
    hb = (y * (1.0 + scale) + shift).astype(BF16)
    col = 0
    for ref, width in zip((qkv_ref, z_ref, sq_ref, skv_ref, ab_ref), PROJ_SPLITS):
        ref[0] = _dot(hb, w_ref[:, col:col + width])
        col += width


def _proj_call(x, mod3, norm_w, w_cat):
    b, t, _ = x.shape
    tm = PROJ_TM
    row = lambda width: pl.BlockSpec((1, tm, width), lambda bi, ti: (bi, ti, 0))
    const = lambda shape: pl.BlockSpec(shape, lambda bi, ti: (0,) * len(shape))
    return pl.pallas_call(
        _proj_kernel,
        grid=(b, t // tm),
        in_specs=[
            row(D_MODEL),
            pl.BlockSpec((1, 1, 6 * D_MODEL), lambda bi, ti: (bi, 0, 0)),
            const((1, D_MODEL)),
            const((D_MODEL, PROJ_COLS)),
        ],
        out_specs=[row(w) for w in PROJ_SPLITS],
        out_shape=[jax.ShapeDtypeStruct((b, t, w), F32) for w in PROJ_SPLITS],
        compiler_params=pltpu.CompilerParams(
            dimension_semantics=("arbitrary", "arbitrary"), vmem_limit_bytes=VMEM_LIMIT),
        name="proj",
    )(x, mod3, norm_w, w_cat)


def _block_diag_rows(p, mask_bd16):
    return jnp.concatenate([p.astype(BF16)] * HEADS_PER_GROUP, axis=0) * mask_bd16


def _gdn_kernel(qkv_ref, z_ref, ab_ref, cw_ref, alog_ref, dtb_ref, gnw_ref, bd_ref, eg_ref, eb_ref,
                ltri_ref, o_ref, xbuf, s_ref, obuf):
    tt = GDN_TT
    ti = pl.program_id(1)

    @pl.when(ti == 0)
    def _():
        xbuf[0:8, :] = jnp.zeros((8, QKV_W), F32)
        s_ref[...] = jnp.zeros_like(s_ref)

    xbuf[8:8 + tt, :] = qkv_ref[0]
    acc = cw_ref[CONV_WIDTH - 1:CONV_WIDTH, :] * xbuf[8:8 + tt, :]
    for j in range(CONV_WIDTH - 1):
        off = 8 - (CONV_WIDTH - 1) + j
        acc = acc + cw_ref[j:j + 1, :] * xbuf[off:off + tt, :]
    xbuf[0:8, :] = xbuf[tt:tt + 8, :]
    y = acc * _sigmoid(acc)

    bd = bd_ref[...]
    q = y[:, 0:GDN_WIDTH]
    k = y[:, GDN_WIDTH:2 * GDN_WIDTH]
    v = y[:, 2 * GDN_WIDTH:]
    q = q * lax.rsqrt(_dot((q * q).astype(BF16), bd) + EPS) * (HEAD_DIM ** -0.5)
    k = k * lax.rsqrt(_dot((k * k).astype(BF16), bd) + EPS)

    ab = ab_ref[0]
    lane = lax.broadcasted_iota(jnp.int32, (1, GATE_PAD), 1)
    xs = ab + dtb_ref[...]
    softplus = jnp.maximum(xs, 0.0) + jnp.log1p(jnp.exp(-jnp.abs(xs)))
    g = jnp.where(lane < GDN_HEADS, -jnp.exp(alog_ref[...]) * softplus, 0.0)
    beta = _sigmoid(ab)
    ltri = ltri_ref[...]
    g1, g2, g3 = _split3(g)
    gcum = _dot(ltri, g1) + _dot(ltri, g2) + _dot(ltri, g3)
    eg = eg_ref[...]
    c1, c2, c3 = _split3(gcum)
    g_x = _dot(c1, eg) + _dot(c2, eg) + _dot(c3, eg)
    beta_x = _dot(beta.astype(BF16), eb_ref[...])

    r64 = lax.broadcasted_iota(jnp.int32, (CHUNK, GROUP_W), 0)
    l64 = lax.broadcasted_iota(jnp.int32, (CHUNK, GROUP_W), 1) % CHUNK
    causal = r64 >= l64
    eye_x = (r64 == l64).astype(F32)
    strict_x = (r64 > l64).astype(F32)
    same_block = lambda size: (r64 // size) == (l64 // size)
    base_x = same_block(INV_BASE).astype(F32)
    level_x = [(same_block(2 * size) & ~same_block(size)).astype(F32) for size in INV_LEVELS]
    rb = lax.broadcasted_iota(jnp.int32, (GROUP_W, GROUP_W), 0) // HEAD_DIM
    cb = lax.broadcasted_iota(jnp.int32, (GROUP_W, GROUP_W), 1) // HEAD_DIM
    mask_bd = (rb == cb).astype(F32)
    mask_bd16 = mask_bd.astype(BF16)

    n_groups = GDN_HEADS // HEADS_PER_GROUP
    n_chunks = tt // CHUNK
    units = [(c, gi) for c in range(n_chunks) for gi in range(n_groups)]
    tile = lambda a: [a[c * CHUNK:(c + 1) * CHUNK, gi * GROUP_W:(gi + 1) * GROUP_W] for c, gi in units]
    each = lambda f, *lists: [f(*args) for args in zip(*lists)]
    bdr = lambda t: _block_diag_rows(t, mask_bd16)
    mm = lambda a, w16: _dot(a.astype(BF16), w16)
    stack = lambda a, b_: jnp.concatenate([a, b_], axis=0)

    kc, qc, vc, bx, gx = tile(k), tile(q), tile(v), tile(beta_x), tile(g_x)
    eg_c = each(jnp.exp, gx)
    glast = each(lambda g_: g_[CHUNK - 1:CHUNK, :], gx)
    kb = each(jnp.multiply, kc, bx)
    vb = each(jnp.multiply, vc, bx)
    wr = each(jnp.multiply, kb, eg_c)
    qd = each(jnp.multiply, qc, eg_c)
    kd = each(lambda k_, g_, gl: k_ * jnp.exp(gl - g_), kc, gx, glast)
    dec_row = each(jnp.exp, glast)
    dm = each(lambda g_: jnp.exp(jnp.where(causal, g_ - jnp.sum(g_ * eye_x, axis=0, keepdims=True), NEG_BIG)), gx)
    aq = each(lambda a, b_, k_: _dot_nt(stack(a, b_).astype(BF16), bdr(k_)), kb, qc, kc)
    xm = each(lambda a, d_: -(a[0:CHUNK] * d_ * strict_x), aq, dm)
    qk = each(lambda a, d_: a[CHUNK:] * d_, aq, dm)

    xd = each(lambda x_: x_ * base_x, xm)
    tinv = each(lambda x_: eye_x + x_, xd)
    pw = each(lambda x_: mm(x_, bdr(x_)), xd)
    r = each(lambda t_, p_: mm(stack(t_, p_), bdr(p_)), tinv, pw)
    tinv = each(lambda t_, r_: t_ + r_[0:CHUNK], tinv, r)
    pw = each(lambda r_: r_[CHUNK:], r)
    tinv = each(lambda t_, p_: t_ + mm(t_, bdr(p_)), tinv, pw)
    for lm in level_x:
        e = each(lambda x_, t_: mm(x_ * lm, bdr(t_)), xm, tinv)
        tinv = each(lambda t_, e_: t_ + mm(t_, bdr(e_)), tinv, e)
    u = each(lambda t_, v_: mm(t_, bdr(v_)), tinv, vb)
    w = each(lambda t_, w_: mm(t_, bdr(w_)), tinv, wr)

    s_state = [s_ref[gi] for gi in range(n_groups)]
    for c in range(n_chunks):
        ids = [c * n_groups + gi for gi in range(n_groups)]
        r2 = [mm(stack(w[i], qd[i]), s_state[gi].astype(BF16)) for gi, i in enumerate(ids)]
        vn = [u[i] - r2[gi][0:CHUNK] for gi, i in enumerate(ids)]
        o = [r2[gi][CHUNK:] + mm(qk[i], bdr(vn[gi])) for gi, i in enumerate(ids)]
        s_state = [s_state[gi] * dec_row[i] + mask_bd * _dot_tn(kd[i].astype(BF16), vn[gi].astype(BF16))
                   for gi, i in enumerate(ids)]
        for gi in range(n_groups):
            obuf[c * CHUNK:(c + 1) * CHUNK, gi * GROUP_W:(gi + 1) * GROUP_W] = o[gi]
    for gi in range(n_groups):
        s_ref[gi] = s_state[gi]

    o = obuf[...]
    ms = _dot((o * o).astype(BF16), bd) * (1.0 / HEAD_DIM)
    zz = z_ref[0]
    o_ref[0] = (o * lax.rsqrt(ms + EPS) * gnw_ref[...] * (zz * _sigmoid(zz))).astype(BF16)


def _gdn_consts():
    h = np.arange(GDN_WIDTH) // HEAD_DIM
    bd = (h[:, None] == h[None, :]).astype(np.float32)
    eg = np.zeros((GATE_PAD, GDN_WIDTH), np.float32)
    eb = np.zeros((GATE_PAD, GDN_WIDTH), np.float32)
    eg[h, np.arange(GDN_WIDTH)] = 1.0
    eb[GDN_HEADS + h, np.arange(GDN_WIDTH)] = 1.0
    t = np.arange(GDN_TT)
    ltri = ((t[:, None] // CHUNK == t[None, :] // CHUNK) & (t[:, None] >= t[None, :])).astype(np.float32)
    return (jnp.asarray(bd, BF16), jnp.asarray(eg, BF16), jnp.asarray(eb, BF16), jnp.asarray(ltri, BF16))


def _gdn_call(qkv, z, ab, conv_w, alog_pad, dtb_pad, gnw_x):
    b, t, _ = qkv.shape
    tt = GDN_TT
    bd, eg, eb, ltri = _gdn_consts()
    row = lambda width: pl.BlockSpec((1, tt, width), lambda bi, ti: (bi, ti, 0))
    const = lambda shape: pl.BlockSpec(shape, lambda bi, ti: (0,) * len(shape))
    return pl.pallas_call(
        _gdn_kernel,
        grid=(b, t // tt),
        in_specs=[
            row(QKV_W), row(GDN_WIDTH), row(GATE_PAD),
            const((CONV_WIDTH, QKV_W)), const((1, GATE_PAD)), const((1, GATE_PAD)), const((1, GDN_WIDTH)),
            const((GDN_WIDTH, GDN_WIDTH)), const((GATE_PAD, GDN_WIDTH)), const((GATE_PAD, GDN_WIDTH)),
            const((tt, tt)),
        ],
        out_specs=row(GDN_WIDTH),
        out_shape=jax.ShapeDtypeStruct((b, t, GDN_WIDTH), BF16),
        scratch_shapes=[
            pltpu.VMEM((8 + tt, QKV_W), F32),
            pltpu.VMEM((GDN_HEADS // HEADS_PER_GROUP, GROUP_W, GROUP_W), F32),
            pltpu.VMEM((tt, GDN_WIDTH), F32),
        ],
        compiler_params=pltpu.CompilerParams(
            dimension_semantics=("arbitrary", "arbitrary"), vmem_limit_bytes=VMEM_LIMIT),
        name="gdn",
    )(qkv, z, ab, conv_w, alog_pad, dtb_pad, gnw_x, bd, eg, eb, ltri)


SWA_PAIRS = SWA_Q_HEADS // 2
SWA_ROWS = SWA_Q_HEADS * WINDOW
LOG2E = 1.4426950408889634


def _swa_kernel(q_ref, kvc_ref, kvp_ref, qw_ref, kw_ref, bdq_ref, bdk_ref, bias_ref, sink_ref, o_ref):
    tq = SWA_TQ
    ti = pl.program_id(1)
    q = q_ref[0]
    q = q * lax.rsqrt(_dot((q * q).astype(BF16), bdq_ref[...]) * (1.0 / HEAD_DIM) + EPS)
    q = q * (qw_ref[...] * (HEAD_DIM ** -0.5 * LOG2E))
    kv = jnp.concatenate([kvp_ref[0], kvc_ref[0]], axis=0)
    k = kv[:, 0:SWA_KV_WIDTH]
    v = kv[:, SWA_KV_WIDTH:]
    k = k * lax.rsqrt(_dot((k * k).astype(BF16), bdk_ref[...]) * (1.0 / HEAD_DIM) + EPS) * kw_ref[...]
    k16 = k.astype(BF16)
    lo = lax.broadcasted_iota(jnp.int32, (1, LANES), 1) < HEAD_DIM
    qcol = lax.broadcasted_iota(jnp.int32, (WINDOW, SWA_ROWS), 1) % WINDOW
    from_prev = lax.broadcasted_iota(jnp.int32, (WINDOW, SWA_ROWS), 0) > qcol
    first = jnp.where(ti == 0, 1, 0)
    sink = sink_ref[...]
    vt16 = v.T.astype(BF16)
    zero = jnp.zeros((), BF16)
    for j in range(tq // WINDOW):
        qj = q[j * WINDOW:(j + 1) * WINDOW]
        parts = []
        for p in range(SWA_PAIRS):
            qp = qj[:, p * LANES:(p + 1) * LANES]
            parts += [jnp.where(lo, qp, 0.0), jnp.where(lo, 0.0, qp)]
        qs = jnp.concatenate(parts, axis=0).astype(BF16)
        st = _dot_nt(k16[j * WINDOW:(j + 2) * WINDOW], qs)
        bias = bias_ref[first] if j == 0 else bias_ref[0]
        sm = jnp.where(from_prev, st[0:WINDOW], st[WINDOW:]) + bias
        m = jnp.max(sm, axis=0, keepdims=True)
        pe = jnp.exp2(sm - m)
        den = jnp.sum(pe, axis=0, keepdims=True) + jnp.exp2(sink - m)
        pb = pe.astype(BF16)
        pt2 = jnp.concatenate([jnp.where(from_prev, pb, zero), jnp.where(from_prev, zero, pb)], axis=0)
        ot = _dot(vt16[:, j * WINDOW:(j + 2) * WINDOW], pt2) * (1.0 / den)
        for p in range(SWA_PAIRS):
            c0 = p * 2 * WINDOW
            pair_t = jnp.concatenate([ot[0:HEAD_DIM, c0:c0 + WINDOW],
                                      ot[HEAD_DIM:, c0 + WINDOW:c0 + 2 * WINDOW]], axis=0)
            o_ref[0, j * WINDOW:(j + 1) * WINDOW, p * LANES:(p + 1) * LANES] = pair_t.T.astype(BF16)


def _swa_consts():
    qi = np.arange(WINDOW)[:, None]
    kj = np.arange(WINDOW)[None, :]
    from_prev = kj > qi
    dist = np.where(from_prev, qi + WINDOW - kj, qi - kj).astype(np.float32)
    slopes = 2.0 ** (-8.0 * (np.arange(SWA_Q_HEADS, dtype=np.float32) + 1.0) / SWA_Q_HEADS)
    bias = np.zeros((2, SWA_PAIRS, 2, WINDOW, WINDOW), np.float32)
    for first in range(2):
        for p in range(SWA_PAIRS):
            for half, head in enumerate((p, SWA_GROUP + p)):
                b = (-slopes[head].astype(np.float32) * dist) * np.float32(LOG2E)
                bias[first, p, half] = np.where(from_prev & bool(first), np.float32(NEG_BIG), b)
    bias = bias.reshape(2, SWA_ROWS, WINDOW).transpose(0, 2, 1)
    h = np.arange(SWA_WIDTH) // HEAD_DIM
    bdq = (h[:, None] == h[None, :]).astype(np.float32)
    hk = np.arange(SWA_KV_WIDTH) // HEAD_DIM
    bdk = (hk[:, None] == hk[None, :]).astype(np.float32)
    return jnp.asarray(bias), jnp.asarray(bdq, BF16), jnp.asarray(bdk, BF16)


def _swa_call(sq, skv, qw_x, kw_x, sink_col):
    b, t, _ = sq.shape
    tq = SWA_TQ
    nb = tq // WINDOW
    bias, bdq, bdk = _swa_consts()
    const = lambda shape: pl.BlockSpec(shape, lambda bi, ti: (0,) * len(shape))
    return pl.pallas_call(
        _swa_kernel,
        grid=(b, t // tq),
        in_specs=[
            pl.BlockSpec((1, tq, SWA_WIDTH), lambda bi, ti: (bi, ti, 0)),
            pl.BlockSpec((1, tq, 2 * SWA_KV_WIDTH), lambda bi, ti: (bi, ti, 0)),
            pl.BlockSpec((1, WINDOW, 2 * SWA_KV_WIDTH), lambda bi, ti: (bi, jnp.maximum(ti * nb - 1, 0), 0)),
            const((1, SWA_WIDTH)), const((1, SWA_KV_WIDTH)),
            const((SWA_WIDTH, SWA_WIDTH)), const((SWA_KV_WIDTH, SWA_KV_WIDTH)),
            const((2, WINDOW, SWA_ROWS)),
            const((1, SWA_ROWS)),
        ],
        out_specs=pl.BlockSpec((1, tq, SWA_WIDTH), lambda bi, ti: (bi, ti, 0)),
        out_shape=jax.ShapeDtypeStruct((b, t, SWA_WIDTH), BF16),
        compiler_params=pltpu.CompilerParams(
            dimension_semantics=("arbitrary", "arbitrary"), vmem_limit_bytes=VMEM_LIMIT),
        name="swa",
    )(sq, skv, skv, qw_x, kw_x, bdq, bdk, bias, sink_col)


def _ffn_kernel(x_ref, og_ref, os_ref, mod_ref, nw_ref, wog_ref, wos_ref, wg_ref, wu_ref, wd_ref, o_ref):
    x = x_ref[0]
    mod = mod_ref[0]
    gate1 = mod[:, 2 * D_MODEL:3 * D_MODEL]
    shift2 = mod[:, 3 * D_MODEL:4 * D_MODEL]
    scale2 = mod[:, 4 * D_MODEL:5 * D_MODEL]
    gate2 = mod[:, 5 * D_MODEL:]
    mixed = _dot(og_ref[0], wog_ref[...]) + _dot(os_ref[0], wos_ref[...])
    x1 = x + gate1 * mixed
    ms = jnp.mean(x1 * x1, axis=-1, keepdims=True)
    hb = ((x1 * lax.rsqrt(ms + EPS) * nw_ref[...]) * (1.0 + scale2) + shift2).astype(BF16)
    gt = _dot(hb, wg_ref[...])
    up = _dot(hb, wu_ref[...])
    act = ((gt * _sigmoid(gt)) * up).astype(BF16)
    o_ref[0] = x1 + gate2 * _dot(act, wd_ref[...])


def _ffn_call(x, og, osw, mod3, norm_w, wog, wos, wg, wu, wd):
    b, t, _ = x.shape
    tm = FFN_TM
    row = lambda width: pl.BlockSpec((1, tm, width), lambda bi, ti: (bi, ti, 0))
    const = lambda shape: pl.BlockSpec(shape, lambda bi, ti: (0,) * len(shape), pipeline_mode=pl.Buffered(1))
    return pl.pallas_call(
        _ffn_kernel,
        grid=(b, t // tm),
        in_specs=[
            row(D_MODEL), row(GDN_WIDTH), row(SWA_WIDTH),
            pl.BlockSpec((1, 1, 6 * D_MODEL), lambda bi, ti: (bi, 0, 0)),
            const((1, D_MODEL)),
            const((GDN_WIDTH, D_MODEL)), const((SWA_WIDTH, D_MODEL)),
            const((D_MODEL, D_FF)), const((D_MODEL, D_FF)), const((D_FF, D_MODEL)),
        ],
        out_specs=row(D_MODEL),
        out_shape=jax.ShapeDtypeStruct((b, t, D_MODEL), F32),
        compiler_params=pltpu.CompilerParams(
            dimension_semantics=("arbitrary", "arbitrary"), vmem_limit_bytes=VMEM_LIMIT),
        name="ffn",
    )(x, og, osw, mod3, norm_w, wog, wos, wg, wu, wd)


def _pair_perm():
    cols = []
    for p in range(SWA_PAIRS):
        for head in (p, SWA_GROUP + p):
            cols.extend(range(head * HEAD_DIM, (head + 1) * HEAD_DIM))
    return np.asarray(cols, np.int32)


def _layer(x, mod, norm1_w, w_in, conv_w, a_log, dt_bias, gdn_norm_w, q_norm_w, k_norm_w, sinks, w_out,
           norm2_w, w_gate, w_up, w_down):
    b = x.shape[0]
    perm = _pair_perm()
    o0 = 4 * GDN_WIDTH
    o1 = o0 + 2 * GDN_HEADS
    o2 = o1 + SWA_WIDTH
    w_cat = jnp.concatenate([
        w_in[:, 0:QKV_W],
        w_in[:, QKV_W:o0],
        w_in[:, o1:o2][:, perm],
        w_in[:, o2:],
        jnp.pad(w_in[:, o0:o1], ((0, 0), (0, GATE_PAD - 2 * GDN_HEADS))),
    ], axis=1).astype(BF16)
    mod3 = mod[:b].reshape(b, 1, 6 * D_MODEL)
    qkv, z, sq, skv, ab = _proj_call(x, mod3, norm1_w.reshape(1, D_MODEL), w_cat)

    alog_pad = jnp.pad(a_log.reshape(1, GDN_HEADS), ((0, 0), (0, GATE_PAD - GDN_HEADS)))
    dtb_pad = jnp.pad(dt_bias.reshape(1, GDN_HEADS), ((0, 0), (0, GATE_PAD - GDN_HEADS)))
    gnw_x = jnp.tile(gdn_norm_w.reshape(1, HEAD_DIM), (1, GDN_HEADS))
    og = _gdn_call(qkv, z, ab, conv_w.reshape(CONV_WIDTH, QKV_W), alog_pad, dtb_pad, gnw_x)

    qw_x = jnp.tile(q_norm_w.reshape(1, HEAD_DIM), (1, SWA_Q_HEADS))
    kw_x = jnp.tile(k_norm_w.reshape(1, HEAD_DIM), (1, SWA_KV_HEADS))
    sink_pairs = jnp.stack([sinks[:SWA_GROUP], sinks[SWA_GROUP:]], axis=1)
    sink_col = (jnp.repeat(sink_pairs, WINDOW, axis=1) * LOG2E).reshape(1, SWA_ROWS)
    osw = _swa_call(sq, skv, qw_x, kw_x, sink_col)

    wog = w_out[:GDN_WIDTH].astype(BF16)
    wos = w_out[GDN_WIDTH:][perm].astype(BF16)
    return _ffn_call(x, og, osw, mod3, norm2_w.reshape(1, D_MODEL), wog, wos,
                     w_gate.astype(BF16), w_up.astype(BF16), w_down.astype(BF16))


def kernel(x, c, w_ada, b_ada, norm1_w, w_in, conv_w, a_log, dt_bias, gdn_norm_w, q_norm_w, k_norm_w, sinks,
           w_out, norm2_w, w_gate, w_up, w_down):
    depth = w_ada.shape[0]
    b = c.shape[0]
    c_pad = jnp.pad(c, ((0, 8 - b), (0, 0)))
    for l in range(depth):
        mod = _ada_call(c_pad, w_ada[l], b_ada[l].reshape(1, -1))
        x = _layer(x, mod, norm1_w[l], w_in[l], conv_w[l], a_log[l], dt_bias[l], gdn_norm_w[l], q_norm_w[l],
                   k_norm_w[l], sinks[l], w_out[l], norm2_w[l], w_gate[l], w_up[l], w_down[l])
    return x
```

```python
import functools

import numpy as np
import jax
import jax.numpy as jnp
from jax import lax
from jax.experimental import pallas as pl
from jax.experimental.pallas import tpu as pltpu

F32 = jnp.float32
BF16 = jnp.bfloat16

D_MODEL = 1024
HEAD_DIM = 64
GDN_HEADS = 8
GDN_WIDTH = GDN_HEADS * HEAD_DIM
SWA_Q_HEADS = 8
SWA_KV_HEADS = 2
SWA_GROUP = SWA_Q_HEADS // SWA_KV_HEADS
SWA_WIDTH = SWA_Q_HEADS * HEAD_DIM
SWA_KV_WIDTH = SWA_KV_HEADS * HEAD_DIM
WINDOW = 128
CONV_WIDTH = 4
CHUNK = 64
D_FF = 2816
EPS = 1e-6
LANES = 128
GATE_PAD = LANES
HEADS_PER_GROUP = 2
GROUP_W = HEADS_PER_GROUP * HEAD_DIM
INV_BASE = 8
INV_LEVELS = (8, 16, 32)
NEG_BIG = -1e30
VMEM_LIMIT = 56 * 1024 * 1024

PROJ_TM = 512
GDN_TT = 256
SWA_TQ = 512
FFN_TM = 256
ADA_TN = 1536


def _sigmoid(x):
    return 1.0 / (1.0 + jnp.exp(-x))


def _dot(a, b):
    return jnp.dot(a, b, preferred_element_type=F32)


def _dot_nt(a, b):
    return lax.dot_general(a, b, (((1,), (1,)), ((), ())), preferred_element_type=F32)


def _dot_tn(a, b):
    return lax.dot_general(a, b, (((0,), (0,)), ((), ())), preferred_element_type=F32)


def _split3(x):
    x1 = x.astype(BF16)
    r1 = x - x1.astype(F32)
    x2 = r1.astype(BF16)
    r2 = r1 - x2.astype(F32)
    x3 = r2.astype(BF16)
    return x1, x2, x3


def _ada_kernel(c_ref, w_ref, b_ref, o_ref):
    c = c_ref[...]
    ca = c * _sigmoid(c)
    o_ref[...] = _dot(ca.astype(BF16), w_ref[...].astype(BF16)) + b_ref[...]


def _ada_call(c_pad, w_ada, b_ada):
    n = w_ada.shape[1]
    return pl.pallas_call(
        _ada_kernel,
        grid=(n // ADA_TN,),
        in_specs=[
            pl.BlockSpec((8, D_MODEL), lambda j: (0, 0)),
            pl.BlockSpec((D_MODEL, ADA_TN), lambda j: (0, j)),
            pl.BlockSpec((1, ADA_TN), lambda j: (0, j)),
        ],
        out_specs=pl.BlockSpec((8, ADA_TN), lambda j: (0, j)),
        out_shape=jax.ShapeDtypeStruct((8, n), F32),
        compiler_params=pltpu.CompilerParams(
            dimension_semantics=("arbitrary",), vmem_limit_bytes=VMEM_LIMIT),
        name="ada",
    )(c_pad, w_ada, b_ada)


QKV_W = 3 * GDN_WIDTH
PROJ_SPLITS = (QKV_W, GDN_WIDTH, SWA_WIDTH, 2 * SWA_KV_WIDTH, GATE_PAD)
PROJ_COLS = sum(PROJ_SPLITS)


def _proj_kernel(x_ref, mod_ref, nw_ref, w_ref, cw_ref, bd_ref, qkv_ref, z_ref, sq_ref, skv_ref, ab_ref, xbuf):
    tm = PROJ_TM

    @pl.when(pl.program_id(1) == 0)
    def _():
        xbuf[0:8, :] = jnp.zeros((8, QKV_W), F32)

    x = x_ref[0]
    ms = jnp.mean(x * x, axis=-1, keepdims=True)
    y = x * lax.rsqrt(ms + EPS) * nw_ref[...]
    mod = mod_ref[0]
    shift = mod[:, 0:D_MODEL]
    scale = mod[:, D_MODEL:2 * D_MODEL]
    hb = (y * (1.0 + scale) + shift).astype(BF16)
    col = QKV_W
    for ref, width in zip((z_ref, sq_ref, skv_ref, ab_ref), PROJ_SPLITS[1:]):
        ref[0] = _dot(hb, w_ref[:, col:col + width])
        col += width

    xbuf[8:8 + tm, :] = _dot(hb, w_ref[:, 0:QKV_W])
    acc = cw_ref[CONV_WIDTH - 1:CONV_WIDTH, :] * xbuf[8:8 + tm, :]
    for j in range(CONV_WIDTH - 1):
        off = 8 - (CONV_WIDTH - 1) + j
        acc = acc + cw_ref[j:j + 1, :] * xbuf[off:off + tm, :]
    xbuf[0:8, :] = xbuf[tm:tm + 8, :]
    a = acc * _sigmoid(acc)
    bd = bd_ref[...]
    q = a[:, 0:GDN_WIDTH]
    k = a[:, GDN_WIDTH:2 * GDN_WIDTH]
    qkv_ref[0, :, 0:GDN_WIDTH] = q * lax.rsqrt(_dot((q * q).astype(BF16), bd) + EPS) * (HEAD_DIM ** -0.5)
    qkv_ref[0, :, GDN_WIDTH:2 * GDN_WIDTH] = k * lax.rsqrt(_dot((k * k).astype(BF16), bd) + EPS)
    qkv_ref[0, :, 2 * GDN_WIDTH:] = a[:, 2 * GDN_WIDTH:]


def _head_sum_matrix(width):
    h = np.arange(width) // HEAD_DIM
    return jnp.asarray((h[:, None] == h[None, :]).astype(np.float32), BF16)


def _proj_call(x, mod3, norm_w, w_cat, conv_w):
    b, t, _ = x.shape
    tm = PROJ_TM
    row = lambda width: pl.BlockSpec((1, tm, width), lambda bi, ti: (bi, ti, 0))
    const = lambda shape: pl.BlockSpec(shape, lambda bi, ti: (0,) * len(shape))
    return pl.pallas_call(
        _proj_kernel,
        grid=(b, t // tm),
        in_specs=[
            row(D_MODEL),
            pl.BlockSpec((1, 1, 6 * D_MODEL), lambda bi, ti: (bi, 0, 0)),
            const((1, D_MODEL)),
            const((D_MODEL, PROJ_COLS)),
            const((CONV_WIDTH, QKV_W)),
            const((GDN_WIDTH, GDN_WIDTH)),
        ],
        out_specs=[row(w) for w in PROJ_SPLITS],
        out_shape=[jax.ShapeDtypeStruct((b, t, w), F32) for w in PROJ_SPLITS],
        scratch_shapes=[pltpu.VMEM((8 + tm, QKV_W), F32)],
        compiler_params=pltpu.CompilerParams(
            dimension_semantics=("arbitrary", "arbitrary"), vmem_limit_bytes=VMEM_LIMIT),
        name="proj",
    )(x, mod3, norm_w, w_cat, conv_w, _head_sum_matrix(GDN_WIDTH))


GDN_UNITS = (GDN_TT // CHUNK) * (GDN_HEADS // HEADS_PER_GROUP)
BD_SLOTS = 4


def _block_diag_rows(buf, slot, p):
    p16 = p.astype(BF16)
    for h in range(HEADS_PER_GROUP):
        buf[slot, h * HEAD_DIM:(h + 1) * HEAD_DIM, h * HEAD_DIM:(h + 1) * HEAD_DIM] = (
            p16[:, h * HEAD_DIM:(h + 1) * HEAD_DIM])
    return buf[slot]


_DONE = object()


def _gdn_kernel(qkv_ref, z_ref, ab_ref, alog_ref, dtb_ref, gnw_ref, bd_ref, eg_ref, eb_ref,
                ltri_ref, o_ref, s_ref, obuf, bdbuf, wq_buf, u_buf, qk_buf, kd_buf, dec_buf):
    tt = GDN_TT
    ti = pl.program_id(1)
    wslot = ti % 2
    rslot = 1 - wslot

    @pl.when(ti == 0)
    def _():
        s_ref[...] = jnp.zeros_like(s_ref)

    @pl.when((ti == 0) & (pl.program_id(0) == 0))
    def _():
        bdbuf[...] = jnp.zeros_like(bdbuf)
        for buf in (wq_buf, u_buf, qk_buf, kd_buf, dec_buf):
            buf[...] = jnp.zeros_like(buf)

    bd = bd_ref[...]
    q = qkv_ref[0, :, 0:GDN_WIDTH]
    k = qkv_ref[0, :, GDN_WIDTH:2 * GDN_WIDTH]
    v = qkv_ref[0, :, 2 * GDN_WIDTH:]

    ab = ab_ref[0]
    lane = lax.broadcasted_iota(jnp.int32, (1, GATE_PAD), 1)
    xs = ab + dtb_ref[...]
    softplus = jnp.maximum(xs, 0.0) + jnp.log1p(jnp.exp(-jnp.abs(xs)))
    g = jnp.where(lane < GDN_HEADS, -jnp.exp(alog_ref[...]) * softplus, 0.0)
    beta = _sigmoid(ab)
    ltri = ltri_ref[...]
    g1, g2, g3 = _split3(g)
    gcum = _dot(ltri, g1) + _dot(ltri, g2) + _dot(ltri, g3)
    eg = eg_ref[...]
    c1, c2, c3 = _split3(gcum)
    g_x = _dot(c1, eg) + _dot(c2, eg) + _dot(c3, eg)
    beta_x = _dot(beta.astype(BF16), eb_ref[...])

    r64 = lax.broadcasted_iota(jnp.int32, (CHUNK, GROUP_W), 0)
    l64 = lax.broadcasted_iota(jnp.int32, (CHUNK, GROUP_W), 1) % CHUNK
    causal = r64 >= l64
    eye_x = (r64 == l64).astype(F32)
    strict_x = (r64 > l64).astype(F32)
    same_block = lambda size: (r64 // size) == (l64 // size)
    base_x = same_block(INV_BASE).astype(F32)
    level_x = [(same_block(2 * size) & ~same_block(size)).astype(F32) for size in INV_LEVELS]
    rb = lax.broadcasted_iota(jnp.int32, (GROUP_W, GROUP_W), 0) // HEAD_DIM
    cb = lax.broadcasted_iota(jnp.int32, (GROUP_W, GROUP_W), 1) // HEAD_DIM
    mask_bd = (rb == cb).astype(F32)

    n_groups = GDN_HEADS // HEADS_PER_GROUP
    n_chunks = tt // CHUNK
    units = [(c, gi) for c in range(n_chunks) for gi in range(n_groups)]
    ids = list(range(len(units)))
    tile = lambda a: [a[c * CHUNK:(c + 1) * CHUNK, gi * GROUP_W:(gi + 1) * GROUP_W] for c, gi in units]
    each = lambda f, *lists: [f(*args) for args in zip(*lists)]
    bdr = lambda i, slot, t: _block_diag_rows(bdbuf, i * BD_SLOTS + slot, t)
    mm = lambda a, w16: _dot(a.astype(BF16), w16)
    stack = lambda a, b_: jnp.concatenate([a, b_], axis=0)

    def chunk_parallel_part():
        kc, qc, vc, bx, gx = tile(k), tile(q), tile(v), tile(beta_x), tile(g_x)
        eg_c = each(jnp.exp, gx)
        glast = each(lambda g_: g_[CHUNK - 1:CHUNK, :], gx)
        kb = each(jnp.multiply, kc, bx)
        vb = each(jnp.multiply, vc, bx)
        wr = each(jnp.multiply, kb, eg_c)
        qd = each(jnp.multiply, qc, eg_c)
        for i in ids:
            kd_buf[wslot, i] = (kc[i] * jnp.exp(glast[i] - gx[i])).astype(BF16)
            dec_buf[wslot, i] = jnp.broadcast_to(jnp.exp(glast[i]), (8, GROUP_W))
        dm = each(lambda g_: jnp.exp(jnp.where(causal, g_ - jnp.sum(g_ * eye_x, axis=0, keepdims=True), NEG_BIG)),
                  gx)
        yield
        aq = each(lambda i, a, b_, k_: _dot_nt(stack(a, b_).astype(BF16), bdr(i, 1, k_)), ids, kb, qc, kc)
        xm = each(lambda a, d_: -(a[0:CHUNK] * d_ * strict_x), aq, dm)
        for i in ids:
            qk_buf[wslot, i] = (aq[i][CHUNK:] * dm[i]).astype(BF16)
        xd = each(lambda x_: x_ * base_x, xm)
        tinv = each(lambda x_: eye_x + x_, xd)
        yield
        pw = each(lambda i, x_: mm(x_, bdr(i, 0, x_)), ids, xd)
        yield
        r = each(lambda i, t_, p_: mm(stack(t_, p_), bdr(i, 0, p_)), ids, tinv, pw)
        tinv = each(lambda t_, r_: t_ + r_[0:CHUNK], tinv, r)
        pw = each(lambda r_: r_[CHUNK:], r)
        yield
        tinv = each(lambda i, t_, p_: t_ + mm(t_, bdr(i, 0, p_)), ids, tinv, pw)
        yield
        for lm in level_x:
            e = each(lambda i, x_, t_: mm(x_ * lm, bdr(i, 0, t_)), ids, xm, tinv)
            yield
            tinv = each(lambda i, t_, e_: t_ + mm(t_, bdr(i, 0, e_)), ids, tinv, e)
            yield
        for i in ids:
            u_buf[wslot, i] = mm(tinv[i], bdr(i, 2, vb[i]))
            wq_buf[wslot, i, 0:CHUNK] = mm(tinv[i], bdr(i, 3, wr[i])).astype(BF16)
            wq_buf[wslot, i, CHUNK:] = qd[i].astype(BF16)

    def sequential_part():
        s_state = [s_ref[gi] for gi in range(n_groups)]
        for c in range(n_chunks):
            cids = [c * n_groups + gi for gi in range(n_groups)]
            r2 = [_dot(wq_buf[rslot, i], s_state[gi].astype(BF16)) for gi, i in enumerate(cids)]
            yield
            vn = [u_buf[rslot, i] - r2[gi][0:CHUNK] for gi, i in enumerate(cids)]
            o = [r2[gi][CHUNK:] + _dot(qk_buf[rslot, i], bdr(GDN_UNITS, gi, vn[gi])) for gi, i in enumerate(cids)]
            s_state = [s_state[gi] * dec_buf[rslot, i][0:1] + mask_bd * _dot_tn(kd_buf[rslot, i], vn[gi].astype(BF16))
                       for gi, i in enumerate(cids)]
            for gi in range(n_groups):
                obuf[c * CHUNK:(c + 1) * CHUNK, gi * GROUP_W:(gi + 1) * GROUP_W] = o[gi]
            yield
        for gi in range(n_groups):
            s_ref[gi] = jnp.where(ti > 0, s_state[gi], 0.0)

    parts = [sequential_part(), chunk_parallel_part()]
    while parts:
        parts = [p for p in parts if next(p, _DONE) is not _DONE]

    o = obuf[...]
    ms = _dot((o * o).astype(BF16), bd) * (1.0 / HEAD_DIM)
    zz = z_ref[0]
    o_ref[0] = (o * lax.rsqrt(ms + EPS) * gnw_ref[...] * (zz * _sigmoid(zz))).astype(BF16)


def _gdn_consts():
    h = np.arange(GDN_WIDTH) // HEAD_DIM
    eg = np.zeros((GATE_PAD, GDN_WIDTH), np.float32)
    eb = np.zeros((GATE_PAD, GDN_WIDTH), np.float32)
    eg[h, np.arange(GDN_WIDTH)] = 1.0
    eb[GDN_HEADS + h, np.arange(GDN_WIDTH)] = 1.0
    t = np.arange(GDN_TT)
    ltri = ((t[:, None] // CHUNK == t[None, :] // CHUNK) & (t[:, None] >= t[None, :])).astype(np.float32)
    return (_head_sum_matrix(GDN_WIDTH), jnp.asarray(eg, BF16), jnp.asarray(eb, BF16), jnp.asarray(ltri, BF16))


def _gdn_call(qkv, z, ab, alog_pad, dtb_pad, gnw_x):
    b, t, _ = qkv.shape
    tt = GDN_TT
    bd, eg, eb, ltri = _gdn_consts()
    n_tiles = t // tt
    n_groups = GDN_HEADS // HEADS_PER_GROUP
    ahead = lambda width: pl.BlockSpec((1, tt, width), lambda bi, ti: (bi, jnp.minimum(ti, n_tiles - 1), 0))
    behind = lambda width: pl.BlockSpec((1, tt, width), lambda bi, ti: (bi, jnp.maximum(ti - 1, 0), 0))
    const = lambda shape: pl.BlockSpec(shape, lambda bi, ti: (0,) * len(shape))
    per_unit = lambda rows, dtype: pltpu.VMEM((2, GDN_UNITS, rows, GROUP_W), dtype)
    return pl.pallas_call(
        _gdn_kernel,
        grid=(b, n_tiles + 1),
        in_specs=[
            ahead(QKV_W), behind(GDN_WIDTH), ahead(GATE_PAD),
            const((1, GATE_PAD)), const((1, GATE_PAD)), const((1, GDN_WIDTH)),
            const((GDN_WIDTH, GDN_WIDTH)), const((GATE_PAD, GDN_WIDTH)), const((GATE_PAD, GDN_WIDTH)),
            const((tt, tt)),
        ],
        out_specs=behind(GDN_WIDTH),
        out_shape=jax.ShapeDtypeStruct((b, t, GDN_WIDTH), BF16),
        scratch_shapes=[
            pltpu.VMEM((n_groups, GROUP_W, GROUP_W), F32),
            pltpu.VMEM((tt, GDN_WIDTH), F32),
            pltpu.VMEM((GDN_UNITS * BD_SLOTS + n_groups, GROUP_W, GROUP_W), BF16),
            per_unit(2 * CHUNK, BF16), per_unit(CHUNK, F32), per_unit(CHUNK, BF16), per_unit(CHUNK, BF16),
            per_unit(8, F32),
        ],
        compiler_params=pltpu.CompilerParams(
            dimension_semantics=("arbitrary", "arbitrary"), vmem_limit_bytes=VMEM_LIMIT),
        name="gdn",
    )(qkv, z, ab, alog_pad, dtb_pad, gnw_x, bd, eg, eb, ltri)


SWA_PAIRS = SWA_Q_HEADS // 2
SWA_ROWS = SWA_Q_HEADS * WINDOW
LOG2E = 1.4426950408889634


def _swa_kernel(q_ref, kvc_ref, kvp_ref, qw_ref, kw_ref, bdq_ref, bdk_ref, bias_ref, sink_ref, o_ref):
    tq = SWA_TQ
    ti = pl.program_id(1)
    q = q_ref[0]
    q = q * lax.rsqrt(_dot((q * q).astype(BF16), bdq_ref[...]) * (1.0 / HEAD_DIM) + EPS)
    q = q * (qw_ref[...] * (HEAD_DIM ** -0.5 * LOG2E))
    kv = jnp.concatenate([kvp_ref[0], kvc_ref[0]], axis=0)
    k = kv[:, 0:SWA_KV_WIDTH]
    v = kv[:, SWA_KV_WIDTH:]
    k = k * lax.rsqrt(_dot((k * k).astype(BF16), bdk_ref[...]) * (1.0 / HEAD_DIM) + EPS) * kw_ref[...]
    k16 = k.astype(BF16)
    lo = lax.broadcasted_iota(jnp.int32, (1, LANES), 1) < HEAD_DIM
    qcol = lax.broadcasted_iota(jnp.int32, (WINDOW, SWA_ROWS), 1) % WINDOW
    from_prev = lax.broadcasted_iota(jnp.int32, (WINDOW, SWA_ROWS), 0) > qcol
    first = jnp.where(ti == 0, 1, 0)
    sink = sink_ref[...]
    vt16 = v.T.astype(BF16)
    zero = jnp.zeros((), BF16)
    for j in range(tq // WINDOW):
        qj = q[j * WINDOW:(j + 1) * WINDOW]
        parts = []
        for p in range(SWA_PAIRS):
            qp = qj[:, p * LANES:(p + 1) * LANES]
            parts += [jnp.where(lo, qp, 0.0), jnp.where(lo, 0.0, qp)]
        qs = jnp.concatenate(parts, axis=0).astype(BF16)
        st = _dot_nt(k16[j * WINDOW:(j + 2) * WINDOW], qs)
        bias = bias_ref[first] if j == 0 else bias_ref[0]
        sm = jnp.where(from_prev, st[0:WINDOW], st[WINDOW:]) + bias
        m = jnp.max(sm, axis=0, keepdims=True)
        pe = jnp.exp2(sm - m)
        den = jnp.sum(pe, axis=0, keepdims=True) + jnp.exp2(sink - m)
        pb = pe.astype(BF16)
        pt2 = jnp.concatenate([jnp.where(from_prev, pb, zero), jnp.where(from_prev, zero, pb)], axis=0)
        ot = _dot(vt16[:, j * WINDOW:(j + 2) * WINDOW], pt2) * (1.0 / den)
        for p in range(SWA_PAIRS):
            c0 = p * 2 * WINDOW
            pair_t = jnp.concatenate([ot[0:HEAD_DIM, c0:c0 + WINDOW],
                                      ot[HEAD_DIM:, c0 + WINDOW:c0 + 2 * WINDOW]], axis=0)
            o_ref[0, j * WINDOW:(j + 1) * WINDOW, p * LANES:(p + 1) * LANES] = pair_t.T.astype(BF16)


def _swa_consts():
    qi = np.arange(WINDOW)[:, None]
    kj = np.arange(WINDOW)[None, :]
    from_prev = kj > qi
    dist = np.where(from_prev, qi + WINDOW - kj, qi - kj).astype(np.float32)
    slopes = 2.0 ** (-8.0 * (np.arange(SWA_Q_HEADS, dtype=np.float32) + 1.0) / SWA_Q_HEADS)
    bias = np.zeros((2, SWA_PAIRS, 2, WINDOW, WINDOW), np.float32)
    for first in range(2):
        for p in range(SWA_PAIRS):
            for half, head in enumerate((p, SWA_GROUP + p)):
                b = (-slopes[head].astype(np.float32) * dist) * np.float32(LOG2E)
                bias[first, p, half] = np.where(from_prev & bool(first), np.float32(NEG_BIG), b)
    bias = bias.reshape(2, SWA_ROWS, WINDOW).transpose(0, 2, 1)
    h = np.arange(SWA_WIDTH) // HEAD_DIM
    bdq = (h[:, None] == h[None, :]).astype(np.float32)
    hk = np.arange(SWA_KV_WIDTH) // HEAD_DIM
    bdk = (hk[:, None] == hk[None, :]).astype(np.float32)
    return jnp.asarray(bias), jnp.asarray(bdq, BF16), jnp.asarray(bdk, BF16)


def _swa_call(sq, skv, qw_x, kw_x, sink_col):
    b, t, _ = sq.shape
    tq = SWA_TQ
    nb = tq // WINDOW
    bias, bdq, bdk = _swa_consts()
    const = lambda shape: pl.BlockSpec(shape, lambda bi, ti: (0,) * len(shape))
    return pl.pallas_call(
        _swa_kernel,
        grid=(b, t // tq),
        in_specs=[
            pl.BlockSpec((1, tq, SWA_WIDTH), lambda bi, ti: (bi, ti, 0)),
            pl.BlockSpec((1, tq, 2 * SWA_KV_WIDTH), lambda bi, ti: (bi, ti, 0)),
            pl.BlockSpec((1, WINDOW, 2 * SWA_KV_WIDTH), lambda bi, ti: (bi, jnp.maximum(ti * nb - 1, 0), 0)),
            const((1, SWA_WIDTH)), const((1, SWA_KV_WIDTH)),
            const((SWA_WIDTH, SWA_WIDTH)), const((SWA_KV_WIDTH, SWA_KV_WIDTH)),
            const((2, WINDOW, SWA_ROWS)),
            const((1, SWA_ROWS)),
        ],
        out_specs=pl.BlockSpec((1, tq, SWA_WIDTH), lambda bi, ti: (bi, ti, 0)),
        out_shape=jax.ShapeDtypeStruct((b, t, SWA_WIDTH), BF16),
        compiler_params=pltpu.CompilerParams(
            dimension_semantics=("arbitrary", "arbitrary"), vmem_limit_bytes=VMEM_LIMIT),
        name="swa",
    )(sq, skv, skv, qw_x, kw_x, bdq, bdk, bias, sink_col)


def _ffn_kernel(x_ref, og_ref, os_ref, mod_ref, nw_ref, wog_ref, wos_ref, wg_ref, wu_ref, wd_ref, o_ref):
    x = x_ref[0]
    mod = mod_ref[0]
    gate1 = mod[:, 2 * D_MODEL:3 * D_MODEL]
    shift2 = mod[:, 3 * D_MODEL:4 * D_MODEL]
    scale2 = mod[:, 4 * D_MODEL:5 * D_MODEL]
    gate2 = mod[:, 5 * D_MODEL:]
    mixed = _dot(og_ref[0], wog_ref[...]) + _dot(os_ref[0], wos_ref[...])
    x1 = x + gate1 * mixed
    ms = jnp.mean(x1 * x1, axis=-1, keepdims=True)
    hb = ((x1 * lax.rsqrt(ms + EPS) * nw_ref[...]) * (1.0 + scale2) + shift2).astype(BF16)
    gt = _dot(hb, wg_ref[...])
    up = _dot(hb, wu_ref[...])
    act = ((gt * _sigmoid(gt)) * up).astype(BF16)
    o_ref[0] = x1 + gate2 * _dot(act, wd_ref[...])


def _ffn_call(x, og, osw, mod3, norm_w, wog, wos, wg, wu, wd):
    b, t, _ = x.shape
    tm = FFN_TM
    row = lambda width: pl.BlockSpec((1, tm, width), lambda bi, ti: (bi, ti, 0))
    const = lambda shape: pl.BlockSpec(shape, lambda bi, ti: (0,) * len(shape), pipeline_mode=pl.Buffered(1))
    return pl.pallas_call(
        _ffn_kernel,
        grid=(b, t // tm),
        in_specs=[
            row(D_MODEL), row(GDN_WIDTH), row(SWA_WIDTH),
            pl.BlockSpec((1, 1, 6 * D_MODEL), lambda bi, ti: (bi, 0, 0)),
            const((1, D_MODEL)),
            const((GDN_WIDTH, D_MODEL)), const((SWA_WIDTH, D_MODEL)),
            const((D_MODEL, D_FF)), const((D_MODEL, D_FF)), const((D_FF, D_MODEL)),
        ],
        out_specs=row(D_MODEL),
        out_shape=jax.ShapeDtypeStruct((b, t, D_MODEL), F32),
        compiler_params=pltpu.CompilerParams(
            dimension_semantics=("arbitrary", "arbitrary"), vmem_limit_bytes=VMEM_LIMIT),
        name="ffn",
    )(x, og, osw, mod3, norm_w, wog, wos, wg, wu, wd)


def _pair_perm():
    cols = []
    for p in range(SWA_PAIRS):
        for head in (p, SWA_GROUP + p):
            cols.extend(range(head * HEAD_DIM, (head + 1) * HEAD_DIM))
    return np.asarray(cols, np.int32)


def _layer(x, mod, norm1_w, w_in, conv_w, a_log, dt_bias, gdn_norm_w, q_norm_w, k_norm_w, sinks, w_out,
           norm2_w, w_gate, w_up, w_down):
    b = x.shape[0]
    perm = _pair_perm()
    o0 = 4 * GDN_WIDTH
    o1 = o0 + 2 * GDN_HEADS
    o2 = o1 + SWA_WIDTH
    w_cat = jnp.concatenate([
        w_in[:, 0:QKV_W],
        w_in[:, QKV_W:o0],
        w_in[:, o1:o2][:, perm],
        w_in[:, o2:],
        jnp.pad(w_in[:, o0:o1], ((0, 0), (0, GATE_PAD - 2 * GDN_HEADS))),
    ], axis=1).astype(BF16)
    mod3 = mod[:b].reshape(b, 1, 6 * D_MODEL)
    qkv, z, sq, skv, ab = _proj_call(x, mod3, norm1_w.reshape(1, D_MODEL), w_cat,
                                     conv_w.reshape(CONV_WIDTH, QKV_W))

    alog_pad = jnp.pad(a_log.reshape(1, GDN_HEADS), ((0, 0), (0, GATE_PAD - GDN_HEADS)))
    dtb_pad = jnp.pad(dt_bias.reshape(1, GDN_HEADS), ((0, 0), (0, GATE_PAD - GDN_HEADS)))
    gnw_x = jnp.tile(gdn_norm_w.reshape(1, HEAD_DIM), (1, GDN_HEADS))
    og = _gdn_call(qkv, z, ab, alog_pad, dtb_pad, gnw_x)

    qw_x = jnp.tile(q_norm_w.reshape(1, HEAD_DIM), (1, SWA_Q_HEADS))
    kw_x = jnp.tile(k_norm_w.reshape(1, HEAD_DIM), (1, SWA_KV_HEADS))
    sink_pairs = jnp.stack([sinks[:SWA_GROUP], sinks[SWA_GROUP:]], axis=1)
    sink_col = (jnp.repeat(sink_pairs, WINDOW, axis=1) * LOG2E).reshape(1, SWA_ROWS)
    osw = _swa_call(sq, skv, qw_x, kw_x, sink_col)

    wog = w_out[:GDN_WIDTH].astype(BF16)
    wos = w_out[GDN_WIDTH:][perm].astype(BF16)
    return _ffn_call(x, og, osw, mod3, norm2_w.reshape(1, D_MODEL), wog, wos,
                     w_gate.astype(BF16), w_up.astype(BF16), w_down.astype(BF16))


def kernel(x, c, w_ada, b_ada, norm1_w, w_in, conv_w, a_log, dt_bias, gdn_norm_w, q_norm_w, k_norm_w, sinks,
           w_out, norm2_w, w_gate, w_up, w_down):
    depth = w_ada.shape[0]
    b = c.shape[0]
    c_pad = jnp.pad(c, ((0, 8 - b), (0, 0)))
    for l in range(depth):
        mod = _ada_call(c_pad, w_ada[l], b_ada[l].reshape(1, -1))
        x = _layer(x, mod, norm1_w[l], w_in[l], conv_w[l], a_log[l], dt_bias[l], gdn_norm_w[l], q_norm_w[l],
                   k_norm_w[l], sinks[l], w_out[l], norm2_w[l], w_gate[l], w_up[l], w_down[l])
    return x
```

```python
import functools

import numpy as np
import jax
import jax.numpy as jnp
from jax import lax
from jax.experimental import pallas as pl
from jax.experimental.pallas import tpu as pltpu

F32 = jnp.float32
BF16 = jnp.bfloat16

D_MODEL = 1024
HEAD_DIM = 64
GDN_HEADS = 8
GDN_WIDTH = GDN_HEADS * HEAD_DIM
SWA_Q_HEADS = 8
SWA_KV_HEADS = 2
SWA_GROUP = SWA_Q_HEADS // SWA_KV_HEADS
SWA_WIDTH = SWA_Q_HEADS * HEAD_DIM
SWA_KV_WIDTH = SWA_KV_HEADS * HEAD_DIM
WINDOW = 128
CONV_WIDTH = 4
CHUNK = 64
D_FF = 2816
EPS = 1e-6
LANES = 128
GATE_PAD = LANES
HEADS_PER_GROUP = 2
GROUP_W = HEADS_PER_GROUP * HEAD_DIM
INV_BASE = 8
INV_LEVELS = (8, 16, 32)
NEG_BIG = -1e30
VMEM_LIMIT = 56 * 1024 * 1024

PROJ_TM = 512
GDN_TT = 256
SWA_TQ = 512
FFN_TM = 512
ADA_TN = 1536


def _sigmoid(x):
    return 1.0 / (1.0 + jnp.exp(-x))


def _dot(a, b):
    return jnp.dot(a, b, preferred_element_type=F32)


def _dot_nt(a, b):
    return lax.dot_general(a, b, (((1,), (1,)), ((), ())), preferred_element_type=F32)


def _dot_tn(a, b):
    return lax.dot_general(a, b, (((0,), (0,)), ((), ())), preferred_element_type=F32)


def _split3(x):
    x1 = x.astype(BF16)
    r1 = x - x1.astype(F32)
    x2 = r1.astype(BF16)
    r2 = r1 - x2.astype(F32)
    x3 = r2.astype(BF16)
    return x1, x2, x3


def _ada_kernel(c_ref, w_ref, b_ref, o_ref):
    c = c_ref[...]
    ca = c * _sigmoid(c)
    o_ref[...] = _dot(ca.astype(BF16), w_ref[...].astype(BF16)) + b_ref[...]


def _ada_call(c_pad, w_ada, b_ada):
    n = w_ada.shape[1]
    return pl.pallas_call(
        _ada_kernel,
        grid=(n // ADA_TN,),
        in_specs=[
            pl.BlockSpec((8, D_MODEL), lambda j: (0, 0)),
            pl.BlockSpec((D_MODEL, ADA_TN), lambda j: (0, j)),
            pl.BlockSpec((1, ADA_TN), lambda j: (0, j)),
        ],
        out_specs=pl.BlockSpec((8, ADA_TN), lambda j: (0, j)),
        out_shape=jax.ShapeDtypeStruct((8, n), F32),
        compiler_params=pltpu.CompilerParams(
            dimension_semantics=("arbitrary",), vmem_limit_bytes=VMEM_LIMIT),
        name="ada",
    )(c_pad, w_ada, b_ada)


QKV_W = 3 * GDN_WIDTH
PROJ_SPLITS = (QKV_W, GDN_WIDTH, SWA_WIDTH, 2 * SWA_KV_WIDTH, GATE_PAD)
PROJ_COLS = sum(PROJ_SPLITS)


def _proj_kernel(x_ref, mod_ref, nw_ref, w_ref, cw_ref, bd_ref, qkv_ref, z_ref, sq_ref, skv_ref, ab_ref,
                 qbuf, kbuf, vbuf):
    tm = PROJ_TM
    bufs = (qbuf, kbuf, vbuf)

    @pl.when(pl.program_id(1) == 0)
    def _():
        for buf in bufs:
            buf[0:8, :] = jnp.zeros((8, GDN_WIDTH), F32)

    x = x_ref[0]
    ms = jnp.mean(x * x, axis=-1, keepdims=True)
    y = x * lax.rsqrt(ms + EPS) * nw_ref[...]
    mod = mod_ref[0]
    shift = mod[:, 0:D_MODEL]
    scale = mod[:, D_MODEL:2 * D_MODEL]
    hb = (y * (1.0 + scale) + shift).astype(BF16)
    def project(s):
        bufs[s][8:8 + tm, :] = _dot(hb, w_ref[:, s * GDN_WIDTH:(s + 1) * GDN_WIDTH])

    def conv_silu(s):
        buf = bufs[s]
        cw = lambda j: cw_ref[j:j + 1, s * GDN_WIDTH:(s + 1) * GDN_WIDTH]
        acc = cw(CONV_WIDTH - 1) * buf[8:8 + tm, :]
        for j in range(CONV_WIDTH - 1):
            off = 8 - (CONV_WIDTH - 1) + j
            acc = acc + cw(j) * buf[off:off + tm, :]
        buf[0:8, :] = buf[tm:tm + 8, :]
        return acc * _sigmoid(acc)

    def l2_normed(a, scale):
        return a * lax.rsqrt(_dot((a * a).astype(BF16), bd_ref[...]) + EPS) * scale

    def rest(ref, c0):
        ref[0] = _dot(hb, w_ref[:, c0:c0 + ref.shape[-1]])

    project(0)
    project(1)
    qkv_ref[0, :, 0:GDN_WIDTH] = l2_normed(conv_silu(0), HEAD_DIM ** -0.5)
    project(2)
    qkv_ref[0, :, GDN_WIDTH:2 * GDN_WIDTH] = l2_normed(conv_silu(1), 1.0)
    rest(z_ref, QKV_W)
    qkv_ref[0, :, 2 * GDN_WIDTH:] = conv_silu(2)
    col = QKV_W + GDN_WIDTH
    for ref in (sq_ref, skv_ref, ab_ref):
        rest(ref, col)
        col += ref.shape[-1]


def _head_sum_matrix(width):
    h = np.arange(width) // HEAD_DIM
    return jnp.asarray((h[:, None] == h[None, :]).astype(np.float32), BF16)


def _proj_call(x, mod3, norm_w, w_cat, conv_w):
    b, t, _ = x.shape
    tm = PROJ_TM
    row = lambda width: pl.BlockSpec((1, tm, width), lambda bi, ti: (bi, ti, 0))
    const = lambda shape: pl.BlockSpec(shape, lambda bi, ti: (0,) * len(shape))
    return pl.pallas_call(
        _proj_kernel,
        grid=(b, t // tm),
        in_specs=[
            row(D_MODEL),
            pl.BlockSpec((1, 1, 6 * D_MODEL), lambda bi, ti: (bi, 0, 0)),
            const((1, D_MODEL)),
            const((D_MODEL, PROJ_COLS)),
            const((CONV_WIDTH, QKV_W)),
            const((GDN_WIDTH, GDN_WIDTH)),
        ],
        out_specs=[row(w) for w in PROJ_SPLITS],
        out_shape=[jax.ShapeDtypeStruct((b, t, w), F32) for w in PROJ_SPLITS],
        scratch_shapes=[pltpu.VMEM((8 + tm, GDN_WIDTH), F32)] * 3,
        compiler_params=pltpu.CompilerParams(
            dimension_semantics=("arbitrary", "arbitrary"), vmem_limit_bytes=VMEM_LIMIT),
        name="proj",
    )(x, mod3, norm_w, w_cat, conv_w, _head_sum_matrix(GDN_WIDTH))


GDN_UNITS = (GDN_TT // CHUNK) * (GDN_HEADS // HEADS_PER_GROUP)
BD_SLOTS = 4


def _block_diag_rows(buf, slot, p):
    p16 = p.astype(BF16)
    for h in range(HEADS_PER_GROUP):
        buf[slot, h * HEAD_DIM:(h + 1) * HEAD_DIM, h * HEAD_DIM:(h + 1) * HEAD_DIM] = (
            p16[:, h * HEAD_DIM:(h + 1) * HEAD_DIM])
    return buf[slot]


_DONE = object()


def _gdn_kernel(qkv_ref, z_ref, ab_ref, alog_ref, dtb_ref, gnw_ref, bd_ref, eg_ref, eb_ref,
                ltri_ref, o_ref, s_ref, obuf, bdbuf, wq_buf, u_buf, qk_buf, kd_buf, dec_buf):
    tt = GDN_TT
    ti = pl.program_id(1)
    wslot = ti % 2
    rslot = 1 - wslot

    @pl.when(ti == 0)
    def _():
        s_ref[...] = jnp.zeros_like(s_ref)

    @pl.when((ti == 0) & (pl.program_id(0) == 0))
    def _():
        bdbuf[...] = jnp.zeros_like(bdbuf)
        for buf in (wq_buf, u_buf, qk_buf, kd_buf, dec_buf):
            buf[...] = jnp.zeros_like(buf)

    bd = bd_ref[...]
    q = qkv_ref[0, :, 0:GDN_WIDTH]
    k = qkv_ref[0, :, GDN_WIDTH:2 * GDN_WIDTH]
    v = qkv_ref[0, :, 2 * GDN_WIDTH:]

    ab = ab_ref[0]
    lane = lax.broadcasted_iota(jnp.int32, (1, GATE_PAD), 1)
    xs = ab + dtb_ref[...]
    softplus = jnp.maximum(xs, 0.0) + jnp.log1p(jnp.exp(-jnp.abs(xs)))
    g = jnp.where(lane < GDN_HEADS, -jnp.exp(alog_ref[...]) * softplus, 0.0)
    beta = _sigmoid(ab)
    ltri = ltri_ref[...]
    g1, g2, g3 = _split3(g)
    gcum = _dot(ltri, g1) + _dot(ltri, g2) + _dot(ltri, g3)
    eg = eg_ref[...]
    c1, c2, c3 = _split3(gcum)
    g_x = _dot(c1, eg) + _dot(c2, eg) + _dot(c3, eg)
    beta_x = _dot(beta.astype(BF16), eb_ref[...])

    r64 = lax.broadcasted_iota(jnp.int32, (CHUNK, GROUP_W), 0)
    l64 = lax.broadcasted_iota(jnp.int32, (CHUNK, GROUP_W), 1) % CHUNK
    causal = r64 >= l64
    eye_x = (r64 == l64).astype(F32)
    strict_x = (r64 > l64).astype(F32)
    same_block = lambda size: (r64 // size) == (l64 // size)
    base_x = same_block(INV_BASE).astype(F32)
    level_x = [(same_block(2 * size) & ~same_block(size)).astype(F32) for size in INV_LEVELS]
    rb = lax.broadcasted_iota(jnp.int32, (GROUP_W, GROUP_W), 0) // HEAD_DIM
    cb = lax.broadcasted_iota(jnp.int32, (GROUP_W, GROUP_W), 1) // HEAD_DIM
    mask_bd = (rb == cb).astype(F32)

    n_groups = GDN_HEADS // HEADS_PER_GROUP
    n_chunks = tt // CHUNK
    units = [(c, gi) for c in range(n_chunks) for gi in range(n_groups)]
    ids = list(range(len(units)))
    tile = lambda a: [a[c * CHUNK:(c + 1) * CHUNK, gi * GROUP_W:(gi + 1) * GROUP_W] for c, gi in units]
    each = lambda f, *lists: [f(*args) for args in zip(*lists)]
    bdr = lambda i, slot, t: _block_diag_rows(bdbuf, i * BD_SLOTS + slot, t)
    mm = lambda a, w16: _dot(a.astype(BF16), w16)
    stack = lambda a, b_: jnp.concatenate([a, b_], axis=0)

    def chunk_parallel_part():
        kc, qc, vc, bx, gx = tile(k), tile(q), tile(v), tile(beta_x), tile(g_x)
        eg_c = each(jnp.exp, gx)
        glast = each(lambda g_: g_[CHUNK - 1:CHUNK, :], gx)
        kb = each(jnp.multiply, kc, bx)
        vb = each(jnp.multiply, vc, bx)
        wr = each(jnp.multiply, kb, eg_c)
        qd = each(jnp.multiply, qc, eg_c)
        for i in ids:
            kd_buf[wslot, i] = (kc[i] * jnp.exp(glast[i] - gx[i])).astype(BF16)
            dec_buf[wslot, i] = jnp.broadcast_to(jnp.exp(glast[i]), (8, GROUP_W))
        dm = each(lambda g_: jnp.exp(jnp.where(causal, g_ - jnp.sum(g_ * eye_x, axis=0, keepdims=True), NEG_BIG)),
                  gx)
        yield
        aq = each(lambda i, a, b_, k_: _dot_nt(stack(a, b_).astype(BF16), bdr(i, 1, k_)), ids, kb, qc, kc)
        xm = each(lambda a, d_: -(a[0:CHUNK] * d_ * strict_x), aq, dm)
        for i in ids:
            qk_buf[wslot, i] = (aq[i][CHUNK:] * dm[i]).astype(BF16)
        xd = each(lambda x_: x_ * base_x, xm)
        tinv = each(lambda x_: eye_x + x_, xd)
        yield
        pw = each(lambda i, x_: mm(x_, bdr(i, 0, x_)), ids, xd)
        yield
        r = each(lambda i, t_, p_: mm(stack(t_, p_), bdr(i, 0, p_)), ids, tinv, pw)
        tinv = each(lambda t_, r_: t_ + r_[0:CHUNK], tinv, r)
        pw = each(lambda r_: r_[CHUNK:], r)
        yield
        tinv = each(lambda i, t_, p_: t_ + mm(t_, bdr(i, 0, p_)), ids, tinv, pw)
        yield
        for lm in level_x:
            e = each(lambda i, x_, t_: mm(x_ * lm, bdr(i, 0, t_)), ids, xm, tinv)
            yield
            tinv = each(lambda i, t_, e_: t_ + mm(t_, bdr(i, 0, e_)), ids, tinv, e)
            yield
        for i in ids:
            u_buf[wslot, i] = mm(tinv[i], bdr(i, 2, vb[i]))
            wq_buf[wslot, i, 0:CHUNK] = mm(tinv[i], bdr(i, 3, wr[i])).astype(BF16)
            wq_buf[wslot, i, CHUNK:] = qd[i].astype(BF16)

    def sequential_part():
        s_state = [s_ref[gi] for gi in range(n_groups)]
        for c in range(n_chunks):
            cids = [c * n_groups + gi for gi in range(n_groups)]
            r2 = [_dot(wq_buf[rslot, i], s_state[gi].astype(BF16)) for gi, i in enumerate(cids)]
            yield
            vn = [u_buf[rslot, i] - r2[gi][0:CHUNK] for gi, i in enumerate(cids)]
            o = [r2[gi][CHUNK:] + _dot(qk_buf[rslot, i], bdr(GDN_UNITS, gi, vn[gi])) for gi, i in enumerate(cids)]
            s_state = [s_state[gi] * dec_buf[rslot, i][0:1] + mask_bd * _dot_tn(kd_buf[rslot, i], vn[gi].astype(BF16))
                       for gi, i in enumerate(cids)]
            for gi in range(n_groups):
                obuf[c * CHUNK:(c + 1) * CHUNK, gi * GROUP_W:(gi + 1) * GROUP_W] = o[gi]
            yield
        for gi in range(n_groups):
            s_ref[gi] = jnp.where(ti > 0, s_state[gi], 0.0)

    parts = [sequential_part(), chunk_parallel_part()]
    while parts:
        parts = [p for p in parts if next(p, _DONE) is not _DONE]

    o = obuf[...]
    ms = _dot((o * o).astype(BF16), bd) * (1.0 / HEAD_DIM)
    zz = z_ref[0]
    o_ref[0] = (o * lax.rsqrt(ms + EPS) * gnw_ref[...] * (zz * _sigmoid(zz))).astype(BF16)


def _gdn_consts():
    h = np.arange(GDN_WIDTH) // HEAD_DIM
    eg = np.zeros((GATE_PAD, GDN_WIDTH), np.float32)
    eb = np.zeros((GATE_PAD, GDN_WIDTH), np.float32)
    eg[h, np.arange(GDN_WIDTH)] = 1.0
    eb[GDN_HEADS + h, np.arange(GDN_WIDTH)] = 1.0
    t = np.arange(GDN_TT)
    ltri = ((t[:, None] // CHUNK == t[None, :] // CHUNK) & (t[:, None] >= t[None, :])).astype(np.float32)
    return (_head_sum_matrix(GDN_WIDTH), jnp.asarray(eg, BF16), jnp.asarray(eb, BF16), jnp.asarray(ltri, BF16))


def _gdn_call(qkv, z, ab, alog_pad, dtb_pad, gnw_x):
    b, t, _ = qkv.shape
    tt = GDN_TT
    bd, eg, eb, ltri = _gdn_consts()
    n_tiles = t // tt
    n_groups = GDN_HEADS // HEADS_PER_GROUP
    ahead = lambda width: pl.BlockSpec((1, tt, width), lambda bi, ti: (bi, jnp.minimum(ti, n_tiles - 1), 0))
    behind = lambda width: pl.BlockSpec((1, tt, width), lambda bi, ti: (bi, jnp.maximum(ti - 1, 0), 0))
    const = lambda shape: pl.BlockSpec(shape, lambda bi, ti: (0,) * len(shape))
    per_unit = lambda rows, dtype: pltpu.VMEM((2, GDN_UNITS, rows, GROUP_W), dtype)
    return pl.pallas_call(
        _gdn_kernel,
        grid=(b, n_tiles + 1),
        in_specs=[
            ahead(QKV_W), behind(GDN_WIDTH), ahead(GATE_PAD),
            const((1, GATE_PAD)), const((1, GATE_PAD)), const((1, GDN_WIDTH)),
            const((GDN_WIDTH, GDN_WIDTH)), const((GATE_PAD, GDN_WIDTH)), const((GATE_PAD, GDN_WIDTH)),
            const((tt, tt)),
        ],
        out_specs=behind(GDN_WIDTH),
        out_shape=jax.ShapeDtypeStruct((b, t, GDN_WIDTH), BF16),
        scratch_shapes=[
            pltpu.VMEM((n_groups, GROUP_W, GROUP_W), F32),
            pltpu.VMEM((tt, GDN_WIDTH), F32),
            pltpu.VMEM((GDN_UNITS * BD_SLOTS + n_groups, GROUP_W, GROUP_W), BF16),
            per_unit(2 * CHUNK, BF16), per_unit(CHUNK, F32), per_unit(CHUNK, BF16), per_unit(CHUNK, BF16),
            per_unit(8, F32),
        ],
        compiler_params=pltpu.CompilerParams(
            dimension_semantics=("arbitrary", "arbitrary"), vmem_limit_bytes=VMEM_LIMIT),
        name="gdn",
    )(qkv, z, ab, alog_pad, dtb_pad, gnw_x, bd, eg, eb, ltri)


SWA_PAIRS = SWA_Q_HEADS // 2
SWA_ROWS = SWA_Q_HEADS * WINDOW
LOG2E = 1.4426950408889634


def _swa_kernel(q_ref, kvc_ref, kvp_ref, qw_ref, kw_ref, bdq_ref, bdk_ref, bias_ref, sink_ref, o_ref):
    tq = SWA_TQ
    ti = pl.program_id(1)
    q = q_ref[0]
    q = q * lax.rsqrt(_dot((q * q).astype(BF16), bdq_ref[...]) * (1.0 / HEAD_DIM) + EPS)
    q = q * (qw_ref[...] * (HEAD_DIM ** -0.5 * LOG2E))
    kv = jnp.concatenate([kvp_ref[0], kvc_ref[0]], axis=0)
    k = kv[:, 0:SWA_KV_WIDTH]
    v = kv[:, SWA_KV_WIDTH:]
    k = k * lax.rsqrt(_dot((k * k).astype(BF16), bdk_ref[...]) * (1.0 / HEAD_DIM) + EPS) * kw_ref[...]
    k16 = k.astype(BF16)
    lo = lax.broadcasted_iota(jnp.int32, (1, LANES), 1) < HEAD_DIM
    qcol = lax.broadcasted_iota(jnp.int32, (WINDOW, SWA_ROWS), 1) % WINDOW
    from_prev = lax.broadcasted_iota(jnp.int32, (WINDOW, SWA_ROWS), 0) > qcol
    first = jnp.where(ti == 0, 1, 0)
    sink = sink_ref[...]
    vt16 = v.T.astype(BF16)
    zero = jnp.zeros((), BF16)
    for j in range(tq // WINDOW):
        qj = q[j * WINDOW:(j + 1) * WINDOW]
        parts = []
        for p in range(SWA_PAIRS):
            qp = qj[:, p * LANES:(p + 1) * LANES]
            parts += [jnp.where(lo, qp, 0.0), jnp.where(lo, 0.0, qp)]
        qs = jnp.concatenate(parts, axis=0).astype(BF16)
        st = _dot_nt(k16[j * WINDOW:(j + 2) * WINDOW], qs)
        bias = bias_ref[first] if j == 0 else bias_ref[0]
        sm = jnp.where(from_prev, st[0:WINDOW], st[WINDOW:]) + bias
        m = jnp.max(sm, axis=0, keepdims=True)
        pe = jnp.exp2(sm - m)
        den = jnp.sum(pe, axis=0, keepdims=True) + jnp.exp2(sink - m)
        pb = pe.astype(BF16)
        pt2 = jnp.concatenate([jnp.where(from_prev, pb, zero), jnp.where(from_prev, zero, pb)], axis=0)
        ot = _dot(vt16[:, j * WINDOW:(j + 2) * WINDOW], pt2) * (1.0 / den)
        for p in range(SWA_PAIRS):
            c0 = p * 2 * WINDOW
            pair_t = jnp.concatenate([ot[0:HEAD_DIM, c0:c0 + WINDOW],
                                      ot[HEAD_DIM:, c0 + WINDOW:c0 + 2 * WINDOW]], axis=0)
            o_ref[0, j * WINDOW:(j + 1) * WINDOW, p * LANES:(p + 1) * LANES] = pair_t.T.astype(BF16)


def _swa_consts():
    qi = np.arange(WINDOW)[:, None]
    kj = np.arange(WINDOW)[None, :]
    from_prev = kj > qi
    dist = np.where(from_prev, qi + WINDOW - kj, qi - kj).astype(np.float32)
    slopes = 2.0 ** (-8.0 * (np.arange(SWA_Q_HEADS, dtype=np.float32) + 1.0) / SWA_Q_HEADS)
    bias = np.zeros((2, SWA_PAIRS, 2, WINDOW, WINDOW), np.float32)
    for first in range(2):
        for p in range(SWA_PAIRS):
            for half, head in enumerate((p, SWA_GROUP + p)):
                b = (-slopes[head].astype(np.float32) * dist) * np.float32(LOG2E)
                bias[first, p, half] = np.where(from_prev & bool(first), np.float32(NEG_BIG), b)
    bias = bias.reshape(2, SWA_ROWS, WINDOW).transpose(0, 2, 1)
    h = np.arange(SWA_WIDTH) // HEAD_DIM
    bdq = (h[:, None] == h[None, :]).astype(np.float32)
    hk = np.arange(SWA_KV_WIDTH) // HEAD_DIM
    bdk = (hk[:, None] == hk[None, :]).astype(np.float32)
    return jnp.asarray(bias), jnp.asarray(bdq, BF16), jnp.asarray(bdk, BF16)


def _swa_call(sq, skv, qw_x, kw_x, sink_col):
    b, t, _ = sq.shape
    tq = SWA_TQ
    nb = tq // WINDOW
    bias, bdq, bdk = _swa_consts()
    const = lambda shape: pl.BlockSpec(shape, lambda bi, ti: (0,) * len(shape))
    return pl.pallas_call(
        _swa_kernel,
        grid=(b, t // tq),
        in_specs=[
            pl.BlockSpec((1, tq, SWA_WIDTH), lambda bi, ti: (bi, ti, 0)),
            pl.BlockSpec((1, tq, 2 * SWA_KV_WIDTH), lambda bi, ti: (bi, ti, 0)),
            pl.BlockSpec((1, WINDOW, 2 * SWA_KV_WIDTH), lambda bi, ti: (bi, jnp.maximum(ti * nb - 1, 0), 0)),
            const((1, SWA_WIDTH)), const((1, SWA_KV_WIDTH)),
            const((SWA_WIDTH, SWA_WIDTH)), const((SWA_KV_WIDTH, SWA_KV_WIDTH)),
            const((2, WINDOW, SWA_ROWS)),
            const((1, SWA_ROWS)),
        ],
        out_specs=pl.BlockSpec((1, tq, SWA_WIDTH), lambda bi, ti: (bi, ti, 0)),
        out_shape=jax.ShapeDtypeStruct((b, t, SWA_WIDTH), BF16),
        compiler_params=pltpu.CompilerParams(
            dimension_semantics=("arbitrary", "arbitrary"), vmem_limit_bytes=VMEM_LIMIT),
        name="swa",
    )(sq, skv, skv, qw_x, kw_x, bdq, bdk, bias, sink_col)


def _ffn_kernel(x_ref, og_ref, os_ref, mod_ref, nw_ref, wo_ref, wg_ref, wu_ref, wd_ref, o_ref):
    x = x_ref[0]
    mod = mod_ref[0]
    gate1 = mod[:, 2 * D_MODEL:3 * D_MODEL]
    shift2 = mod[:, 3 * D_MODEL:4 * D_MODEL]
    scale2 = mod[:, 4 * D_MODEL:5 * D_MODEL]
    gate2 = mod[:, 5 * D_MODEL:]
    mixed = _dot(og_ref[0], wo_ref[0:GDN_WIDTH, :]) + _dot(os_ref[0], wo_ref[GDN_WIDTH:, :])
    x1 = x + gate1 * mixed
    ms = jnp.mean(x1 * x1, axis=-1, keepdims=True)
    hb = ((x1 * lax.rsqrt(ms + EPS) * nw_ref[...]) * (1.0 + scale2) + shift2).astype(BF16)
    gt = _dot(hb, wg_ref[...])
    up = _dot(hb, wu_ref[...])
    act = ((gt * _sigmoid(gt)) * up).astype(BF16)
    o_ref[0] = x1 + gate2 * _dot(act, wd_ref[...])


def _ffn_call(x, og, osw, mod3, norm_w, wo, wg, wu, wd):
    b, t, _ = x.shape
    tm = FFN_TM
    row = lambda width: pl.BlockSpec((1, tm, width), lambda bi, ti: (bi, ti, 0))
    const = lambda shape: pl.BlockSpec(shape, lambda bi, ti: (0,) * len(shape), pipeline_mode=pl.Buffered(1))
    return pl.pallas_call(
        _ffn_kernel,
        grid=(b, t // tm),
        in_specs=[
            row(D_MODEL), row(GDN_WIDTH), row(SWA_WIDTH),
            pl.BlockSpec((1, 1, 6 * D_MODEL), lambda bi, ti: (bi, 0, 0)),
            const((1, D_MODEL)),
            const((GDN_WIDTH + SWA_WIDTH, D_MODEL)),
            const((D_MODEL, D_FF)), const((D_MODEL, D_FF)), const((D_FF, D_MODEL)),
        ],
        out_specs=row(D_MODEL),
        out_shape=jax.ShapeDtypeStruct((b, t, D_MODEL), F32),
        compiler_params=pltpu.CompilerParams(
            dimension_semantics=("arbitrary", "arbitrary"), vmem_limit_bytes=VMEM_LIMIT),
        name="ffn",
    )(x, og, osw, mod3, norm_w, wo, wg, wu, wd)


def _layer(x, mod, norm1_w, w_in, conv_w, a_log, dt_bias, gdn_norm_w, q_norm_w, k_norm_w, sinks, w_out,
           norm2_w, w_gate, w_up, w_down):
    b = x.shape[0]
    o0 = 4 * GDN_WIDTH
    o1 = o0 + 2 * GDN_HEADS
    o2 = o1 + SWA_WIDTH
    sq_cols = w_in[:, o1:o2].reshape(D_MODEL, SWA_KV_HEADS, SWA_GROUP, HEAD_DIM)
    sq_cols = sq_cols.transpose(0, 2, 1, 3).reshape(D_MODEL, SWA_WIDTH)
    w_cat = jnp.concatenate([
        w_in[:, 0:o0],
        sq_cols,
        w_in[:, o2:],
        jnp.pad(w_in[:, o0:o1], ((0, 0), (0, GATE_PAD - 2 * GDN_HEADS))),
    ], axis=1).astype(BF16)
    mod3 = mod[:b].reshape(b, 1, 6 * D_MODEL)
    qkv, z, sq, skv, ab = _proj_call(x, mod3, norm1_w.reshape(1, D_MODEL), w_cat,
                                     conv_w.reshape(CONV_WIDTH, QKV_W))

    alog_pad = jnp.pad(a_log.reshape(1, GDN_HEADS), ((0, 0), (0, GATE_PAD - GDN_HEADS)))
    dtb_pad = jnp.pad(dt_bias.reshape(1, GDN_HEADS), ((0, 0), (0, GATE_PAD - GDN_HEADS)))
    gnw_x = jnp.tile(gdn_norm_w.reshape(1, HEAD_DIM), (1, GDN_HEADS))
    og = _gdn_call(qkv, z, ab, alog_pad, dtb_pad, gnw_x)

    qw_x = jnp.tile(q_norm_w.reshape(1, HEAD_DIM), (1, SWA_Q_HEADS))
    kw_x = jnp.tile(k_norm_w.reshape(1, HEAD_DIM), (1, SWA_KV_HEADS))
    sink_pairs = jnp.stack([sinks[:SWA_GROUP], sinks[SWA_GROUP:]], axis=1)
    sink_col = (jnp.repeat(sink_pairs, WINDOW, axis=1) * LOG2E).reshape(1, SWA_ROWS)
    osw = _swa_call(sq, skv, qw_x, kw_x, sink_col)

    os_rows = w_out[GDN_WIDTH:].reshape(SWA_KV_HEADS, SWA_GROUP, HEAD_DIM, D_MODEL)
    os_rows = os_rows.transpose(1, 0, 2, 3).reshape(SWA_WIDTH, D_MODEL)
    wo = jnp.concatenate([w_out[:GDN_WIDTH], os_rows], axis=0).astype(BF16)
    return _ffn_call(x, og, osw, mod3, norm2_w.reshape(1, D_MODEL), wo,
                     w_gate.astype(BF16), w_up.astype(BF16), w_down.astype(BF16))


def kernel(x, c, w_ada, b_ada, norm1_w, w_in, conv_w, a_log, dt_bias, gdn_norm_w, q_norm_w, k_norm_w, sinks,
           w_out, norm2_w, w_gate, w_up, w_down):
    depth = w_ada.shape[0]
    b = c.shape[0]
    c_pad = jnp.pad(c, ((0, 8 - b), (0, 0)))
    for l in range(depth):
        mod = _ada_call(c_pad, w_ada[l], b_ada[l].reshape(1, -1))
        x = _layer(x, mod, norm1_w[l], w_in[l], conv_w[l], a_log[l], dt_bias[l], gdn_norm_w[l], q_norm_w[l],
                   k_norm_w[l], sinks[l], w_out[l], norm2_w[l], w_gate[l], w_up[l], w_down[l])
    return x
```

```python
import functools

import numpy as np
import jax
import jax.numpy as jnp
from jax import lax
from jax.experimental import pallas as pl
from jax.experimental.pallas import tpu as pltpu

F32 = jnp.float32
BF16 = jnp.bfloat16

D_MODEL = 1024
HEAD_DIM = 64
GDN_HEADS = 8
GDN_WIDTH = GDN_HEADS * HEAD_DIM
SWA_Q_HEADS = 8
SWA_KV_HEADS = 2
SWA_GROUP = SWA_Q_HEADS // SWA_KV_HEADS
SWA_WIDTH = SWA_Q_HEADS * HEAD_DIM
SWA_KV_WIDTH = SWA_KV_HEADS * HEAD_DIM
WINDOW = 128
CONV_WIDTH = 4
CHUNK = 64
D_FF = 2816
EPS = 1e-6
LANES = 128
GATE_PAD = LANES
HEADS_PER_GROUP = 2
GROUP_W = HEADS_PER_GROUP * HEAD_DIM
INV_BASE = 8
INV_LEVELS = (8, 16, 32)
NEG_BIG = -1e30
VMEM_LIMIT = 56 * 1024 * 1024

PROJ_TM = 512
GDN_TT = 256
SWA_TQ = 1024
FFN_TM = 512
ADA_TN = 1536


def _sigmoid(x):
    return 1.0 / (1.0 + jnp.exp(-x))


def _dot(a, b):
    return jnp.dot(a, b, preferred_element_type=F32)


def _dot_nt(a, b):
    return lax.dot_general(a, b, (((1,), (1,)), ((), ())), preferred_element_type=F32)


def _dot_tn(a, b):
    return lax.dot_general(a, b, (((0,), (0,)), ((), ())), preferred_element_type=F32)


def _split3(x):
    x1 = x.astype(BF16)
    r1 = x - x1.astype(F32)
    x2 = r1.astype(BF16)
    r2 = r1 - x2.astype(F32)
    x3 = r2.astype(BF16)
    return x1, x2, x3


def _ada_kernel(c_ref, w_ref, b_ref, o_ref):
    c = c_ref[...]
    ca = c * _sigmoid(c)
    o_ref[...] = _dot(ca.astype(BF16), w_ref[...].astype(BF16)) + b_ref[...]


def _ada_call(c_pad, w_ada, b_ada):
    n = w_ada.shape[1]
    return pl.pallas_call(
        _ada_kernel,
        grid=(n // ADA_TN,),
        in_specs=[
            pl.BlockSpec((8, D_MODEL), lambda j: (0, 0)),
            pl.BlockSpec((D_MODEL, ADA_TN), lambda j: (0, j)),
            pl.BlockSpec((1, ADA_TN), lambda j: (0, j)),
        ],
        out_specs=pl.BlockSpec((8, ADA_TN), lambda j: (0, j)),
        out_shape=jax.ShapeDtypeStruct((8, n), F32),
        compiler_params=pltpu.CompilerParams(
            dimension_semantics=("arbitrary",), vmem_limit_bytes=VMEM_LIMIT),
        name="ada",
    )(c_pad, w_ada, b_ada)


QKV_W = 3 * GDN_WIDTH
PROJ_SPLITS = (QKV_W, GDN_WIDTH, SWA_WIDTH, 2 * SWA_KV_WIDTH, GATE_PAD)
PROJ_COLS = sum(PROJ_SPLITS)


def _proj_kernel(x_ref, mod_ref, nw_ref, w_ref, cw_ref, bd_ref, qkv_ref, z_ref, sq_ref, skv_ref, ab_ref,
                 qbuf, kbuf, vbuf):
    tm = PROJ_TM
    bufs = (qbuf, kbuf, vbuf)

    @pl.when(pl.program_id(1) == 0)
    def _():
        for buf in bufs:
            buf[0:8, :] = jnp.zeros((8, GDN_WIDTH), F32)

    x = x_ref[0]
    ms = jnp.mean(x * x, axis=-1, keepdims=True)
    y = x * lax.rsqrt(ms + EPS) * nw_ref[...]
    mod = mod_ref[0]
    shift = mod[:, 0:D_MODEL]
    scale = mod[:, D_MODEL:2 * D_MODEL]
    hb = (y * (1.0 + scale) + shift).astype(BF16)
    def project(s):
        bufs[s][8:8 + tm, :] = _dot(hb, w_ref[:, s * GDN_WIDTH:(s + 1) * GDN_WIDTH])

    def conv_silu(s):
        buf = bufs[s]
        cw = lambda j: cw_ref[j:j + 1, s * GDN_WIDTH:(s + 1) * GDN_WIDTH]
        acc = cw(CONV_WIDTH - 1) * buf[8:8 + tm, :]
        for j in range(CONV_WIDTH - 1):
            off = 8 - (CONV_WIDTH - 1) + j
            acc = acc + cw(j) * buf[off:off + tm, :]
        buf[0:8, :] = buf[tm:tm + 8, :]
        return acc * _sigmoid(acc)

    def l2_normed(a, scale):
        return a * lax.rsqrt(_dot((a * a).astype(BF16), bd_ref[...]) + EPS) * scale

    def rest(ref, c0):
        ref[0] = _dot(hb, w_ref[:, c0:c0 + ref.shape[-1]])

    project(0)
    project(1)
    qkv_ref[0, :, 0:GDN_WIDTH] = l2_normed(conv_silu(0), HEAD_DIM ** -0.5)
    project(2)
    qkv_ref[0, :, GDN_WIDTH:2 * GDN_WIDTH] = l2_normed(conv_silu(1), 1.0)
    rest(z_ref, QKV_W)
    qkv_ref[0, :, 2 * GDN_WIDTH:] = conv_silu(2)
    col = QKV_W + GDN_WIDTH
    for ref in (sq_ref, skv_ref, ab_ref):
        rest(ref, col)
        col += ref.shape[-1]


def _head_sum_matrix(width):
    h = np.arange(width) // HEAD_DIM
    return jnp.asarray((h[:, None] == h[None, :]).astype(np.float32), BF16)


def _proj_call(x, mod3, norm_w, w_cat, conv_w):
    b, t, _ = x.shape
    tm = PROJ_TM
    row = lambda width: pl.BlockSpec((1, tm, width), lambda bi, ti: (bi, ti, 0))
    const = lambda shape: pl.BlockSpec(shape, lambda bi, ti: (0,) * len(shape))
    return pl.pallas_call(
        _proj_kernel,
        grid=(b, t // tm),
        in_specs=[
            row(D_MODEL),
            pl.BlockSpec((1, 1, 6 * D_MODEL), lambda bi, ti: (bi, 0, 0)),
            const((1, D_MODEL)),
            const((D_MODEL, PROJ_COLS)),
            const((CONV_WIDTH, QKV_W)),
            const((GDN_WIDTH, GDN_WIDTH)),
        ],
        out_specs=[row(w) for w in PROJ_SPLITS],
        out_shape=[jax.ShapeDtypeStruct((b, t, w), F32) for w in PROJ_SPLITS],
        scratch_shapes=[pltpu.VMEM((8 + tm, GDN_WIDTH), F32)] * 3,
        compiler_params=pltpu.CompilerParams(
            dimension_semantics=("arbitrary", "arbitrary"), vmem_limit_bytes=VMEM_LIMIT),
        name="proj",
    )(x, mod3, norm_w, w_cat, conv_w, _head_sum_matrix(GDN_WIDTH))


GDN_UNITS = (GDN_TT // CHUNK) * (GDN_HEADS // HEADS_PER_GROUP)
BD_SLOTS = 4


def _block_diag_rows(buf, slot, p):
    p16 = p.astype(BF16)
    for h in range(HEADS_PER_GROUP):
        buf[slot, h * HEAD_DIM:(h + 1) * HEAD_DIM, h * HEAD_DIM:(h + 1) * HEAD_DIM] = (
            p16[:, h * HEAD_DIM:(h + 1) * HEAD_DIM])
    return buf[slot]


_DONE = object()


def _gdn_kernel(qkv_ref, z_ref, ab_ref, alog_ref, dtb_ref, gnw_ref, bd_ref, eg_ref, eb_ref,
                ltri_ref, o_ref, s_ref, obuf, bdbuf, wq_buf, u_buf, qk_buf, kd_buf, dec_buf, *, tiles_per_seq):
    tt = GDN_TT
    step = pl.program_id(0)
    wslot = step % 2
    rslot = 1 - wslot
    starts_sequence = (step - 1) % tiles_per_seq == 0

    @pl.when(step == 0)
    def _():
        for buf in (s_ref, bdbuf, wq_buf, u_buf, qk_buf, kd_buf, dec_buf):
            buf[...] = jnp.zeros_like(buf)

    bd = bd_ref[...]
    q = qkv_ref[0, :, 0:GDN_WIDTH]
    k = qkv_ref[0, :, GDN_WIDTH:2 * GDN_WIDTH]
    v = qkv_ref[0, :, 2 * GDN_WIDTH:]

    ab = ab_ref[0]
    lane = lax.broadcasted_iota(jnp.int32, (1, GATE_PAD), 1)
    xs = ab + dtb_ref[...]
    softplus = jnp.maximum(xs, 0.0) + jnp.log1p(jnp.exp(-jnp.abs(xs)))
    g = jnp.where(lane < GDN_HEADS, -jnp.exp(alog_ref[...]) * softplus, 0.0)
    beta = _sigmoid(ab)
    ltri = ltri_ref[...]
    g1, g2, g3 = _split3(g)
    gcum = _dot(ltri, g1) + _dot(ltri, g2) + _dot(ltri, g3)
    eg = eg_ref[...]
    c1, c2, c3 = _split3(gcum)
    g_x = _dot(c1, eg) + _dot(c2, eg) + _dot(c3, eg)
    beta_x = _dot(beta.astype(BF16), eb_ref[...])

    r64 = lax.broadcasted_iota(jnp.int32, (CHUNK, GROUP_W), 0)
    l64 = lax.broadcasted_iota(jnp.int32, (CHUNK, GROUP_W), 1) % CHUNK
    causal = r64 >= l64
    eye_x = (r64 == l64).astype(F32)
    strict_x = (r64 > l64).astype(F32)
    same_block = lambda size: (r64 // size) == (l64 // size)
    base_x = same_block(INV_BASE).astype(F32)
    level_x = [(same_block(2 * size) & ~same_block(size)).astype(F32) for size in INV_LEVELS]
    rb = lax.broadcasted_iota(jnp.int32, (GROUP_W, GROUP_W), 0) // HEAD_DIM
    cb = lax.broadcasted_iota(jnp.int32, (GROUP_W, GROUP_W), 1) // HEAD_DIM
    mask_bd = (rb == cb).astype(F32)

    n_groups = GDN_HEADS // HEADS_PER_GROUP
    n_chunks = tt // CHUNK
    units = [(c, gi) for c in range(n_chunks) for gi in range(n_groups)]
    ids = list(range(len(units)))
    tile = lambda a: [a[c * CHUNK:(c + 1) * CHUNK, gi * GROUP_W:(gi + 1) * GROUP_W] for c, gi in units]
    each = lambda f, *lists: [f(*args) for args in zip(*lists)]
    bdr = lambda i, slot, t: _block_diag_rows(bdbuf, i * BD_SLOTS + slot, t)
    mm = lambda a, w16: _dot(a.astype(BF16), w16)
    stack = lambda a, b_: jnp.concatenate([a, b_], axis=0)

    def chunk_parallel_part():
        kc, qc, vc, bx, gx = tile(k), tile(q), tile(v), tile(beta_x), tile(g_x)
        eg_c = each(jnp.exp, gx)
        glast = each(lambda g_: g_[CHUNK - 1:CHUNK, :], gx)
        kb = each(jnp.multiply, kc, bx)
        vb = each(jnp.multiply, vc, bx)
        wr = each(jnp.multiply, kb, eg_c)
        qd = each(jnp.multiply, qc, eg_c)
        for i in ids:
            kd_buf[wslot, i] = (kc[i] * jnp.exp(glast[i] - gx[i])).astype(BF16)
            dec_buf[wslot, i] = jnp.broadcast_to(jnp.exp(glast[i]), (8, GROUP_W))
        dm = each(lambda g_: jnp.exp(jnp.where(causal, g_ - jnp.sum(g_ * eye_x, axis=0, keepdims=True), NEG_BIG)),
                  gx)
        yield
        aq = each(lambda i, a, b_, k_: _dot_nt(stack(a, b_).astype(BF16), bdr(i, 1, k_)), ids, kb, qc, kc)
        xm = each(lambda a, d_: -(a[0:CHUNK] * d_ * strict_x), aq, dm)
        for i in ids:
            qk_buf[wslot, i] = (aq[i][CHUNK:] * dm[i]).astype(BF16)
        xd = each(lambda x_: x_ * base_x, xm)
        tinv = each(lambda x_: eye_x + x_, xd)
        yield
        pw = each(lambda i, x_: mm(x_, bdr(i, 0, x_)), ids, xd)
        yield
        r = each(lambda i, t_, p_: mm(stack(t_, p_), bdr(i, 0, p_)), ids, tinv, pw)
        tinv = each(lambda t_, r_: t_ + r_[0:CHUNK], tinv, r)
        pw = each(lambda r_: r_[CHUNK:], r)
        yield
        tinv = each(lambda i, t_, p_: t_ + mm(t_, bdr(i, 0, p_)), ids, tinv, pw)
        yield
        for lm in level_x:
            e = each(lambda i, x_, t_: mm(x_ * lm, bdr(i, 0, t_)), ids, xm, tinv)
            yield
            tinv = each(lambda i, t_, e_: t_ + mm(t_, bdr(i, 0, e_)), ids, tinv, e)
            yield
        for i in ids:
            u_buf[wslot, i] = mm(tinv[i], bdr(i, 2, vb[i]))
            wq_buf[wslot, i, 0:CHUNK] = mm(tinv[i], bdr(i, 3, wr[i])).astype(BF16)
            wq_buf[wslot, i, CHUNK:] = qd[i].astype(BF16)

    def sequential_part():
        s_state = [jnp.where(starts_sequence, 0.0, s_ref[gi]) for gi in range(n_groups)]
        for c in range(n_chunks):
            cids = [c * n_groups + gi for gi in range(n_groups)]
            r2 = [_dot(wq_buf[rslot, i], s_state[gi].astype(BF16)) for gi, i in enumerate(cids)]
            yield
            vn = [u_buf[rslot, i] - r2[gi][0:CHUNK] for gi, i in enumerate(cids)]
            o = [r2[gi][CHUNK:] + _dot(qk_buf[rslot, i], bdr(GDN_UNITS, gi, vn[gi])) for gi, i in enumerate(cids)]
            s_state = [s_state[gi] * dec_buf[rslot, i][0:1] + mask_bd * _dot_tn(kd_buf[rslot, i], vn[gi].astype(BF16))
                       for gi, i in enumerate(cids)]
            for gi in range(n_groups):
                obuf[c * CHUNK:(c + 1) * CHUNK, gi * GROUP_W:(gi + 1) * GROUP_W] = o[gi]
            yield
        for gi in range(n_groups):
            s_ref[gi] = s_state[gi]

    parts = [sequential_part(), chunk_parallel_part()]
    while parts:
        parts = [p for p in parts if next(p, _DONE) is not _DONE]

    o = obuf[...]
    ms = _dot((o * o).astype(BF16), bd) * (1.0 / HEAD_DIM)
    zz = z_ref[0]
    o_ref[0] = (o * lax.rsqrt(ms + EPS) * gnw_ref[...] * (zz * _sigmoid(zz))).astype(BF16)


def _gdn_consts():
    h = np.arange(GDN_WIDTH) // HEAD_DIM
    eg = np.zeros((GATE_PAD, GDN_WIDTH), np.float32)
    eb = np.zeros((GATE_PAD, GDN_WIDTH), np.float32)
    eg[h, np.arange(GDN_WIDTH)] = 1.0
    eb[GDN_HEADS + h, np.arange(GDN_WIDTH)] = 1.0
    t = np.arange(GDN_TT)
    ltri = ((t[:, None] // CHUNK == t[None, :] // CHUNK) & (t[:, None] >= t[None, :])).astype(np.float32)
    return (_head_sum_matrix(GDN_WIDTH), jnp.asarray(eg, BF16), jnp.asarray(eb, BF16), jnp.asarray(ltri, BF16))


def _gdn_call(qkv, z, ab, alog_pad, dtb_pad, gnw_x):
    b, t, _ = qkv.shape
    tt = GDN_TT
    bd, eg, eb, ltri = _gdn_consts()
    n_tiles = t // tt
    n_groups = GDN_HEADS // HEADS_PER_GROUP
    total = b * n_tiles

    def tile_block(width, lag):
        def index(n):
            m = jnp.clip(n - lag, 0, total - 1)
            return (m // n_tiles, m % n_tiles, 0)
        return pl.BlockSpec((1, tt, width), index)

    ahead = lambda width: tile_block(width, 0)
    behind = lambda width: tile_block(width, 1)
    const = lambda shape: pl.BlockSpec(shape, lambda n: (0,) * len(shape))
    per_unit = lambda rows, dtype: pltpu.VMEM((2, GDN_UNITS, rows, GROUP_W), dtype)
    return pl.pallas_call(
        functools.partial(_gdn_kernel, tiles_per_seq=n_tiles),
        grid=(total + 1,),
        in_specs=[
            ahead(QKV_W), behind(GDN_WIDTH), ahead(GATE_PAD),
            const((1, GATE_PAD)), const((1, GATE_PAD)), const((1, GDN_WIDTH)),
            const((GDN_WIDTH, GDN_WIDTH)), const((GATE_PAD, GDN_WIDTH)), const((GATE_PAD, GDN_WIDTH)),
            const((tt, tt)),
        ],
        out_specs=behind(GDN_WIDTH),
        out_shape=jax.ShapeDtypeStruct((b, t, GDN_WIDTH), BF16),
        scratch_shapes=[
            pltpu.VMEM((n_groups, GROUP_W, GROUP_W), F32),
            pltpu.VMEM((tt, GDN_WIDTH), F32),
            pltpu.VMEM((GDN_UNITS * BD_SLOTS + n_groups, GROUP_W, GROUP_W), BF16),
            per_unit(2 * CHUNK, BF16), per_unit(CHUNK, F32), per_unit(CHUNK, BF16), per_unit(CHUNK, BF16),
            per_unit(8, F32),
        ],
        compiler_params=pltpu.CompilerParams(
            dimension_semantics=("arbitrary",), vmem_limit_bytes=VMEM_LIMIT),
        name="gdn",
    )(qkv, z, ab, alog_pad, dtb_pad, gnw_x, bd, eg, eb, ltri)


SWA_PAIRS = SWA_Q_HEADS // 2
SWA_ROWS = SWA_Q_HEADS * WINDOW
LOG2E = 1.4426950408889634


def _swa_kernel(q_ref, kvc_ref, kvp_ref, qw_ref, kw_ref, bdq_ref, bdk_ref, bias_ref, sink_ref, o_ref):
    tq = SWA_TQ
    ti = pl.program_id(1)
    q = q_ref[0]
    q = q * lax.rsqrt(_dot((q * q).astype(BF16), bdq_ref[...]) * (1.0 / HEAD_DIM) + EPS)
    q = q * (qw_ref[...] * (HEAD_DIM ** -0.5 * LOG2E))
    kv = jnp.concatenate([kvp_ref[0], kvc_ref[0]], axis=0)
    k = kv[:, 0:SWA_KV_WIDTH]
    v = kv[:, SWA_KV_WIDTH:]
    k = k * lax.rsqrt(_dot((k * k).astype(BF16), bdk_ref[...]) * (1.0 / HEAD_DIM) + EPS) * kw_ref[...]
    k16 = k.astype(BF16)
    lo = lax.broadcasted_iota(jnp.int32, (1, LANES), 1) < HEAD_DIM
    qcol = lax.broadcasted_iota(jnp.int32, (WINDOW, SWA_ROWS), 1) % WINDOW
    from_prev = lax.broadcasted_iota(jnp.int32, (WINDOW, SWA_ROWS), 0) > qcol
    first = jnp.where(ti == 0, 1, 0)
    sink = sink_ref[...]
    vt16 = v.T.astype(BF16)
    zero = jnp.zeros((), BF16)
    for j in range(tq // WINDOW):
        qj = q[j * WINDOW:(j + 1) * WINDOW]
        parts = []
        for p in range(SWA_PAIRS):
            qp = qj[:, p * LANES:(p + 1) * LANES]
            parts += [jnp.where(lo, qp, 0.0), jnp.where(lo, 0.0, qp)]
        qs = jnp.concatenate(parts, axis=0).astype(BF16)
        st = _dot_nt(k16[j * WINDOW:(j + 2) * WINDOW], qs)
        bias = bias_ref[first] if j == 0 else bias_ref[0]
        sm = jnp.where(from_prev, st[0:WINDOW], st[WINDOW:]) + bias
        m = jnp.max(sm, axis=0, keepdims=True)
        pe = jnp.exp2(sm - m)
        den = jnp.sum(pe, axis=0, keepdims=True) + jnp.exp2(sink - m)
        pb = pe.astype(BF16)
        pt2 = jnp.concatenate([jnp.where(from_prev, pb, zero), jnp.where(from_prev, zero, pb)], axis=0)
        ot = _dot(vt16[:, j * WINDOW:(j + 2) * WINDOW], pt2) * (1.0 / den)
        for p in range(SWA_PAIRS):
            c0 = p * 2 * WINDOW
            pair_t = jnp.concatenate([ot[0:HEAD_DIM, c0:c0 + WINDOW],
                                      ot[HEAD_DIM:, c0 + WINDOW:c0 + 2 * WINDOW]], axis=0)
            o_ref[0, j * WINDOW:(j + 1) * WINDOW, p * LANES:(p + 1) * LANES] = pair_t.T.astype(BF16)


def _swa_consts():
    qi = np.arange(WINDOW)[:, None]
    kj = np.arange(WINDOW)[None, :]
    from_prev = kj > qi
    dist = np.where(from_prev, qi + WINDOW - kj, qi - kj).astype(np.float32)
    slopes = 2.0 ** (-8.0 * (np.arange(SWA_Q_HEADS, dtype=np.float32) + 1.0) / SWA_Q_HEADS)
    bias = np.zeros((2, SWA_PAIRS, 2, WINDOW, WINDOW), np.float32)
    for first in range(2):
        for p in range(SWA_PAIRS):
            for half, head in enumerate((p, SWA_GROUP + p)):
                b = (-slopes[head].astype(np.float32) * dist) * np.float32(LOG2E)
                bias[first, p, half] = np.where(from_prev & bool(first), np.float32(NEG_BIG), b)
    bias = bias.reshape(2, SWA_ROWS, WINDOW).transpose(0, 2, 1)
    h = np.arange(SWA_WIDTH) // HEAD_DIM
    bdq = (h[:, None] == h[None, :]).astype(np.float32)
    hk = np.arange(SWA_KV_WIDTH) // HEAD_DIM
    bdk = (hk[:, None] == hk[None, :]).astype(np.float32)
    return jnp.asarray(bias), jnp.asarray(bdq, BF16), jnp.asarray(bdk, BF16)


def _swa_call(sq, skv, qw_x, kw_x, sink_col):
    b, t, _ = sq.shape
    tq = SWA_TQ
    nb = tq // WINDOW
    bias, bdq, bdk = _swa_consts()
    const = lambda shape: pl.BlockSpec(shape, lambda bi, ti: (0,) * len(shape))
    return pl.pallas_call(
        _swa_kernel,
        grid=(b, t // tq),
        in_specs=[
            pl.BlockSpec((1, tq, SWA_WIDTH), lambda bi, ti: (bi, ti, 0)),
            pl.BlockSpec((1, tq, 2 * SWA_KV_WIDTH), lambda bi, ti: (bi, ti, 0)),
            pl.BlockSpec((1, WINDOW, 2 * SWA_KV_WIDTH), lambda bi, ti: (bi, jnp.maximum(ti * nb - 1, 0), 0)),
            const((1, SWA_WIDTH)), const((1, SWA_KV_WIDTH)),
            const((SWA_WIDTH, SWA_WIDTH)), const((SWA_KV_WIDTH, SWA_KV_WIDTH)),
            const((2, WINDOW, SWA_ROWS)),
            const((1, SWA_ROWS)),
        ],
        out_specs=pl.BlockSpec((1, tq, SWA_WIDTH), lambda bi, ti: (bi, ti, 0)),
        out_shape=jax.ShapeDtypeStruct((b, t, SWA_WIDTH), BF16),
        compiler_params=pltpu.CompilerParams(
            dimension_semantics=("arbitrary", "arbitrary"), vmem_limit_bytes=VMEM_LIMIT),
        name="swa",
    )(sq, skv, skv, qw_x, kw_x, bdq, bdk, bias, sink_col)


def _ffn_kernel(x_ref, og_ref, os_ref, mod_ref, nw_ref, wo_ref, wg_ref, wu_ref, wd_ref, o_ref):
    x = x_ref[0]
    mod = mod_ref[0]
    gate1 = mod[:, 2 * D_MODEL:3 * D_MODEL]
    shift2 = mod[:, 3 * D_MODEL:4 * D_MODEL]
    scale2 = mod[:, 4 * D_MODEL:5 * D_MODEL]
    gate2 = mod[:, 5 * D_MODEL:]
    mixed = _dot(og_ref[0], wo_ref[0:GDN_WIDTH, :]) + _dot(os_ref[0], wo_ref[GDN_WIDTH:, :])
    x1 = x + gate1 * mixed
    ms = jnp.mean(x1 * x1, axis=-1, keepdims=True)
    hb = ((x1 * lax.rsqrt(ms + EPS) * nw_ref[...]) * (1.0 + scale2) + shift2).astype(BF16)
    gt = _dot(hb, wg_ref[...])
    up = _dot(hb, wu_ref[...])
    act = ((gt * _sigmoid(gt)) * up).astype(BF16)
    o_ref[0] = x1 + gate2 * _dot(act, wd_ref[...])


def _ffn_call(x, og, osw, mod3, norm_w, wo, wg, wu, wd):
    b, t, _ = x.shape
    tm = FFN_TM
    row = lambda width: pl.BlockSpec((1, tm, width), lambda bi, ti: (bi, ti, 0))
    const = lambda shape: pl.BlockSpec(shape, lambda bi, ti: (0,) * len(shape), pipeline_mode=pl.Buffered(1))
    return pl.pallas_call(
        _ffn_kernel,
        grid=(b, t // tm),
        in_specs=[
            row(D_MODEL), row(GDN_WIDTH), row(SWA_WIDTH),
            pl.BlockSpec((1, 1, 6 * D_MODEL), lambda bi, ti: (bi, 0, 0)),
            const((1, D_MODEL)),
            const((GDN_WIDTH + SWA_WIDTH, D_MODEL)),
            const((D_MODEL, D_FF)), const((D_MODEL, D_FF)), const((D_FF, D_MODEL)),
        ],
        out_specs=row(D_MODEL),
        out_shape=jax.ShapeDtypeStruct((b, t, D_MODEL), F32),
        compiler_params=pltpu.CompilerParams(
            dimension_semantics=("arbitrary", "arbitrary"), vmem_limit_bytes=VMEM_LIMIT),
        name="ffn",
    )(x, og, osw, mod3, norm_w, wo, wg, wu, wd)


def _layer(x, mod, norm1_w, w_in, conv_w, a_log, dt_bias, gdn_norm_w, q_norm_w, k_norm_w, sinks, w_out,
           norm2_w, w_gate, w_up, w_down):
    b = x.shape[0]
    o0 = 4 * GDN_WIDTH
    o1 = o0 + 2 * GDN_HEADS
    o2 = o1 + SWA_WIDTH
    sq_cols = w_in[:, o1:o2].reshape(D_MODEL, SWA_KV_HEADS, SWA_GROUP, HEAD_DIM)
    sq_cols = sq_cols.transpose(0, 2, 1, 3).reshape(D_MODEL, SWA_WIDTH)
    w_cat = jnp.concatenate([
        w_in[:, 0:o0],
        sq_cols,
        w_in[:, o2:],
        jnp.pad(w_in[:, o0:o1], ((0, 0), (0, GATE_PAD - 2 * GDN_HEADS))),
    ], axis=1).astype(BF16)
    mod3 = mod[:b].reshape(b, 1, 6 * D_MODEL)
    qkv, z, sq, skv, ab = _proj_call(x, mod3, norm1_w.reshape(1, D_MODEL), w_cat,
                                     conv_w.reshape(CONV_WIDTH, QKV_W))

    alog_pad = jnp.pad(a_log.reshape(1, GDN_HEADS), ((0, 0), (0, GATE_PAD - GDN_HEADS)))
    dtb_pad = jnp.pad(dt_bias.reshape(1, GDN_HEADS), ((0, 0), (0, GATE_PAD - GDN_HEADS)))
    gnw_x = jnp.tile(gdn_norm_w.reshape(1, HEAD_DIM), (1, GDN_HEADS))
    og = _gdn_call(qkv, z, ab, alog_pad, dtb_pad, gnw_x)

    qw_x = jnp.tile(q_norm_w.reshape(1, HEAD_DIM), (1, SWA_Q_HEADS))
    kw_x = jnp.tile(k_norm_w.reshape(1, HEAD_DIM), (1, SWA_KV_HEADS))
    sink_pairs = jnp.stack([sinks[:SWA_GROUP], sinks[SWA_GROUP:]], axis=1)
    sink_col = (jnp.repeat(sink_pairs, WINDOW, axis=1) * LOG2E).reshape(1, SWA_ROWS)
    osw = _swa_call(sq, skv, qw_x, kw_x, sink_col)

    os_rows = w_out[GDN_WIDTH:].reshape(SWA_KV_HEADS, SWA_GROUP, HEAD_DIM, D_MODEL)
    os_rows = os_rows.transpose(1, 0, 2, 3).reshape(SWA_WIDTH, D_MODEL)
    wo = jnp.concatenate([w_out[:GDN_WIDTH], os_rows], axis=0).astype(BF16)
    return _ffn_call(x, og, osw, mod3, norm2_w.reshape(1, D_MODEL), wo,
                     w_gate.astype(BF16), w_up.astype(BF16), w_down.astype(BF16))


def kernel(x, c, w_ada, b_ada, norm1_w, w_in, conv_w, a_log, dt_bias, gdn_norm_w, q_norm_w, k_norm_w, sinks,
           w_out, norm2_w, w_gate, w_up, w_down):
    depth = w_ada.shape[0]
    b = c.shape[0]
    c_pad = jnp.pad(c, ((0, 8 - b), (0, 0)))
    for l in range(depth):
        mod = _ada_call(c_pad, w_ada[l], b_ada[l].reshape(1, -1))
        x = _layer(x, mod, norm1_w[l], w_in[l], conv_w[l], a_log[l], dt_bias[l], gdn_norm_w[l], q_norm_w[l],
                   k_norm_w[l], sinks[l], w_out[l], norm2_w[l], w_gate[l], w_up[l], w_down[l])
    return x
```

```python
import functools

import numpy as np
import jax
import jax.numpy as jnp
from jax import lax
from jax.experimental import pallas as pl
from jax.experimental.pallas import tpu as pltpu

F32 = jnp.float32
BF16 = jnp.bfloat16

D_MODEL = 1024
HEAD_DIM = 64
GDN_HEADS = 8
GDN_WIDTH = GDN_HEADS * HEAD_DIM
SWA_Q_HEADS = 8
SWA_KV_HEADS = 2
SWA_GROUP = SWA_Q_HEADS // SWA_KV_HEADS
SWA_WIDTH = SWA_Q_HEADS * HEAD_DIM
SWA_KV_WIDTH = SWA_KV_HEADS * HEAD_DIM
WINDOW = 128
CONV_WIDTH = 4
CHUNK = 64
D_FF = 2816
EPS = 1e-6
LANES = 128
GATE_PAD = LANES
HEADS_PER_GROUP = 2
GROUP_W = HEADS_PER_GROUP * HEAD_DIM
INV_BASE = 8
INV_LEVELS = (8, 16, 32)
NEG_BIG = -1e30
VMEM_LIMIT = 56 * 1024 * 1024

PROJ_TM = 512
GDN_TT = 256
SWA_TQ = 1024
FFN_TM = 512
ADA_TN = 1536


def _sigmoid(x):
    return 1.0 / (1.0 + jnp.exp(-x))


def _dot(a, b):
    return jnp.dot(a, b, preferred_element_type=F32)


def _dot_nt(a, b):
    return lax.dot_general(a, b, (((1,), (1,)), ((), ())), preferred_element_type=F32)


def _dot_tn(a, b):
    return lax.dot_general(a, b, (((0,), (0,)), ((), ())), preferred_element_type=F32)


def _split3(x):
    x1 = x.astype(BF16)
    r1 = x - x1.astype(F32)
    x2 = r1.astype(BF16)
    r2 = r1 - x2.astype(F32)
    x3 = r2.astype(BF16)
    return x1, x2, x3


def _ada_kernel(c_ref, w_ref, b_ref, win_ref, o_ref, win16_ref):
    c = c_ref[...]
    ca = c * _sigmoid(c)
    o_ref[...] = _dot(ca.astype(BF16), w_ref[...].astype(BF16)) + b_ref[...]
    win16_ref[...] = win_ref[...].astype(BF16)


def _ada_call(c_pad, w_ada, b_ada, w_in):
    n = w_ada.shape[1]
    steps = n // ADA_TN
    tw = PROJ_ALIGNED // steps
    return pl.pallas_call(
        _ada_kernel,
        grid=(steps,),
        in_specs=[
            pl.BlockSpec((8, D_MODEL), lambda j: (0, 0)),
            pl.BlockSpec((D_MODEL, ADA_TN), lambda j: (0, j)),
            pl.BlockSpec((1, ADA_TN), lambda j: (0, j)),
            pl.BlockSpec((D_MODEL, tw), lambda j: (0, j)),
        ],
        out_specs=[pl.BlockSpec((8, ADA_TN), lambda j: (0, j)), pl.BlockSpec((D_MODEL, tw), lambda j: (0, j))],
        out_shape=[jax.ShapeDtypeStruct((8, n), F32), jax.ShapeDtypeStruct((D_MODEL, PROJ_ALIGNED), BF16)],
        compiler_params=pltpu.CompilerParams(
            dimension_semantics=("arbitrary",), vmem_limit_bytes=VMEM_LIMIT),
        name="ada",
    )(c_pad, w_ada, b_ada, w_in)


QKV_W = 3 * GDN_WIDTH
PROJ_SPLITS = (QKV_W, GDN_WIDTH, SWA_WIDTH, 2 * SWA_KV_WIDTH, GATE_PAD)
PROJ_ALIGNED = QKV_W + GDN_WIDTH
PROJ_REGROUPED = sum(PROJ_SPLITS) - PROJ_ALIGNED


def _proj_kernel(x_ref, mod_ref, nw_ref, wa_ref, wb_ref, cw_ref, bd_ref, qkv_ref, z_ref, sq_ref, skv_ref, ab_ref,
                 qbuf, kbuf, vbuf):
    tm = PROJ_TM
    bufs = (qbuf, kbuf, vbuf)

    @pl.when(pl.program_id(1) == 0)
    def _():
        for buf in bufs:
            buf[0:8, :] = jnp.zeros((8, GDN_WIDTH), F32)

    x = x_ref[0]
    ms = jnp.mean(x * x, axis=-1, keepdims=True)
    y = x * lax.rsqrt(ms + EPS) * nw_ref[...]
    mod = mod_ref[0]
    shift = mod[:, 0:D_MODEL]
    scale = mod[:, D_MODEL:2 * D_MODEL]
    hb = (y * (1.0 + scale) + shift).astype(BF16)
    def project(s):
        bufs[s][8:8 + tm, :] = _dot(hb, wa_ref[:, s * GDN_WIDTH:(s + 1) * GDN_WIDTH])

    def conv_silu(s):
        buf = bufs[s]
        cw = lambda j: cw_ref[j:j + 1, s * GDN_WIDTH:(s + 1) * GDN_WIDTH]
        acc = cw(CONV_WIDTH - 1) * buf[8:8 + tm, :]
        for j in range(CONV_WIDTH - 1):
            off = 8 - (CONV_WIDTH - 1) + j
            acc = acc + cw(j) * buf[off:off + tm, :]
        buf[0:8, :] = buf[tm:tm + 8, :]
        return acc * _sigmoid(acc)

    def l2_normed(a, scale):
        return a * lax.rsqrt(_dot((a * a).astype(BF16), bd_ref[...]) + EPS) * scale

    project(0)
    project(1)
    qkv_ref[0, :, 0:GDN_WIDTH] = l2_normed(conv_silu(0), HEAD_DIM ** -0.5)
    project(2)
    qkv_ref[0, :, GDN_WIDTH:2 * GDN_WIDTH] = l2_normed(conv_silu(1), 1.0)
    z_ref[0] = _dot(hb, wa_ref[:, QKV_W:])
    qkv_ref[0, :, 2 * GDN_WIDTH:] = conv_silu(2)
    col = 0
    for ref in (sq_ref, skv_ref, ab_ref):
        ref[0] = _dot(hb, wb_ref[:, col:col + ref.shape[-1]])
        col += ref.shape[-1]


def _head_sum_matrix(width):
    h = np.arange(width) // HEAD_DIM
    return jnp.asarray((h[:, None] == h[None, :]).astype(np.float32), BF16)


def _proj_call(x, mod3, norm_w, w_aligned, w_regrouped, conv_w):
    b, t, _ = x.shape
    tm = PROJ_TM
    row = lambda width: pl.BlockSpec((1, tm, width), lambda bi, ti: (bi, ti, 0))
    const = lambda shape: pl.BlockSpec(shape, lambda bi, ti: (0,) * len(shape))
    return pl.pallas_call(
        _proj_kernel,
        grid=(b, t // tm),
        in_specs=[
            row(D_MODEL),
            pl.BlockSpec((1, 1, 6 * D_MODEL), lambda bi, ti: (bi, 0, 0)),
            const((1, D_MODEL)),
            const((D_MODEL, PROJ_ALIGNED)),
            const((D_MODEL, PROJ_REGROUPED)),
            const((CONV_WIDTH, QKV_W)),
            const((GDN_WIDTH, GDN_WIDTH)),
        ],
        out_specs=[row(w) for w in PROJ_SPLITS],
        out_shape=[jax.ShapeDtypeStruct((b, t, w), F32) for w in PROJ_SPLITS],
        scratch_shapes=[pltpu.VMEM((8 + tm, GDN_WIDTH), F32)] * 3,
        compiler_params=pltpu.CompilerParams(
            dimension_semantics=("arbitrary", "arbitrary"), vmem_limit_bytes=VMEM_LIMIT),
        name="proj",
    )(x, mod3, norm_w, w_aligned, w_regrouped, conv_w, _head_sum_matrix(GDN_WIDTH))


GDN_UNITS = (GDN_TT // CHUNK) * (GDN_HEADS // HEADS_PER_GROUP)
BD_SLOTS = 4


def _block_diag_rows(buf, slot, p):
    p16 = p.astype(BF16)
    for h in range(HEADS_PER_GROUP):
        buf[slot, h * HEAD_DIM:(h + 1) * HEAD_DIM, h * HEAD_DIM:(h + 1) * HEAD_DIM] = (
            p16[:, h * HEAD_DIM:(h + 1) * HEAD_DIM])
    return buf[slot]


_DONE = object()


def _gdn_kernel(qkv_ref, z_ref, ab_ref, alog_ref, dtb_ref, gnw_ref, bd_ref, eg_ref, eb_ref, ltri_ref,
                wg_ref, wu_ref, wd_ref, wo_ref, o_ref, wg16_ref, wu16_ref, wd16_ref, wo16_ref,
                s_ref, obuf, bdbuf, wq_buf, u_buf, qk_buf, kd_buf, dec_buf, *, tiles_per_seq):
    tt = GDN_TT
    step = pl.program_id(0)
    for src, dst in ((wg_ref, wg16_ref), (wu_ref, wu16_ref), (wd_ref, wd16_ref), (wo_ref, wo16_ref)):
        dst[...] = src[...].astype(BF16)
    wslot = step % 2
    rslot = 1 - wslot
    starts_sequence = (step - 1) % tiles_per_seq == 0

    @pl.when(step == 0)
    def _():
        for buf in (s_ref, bdbuf, wq_buf, u_buf, qk_buf, kd_buf, dec_buf):
            buf[...] = jnp.zeros_like(buf)

    bd = bd_ref[...]
    q = qkv_ref[0, :, 0:GDN_WIDTH]
    k = qkv_ref[0, :, GDN_WIDTH:2 * GDN_WIDTH]
    v = qkv_ref[0, :, 2 * GDN_WIDTH:]

    ab = ab_ref[0]
    lane = lax.broadcasted_iota(jnp.int32, (1, GATE_PAD), 1)
    xs = ab + dtb_ref[...]
    softplus = jnp.maximum(xs, 0.0) + jnp.log1p(jnp.exp(-jnp.abs(xs)))
    g = jnp.where(lane < GDN_HEADS, -jnp.exp(alog_ref[...]) * softplus, 0.0)
    beta = _sigmoid(ab)
    ltri = ltri_ref[...]
    g1, g2, g3 = _split3(g)
    gcum = _dot(ltri, g1) + _dot(ltri, g2) + _dot(ltri, g3)
    eg = eg_ref[...]
    c1, c2, c3 = _split3(gcum)
    g_x = _dot(c1, eg) + _dot(c2, eg) + _dot(c3, eg)
    beta_x = _dot(beta.astype(BF16), eb_ref[...])

    r64 = lax.broadcasted_iota(jnp.int32, (CHUNK, GROUP_W), 0)
    l64 = lax.broadcasted_iota(jnp.int32, (CHUNK, GROUP_W), 1) % CHUNK
    causal = r64 >= l64
    eye_x = (r64 == l64).astype(F32)
    strict_x = (r64 > l64).astype(F32)
    same_block = lambda size: (r64 // size) == (l64 // size)
    base_x = same_block(INV_BASE).astype(F32)
    level_x = [(same_block(2 * size) & ~same_block(size)).astype(F32) for size in INV_LEVELS]
    rb = lax.broadcasted_iota(jnp.int32, (GROUP_W, GROUP_W), 0) // HEAD_DIM
    cb = lax.broadcasted_iota(jnp.int32, (GROUP_W, GROUP_W), 1) // HEAD_DIM
    mask_bd = (rb == cb).astype(F32)

    n_groups = GDN_HEADS // HEADS_PER_GROUP
    n_chunks = tt // CHUNK
    units = [(c, gi) for c in range(n_chunks) for gi in range(n_groups)]
    ids = list(range(len(units)))
    tile = lambda a: [a[c * CHUNK:(c + 1) * CHUNK, gi * GROUP_W:(gi + 1) * GROUP_W] for c, gi in units]
    each = lambda f, *lists: [f(*args) for args in zip(*lists)]
    bdr = lambda i, slot, t: _block_diag_rows(bdbuf, i * BD_SLOTS + slot, t)
    mm = lambda a, w16: _dot(a.astype(BF16), w16)
    stack = lambda a, b_: jnp.concatenate([a, b_], axis=0)

    def chunk_parallel_part():
        kc, qc, vc, bx, gx = tile(k), tile(q), tile(v), tile(beta_x), tile(g_x)
        eg_c = each(jnp.exp, gx)
        glast = each(lambda g_: g_[CHUNK - 1:CHUNK, :], gx)
        kb = each(jnp.multiply, kc, bx)
        vb = each(jnp.multiply, vc, bx)
        wr = each(jnp.multiply, kb, eg_c)
        qd = each(jnp.multiply, qc, eg_c)
        for i in ids:
            kd_buf[wslot, i] = (kc[i] * jnp.exp(glast[i] - gx[i])).astype(BF16)
            dec_buf[wslot, i] = jnp.broadcast_to(jnp.exp(glast[i]), (8, GROUP_W))
        dm = each(lambda g_: jnp.exp(jnp.where(causal, g_ - jnp.sum(g_ * eye_x, axis=0, keepdims=True), NEG_BIG)),
                  gx)
        yield
        aq = each(lambda i, a, b_, k_: _dot_nt(stack(a, b_).astype(BF16), bdr(i, 1, k_)), ids, kb, qc, kc)
        xm = each(lambda a, d_: -(a[0:CHUNK] * d_ * strict_x), aq, dm)
        for i in ids:
            qk_buf[wslot, i] = (aq[i][CHUNK:] * dm[i]).astype(BF16)
        xd = each(lambda x_: x_ * base_x, xm)
        tinv = each(lambda x_: eye_x + x_, xd)
        yield
        pw = each(lambda i, x_: mm(x_, bdr(i, 0, x_)), ids, xd)
        yield
        r = each(lambda i, t_, p_: mm(stack(t_, p_), bdr(i, 0, p_)), ids, tinv, pw)
        tinv = each(lambda t_, r_: t_ + r_[0:CHUNK], tinv, r)
        pw = each(lambda r_: r_[CHUNK:], r)
        yield
        tinv = each(lambda i, t_, p_: t_ + mm(t_, bdr(i, 0, p_)), ids, tinv, pw)
        yield
        for lm in level_x:
            e = each(lambda i, x_, t_: mm(x_ * lm, bdr(i, 0, t_)), ids, xm, tinv)
            yield
            tinv = each(lambda i, t_, e_: t_ + mm(t_, bdr(i, 0, e_)), ids, tinv, e)
            yield
        for i in ids:
            u_buf[wslot, i] = mm(tinv[i], bdr(i, 2, vb[i]))
            wq_buf[wslot, i, 0:CHUNK] = mm(tinv[i], bdr(i, 3, wr[i])).astype(BF16)
            wq_buf[wslot, i, CHUNK:] = qd[i].astype(BF16)

    def sequential_part():
        s_state = [jnp.where(starts_sequence, 0.0, s_ref[gi]) for gi in range(n_groups)]
        for c in range(n_chunks):
            cids = [c * n_groups + gi for gi in range(n_groups)]
            r2 = [_dot(wq_buf[rslot, i], s_state[gi].astype(BF16)) for gi, i in enumerate(cids)]
            yield
            vn = [u_buf[rslot, i] - r2[gi][0:CHUNK] for gi, i in enumerate(cids)]
            o = [r2[gi][CHUNK:] + _dot(qk_buf[rslot, i], bdr(GDN_UNITS, gi, vn[gi])) for gi, i in enumerate(cids)]
            s_state = [s_state[gi] * dec_buf[rslot, i][0:1] + mask_bd * _dot_tn(kd_buf[rslot, i], vn[gi].astype(BF16))
                       for gi, i in enumerate(cids)]
            for gi in range(n_groups):
                obuf[c * CHUNK:(c + 1) * CHUNK, gi * GROUP_W:(gi + 1) * GROUP_W] = o[gi]
            yield
        for gi in range(n_groups):
            s_ref[gi] = s_state[gi]

    parts = [sequential_part(), chunk_parallel_part()]
    while parts:
        parts = [p for p in parts if next(p, _DONE) is not _DONE]

    o = obuf[...]
    ms = _dot((o * o).astype(BF16), bd) * (1.0 / HEAD_DIM)
    zz = z_ref[0]
    o_ref[0] = (o * lax.rsqrt(ms + EPS) * gnw_ref[...] * (zz * _sigmoid(zz))).astype(BF16)


def _gdn_consts():
    h = np.arange(GDN_WIDTH) // HEAD_DIM
    eg = np.zeros((GATE_PAD, GDN_WIDTH), np.float32)
    eb = np.zeros((GATE_PAD, GDN_WIDTH), np.float32)
    eg[h, np.arange(GDN_WIDTH)] = 1.0
    eb[GDN_HEADS + h, np.arange(GDN_WIDTH)] = 1.0
    t = np.arange(GDN_TT)
    ltri = ((t[:, None] // CHUNK == t[None, :] // CHUNK) & (t[:, None] >= t[None, :])).astype(np.float32)
    return (_head_sum_matrix(GDN_WIDTH), jnp.asarray(eg, BF16), jnp.asarray(eb, BF16), jnp.asarray(ltri, BF16))


BF16_ROWS = 16


def _swa_head_source(i):
    j = i - GDN_HEADS
    return jnp.where(i < GDN_HEADS, i, GDN_HEADS + (j % 2) * SWA_GROUP + j // 2)


def _cast_specs(weight, unit, steps, source=None):
    rows, cols = weight.shape
    block_rows = next(r for r in range(unit, rows + 1, unit) if rows % r == 0 and rows // r <= steps)
    if source is None:
        source = lambda i: i
    else:
        assert block_rows == unit, "row regrouping needs one unit per block"
    last = rows // block_rows - 1
    src = pl.BlockSpec((block_rows, cols), lambda n: (source(jnp.minimum(n, last)), 0))
    dst = pl.BlockSpec((block_rows, cols), lambda n: (jnp.minimum(n, last), 0))
    return src, dst, jax.ShapeDtypeStruct((rows, cols), BF16)


def _gdn_call(qkv, z, ab, alog_pad, dtb_pad, gnw_x, w_gate, w_up, w_down, w_out):
    b, t, _ = qkv.shape
    tt = GDN_TT
    bd, eg, eb, ltri = _gdn_consts()
    n_tiles = t // tt
    n_groups = GDN_HEADS // HEADS_PER_GROUP
    total = b * n_tiles

    def tile_block(width, lag):
        def index(n):
            m = jnp.clip(n - lag, 0, total - 1)
            return (m // n_tiles, m % n_tiles, 0)
        return pl.BlockSpec((1, tt, width), index)

    ahead = lambda width: tile_block(width, 0)
    behind = lambda width: tile_block(width, 1)
    const = lambda shape: pl.BlockSpec(shape, lambda n: (0,) * len(shape))
    per_unit = lambda rows, dtype: pltpu.VMEM((2, GDN_UNITS, rows, GROUP_W), dtype)
    steps = total + 1
    casts = [
        _cast_specs(w_gate, BF16_ROWS, steps),
        _cast_specs(w_up, BF16_ROWS, steps),
        _cast_specs(w_down, BF16_ROWS, steps),
        _cast_specs(w_out, HEAD_DIM, steps, _swa_head_source),
    ]
    return pl.pallas_call(
        functools.partial(_gdn_kernel, tiles_per_seq=n_tiles),
        grid=(steps,),
        in_specs=[
            ahead(QKV_W), behind(GDN_WIDTH), ahead(GATE_PAD),
            const((1, GATE_PAD)), const((1, GATE_PAD)), const((1, GDN_WIDTH)),
            const((GDN_WIDTH, GDN_WIDTH)), const((GATE_PAD, GDN_WIDTH)), const((GATE_PAD, GDN_WIDTH)),
            const((tt, tt)),
        ] + [c[0] for c in casts],
        out_specs=[behind(GDN_WIDTH)] + [c[1] for c in casts],
        out_shape=[jax.ShapeDtypeStruct((b, t, GDN_WIDTH), BF16)] + [c[2] for c in casts],
        scratch_shapes=[
            pltpu.VMEM((n_groups, GROUP_W, GROUP_W), F32),
            pltpu.VMEM((tt, GDN_WIDTH), F32),
            pltpu.VMEM((GDN_UNITS * BD_SLOTS + n_groups, GROUP_W, GROUP_W), BF16),
            per_unit(2 * CHUNK, BF16), per_unit(CHUNK, F32), per_unit(CHUNK, BF16), per_unit(CHUNK, BF16),
            per_unit(8, F32),
        ],
        compiler_params=pltpu.CompilerParams(
            dimension_semantics=("arbitrary",), vmem_limit_bytes=VMEM_LIMIT),
        name="gdn",
    )(qkv, z, ab, alog_pad, dtb_pad, gnw_x, bd, eg, eb, ltri, w_gate, w_up, w_down, w_out)


SWA_PAIRS = SWA_Q_HEADS // 2
SWA_ROWS = SWA_Q_HEADS * WINDOW
LOG2E = 1.4426950408889634


def _swa_kernel(q_ref, kvc_ref, kvp_ref, qw_ref, kw_ref, bdq_ref, bdk_ref, bias_ref, sink_ref, o_ref):
    tq = SWA_TQ
    ti = pl.program_id(1)
    q = q_ref[0]
    q = q * lax.rsqrt(_dot((q * q).astype(BF16), bdq_ref[...]) * (1.0 / HEAD_DIM) + EPS)
    q = q * (qw_ref[...] * (HEAD_DIM ** -0.5 * LOG2E))
    kv = jnp.concatenate([kvp_ref[0], kvc_ref[0]], axis=0)
    k = kv[:, 0:SWA_KV_WIDTH]
    v = kv[:, SWA_KV_WIDTH:]
    k = k * lax.rsqrt(_dot((k * k).astype(BF16), bdk_ref[...]) * (1.0 / HEAD_DIM) + EPS) * kw_ref[...]
    k16 = k.astype(BF16)
    lo = lax.broadcasted_iota(jnp.int32, (1, LANES), 1) < HEAD_DIM
    qcol = lax.broadcasted_iota(jnp.int32, (WINDOW, SWA_ROWS), 1) % WINDOW
    from_prev = lax.broadcasted_iota(jnp.int32, (WINDOW, SWA_ROWS), 0) > qcol
    first = jnp.where(ti == 0, 1, 0)
    sink = sink_ref[...]
    vt16 = v.T.astype(BF16)
    zero = jnp.zeros((), BF16)
    for j in range(tq // WINDOW):
        qj = q[j * WINDOW:(j + 1) * WINDOW]
        parts = []
        for p in range(SWA_PAIRS):
            qp = qj[:, p * LANES:(p + 1) * LANES]
            parts += [jnp.where(lo, qp, 0.0), jnp.where(lo, 0.0, qp)]
        qs = jnp.concatenate(parts, axis=0).astype(BF16)
        st = _dot_nt(k16[j * WINDOW:(j + 2) * WINDOW], qs)
        bias = bias_ref[first] if j == 0 else bias_ref[0]
        sm = jnp.where(from_prev, st[0:WINDOW], st[WINDOW:]) + bias
        m = jnp.max(sm, axis=0, keepdims=True)
        pe = jnp.exp2(sm - m)
        den = jnp.sum(pe, axis=0, keepdims=True) + jnp.exp2(sink - m)
        pb = pe.astype(BF16)
        pt2 = jnp.concatenate([jnp.where(from_prev, pb, zero), jnp.where(from_prev, zero, pb)], axis=0)
        ot = _dot(vt16[:, j * WINDOW:(j + 2) * WINDOW], pt2) * (1.0 / den)
        for p in range(SWA_PAIRS):
            c0 = p * 2 * WINDOW
            pair_t = jnp.concatenate([ot[0:HEAD_DIM, c0:c0 + WINDOW],
                                      ot[HEAD_DIM:, c0 + WINDOW:c0 + 2 * WINDOW]], axis=0)
            o_ref[0, j * WINDOW:(j + 1) * WINDOW, p * LANES:(p + 1) * LANES] = pair_t.T.astype(BF16)


def _swa_consts():
    qi = np.arange(WINDOW)[:, None]
    kj = np.arange(WINDOW)[None, :]
    from_prev = kj > qi
    dist = np.where(from_prev, qi + WINDOW - kj, qi - kj).astype(np.float32)
    slopes = 2.0 ** (-8.0 * (np.arange(SWA_Q_HEADS, dtype=np.float32) + 1.0) / SWA_Q_HEADS)
    bias = np.zeros((2, SWA_PAIRS, 2, WINDOW, WINDOW), np.float32)
    for first in range(2):
        for p in range(SWA_PAIRS):
            for half, head in enumerate((p, SWA_GROUP + p)):
                b = (-slopes[head].astype(np.float32) * dist) * np.float32(LOG2E)
                bias[first, p, half] = np.where(from_prev & bool(first), np.float32(NEG_BIG), b)
    bias = bias.reshape(2, SWA_ROWS, WINDOW).transpose(0, 2, 1)
    h = np.arange(SWA_WIDTH) // HEAD_DIM
    bdq = (h[:, None] == h[None, :]).astype(np.float32)
    hk = np.arange(SWA_KV_WIDTH) // HEAD_DIM
    bdk = (hk[:, None] == hk[None, :]).astype(np.float32)
    return jnp.asarray(bias), jnp.asarray(bdq, BF16), jnp.asarray(bdk, BF16)


def _swa_call(sq, skv, qw_x, kw_x, sink_col):
    b, t, _ = sq.shape
    tq = SWA_TQ
    nb = tq // WINDOW
    bias, bdq, bdk = _swa_consts()
    const = lambda shape: pl.BlockSpec(shape, lambda bi, ti: (0,) * len(shape))
    return pl.pallas_call(
        _swa_kernel,
        grid=(b, t // tq),
        in_specs=[
            pl.BlockSpec((1, tq, SWA_WIDTH), lambda bi, ti: (bi, ti, 0)),
            pl.BlockSpec((1, tq, 2 * SWA_KV_WIDTH), lambda bi, ti: (bi, ti, 0)),
            pl.BlockSpec((1, WINDOW, 2 * SWA_KV_WIDTH), lambda bi, ti: (bi, jnp.maximum(ti * nb - 1, 0), 0)),
            const((1, SWA_WIDTH)), const((1, SWA_KV_WIDTH)),
            const((SWA_WIDTH, SWA_WIDTH)), const((SWA_KV_WIDTH, SWA_KV_WIDTH)),
            const((2, WINDOW, SWA_ROWS)),
            const((1, SWA_ROWS)),
        ],
        out_specs=pl.BlockSpec((1, tq, SWA_WIDTH), lambda bi, ti: (bi, ti, 0)),
        out_shape=jax.ShapeDtypeStruct((b, t, SWA_WIDTH), BF16),
        compiler_params=pltpu.CompilerParams(
            dimension_semantics=("arbitrary", "arbitrary"), vmem_limit_bytes=VMEM_LIMIT),
        name="swa",
    )(sq, skv, skv, qw_x, kw_x, bdq, bdk, bias, sink_col)


def _ffn_kernel(x_ref, og_ref, os_ref, mod_ref, nw_ref, wo_ref, wg_ref, wu_ref, wd_ref, o_ref):
    x = x_ref[0]
    mod = mod_ref[0]
    gate1 = mod[:, 2 * D_MODEL:3 * D_MODEL]
    shift2 = mod[:, 3 * D_MODEL:4 * D_MODEL]
    scale2 = mod[:, 4 * D_MODEL:5 * D_MODEL]
    gate2 = mod[:, 5 * D_MODEL:]
    mixed = _dot(og_ref[0], wo_ref[0:GDN_WIDTH, :]) + _dot(os_ref[0], wo_ref[GDN_WIDTH:, :])
    x1 = x + gate1 * mixed
    ms = jnp.mean(x1 * x1, axis=-1, keepdims=True)
    hb = ((x1 * lax.rsqrt(ms + EPS) * nw_ref[...]) * (1.0 + scale2) + shift2).astype(BF16)
    gt = _dot(hb, wg_ref[...])
    up = _dot(hb, wu_ref[...])
    act = ((gt * _sigmoid(gt)) * up).astype(BF16)
    o_ref[0] = x1 + gate2 * _dot(act, wd_ref[...])


def _ffn_call(x, og, osw, mod3, norm_w, wo, wg, wu, wd):
    b, t, _ = x.shape
    tm = FFN_TM
    row = lambda width: pl.BlockSpec((1, tm, width), lambda bi, ti: (bi, ti, 0))
    const = lambda shape: pl.BlockSpec(shape, lambda bi, ti: (0,) * len(shape), pipeline_mode=pl.Buffered(1))
    return pl.pallas_call(
        _ffn_kernel,
        grid=(b, t // tm),
        in_specs=[
            row(D_MODEL), row(GDN_WIDTH), row(SWA_WIDTH),
            pl.BlockSpec((1, 1, 6 * D_MODEL), lambda bi, ti: (bi, 0, 0)),
            const((1, D_MODEL)),
            const((GDN_WIDTH + SWA_WIDTH, D_MODEL)),
            const((D_MODEL, D_FF)), const((D_MODEL, D_FF)), const((D_FF, D_MODEL)),
        ],
        out_specs=row(D_MODEL),
        out_shape=jax.ShapeDtypeStruct((b, t, D_MODEL), F32),
        compiler_params=pltpu.CompilerParams(
            dimension_semantics=("arbitrary", "arbitrary"), vmem_limit_bytes=VMEM_LIMIT),
        name="ffn",
    )(x, og, osw, mod3, norm_w, wo, wg, wu, wd)


def _layer(x, c_pad, w_ada, b_ada, norm1_w, w_in, conv_w, a_log, dt_bias, gdn_norm_w, q_norm_w, k_norm_w, sinks,
           w_out, norm2_w, w_gate, w_up, w_down):
    b = x.shape[0]
    mod, w_aligned = _ada_call(c_pad, w_ada, b_ada.reshape(1, -1), w_in)
    o0 = PROJ_ALIGNED
    o1 = o0 + 2 * GDN_HEADS
    o2 = o1 + SWA_WIDTH
    sq_cols = w_in[:, o1:o2].reshape(D_MODEL, SWA_KV_HEADS, SWA_GROUP, HEAD_DIM)
    sq_cols = sq_cols.transpose(0, 2, 1, 3).reshape(D_MODEL, SWA_WIDTH)
    w_regrouped = jnp.concatenate([
        sq_cols,
        w_in[:, o2:],
        jnp.pad(w_in[:, o0:o1], ((0, 0), (0, GATE_PAD - 2 * GDN_HEADS))),
    ], axis=1).astype(BF16)
    mod3 = mod[:b].reshape(b, 1, 6 * D_MODEL)
    qkv, z, sq, skv, ab = _proj_call(x, mod3, norm1_w.reshape(1, D_MODEL), w_aligned, w_regrouped,
                                     conv_w.reshape(CONV_WIDTH, QKV_W))

    alog_pad = jnp.pad(a_log.reshape(1, GDN_HEADS), ((0, 0), (0, GATE_PAD - GDN_HEADS)))
    dtb_pad = jnp.pad(dt_bias.reshape(1, GDN_HEADS), ((0, 0), (0, GATE_PAD - GDN_HEADS)))
    gnw_x = jnp.tile(gdn_norm_w.reshape(1, HEAD_DIM), (1, GDN_HEADS))
    og, wg16, wu16, wd16, wo16 = _gdn_call(qkv, z, ab, alog_pad, dtb_pad, gnw_x, w_gate, w_up, w_down, w_out)

    qw_x = jnp.tile(q_norm_w.reshape(1, HEAD_DIM), (1, SWA_Q_HEADS))
    kw_x = jnp.tile(k_norm_w.reshape(1, HEAD_DIM), (1, SWA_KV_HEADS))
    sink_pairs = jnp.stack([sinks[:SWA_GROUP], sinks[SWA_GROUP:]], axis=1)
    sink_col = (jnp.repeat(sink_pairs, WINDOW, axis=1) * LOG2E).reshape(1, SWA_ROWS)
    osw = _swa_call(sq, skv, qw_x, kw_x, sink_col)

    return _ffn_call(x, og, osw, mod3, norm2_w.reshape(1, D_MODEL), wo16, wg16, wu16, wd16)


def kernel(x, c, w_ada, b_ada, norm1_w, w_in, conv_w, a_log, dt_bias, gdn_norm_w, q_norm_w, k_norm_w, sinks,
           w_out, norm2_w, w_gate, w_up, w_down):
    depth = w_ada.shape[0]
    b = c.shape[0]
    c_pad = jnp.pad(c, ((0, 8 - b), (0, 0)))
    for l in range(depth):
        x = _layer(x, c_pad, w_ada[l], b_ada[l], norm1_w[l], w_in[l], conv_w[l], a_log[l], dt_bias[l],
                   gdn_norm_w[l], q_norm_w[l], k_norm_w[l], sinks[l], w_out[l], norm2_w[l], w_gate[l], w_up[l],
                   w_down[l])
    return x
```

```python
import functools

import numpy as np
import jax
import jax.numpy as jnp
from jax import lax
from jax.experimental import pallas as pl
from jax.experimental.pallas import tpu as pltpu

F32 = jnp.float32
BF16 = jnp.bfloat16

D_MODEL = 1024
HEAD_DIM = 64
GDN_HEADS = 8
GDN_WIDTH = GDN_HEADS * HEAD_DIM
SWA_Q_HEADS = 8
SWA_KV_HEADS = 2
SWA_GROUP = SWA_Q_HEADS // SWA_KV_HEADS
SWA_WIDTH = SWA_Q_HEADS * HEAD_DIM
SWA_KV_WIDTH = SWA_KV_HEADS * HEAD_DIM
WINDOW = 128
CONV_WIDTH = 4
CHUNK = 64
D_FF = 2816
EPS = 1e-6
LANES = 128
GATE_PAD = LANES
HEADS_PER_GROUP = 2
GROUP_W = HEADS_PER_GROUP * HEAD_DIM
INV_BASE = 8
INV_LEVELS = (8, 16, 32)
NEG_BIG = -1e30
VMEM_LIMIT = 56 * 1024 * 1024

PROJ_TM = 512
GDN_TT = 256
SWA_TQ = 1024
FFN_TM = 512
ADA_TN = 1536


def _sigmoid(x):
    return 1.0 / (1.0 + jnp.exp(-x))


def _dot(a, b):
    return jnp.dot(a, b, preferred_element_type=F32)


def _dot_nt(a, b):
    return lax.dot_general(a, b, (((1,), (1,)), ((), ())), preferred_element_type=F32)


def _dot_tn(a, b):
    return lax.dot_general(a, b, (((0,), (0,)), ((), ())), preferred_element_type=F32)


def _split3(x):
    x1 = x.astype(BF16)
    r1 = x - x1.astype(F32)
    x2 = r1.astype(BF16)
    r2 = r1 - x2.astype(F32)
    x3 = r2.astype(BF16)
    return x1, x2, x3


def _ada_kernel(c_ref, w_ref, b_ref, win_ref, o_ref, win16_ref):
    c = c_ref[...]
    ca = c * _sigmoid(c)
    o_ref[...] = _dot(ca.astype(BF16), w_ref[...].astype(BF16)) + b_ref[...]
    win16_ref[...] = win_ref[...].astype(BF16)


def _ada_call(c_pad, w_ada, b_ada, w_in):
    n = w_ada.shape[1]
    steps = n // ADA_TN
    tw = PROJ_ALIGNED // steps
    return pl.pallas_call(
        _ada_kernel,
        grid=(steps,),
        in_specs=[
            pl.BlockSpec((8, D_MODEL), lambda j: (0, 0)),
            pl.BlockSpec((D_MODEL, ADA_TN), lambda j: (0, j)),
            pl.BlockSpec((1, ADA_TN), lambda j: (0, j)),
            pl.BlockSpec((D_MODEL, tw), lambda j: (0, j)),
        ],
        out_specs=[pl.BlockSpec((8, ADA_TN), lambda j: (0, j)), pl.BlockSpec((D_MODEL, tw), lambda j: (0, j))],
        out_shape=[jax.ShapeDtypeStruct((8, n), F32), jax.ShapeDtypeStruct((D_MODEL, PROJ_ALIGNED), BF16)],
        compiler_params=pltpu.CompilerParams(
            dimension_semantics=("arbitrary",), vmem_limit_bytes=VMEM_LIMIT),
        name="ada",
    )(c_pad, w_ada, b_ada, w_in)


QKV_W = 3 * GDN_WIDTH
PROJ_SPLITS = (QKV_W, GDN_WIDTH, SWA_WIDTH, 2 * SWA_KV_WIDTH, GATE_PAD)
PROJ_ALIGNED = QKV_W + GDN_WIDTH
PROJ_REGROUPED = sum(PROJ_SPLITS) - PROJ_ALIGNED


def _proj_kernel(x_ref, mod_ref, nw_ref, wa_ref, wb_ref, cw_ref, bd_ref, qkv_ref, z_ref, sq_ref, skv_ref, ab_ref,
                 qbuf, kbuf, vbuf):
    tm = PROJ_TM
    bufs = (qbuf, kbuf, vbuf)

    @pl.when(pl.program_id(1) == 0)
    def _():
        for buf in bufs:
            buf[0:8, :] = jnp.zeros((8, GDN_WIDTH), F32)

    x = x_ref[0]
    ms = jnp.mean(x * x, axis=-1, keepdims=True)
    y = x * lax.rsqrt(ms + EPS) * nw_ref[...]
    mod = mod_ref[0]
    shift = mod[:, 0:D_MODEL]
    scale = mod[:, D_MODEL:2 * D_MODEL]
    hb = (y * (1.0 + scale) + shift).astype(BF16)
    def project(s):
        bufs[s][8:8 + tm, :] = _dot(hb, wa_ref[:, s * GDN_WIDTH:(s + 1) * GDN_WIDTH])

    def conv_silu(s):
        buf = bufs[s]
        cw = lambda j: cw_ref[j:j + 1, s * GDN_WIDTH:(s + 1) * GDN_WIDTH]
        acc = cw(CONV_WIDTH - 1) * buf[8:8 + tm, :]
        for j in range(CONV_WIDTH - 1):
            off = 8 - (CONV_WIDTH - 1) + j
            acc = acc + cw(j) * buf[off:off + tm, :]
        buf[0:8, :] = buf[tm:tm + 8, :]
        return acc * _sigmoid(acc)

    def l2_normed(a, scale):
        return a * lax.rsqrt(_dot((a * a).astype(BF16), bd_ref[...]) + EPS) * scale

    project(0)
    project(1)
    qkv_ref[0, :, 0:GDN_WIDTH] = l2_normed(conv_silu(0), HEAD_DIM ** -0.5)
    project(2)
    qkv_ref[0, :, GDN_WIDTH:2 * GDN_WIDTH] = l2_normed(conv_silu(1), 1.0)
    z_ref[0] = _dot(hb, wa_ref[:, QKV_W:])
    qkv_ref[0, :, 2 * GDN_WIDTH:] = conv_silu(2)
    col = 0
    for ref in (sq_ref, skv_ref, ab_ref):
        ref[0] = _dot(hb, wb_ref[:, col:col + ref.shape[-1]])
        col += ref.shape[-1]


def _head_sum_matrix(width):
    h = np.arange(width) // HEAD_DIM
    return jnp.asarray((h[:, None] == h[None, :]).astype(np.float32), BF16)


def _proj_call(x, mod3, norm_w, w_aligned, w_regrouped, conv_w):
    b, t, _ = x.shape
    tm = PROJ_TM
    row = lambda width: pl.BlockSpec((1, tm, width), lambda bi, ti: (bi, ti, 0))
    const = lambda shape: pl.BlockSpec(shape, lambda bi, ti: (0,) * len(shape))
    return pl.pallas_call(
        _proj_kernel,
        grid=(b, t // tm),
        in_specs=[
            row(D_MODEL),
            pl.BlockSpec((1, 1, 6 * D_MODEL), lambda bi, ti: (bi, 0, 0)),
            const((1, D_MODEL)),
            const((D_MODEL, PROJ_ALIGNED)),
            const((D_MODEL, PROJ_REGROUPED)),
            const((CONV_WIDTH, QKV_W)),
            const((GDN_WIDTH, GDN_WIDTH)),
        ],
        out_specs=[row(w) for w in PROJ_SPLITS],
        out_shape=[jax.ShapeDtypeStruct((b, t, w), F32) for w in PROJ_SPLITS],
        scratch_shapes=[pltpu.VMEM((8 + tm, GDN_WIDTH), F32)] * 3,
        compiler_params=pltpu.CompilerParams(
            dimension_semantics=("arbitrary", "arbitrary"), vmem_limit_bytes=VMEM_LIMIT),
        name="proj",
    )(x, mod3, norm_w, w_aligned, w_regrouped, conv_w, _head_sum_matrix(GDN_WIDTH))


GDN_UNITS = (GDN_TT // CHUNK) * (GDN_HEADS // HEADS_PER_GROUP)
BD_SLOTS = 4


def _block_diag_rows(buf, slot, p):
    p16 = p.astype(BF16)
    for h in range(HEADS_PER_GROUP):
        buf[slot, h * HEAD_DIM:(h + 1) * HEAD_DIM, h * HEAD_DIM:(h + 1) * HEAD_DIM] = (
            p16[:, h * HEAD_DIM:(h + 1) * HEAD_DIM])
    return buf[slot]


_DONE = object()


def _gdn_kernel(qkv_ref, z_ref, ab_ref, alog_ref, dtb_ref, gnw_ref, bd_ref, eg_ref, eb_ref, ltri_ref,
                wg_ref, wu_ref, wd_ref, wo_ref, o_ref, wg16_ref, wu16_ref, wd16_ref, wo16_ref,
                s_ref, obuf, bdbuf, wq_buf, u_buf, qk_buf, kd_buf, dec_buf, *, tiles_per_seq):
    tt = GDN_TT
    step = pl.program_id(0)
    for src, dst in ((wg_ref, wg16_ref), (wu_ref, wu16_ref), (wd_ref, wd16_ref), (wo_ref, wo16_ref)):
        dst[...] = src[...].astype(BF16)
    wslot = step % 2
    rslot = 1 - wslot
    starts_sequence = (step - 1) % tiles_per_seq == 0

    @pl.when(step == 0)
    def _():
        for buf in (s_ref, bdbuf, wq_buf, u_buf, qk_buf, kd_buf, dec_buf):
            buf[...] = jnp.zeros_like(buf)

    bd = bd_ref[...]
    q = qkv_ref[0, :, 0:GDN_WIDTH]
    k = qkv_ref[0, :, GDN_WIDTH:2 * GDN_WIDTH]
    v = qkv_ref[0, :, 2 * GDN_WIDTH:]

    ab = ab_ref[0]
    lane = lax.broadcasted_iota(jnp.int32, (1, GATE_PAD), 1)
    xs = ab + dtb_ref[...]
    softplus = jnp.maximum(xs, 0.0) + jnp.log1p(jnp.exp(-jnp.abs(xs)))
    g = jnp.where(lane < GDN_HEADS, -jnp.exp(alog_ref[...]) * softplus, 0.0)
    beta = _sigmoid(ab)
    ltri = ltri_ref[...]
    g1, g2, g3 = _split3(g)
    gcum = _dot(ltri, g1) + _dot(ltri, g2) + _dot(ltri, g3)
    eg = eg_ref[...]
    c1, c2, c3 = _split3(gcum)
    g_x = _dot(c1, eg) + _dot(c2, eg) + _dot(c3, eg)
    beta_x = _dot(beta.astype(BF16), eb_ref[...])

    r64 = lax.broadcasted_iota(jnp.int32, (CHUNK, GROUP_W), 0)
    l64 = lax.broadcasted_iota(jnp.int32, (CHUNK, GROUP_W), 1) % CHUNK
    causal = r64 >= l64
    eye_x = (r64 == l64).astype(F32)
    strict_x = (r64 > l64).astype(F32)
    same_block = lambda size: (r64 // size) == (l64 // size)
    base_x = same_block(INV_BASE).astype(F32)
    level_x = [(same_block(2 * size) & ~same_block(size)).astype(F32) for size in INV_LEVELS]
    rb = lax.broadcasted_iota(jnp.int32, (GROUP_W, GROUP_W), 0) // HEAD_DIM
    cb = lax.broadcasted_iota(jnp.int32, (GROUP_W, GROUP_W), 1) // HEAD_DIM
    mask_bd = (rb == cb).astype(F32)

    n_groups = GDN_HEADS // HEADS_PER_GROUP
    n_chunks = tt // CHUNK
    units = [(c, gi) for c in range(n_chunks) for gi in range(n_groups)]
    ids = list(range(len(units)))
    tile = lambda a: [a[c * CHUNK:(c + 1) * CHUNK, gi * GROUP_W:(gi + 1) * GROUP_W] for c, gi in units]
    each = lambda f, *lists: [f(*args) for args in zip(*lists)]
    bdr = lambda i, slot, t: _block_diag_rows(bdbuf, i * BD_SLOTS + slot, t)
    mm = lambda a, w16: _dot(a.astype(BF16), w16)
    stack = lambda a, b_: jnp.concatenate([a, b_], axis=0)

    def chunk_parallel_part():
        kc, qc, vc, bx, gx = tile(k), tile(q), tile(v), tile(beta_x), tile(g_x)
        eg_c = each(jnp.exp, gx)
        glast = each(lambda g_: g_[CHUNK - 1:CHUNK, :], gx)
        kb = each(jnp.multiply, kc, bx)
        vb = each(jnp.multiply, vc, bx)
        wr = each(jnp.multiply, kb, eg_c)
        qd = each(jnp.multiply, qc, eg_c)
        for i in ids:
            kd_buf[wslot, i] = (kc[i] * jnp.exp(glast[i] - gx[i])).astype(BF16)
            dec_buf[wslot, i] = jnp.broadcast_to(jnp.exp(glast[i]), (8, GROUP_W))
        dm = each(lambda g_: jnp.exp(jnp.where(causal, g_ - jnp.sum(g_ * eye_x, axis=0, keepdims=True), NEG_BIG)),
                  gx)
        yield
        aq = each(lambda i, a, b_, k_: _dot_nt(stack(a, b_).astype(BF16), bdr(i, 1, k_)), ids, kb, qc, kc)
        xm = each(lambda a, d_: -(a[0:CHUNK] * d_ * strict_x), aq, dm)
        for i in ids:
            qk_buf[wslot, i] = (aq[i][CHUNK:] * dm[i]).astype(BF16)
        xd = each(lambda x_: x_ * base_x, xm)
        tinv = each(lambda x_: eye_x + x_, xd)
        yield
        pw = each(lambda i, x_: mm(x_, bdr(i, 0, x_)), ids, xd)
        yield
        r = each(lambda i, t_, p_: mm(stack(t_, p_), bdr(i, 0, p_)), ids, tinv, pw)
        tinv = each(lambda t_, r_: t_ + r_[0:CHUNK], tinv, r)
        pw = each(lambda r_: r_[CHUNK:], r)
        yield
        tinv = each(lambda i, t_, p_: t_ + mm(t_, bdr(i, 0, p_)), ids, tinv, pw)
        yield
        for lm in level_x:
            e = each(lambda i, x_, t_: mm(x_ * lm, bdr(i, 0, t_)), ids, xm, tinv)
            yield
            tinv = each(lambda i, t_, e_: t_ + mm(t_, bdr(i, 0, e_)), ids, tinv, e)
            yield
        for i in ids:
            u_buf[wslot, i] = mm(tinv[i], bdr(i, 2, vb[i]))
            wq_buf[wslot, i, 0:CHUNK] = mm(tinv[i], bdr(i, 3, wr[i])).astype(BF16)
            wq_buf[wslot, i, CHUNK:] = qd[i].astype(BF16)

    def sequential_part():
        s_state = [jnp.where(starts_sequence, 0.0, s_ref[gi]) for gi in range(n_groups)]
        for c in range(n_chunks):
            cids = [c * n_groups + gi for gi in range(n_groups)]
            r2 = [_dot(wq_buf[rslot, i], s_state[gi].astype(BF16)) for gi, i in enumerate(cids)]
            yield
            vn = [u_buf[rslot, i] - r2[gi][0:CHUNK] for gi, i in enumerate(cids)]
            o = [r2[gi][CHUNK:] + _dot(qk_buf[rslot, i], bdr(GDN_UNITS, gi, vn[gi])) for gi, i in enumerate(cids)]
            s_state = [s_state[gi] * dec_buf[rslot, i][0:1] + mask_bd * _dot_tn(kd_buf[rslot, i], vn[gi].astype(BF16))
                       for gi, i in enumerate(cids)]
            for gi in range(n_groups):
                obuf[c * CHUNK:(c + 1) * CHUNK, gi * GROUP_W:(gi + 1) * GROUP_W] = o[gi]
            yield
        for gi in range(n_groups):
            s_ref[gi] = s_state[gi]

    parts = [sequential_part(), chunk_parallel_part()]
    while parts:
        parts = [p for p in parts if next(p, _DONE) is not _DONE]

    o = obuf[...]
    ms = _dot((o * o).astype(BF16), bd) * (1.0 / HEAD_DIM)
    zz = z_ref[0]
    o_ref[0] = (o * lax.rsqrt(ms + EPS) * gnw_ref[...] * (zz * _sigmoid(zz))).astype(BF16)


def _gdn_consts():
    h = np.arange(GDN_WIDTH) // HEAD_DIM
    eg = np.zeros((GATE_PAD, GDN_WIDTH), np.float32)
    eb = np.zeros((GATE_PAD, GDN_WIDTH), np.float32)
    eg[h, np.arange(GDN_WIDTH)] = 1.0
    eb[GDN_HEADS + h, np.arange(GDN_WIDTH)] = 1.0
    t = np.arange(GDN_TT)
    ltri = ((t[:, None] // CHUNK == t[None, :] // CHUNK) & (t[:, None] >= t[None, :])).astype(np.float32)
    return (_head_sum_matrix(GDN_WIDTH), jnp.asarray(eg, BF16), jnp.asarray(eb, BF16), jnp.asarray(ltri, BF16))


BF16_ROWS = 16


def _swa_head_source(i):
    j = i - GDN_HEADS
    return jnp.where(i < GDN_HEADS, i, GDN_HEADS + (j % 2) * SWA_GROUP + j // 2)


def _cast_specs(weight, unit, steps, source=None):
    rows, cols = weight.shape
    block_rows = next(r for r in range(unit, rows + 1, unit) if rows % r == 0 and rows // r <= steps)
    if source is None:
        source = lambda i: i
    else:
        assert block_rows == unit, "row regrouping needs one unit per block"
    last = rows // block_rows - 1
    src = pl.BlockSpec((block_rows, cols), lambda n: (source(jnp.minimum(n, last)), 0))
    dst = pl.BlockSpec((block_rows, cols), lambda n: (jnp.minimum(n, last), 0))
    return src, dst, jax.ShapeDtypeStruct((rows, cols), BF16)


def _gdn_call(qkv, z, ab, alog_pad, dtb_pad, gnw_x, w_gate, w_up, w_down, w_out):
    b, t, _ = qkv.shape
    tt = GDN_TT
    bd, eg, eb, ltri = _gdn_consts()
    n_tiles = t // tt
    n_groups = GDN_HEADS // HEADS_PER_GROUP
    total = b * n_tiles

    def tile_block(width, lag):
        def index(n):
            m = jnp.clip(n - lag, 0, total - 1)
            return (m // n_tiles, m % n_tiles, 0)
        return pl.BlockSpec((1, tt, width), index)

    ahead = lambda width: tile_block(width, 0)
    behind = lambda width: tile_block(width, 1)
    const = lambda shape: pl.BlockSpec(shape, lambda n: (0,) * len(shape))
    per_unit = lambda rows, dtype: pltpu.VMEM((2, GDN_UNITS, rows, GROUP_W), dtype)
    steps = total + 1
    casts = [
        _cast_specs(w_gate, BF16_ROWS, steps),
        _cast_specs(w_up, BF16_ROWS, steps),
        _cast_specs(w_down, BF16_ROWS, steps),
        _cast_specs(w_out, HEAD_DIM, steps, _swa_head_source),
    ]
    return pl.pallas_call(
        functools.partial(_gdn_kernel, tiles_per_seq=n_tiles),
        grid=(steps,),
        in_specs=[
            ahead(QKV_W), behind(GDN_WIDTH), ahead(GATE_PAD),
            const((1, GATE_PAD)), const((1, GATE_PAD)), const((1, GDN_WIDTH)),
            const((GDN_WIDTH, GDN_WIDTH)), const((GATE_PAD, GDN_WIDTH)), const((GATE_PAD, GDN_WIDTH)),
            const((tt, tt)),
        ] + [c[0] for c in casts],
        out_specs=[behind(GDN_WIDTH)] + [c[1] for c in casts],
        out_shape=[jax.ShapeDtypeStruct((b, t, GDN_WIDTH), BF16)] + [c[2] for c in casts],
        scratch_shapes=[
            pltpu.VMEM((n_groups, GROUP_W, GROUP_W), F32),
            pltpu.VMEM((tt, GDN_WIDTH), F32),
            pltpu.VMEM((GDN_UNITS * BD_SLOTS + n_groups, GROUP_W, GROUP_W), BF16),
            per_unit(2 * CHUNK, BF16), per_unit(CHUNK, F32), per_unit(CHUNK, BF16), per_unit(CHUNK, BF16),
            per_unit(8, F32),
        ],
        compiler_params=pltpu.CompilerParams(
            dimension_semantics=("arbitrary",), vmem_limit_bytes=VMEM_LIMIT),
        name="gdn",
    )(qkv, z, ab, alog_pad, dtb_pad, gnw_x, bd, eg, eb, ltri, w_gate, w_up, w_down, w_out)


SWA_PAIRS = SWA_Q_HEADS // 2
SWA_ROWS = SWA_Q_HEADS * WINDOW
LOG2E = 1.4426950408889634


def _swa_kernel(q_ref, kvc_ref, kvp_ref, qw_ref, kw_ref, bdq_ref, bdk_ref, bias_ref, sink_ref, o_ref):
    tq = SWA_TQ
    ti = pl.program_id(1)
    q = q_ref[0]
    q = q * lax.rsqrt(_dot((q * q).astype(BF16), bdq_ref[...]) * (1.0 / HEAD_DIM) + EPS)
    q = q * (qw_ref[...] * (HEAD_DIM ** -0.5 * LOG2E))
    kv = jnp.concatenate([kvp_ref[0], kvc_ref[0]], axis=0)
    k = kv[:, 0:SWA_KV_WIDTH]
    v = kv[:, SWA_KV_WIDTH:]
    k = k * lax.rsqrt(_dot((k * k).astype(BF16), bdk_ref[...]) * (1.0 / HEAD_DIM) + EPS) * kw_ref[...]
    k16 = k.astype(BF16)
    lo = lax.broadcasted_iota(jnp.int32, (1, LANES), 1) < HEAD_DIM
    qcol = lax.broadcasted_iota(jnp.int32, (WINDOW, SWA_ROWS), 1) % WINDOW
    from_prev = lax.broadcasted_iota(jnp.int32, (WINDOW, SWA_ROWS), 0) > qcol
    first = jnp.where(ti == 0, 1, 0)
    sink = sink_ref[...]
    vt16 = v.T.astype(BF16)
    zero = jnp.zeros((), BF16)
    for j in range(tq // WINDOW):
        qj = q[j * WINDOW:(j + 1) * WINDOW]
        parts = []
        for p in range(SWA_PAIRS):
            qp = qj[:, p * LANES:(p + 1) * LANES]
            parts += [jnp.where(lo, qp, 0.0), jnp.where(lo, 0.0, qp)]
        qs = jnp.concatenate(parts, axis=0).astype(BF16)
        st = _dot_nt(k16[j * WINDOW:(j + 2) * WINDOW], qs)
        bias = bias_ref[first] if j == 0 else bias_ref[0]
        sm = jnp.where(from_prev, st[0:WINDOW], st[WINDOW:]) + bias
        m = jnp.max(sm, axis=0, keepdims=True)
        pe = jnp.exp2(sm - m)
        den = jnp.sum(pe, axis=0, keepdims=True) + jnp.exp2(sink - m)
        pb = pe.astype(BF16)
        pt2 = jnp.concatenate([jnp.where(from_prev, pb, zero), jnp.where(from_prev, zero, pb)], axis=0)
        ot = _dot(vt16[:, j * WINDOW:(j + 2) * WINDOW], pt2) * (1.0 / den)
        for p in range(SWA_PAIRS):
            c0 = p * 2 * WINDOW
            pair_t = jnp.concatenate([ot[0:HEAD_DIM, c0:c0 + WINDOW],
                                      ot[HEAD_DIM:, c0 + WINDOW:c0 + 2 * WINDOW]], axis=0)
            o_ref[0, j * WINDOW:(j + 1) * WINDOW, p * LANES:(p + 1) * LANES] = pair_t.T.astype(BF16)


def _swa_consts():
    qi = np.arange(WINDOW)[:, None]
    kj = np.arange(WINDOW)[None, :]
    from_prev = kj > qi
    dist = np.where(from_prev, qi + WINDOW - kj, qi - kj).astype(np.float32)
    slopes = 2.0 ** (-8.0 * (np.arange(SWA_Q_HEADS, dtype=np.float32) + 1.0) / SWA_Q_HEADS)
    bias = np.zeros((2, SWA_PAIRS, 2, WINDOW, WINDOW), np.float32)
    for first in range(2):
        for p in range(SWA_PAIRS):
            for half, head in enumerate((p, SWA_GROUP + p)):
                b = (-slopes[head].astype(np.float32) * dist) * np.float32(LOG2E)
                bias[first, p, half] = np.where(from_prev & bool(first), np.float32(NEG_BIG), b)
    bias = bias.reshape(2, SWA_ROWS, WINDOW).transpose(0, 2, 1)
    h = np.arange(SWA_WIDTH) // HEAD_DIM
    bdq = (h[:, None] == h[None, :]).astype(np.float32)
    hk = np.arange(SWA_KV_WIDTH) // HEAD_DIM
    bdk = (hk[:, None] == hk[None, :]).astype(np.float32)
    return jnp.asarray(bias), jnp.asarray(bdq, BF16), jnp.asarray(bdk, BF16)


def _swa_call(sq, skv, qw_x, kw_x, sink_col):
    b, t, _ = sq.shape
    tq = SWA_TQ
    nb = tq // WINDOW
    bias, bdq, bdk = _swa_consts()
    const = lambda shape: pl.BlockSpec(shape, lambda bi, ti: (0,) * len(shape))
    return pl.pallas_call(
        _swa_kernel,
        grid=(b, t // tq),
        in_specs=[
            pl.BlockSpec((1, tq, SWA_WIDTH), lambda bi, ti: (bi, ti, 0)),
            pl.BlockSpec((1, tq, 2 * SWA_KV_WIDTH), lambda bi, ti: (bi, ti, 0)),
            pl.BlockSpec((1, WINDOW, 2 * SWA_KV_WIDTH), lambda bi, ti: (bi, jnp.maximum(ti * nb - 1, 0), 0)),
            const((1, SWA_WIDTH)), const((1, SWA_KV_WIDTH)),
            const((SWA_WIDTH, SWA_WIDTH)), const((SWA_KV_WIDTH, SWA_KV_WIDTH)),
            const((2, WINDOW, SWA_ROWS)),
            const((1, SWA_ROWS)),
        ],
        out_specs=pl.BlockSpec((1, tq, SWA_WIDTH), lambda bi, ti: (bi, ti, 0)),
        out_shape=jax.ShapeDtypeStruct((b, t, SWA_WIDTH), BF16),
        compiler_params=pltpu.CompilerParams(
            dimension_semantics=("arbitrary", "arbitrary"), vmem_limit_bytes=VMEM_LIMIT),
        name="swa",
    )(sq, skv, skv, qw_x, kw_x, bdq, bdk, bias, sink_col)


def _ffn_kernel(x_ref, og_ref, os_ref, mod_ref, nw_ref, wo_ref, wg_ref, wu_ref, wd_ref, o_ref):
    x = x_ref[0]
    mod = mod_ref[0]
    gate1 = mod[:, 2 * D_MODEL:3 * D_MODEL]
    shift2 = mod[:, 3 * D_MODEL:4 * D_MODEL]
    scale2 = mod[:, 4 * D_MODEL:5 * D_MODEL]
    gate2 = mod[:, 5 * D_MODEL:]
    mixed = _dot(og_ref[0], wo_ref[0:GDN_WIDTH, :]) + _dot(os_ref[0], wo_ref[GDN_WIDTH:, :])
    x1 = x + gate1 * mixed
    ms = jnp.mean(x1 * x1, axis=-1, keepdims=True)
    hb = ((x1 * lax.rsqrt(ms + EPS) * nw_ref[...]) * (1.0 + scale2) + shift2).astype(BF16)
    gt = _dot(hb, wg_ref[...])
    up = _dot(hb, wu_ref[...])
    act = ((gt * _sigmoid(gt)) * up).astype(BF16)
    o_ref[0] = x1 + gate2 * _dot(act, wd_ref[...])


def _ffn_call(x, og, osw, mod3, norm_w, wo, wg, wu, wd):
    b, t, _ = x.shape
    tm = FFN_TM
    row = lambda width: pl.BlockSpec((1, tm, width), lambda bi, ti: (bi, ti, 0))
    const = lambda shape: pl.BlockSpec(shape, lambda bi, ti: (0,) * len(shape), pipeline_mode=pl.Buffered(1))
    return pl.pallas_call(
        _ffn_kernel,
        grid=(b, t // tm),
        in_specs=[
            row(D_MODEL), row(GDN_WIDTH), row(SWA_WIDTH),
            pl.BlockSpec((1, 1, 6 * D_MODEL), lambda bi, ti: (bi, 0, 0)),
            const((1, D_MODEL)),
            const((GDN_WIDTH + SWA_WIDTH, D_MODEL)),
            const((D_MODEL, D_FF)), const((D_MODEL, D_FF)), const((D_FF, D_MODEL)),
        ],
        out_specs=row(D_MODEL),
        out_shape=jax.ShapeDtypeStruct((b, t, D_MODEL), F32),
        compiler_params=pltpu.CompilerParams(
            dimension_semantics=("arbitrary", "arbitrary"), vmem_limit_bytes=VMEM_LIMIT),
        name="ffn",
    )(x, og, osw, mod3, norm_w, wo, wg, wu, wd)


def _layer(x, c_pad, w_ada, b_ada, norm1_w, w_in, conv_w, a_log, dt_bias, gdn_norm_w, q_norm_w, k_norm_w, sinks,
           w_out, norm2_w, w_gate, w_up, w_down):
    b = x.shape[0]
    mod, w_aligned = _ada_call(c_pad, w_ada, b_ada.reshape(1, -1), w_in)
    tail = lax.optimization_barrier(w_in[:, PROJ_ALIGNED:])
    o1 = 2 * GDN_HEADS
    o2 = o1 + SWA_WIDTH
    sq_cols = tail[:, o1:o2].reshape(D_MODEL, SWA_KV_HEADS, SWA_GROUP, HEAD_DIM)
    sq_cols = sq_cols.transpose(0, 2, 1, 3).reshape(D_MODEL, SWA_WIDTH)
    w_regrouped = jnp.concatenate([
        sq_cols,
        tail[:, o2:],
        jnp.pad(tail[:, 0:o1], ((0, 0), (0, GATE_PAD - 2 * GDN_HEADS))),
    ], axis=1).astype(BF16)
    mod3 = mod[:b].reshape(b, 1, 6 * D_MODEL)
    qkv, z, sq, skv, ab = _proj_call(x, mod3, norm1_w.reshape(1, D_MODEL), w_aligned, w_regrouped,
                                     conv_w.reshape(CONV_WIDTH, QKV_W))

    alog_pad = jnp.pad(a_log.reshape(1, GDN_HEADS), ((0, 0), (0, GATE_PAD - GDN_HEADS)))
    dtb_pad = jnp.pad(dt_bias.reshape(1, GDN_HEADS), ((0, 0), (0, GATE_PAD - GDN_HEADS)))
    gnw_x = jnp.tile(gdn_norm_w.reshape(1, HEAD_DIM), (1, GDN_HEADS))
    og, wg16, wu16, wd16, wo16 = _gdn_call(qkv, z, ab, alog_pad, dtb_pad, gnw_x, w_gate, w_up, w_down, w_out)

    qw_x = jnp.tile(q_norm_w.reshape(1, HEAD_DIM), (1, SWA_Q_HEADS))
    kw_x = jnp.tile(k_norm_w.reshape(1, HEAD_DIM), (1, SWA_KV_HEADS))
    sink_pairs = jnp.stack([sinks[:SWA_GROUP], sinks[SWA_GROUP:]], axis=1)
    sink_col = (jnp.repeat(sink_pairs, WINDOW, axis=1) * LOG2E).reshape(1, SWA_ROWS)
    osw = _swa_call(sq, skv, qw_x, kw_x, sink_col)

    return _ffn_call(x, og, osw, mod3, norm2_w.reshape(1, D_MODEL), wo16, wg16, wu16, wd16)


def kernel(x, c, w_ada, b_ada, norm1_w, w_in, conv_w, a_log, dt_bias, gdn_norm_w, q_norm_w, k_norm_w, sinks,
           w_out, norm2_w, w_gate, w_up, w_down):
    depth = w_ada.shape[0]
    b = c.shape[0]
    c_pad = jnp.pad(c, ((0, 8 - b), (0, 0)))
    for l in range(depth):
        x = _layer(x, c_pad, w_ada[l], b_ada[l], norm1_w[l], w_in[l], conv_w[l], a_log[l], dt_bias[l],
                   gdn_norm_w[l], q_norm_w[l], k_norm_w[l], sinks[l], w_out[l], norm2_w[l], w_gate[l], w_up[l],
                   w_down[l])
    return x
```

```python
import functools

import numpy as np
import jax
import jax.numpy as jnp
from jax import lax
from jax.experimental import pallas as pl
from jax.experimental.pallas import tpu as pltpu

F32 = jnp.float32
BF16 = jnp.bfloat16

D_MODEL = 1024
HEAD_DIM = 64
GDN_HEADS = 8
GDN_WIDTH = GDN_HEADS * HEAD_DIM
SWA_Q_HEADS = 8
SWA_KV_HEADS = 2
SWA_GROUP = SWA_Q_HEADS // SWA_KV_HEADS
SWA_WIDTH = SWA_Q_HEADS * HEAD_DIM
SWA_KV_WIDTH = SWA_KV_HEADS * HEAD_DIM
WINDOW = 128
CONV_WIDTH = 4
CHUNK = 64
D_FF = 2816
EPS = 1e-6
LANES = 128
GATE_PAD = LANES
HEADS_PER_GROUP = 2
GROUP_W = HEADS_PER_GROUP * HEAD_DIM
INV_BASE = 8
INV_LEVELS = (8, 16, 32)
NEG_BIG = -1e30
VMEM_LIMIT = 56 * 1024 * 1024

PROJ_TM = 512
GDN_TT = 256
SWA_TQ = 1024
FFN_TM = 512
ADA_TN = 1536


def _sigmoid(x):
    return 1.0 / (1.0 + jnp.exp(-x))


def _dot(a, b):
    return jnp.dot(a, b, preferred_element_type=F32)


def _dot_nt(a, b):
    return lax.dot_general(a, b, (((1,), (1,)), ((), ())), preferred_element_type=F32)


def _dot_tn(a, b):
    return lax.dot_general(a, b, (((0,), (0,)), ((), ())), preferred_element_type=F32)


def _split3(x):
    x1 = x.astype(BF16)
    r1 = x - x1.astype(F32)
    x2 = r1.astype(BF16)
    r2 = r1 - x2.astype(F32)
    x3 = r2.astype(BF16)
    return x1, x2, x3


def _ada_kernel(c_ref, w_ref, b_ref, win_ref, o_ref, wa16_ref, wb16_ref):
    c = c_ref[...]
    ca = c * _sigmoid(c)
    o_ref[...] = _dot(ca.astype(BF16), w_ref[...].astype(BF16)) + b_ref[...]

    @pl.when(pl.program_id(0) == 0)
    def _():
        wa16_ref[...] = win_ref[0:PROJ_ALIGNED, :].T.astype(BF16)
        o1 = PROJ_ALIGNED + 2 * GDN_HEADS
        o2 = o1 + SWA_WIDTH
        head = lambda h: win_ref[o1 + h * HEAD_DIM:o1 + (h + 1) * HEAD_DIM, :]
        pieces = [head(half * SWA_GROUP + p) for p in range(SWA_PAIRS) for half in range(2)]
        pieces.append(win_ref[o2:o2 + 2 * SWA_KV_WIDTH, :])
        pieces.append(win_ref[PROJ_ALIGNED:o1, :])
        pieces.append(jnp.zeros((GATE_PAD - 2 * GDN_HEADS, D_MODEL), F32))
        wb16_ref[...] = jnp.concatenate(pieces, axis=0).T.astype(BF16)


def _ada_call(c_pad, w_ada, b_ada, w_in):
    n = w_ada.shape[1]
    whole = lambda shape: pl.BlockSpec(shape, lambda j: (0, 0))
    return pl.pallas_call(
        _ada_kernel,
        grid=(n // ADA_TN,),
        in_specs=[
            whole((8, D_MODEL)),
            pl.BlockSpec((D_MODEL, ADA_TN), lambda j: (0, j)),
            pl.BlockSpec((1, ADA_TN), lambda j: (0, j)),
            pl.BlockSpec(w_in.shape, lambda j: (0, 0), pipeline_mode=pl.Buffered(1)),
        ],
        out_specs=[pl.BlockSpec((8, ADA_TN), lambda j: (0, j)),
                   whole((D_MODEL, PROJ_ALIGNED)), whole((D_MODEL, PROJ_REGROUPED))],
        out_shape=[jax.ShapeDtypeStruct((8, n), F32),
                   jax.ShapeDtypeStruct((D_MODEL, PROJ_ALIGNED), BF16),
                   jax.ShapeDtypeStruct((D_MODEL, PROJ_REGROUPED), BF16)],
        compiler_params=pltpu.CompilerParams(
            dimension_semantics=("arbitrary",), vmem_limit_bytes=VMEM_LIMIT),
        name="ada",
    )(c_pad, w_ada, b_ada, w_in)


QKV_W = 3 * GDN_WIDTH
PROJ_SPLITS = (QKV_W, GDN_WIDTH, SWA_WIDTH, 2 * SWA_KV_WIDTH, GATE_PAD)
PROJ_ALIGNED = QKV_W + GDN_WIDTH
PROJ_REGROUPED = sum(PROJ_SPLITS) - PROJ_ALIGNED


def _proj_kernel(x_ref, mod_ref, nw_ref, wa_ref, wb_ref, cw_ref, bd_ref, qkv_ref, z_ref, sq_ref, skv_ref, ab_ref,
                 qbuf, kbuf, vbuf):
    tm = PROJ_TM
    bufs = (qbuf, kbuf, vbuf)

    @pl.when(pl.program_id(1) == 0)
    def _():
        for buf in bufs:
            buf[0:8, :] = jnp.zeros((8, GDN_WIDTH), F32)

    x = x_ref[0]
    ms = jnp.mean(x * x, axis=-1, keepdims=True)
    y = x * lax.rsqrt(ms + EPS) * nw_ref[...]
    mod = mod_ref[0]
    shift = mod[:, 0:D_MODEL]
    scale = mod[:, D_MODEL:2 * D_MODEL]
    hb = (y * (1.0 + scale) + shift).astype(BF16)
    def project(s):
        bufs[s][8:8 + tm, :] = _dot(hb, wa_ref[:, s * GDN_WIDTH:(s + 1) * GDN_WIDTH])

    def conv_silu(s):
        buf = bufs[s]
        cw = lambda j: cw_ref[j:j + 1, s * GDN_WIDTH:(s + 1) * GDN_WIDTH]
        acc = cw(CONV_WIDTH - 1) * buf[8:8 + tm, :]
        for j in range(CONV_WIDTH - 1):
            off = 8 - (CONV_WIDTH - 1) + j
            acc = acc + cw(j) * buf[off:off + tm, :]
        buf[0:8, :] = buf[tm:tm + 8, :]
        return acc * _sigmoid(acc)

    def l2_normed(a, scale):
        return a * lax.rsqrt(_dot((a * a).astype(BF16), bd_ref[...]) + EPS) * scale

    project(0)
    project(1)
    qkv_ref[0, :, 0:GDN_WIDTH] = l2_normed(conv_silu(0), HEAD_DIM ** -0.5)
    project(2)
    qkv_ref[0, :, GDN_WIDTH:2 * GDN_WIDTH] = l2_normed(conv_silu(1), 1.0)
    z_ref[0] = _dot(hb, wa_ref[:, QKV_W:])
    qkv_ref[0, :, 2 * GDN_WIDTH:] = conv_silu(2)
    col = 0
    for ref in (sq_ref, skv_ref, ab_ref):
        ref[0] = _dot(hb, wb_ref[:, col:col + ref.shape[-1]])
        col += ref.shape[-1]


def _head_sum_matrix(width):
    h = np.arange(width) // HEAD_DIM
    return jnp.asarray((h[:, None] == h[None, :]).astype(np.float32), BF16)


def _proj_call(x, mod3, norm_w, w_aligned, w_regrouped, conv_w):
    b, t, _ = x.shape
    tm = PROJ_TM
    row = lambda width: pl.BlockSpec((1, tm, width), lambda bi, ti: (bi, ti, 0))
    const = lambda shape: pl.BlockSpec(shape, lambda bi, ti: (0,) * len(shape))
    return pl.pallas_call(
        _proj_kernel,
        grid=(b, t // tm),
        in_specs=[
            row(D_MODEL),
            pl.BlockSpec((1, 1, 6 * D_MODEL), lambda bi, ti: (bi, 0, 0)),
            const((1, D_MODEL)),
            const((D_MODEL, PROJ_ALIGNED)),
            const((D_MODEL, PROJ_REGROUPED)),
            const((CONV_WIDTH, QKV_W)),
            const((GDN_WIDTH, GDN_WIDTH)),
        ],
        out_specs=[row(w) for w in PROJ_SPLITS],
        out_shape=[jax.ShapeDtypeStruct((b, t, w), F32) for w in PROJ_SPLITS],
        scratch_shapes=[pltpu.VMEM((8 + tm, GDN_WIDTH), F32)] * 3,
        compiler_params=pltpu.CompilerParams(
            dimension_semantics=("arbitrary", "arbitrary"), vmem_limit_bytes=VMEM_LIMIT),
        name="proj",
    )(x, mod3, norm_w, w_aligned, w_regrouped, conv_w, _head_sum_matrix(GDN_WIDTH))


GDN_UNITS = (GDN_TT // CHUNK) * (GDN_HEADS // HEADS_PER_GROUP)
BD_SLOTS = 4


def _block_diag_rows(buf, slot, p):
    p16 = p.astype(BF16)
    for h in range(HEADS_PER_GROUP):
        buf[slot, h * HEAD_DIM:(h + 1) * HEAD_DIM, h * HEAD_DIM:(h + 1) * HEAD_DIM] = (
            p16[:, h * HEAD_DIM:(h + 1) * HEAD_DIM])
    return buf[slot]


_DONE = object()


def _gdn_kernel(qkv_ref, z_ref, ab_ref, alog_ref, dtb_ref, gnw_ref, bd_ref, eg_ref, eb_ref, ltri_ref,
                wg_ref, wu_ref, wd_ref, wo_ref, o_ref, wg16_ref, wu16_ref, wd16_ref, wo16_ref,
                s_ref, obuf, bdbuf, wq_buf, u_buf, qk_buf, kd_buf, dec_buf, *, tiles_per_seq):
    tt = GDN_TT
    step = pl.program_id(0)
    for src, dst in ((wg_ref, wg16_ref), (wu_ref, wu16_ref), (wd_ref, wd16_ref), (wo_ref, wo16_ref)):
        dst[...] = src[...].astype(BF16)
    wslot = step % 2
    rslot = 1 - wslot
    starts_sequence = (step - 1) % tiles_per_seq == 0

    @pl.when(step == 0)
    def _():
        for buf in (s_ref, bdbuf, wq_buf, u_buf, qk_buf, kd_buf, dec_buf):
            buf[...] = jnp.zeros_like(buf)

    bd = bd_ref[...]
    q = qkv_ref[0, :, 0:GDN_WIDTH]
    k = qkv_ref[0, :, GDN_WIDTH:2 * GDN_WIDTH]
    v = qkv_ref[0, :, 2 * GDN_WIDTH:]

    ab = ab_ref[0]
    lane = lax.broadcasted_iota(jnp.int32, (1, GATE_PAD), 1)
    xs = ab + dtb_ref[...]
    softplus = jnp.maximum(xs, 0.0) + jnp.log1p(jnp.exp(-jnp.abs(xs)))
    g = jnp.where(lane < GDN_HEADS, -jnp.exp(alog_ref[...]) * softplus, 0.0)
    beta = _sigmoid(ab)
    ltri = ltri_ref[...]
    g1, g2, g3 = _split3(g)
    gcum = _dot(ltri, g1) + _dot(ltri, g2) + _dot(ltri, g3)
    eg = eg_ref[...]
    c1, c2, c3 = _split3(gcum)
    g_x = _dot(c1, eg) + _dot(c2, eg) + _dot(c3, eg)
    beta_x = _dot(beta.astype(BF16), eb_ref[...])

    r64 = lax.broadcasted_iota(jnp.int32, (CHUNK, GROUP_W), 0)
    l64 = lax.broadcasted_iota(jnp.int32, (CHUNK, GROUP_W), 1) % CHUNK
    causal = r64 >= l64
    eye_x = (r64 == l64).astype(F32)
    strict_x = (r64 > l64).astype(F32)
    same_block = lambda size: (r64 // size) == (l64 // size)
    base_x = same_block(INV_BASE).astype(F32)
    level_x = [(same_block(2 * size) & ~same_block(size)).astype(F32) for size in INV_LEVELS]
    rb = lax.broadcasted_iota(jnp.int32, (GROUP_W, GROUP_W), 0) // HEAD_DIM
    cb = lax.broadcasted_iota(jnp.int32, (GROUP_W, GROUP_W), 1) // HEAD_DIM
    mask_bd = (rb == cb).astype(F32)

    n_groups = GDN_HEADS // HEADS_PER_GROUP
    n_chunks = tt // CHUNK
    units = [(c, gi) for c in range(n_chunks) for gi in range(n_groups)]
    ids = list(range(len(units)))
    tile = lambda a: [a[c * CHUNK:(c + 1) * CHUNK, gi * GROUP_W:(gi + 1) * GROUP_W] for c, gi in units]
    each = lambda f, *lists: [f(*args) for args in zip(*lists)]
    bdr = lambda i, slot, t: _block_diag_rows(bdbuf, i * BD_SLOTS + slot, t)
    mm = lambda a, w16: _dot(a.astype(BF16), w16)
    stack = lambda a, b_: jnp.concatenate([a, b_], axis=0)

    def chunk_parallel_part():
        kc, qc, vc, bx, gx = tile(k), tile(q), tile(v), tile(beta_x), tile(g_x)
        eg_c = each(jnp.exp, gx)
        glast = each(lambda g_: g_[CHUNK - 1:CHUNK, :], gx)
        kb = each(jnp.multiply, kc, bx)
        vb = each(jnp.multiply, vc, bx)
        wr = each(jnp.multiply, kb, eg_c)
        qd = each(jnp.multiply, qc, eg_c)
        for i in ids:
            kd_buf[wslot, i] = (kc[i] * jnp.exp(glast[i] - gx[i])).astype(BF16)
            dec_buf[wslot, i] = jnp.broadcast_to(jnp.exp(glast[i]), (8, GROUP_W))
        dm = each(lambda g_: jnp.exp(jnp.where(causal, g_ - jnp.sum(g_ * eye_x, axis=0, keepdims=True), NEG_BIG)),
                  gx)
        yield
        aq = each(lambda i, a, b_, k_: _dot_nt(stack(a, b_).astype(BF16), bdr(i, 1, k_)), ids, kb, qc, kc)
        xm = each(lambda a, d_: -(a[0:CHUNK] * d_ * strict_x), aq, dm)
        for i in ids:
            qk_buf[wslot, i] = (aq[i][CHUNK:] * dm[i]).astype(BF16)
        xd = each(lambda x_: x_ * base_x, xm)
        tinv = each(lambda x_: eye_x + x_, xd)
        yield
        pw = each(lambda i, x_: mm(x_, bdr(i, 0, x_)), ids, xd)
        yield
        r = each(lambda i, t_, p_: mm(stack(t_, p_), bdr(i, 0, p_)), ids, tinv, pw)
        tinv = each(lambda t_, r_: t_ + r_[0:CHUNK], tinv, r)
        pw = each(lambda r_: r_[CHUNK:], r)
        yield
        tinv = each(lambda i, t_, p_: t_ + mm(t_, bdr(i, 0, p_)), ids, tinv, pw)
        yield
        for lm in level_x:
            e = each(lambda i, x_, t_: mm(x_ * lm, bdr(i, 0, t_)), ids, xm, tinv)
            yield
            tinv = each(lambda i, t_, e_: t_ + mm(t_, bdr(i, 0, e_)), ids, tinv, e)
            yield
        for i in ids:
            u_buf[wslot, i] = mm(tinv[i], bdr(i, 2, vb[i]))
            wq_buf[wslot, i, 0:CHUNK] = mm(tinv[i], bdr(i, 3, wr[i])).astype(BF16)
            wq_buf[wslot, i, CHUNK:] = qd[i].astype(BF16)

    def sequential_part():
        s_state = [jnp.where(starts_sequence, 0.0, s_ref[gi]) for gi in range(n_groups)]
        for c in range(n_chunks):
            cids = [c * n_groups + gi for gi in range(n_groups)]
            r2 = [_dot(wq_buf[rslot, i], s_state[gi].astype(BF16)) for gi, i in enumerate(cids)]
            yield
            vn = [u_buf[rslot, i] - r2[gi][0:CHUNK] for gi, i in enumerate(cids)]
            o = [r2[gi][CHUNK:] + _dot(qk_buf[rslot, i], bdr(GDN_UNITS, gi, vn[gi])) for gi, i in enumerate(cids)]
            s_state = [s_state[gi] * dec_buf[rslot, i][0:1] + mask_bd * _dot_tn(kd_buf[rslot, i], vn[gi].astype(BF16))
                       for gi, i in enumerate(cids)]
            for gi in range(n_groups):
                obuf[c * CHUNK:(c + 1) * CHUNK, gi * GROUP_W:(gi + 1) * GROUP_W] = o[gi]
            yield
        for gi in range(n_groups):
            s_ref[gi] = s_state[gi]

    parts = [sequential_part(), chunk_parallel_part()]
    while parts:
        parts = [p for p in parts if next(p, _DONE) is not _DONE]

    o = obuf[...]
    ms = _dot((o * o).astype(BF16), bd) * (1.0 / HEAD_DIM)
    zz = z_ref[0]
    o_ref[0] = (o * lax.rsqrt(ms + EPS) * gnw_ref[...] * (zz * _sigmoid(zz))).astype(BF16)


def _gdn_consts():
    h = np.arange(GDN_WIDTH) // HEAD_DIM
    eg = np.zeros((GATE_PAD, GDN_WIDTH), np.float32)
    eb = np.zeros((GATE_PAD, GDN_WIDTH), np.float32)
    eg[h, np.arange(GDN_WIDTH)] = 1.0
    eb[GDN_HEADS + h, np.arange(GDN_WIDTH)] = 1.0
    t = np.arange(GDN_TT)
    ltri = ((t[:, None] // CHUNK == t[None, :] // CHUNK) & (t[:, None] >= t[None, :])).astype(np.float32)
    return (_head_sum_matrix(GDN_WIDTH), jnp.asarray(eg, BF16), jnp.asarray(eb, BF16), jnp.asarray(ltri, BF16))


BF16_ROWS = 16


def _swa_head_source(i):
    j = i - GDN_HEADS
    return jnp.where(i < GDN_HEADS, i, GDN_HEADS + (j % 2) * SWA_GROUP + j // 2)


def _cast_specs(weight, unit, steps, source=None):
    rows, cols = weight.shape
    block_rows = next(r for r in range(unit, rows + 1, unit) if rows % r == 0 and rows // r <= steps)
    if source is None:
        source = lambda i: i
    else:
        assert block_rows == unit, "row regrouping needs one unit per block"
    last = rows // block_rows - 1
    src = pl.BlockSpec((block_rows, cols), lambda n: (source(jnp.minimum(n, last)), 0))
    dst = pl.BlockSpec((block_rows, cols), lambda n: (jnp.minimum(n, last), 0))
    return src, dst, jax.ShapeDtypeStruct((rows, cols), BF16)


def _gdn_call(qkv, z, ab, alog_pad, dtb_pad, gnw_x, w_gate, w_up, w_down, w_out):
    b, t, _ = qkv.shape
    tt = GDN_TT
    bd, eg, eb, ltri = _gdn_consts()
    n_tiles = t // tt
    n_groups = GDN_HEADS // HEADS_PER_GROUP
    total = b * n_tiles

    def tile_block(width, lag):
        def index(n):
            m = jnp.clip(n - lag, 0, total - 1)
            return (m // n_tiles, m % n_tiles, 0)
        return pl.BlockSpec((1, tt, width), index)

    ahead = lambda width: tile_block(width, 0)
    behind = lambda width: tile_block(width, 1)
    const = lambda shape: pl.BlockSpec(shape, lambda n: (0,) * len(shape))
    per_unit = lambda rows, dtype: pltpu.VMEM((2, GDN_UNITS, rows, GROUP_W), dtype)
    steps = total + 1
    casts = [
        _cast_specs(w_gate, BF16_ROWS, steps),
        _cast_specs(w_up, BF16_ROWS, steps),
        _cast_specs(w_down, BF16_ROWS, steps),
        _cast_specs(w_out, HEAD_DIM, steps, _swa_head_source),
    ]
    return pl.pallas_call(
        functools.partial(_gdn_kernel, tiles_per_seq=n_tiles),
        grid=(steps,),
        in_specs=[
            ahead(QKV_W), behind(GDN_WIDTH), ahead(GATE_PAD),
            const((1, GATE_PAD)), const((1, GATE_PAD)), const((1, GDN_WIDTH)),
            const((GDN_WIDTH, GDN_WIDTH)), const((GATE_PAD, GDN_WIDTH)), const((GATE_PAD, GDN_WIDTH)),
            const((tt, tt)),
        ] + [c[0] for c in casts],
        out_specs=[behind(GDN_WIDTH)] + [c[1] for c in casts],
        out_shape=[jax.ShapeDtypeStruct((b, t, GDN_WIDTH), BF16)] + [c[2] for c in casts],
        scratch_shapes=[
            pltpu.VMEM((n_groups, GROUP_W, GROUP_W), F32),
            pltpu.VMEM((tt, GDN_WIDTH), F32),
            pltpu.VMEM((GDN_UNITS * BD_SLOTS + n_groups, GROUP_W, GROUP_W), BF16),
            per_unit(2 * CHUNK, BF16), per_unit(CHUNK, F32), per_unit(CHUNK, BF16), per_unit(CHUNK, BF16),
            per_unit(8, F32),
        ],
        compiler_params=pltpu.CompilerParams(
            dimension_semantics=("arbitrary",), vmem_limit_bytes=VMEM_LIMIT),
        name="gdn",
    )(qkv, z, ab, alog_pad, dtb_pad, gnw_x, bd, eg, eb, ltri, w_gate, w_up, w_down, w_out)


SWA_PAIRS = SWA_Q_HEADS // 2
SWA_ROWS = SWA_Q_HEADS * WINDOW
LOG2E = 1.4426950408889634


def _swa_kernel(q_ref, kvc_ref, kvp_ref, qw_ref, kw_ref, bdq_ref, bdk_ref, bias_ref, sink_ref, o_ref):
    tq = SWA_TQ
    ti = pl.program_id(1)
    q = q_ref[0]
    q = q * lax.rsqrt(_dot((q * q).astype(BF16), bdq_ref[...]) * (1.0 / HEAD_DIM) + EPS)
    q = q * (qw_ref[...] * (HEAD_DIM ** -0.5 * LOG2E))
    kv = jnp.concatenate([kvp_ref[0], kvc_ref[0]], axis=0)
    k = kv[:, 0:SWA_KV_WIDTH]
    v = kv[:, SWA_KV_WIDTH:]
    k = k * lax.rsqrt(_dot((k * k).astype(BF16), bdk_ref[...]) * (1.0 / HEAD_DIM) + EPS) * kw_ref[...]
    k16 = k.astype(BF16)
    lo = lax.broadcasted_iota(jnp.int32, (1, LANES), 1) < HEAD_DIM
    qcol = lax.broadcasted_iota(jnp.int32, (WINDOW, SWA_ROWS), 1) % WINDOW
    from_prev = lax.broadcasted_iota(jnp.int32, (WINDOW, SWA_ROWS), 0) > qcol
    first = jnp.where(ti == 0, 1, 0)
    sink = sink_ref[...]
    vt16 = v.T.astype(BF16)
    zero = jnp.zeros((), BF16)
    for j in range(tq // WINDOW):
        qj = q[j * WINDOW:(j + 1) * WINDOW]
        parts = []
        for p in range(SWA_PAIRS):
            qp = qj[:, p * LANES:(p + 1) * LANES]
            parts += [jnp.where(lo, qp, 0.0), jnp.where(lo, 0.0, qp)]
        qs = jnp.concatenate(parts, axis=0).astype(BF16)
        st = _dot_nt(k16[j * WINDOW:(j + 2) * WINDOW], qs)
        bias = bias_ref[first] if j == 0 else bias_ref[0]
        sm = jnp.where(from_prev, st[0:WINDOW], st[WINDOW:]) + bias
        m = jnp.max(sm, axis=0, keepdims=True)
        pe = jnp.exp2(sm - m)
        den = jnp.sum(pe, axis=0, keepdims=True) + jnp.exp2(sink - m)
        pb = pe.astype(BF16)
        pt2 = jnp.concatenate([jnp.where(from_prev, pb, zero), jnp.where(from_prev, zero, pb)], axis=0)
        ot = _dot(vt16[:, j * WINDOW:(j + 2) * WINDOW], pt2) * (1.0 / den)
        for p in range(SWA_PAIRS):
            c0 = p * 2 * WINDOW
            pair_t = jnp.concatenate([ot[0:HEAD_DIM, c0:c0 + WINDOW],
                                      ot[HEAD_DIM:, c0 + WINDOW:c0 + 2 * WINDOW]], axis=0)
            o_ref[0, j * WINDOW:(j + 1) * WINDOW, p * LANES:(p + 1) * LANES] = pair_t.T.astype(BF16)


def _swa_consts():
    qi = np.arange(WINDOW)[:, None]
    kj = np.arange(WINDOW)[None, :]
    from_prev = kj > qi
    dist = np.where(from_prev, qi + WINDOW - kj, qi - kj).astype(np.float32)
    slopes = 2.0 ** (-8.0 * (np.arange(SWA_Q_HEADS, dtype=np.float32) + 1.0) / SWA_Q_HEADS)
    bias = np.zeros((2, SWA_PAIRS, 2, WINDOW, WINDOW), np.float32)
    for first in range(2):
        for p in range(SWA_PAIRS):
            for half, head in enumerate((p, SWA_GROUP + p)):
                b = (-slopes[head].astype(np.float32) * dist) * np.float32(LOG2E)
                bias[first, p, half] = np.where(from_prev & bool(first), np.float32(NEG_BIG), b)
    bias = bias.reshape(2, SWA_ROWS, WINDOW).transpose(0, 2, 1)
    h = np.arange(SWA_WIDTH) // HEAD_DIM
    bdq = (h[:, None] == h[None, :]).astype(np.float32)
    hk = np.arange(SWA_KV_WIDTH) // HEAD_DIM
    bdk = (hk[:, None] == hk[None, :]).astype(np.float32)
    return jnp.asarray(bias), jnp.asarray(bdq, BF16), jnp.asarray(bdk, BF16)


def _swa_call(sq, skv, qw_x, kw_x, sink_col):
    b, t, _ = sq.shape
    tq = SWA_TQ
    nb = tq // WINDOW
    bias, bdq, bdk = _swa_consts()
    const = lambda shape: pl.BlockSpec(shape, lambda bi, ti: (0,) * len(shape))
    return pl.pallas_call(
        _swa_kernel,
        grid=(b, t // tq),
        in_specs=[
            pl.BlockSpec((1, tq, SWA_WIDTH), lambda bi, ti: (bi, ti, 0)),
            pl.BlockSpec((1, tq, 2 * SWA_KV_WIDTH), lambda bi, ti: (bi, ti, 0)),
            pl.BlockSpec((1, WINDOW, 2 * SWA_KV_WIDTH), lambda bi, ti: (bi, jnp.maximum(ti * nb - 1, 0), 0)),
            const((1, SWA_WIDTH)), const((1, SWA_KV_WIDTH)),
            const((SWA_WIDTH, SWA_WIDTH)), const((SWA_KV_WIDTH, SWA_KV_WIDTH)),
            const((2, WINDOW, SWA_ROWS)),
            const((1, SWA_ROWS)),
        ],
        out_specs=pl.BlockSpec((1, tq, SWA_WIDTH), lambda bi, ti: (bi, ti, 0)),
        out_shape=jax.ShapeDtypeStruct((b, t, SWA_WIDTH), BF16),
        compiler_params=pltpu.CompilerParams(
            dimension_semantics=("arbitrary", "arbitrary"), vmem_limit_bytes=VMEM_LIMIT),
        name="swa",
    )(sq, skv, skv, qw_x, kw_x, bdq, bdk, bias, sink_col)


def _ffn_kernel(x_ref, og_ref, os_ref, mod_ref, nw_ref, wo_ref, wg_ref, wu_ref, wd_ref, o_ref):
    x = x_ref[0]
    mod = mod_ref[0]
    gate1 = mod[:, 2 * D_MODEL:3 * D_MODEL]
    shift2 = mod[:, 3 * D_MODEL:4 * D_MODEL]
    scale2 = mod[:, 4 * D_MODEL:5 * D_MODEL]
    gate2 = mod[:, 5 * D_MODEL:]
    mixed = _dot(og_ref[0], wo_ref[0:GDN_WIDTH, :]) + _dot(os_ref[0], wo_ref[GDN_WIDTH:, :])
    x1 = x + gate1 * mixed
    ms = jnp.mean(x1 * x1, axis=-1, keepdims=True)
    hb = ((x1 * lax.rsqrt(ms + EPS) * nw_ref[...]) * (1.0 + scale2) + shift2).astype(BF16)
    gt = _dot(hb, wg_ref[...])
    up = _dot(hb, wu_ref[...])
    act = ((gt * _sigmoid(gt)) * up).astype(BF16)
    o_ref[0] = x1 + gate2 * _dot(act, wd_ref[...])


def _ffn_call(x, og, osw, mod3, norm_w, wo, wg, wu, wd):
    b, t, _ = x.shape
    tm = FFN_TM
    row = lambda width: pl.BlockSpec((1, tm, width), lambda bi, ti: (bi, ti, 0))
    const = lambda shape: pl.BlockSpec(shape, lambda bi, ti: (0,) * len(shape), pipeline_mode=pl.Buffered(1))
    return pl.pallas_call(
        _ffn_kernel,
        grid=(b, t // tm),
        in_specs=[
            row(D_MODEL), row(GDN_WIDTH), row(SWA_WIDTH),
            pl.BlockSpec((1, 1, 6 * D_MODEL), lambda bi, ti: (bi, 0, 0)),
            const((1, D_MODEL)),
            const((GDN_WIDTH + SWA_WIDTH, D_MODEL)),
            const((D_MODEL, D_FF)), const((D_MODEL, D_FF)), const((D_FF, D_MODEL)),
        ],
        out_specs=row(D_MODEL),
        out_shape=jax.ShapeDtypeStruct((b, t, D_MODEL), F32),
        compiler_params=pltpu.CompilerParams(
            dimension_semantics=("arbitrary", "arbitrary"), vmem_limit_bytes=VMEM_LIMIT),
        name="ffn",
    )(x, og, osw, mod3, norm_w, wo, wg, wu, wd)


def _layer(x, c_pad, w_ada, b_ada, norm1_w, w_in, conv_w, a_log, dt_bias, gdn_norm_w, q_norm_w, k_norm_w, sinks,
           w_out, norm2_w, w_gate, w_up, w_down):
    b = x.shape[0]
    mod, w_aligned, w_regrouped = _ada_call(c_pad, w_ada, b_ada.reshape(1, -1), w_in.T)
    mod3 = mod[:b].reshape(b, 1, 6 * D_MODEL)
    qkv, z, sq, skv, ab = _proj_call(x, mod3, norm1_w.reshape(1, D_MODEL), w_aligned, w_regrouped,
                                     conv_w.reshape(CONV_WIDTH, QKV_W))

    alog_pad = jnp.pad(a_log.reshape(1, GDN_HEADS), ((0, 0), (0, GATE_PAD - GDN_HEADS)))
    dtb_pad = jnp.pad(dt_bias.reshape(1, GDN_HEADS), ((0, 0), (0, GATE_PAD - GDN_HEADS)))
    gnw_x = jnp.tile(gdn_norm_w.reshape(1, HEAD_DIM), (1, GDN_HEADS))
    og, wg16, wu16, wd16, wo16 = _gdn_call(qkv, z, ab, alog_pad, dtb_pad, gnw_x, w_gate, w_up, w_down, w_out)

    qw_x = jnp.tile(q_norm_w.reshape(1, HEAD_DIM), (1, SWA_Q_HEADS))
    kw_x = jnp.tile(k_norm_w.reshape(1, HEAD_DIM), (1, SWA_KV_HEADS))
    sink_pairs = jnp.stack([sinks[:SWA_GROUP], sinks[SWA_GROUP:]], axis=1)
    sink_col = (jnp.repeat(sink_pairs, WINDOW, axis=1) * LOG2E).reshape(1, SWA_ROWS)
    osw = _swa_call(sq, skv, qw_x, kw_x, sink_col)

    return _ffn_call(x, og, osw, mod3, norm2_w.reshape(1, D_MODEL), wo16, wg16, wu16, wd16)


def kernel(x, c, w_ada, b_ada, norm1_w, w_in, conv_w, a_log, dt_bias, gdn_norm_w, q_norm_w, k_norm_w, sinks,
           w_out, norm2_w, w_gate, w_up, w_down):
    depth = w_ada.shape[0]
    b = c.shape[0]
    c_pad = jnp.pad(c, ((0, 8 - b), (0, 0)))
    for l in range(depth):
        x = _layer(x, c_pad, w_ada[l], b_ada[l], norm1_w[l], w_in[l], conv_w[l], a_log[l], dt_bias[l],
                   gdn_norm_w[l], q_norm_w[l], k_norm_w[l], sinks[l], w_out[l], norm2_w[l], w_gate[l], w_up[l],
                   w_down[l])
    return x
```

```python
import functools

import numpy as np
import jax
import jax.numpy as jnp
from jax import lax
from jax.experimental import pallas as pl
from jax.experimental.pallas import tpu as pltpu

F32 = jnp.float32
BF16 = jnp.bfloat16

D_MODEL = 1024
HEAD_DIM = 64
GDN_HEADS = 8
GDN_WIDTH = GDN_HEADS * HEAD_DIM
SWA_Q_HEADS = 8
SWA_KV_HEADS = 2
SWA_GROUP = SWA_Q_HEADS // SWA_KV_HEADS
SWA_WIDTH = SWA_Q_HEADS * HEAD_DIM
SWA_KV_WIDTH = SWA_KV_HEADS * HEAD_DIM
WINDOW = 128
CONV_WIDTH = 4
CHUNK = 64
D_FF = 2816
EPS = 1e-6
LANES = 128
GATE_PAD = LANES
HEADS_PER_GROUP = 2
GROUP_W = HEADS_PER_GROUP * HEAD_DIM
INV_BASE = 16
INV_LEVELS = (16, 32)
NEG_BIG = -1e30
VMEM_LIMIT = 56 * 1024 * 1024

PROJ_TM = 512
GDN_TT = 256
SWA_TQ = 1024
FFN_TM = 512
ADA_TN = 1536


def _sigmoid(x):
    return 1.0 / (1.0 + jnp.exp(-x))


def _dot(a, b):
    return jnp.dot(a, b, preferred_element_type=F32)


def _dot_nt(a, b):
    return lax.dot_general(a, b, (((1,), (1,)), ((), ())), preferred_element_type=F32)


def _dot_tn(a, b):
    return lax.dot_general(a, b, (((0,), (0,)), ((), ())), preferred_element_type=F32)


def _split2(x):
    hi = x.astype(BF16)
    lo = (x - hi.astype(F32)).astype(BF16)
    return hi, lo


def _ada_kernel(c_ref, w_ref, b_ref, win_ref, o_ref, wa16_ref, wb16_ref):
    c = c_ref[...]
    ca = c * _sigmoid(c)
    o_ref[...] = _dot(ca.astype(BF16), w_ref[...].astype(BF16)) + b_ref[...]

    @pl.when(pl.program_id(0) == 0)
    def _():
        wa16_ref[...] = win_ref[0:PROJ_ALIGNED, :].T.astype(BF16)
        o1 = PROJ_ALIGNED + 2 * GDN_HEADS
        o2 = o1 + SWA_WIDTH
        head = lambda h: win_ref[o1 + h * HEAD_DIM:o1 + (h + 1) * HEAD_DIM, :]
        pieces = [head(half * SWA_GROUP + p) for p in range(SWA_PAIRS) for half in range(2)]
        pieces.append(win_ref[o2:o2 + 2 * SWA_KV_WIDTH, :])
        pieces.append(win_ref[PROJ_ALIGNED:o1, :])
        pieces.append(jnp.zeros((GATE_PAD - 2 * GDN_HEADS, D_MODEL), F32))
        wb16_ref[...] = jnp.concatenate(pieces, axis=0).T.astype(BF16)


def _ada_call(c_pad, w_ada, b_ada, w_in):
    n = w_ada.shape[1]
    whole = lambda shape: pl.BlockSpec(shape, lambda j: (0, 0))
    return pl.pallas_call(
        _ada_kernel,
        grid=(n // ADA_TN,),
        in_specs=[
            whole((8, D_MODEL)),
            pl.BlockSpec((D_MODEL, ADA_TN), lambda j: (0, j)),
            pl.BlockSpec((1, ADA_TN), lambda j: (0, j)),
            pl.BlockSpec(w_in.shape, lambda j: (0, 0), pipeline_mode=pl.Buffered(1)),
        ],
        out_specs=[pl.BlockSpec((8, ADA_TN), lambda j: (0, j)),
                   whole((D_MODEL, PROJ_ALIGNED)), whole((D_MODEL, PROJ_REGROUPED))],
        out_shape=[jax.ShapeDtypeStruct((8, n), F32),
                   jax.ShapeDtypeStruct((D_MODEL, PROJ_ALIGNED), BF16),
                   jax.ShapeDtypeStruct((D_MODEL, PROJ_REGROUPED), BF16)],
        compiler_params=pltpu.CompilerParams(
            dimension_semantics=("arbitrary",), vmem_limit_bytes=VMEM_LIMIT),
        name="ada",
    )(c_pad, w_ada, b_ada, w_in)


QKV_W = 3 * GDN_WIDTH
PROJ_SPLITS = (QKV_W, GDN_WIDTH, SWA_WIDTH, 2 * SWA_KV_WIDTH, GATE_PAD)
PROJ_ALIGNED = QKV_W + GDN_WIDTH
PROJ_REGROUPED = sum(PROJ_SPLITS) - PROJ_ALIGNED


def _proj_kernel(x_ref, mod_ref, nw_ref, wa_ref, wb_ref, cw_ref, bd_ref, qkv_ref, z_ref, sq_ref, skv_ref, ab_ref,
                 qbuf, kbuf, vbuf):
    tm = PROJ_TM
    bufs = (qbuf, kbuf, vbuf)

    @pl.when(pl.program_id(1) == 0)
    def _():
        for buf in bufs:
            buf[0:8, :] = jnp.zeros((8, GDN_WIDTH), F32)

    x = x_ref[0]
    ms = jnp.mean(x * x, axis=-1, keepdims=True)
    y = x * lax.rsqrt(ms + EPS) * nw_ref[...]
    mod = mod_ref[0]
    shift = mod[:, 0:D_MODEL]
    scale = mod[:, D_MODEL:2 * D_MODEL]
    hb = (y * (1.0 + scale) + shift).astype(BF16)

    def project(s):
        bufs[s][8:8 + tm, :] = _dot(hb, wa_ref[:, s * GDN_WIDTH:(s + 1) * GDN_WIDTH])

    def conv_silu(s):
        buf = bufs[s]
        cw = lambda j: cw_ref[j:j + 1, s * GDN_WIDTH:(s + 1) * GDN_WIDTH]
        acc = cw(CONV_WIDTH - 1) * buf[8:8 + tm, :]
        for j in range(CONV_WIDTH - 1):
            off = 8 - (CONV_WIDTH - 1) + j
            acc = acc + cw(j) * buf[off:off + tm, :]
        buf[0:8, :] = buf[tm:tm + 8, :]
        return acc * _sigmoid(acc)

    def l2_normed(a, scale):
        return a * lax.rsqrt(_dot((a * a).astype(BF16), bd_ref[...]) + EPS) * scale

    project(0)
    project(1)
    qkv_ref[0, :, 0:GDN_WIDTH] = l2_normed(conv_silu(0), HEAD_DIM ** -0.5)
    project(2)
    qkv_ref[0, :, GDN_WIDTH:2 * GDN_WIDTH] = l2_normed(conv_silu(1), 1.0)
    z_ref[0] = _dot(hb, wa_ref[:, QKV_W:])
    qkv_ref[0, :, 2 * GDN_WIDTH:] = conv_silu(2)
    col = 0
    for ref in (sq_ref, skv_ref, ab_ref):
        ref[0] = _dot(hb, wb_ref[:, col:col + ref.shape[-1]])
        col += ref.shape[-1]


def _head_sum_matrix(width):
    h = np.arange(width) // HEAD_DIM
    return jnp.asarray((h[:, None] == h[None, :]).astype(np.float32), BF16)


def _proj_call(x, mod3, norm_w, w_aligned, w_regrouped, conv_w):
    b, t, _ = x.shape
    tm = PROJ_TM
    row = lambda width: pl.BlockSpec((1, tm, width), lambda bi, ti: (bi, ti, 0))
    const = lambda shape: pl.BlockSpec(shape, lambda bi, ti: (0,) * len(shape))
    return pl.pallas_call(
        _proj_kernel,
        grid=(b, t // tm),
        in_specs=[
            row(D_MODEL),
            pl.BlockSpec((1, 1, 6 * D_MODEL), lambda bi, ti: (bi, 0, 0)),
            const((1, D_MODEL)),
            const((D_MODEL, PROJ_ALIGNED)),
            const((D_MODEL, PROJ_REGROUPED)),
            const((CONV_WIDTH, QKV_W)),
            const((GDN_WIDTH, GDN_WIDTH)),
        ],
        out_specs=[row(w) for w in PROJ_SPLITS],
        out_shape=[jax.ShapeDtypeStruct((b, t, w), F32) for w in PROJ_SPLITS],
        scratch_shapes=[pltpu.VMEM((8 + tm, GDN_WIDTH), F32)] * 3,
        compiler_params=pltpu.CompilerParams(
            dimension_semantics=("arbitrary", "arbitrary"), vmem_limit_bytes=VMEM_LIMIT),
        name="proj",
    )(x, mod3, norm_w, w_aligned, w_regrouped, conv_w, _head_sum_matrix(GDN_WIDTH))


GDN_UNITS = (GDN_TT // CHUNK) * (GDN_HEADS // HEADS_PER_GROUP)
BD_SLOTS = 4


def _block_diag_rows(buf, slot, p):
    p16 = p.astype(BF16)
    for h in range(HEADS_PER_GROUP):
        buf[slot, h * HEAD_DIM:(h + 1) * HEAD_DIM, h * HEAD_DIM:(h + 1) * HEAD_DIM] = (
            p16[:, h * HEAD_DIM:(h + 1) * HEAD_DIM])
    return buf[slot]


_DONE = object()


def _gdn_kernel(qkv_ref, z_ref, ab_ref, alog_ref, dtb_ref, gnw_ref, bd_ref, eg_ref, eb_ref, ltri_ref,
                wg_ref, wu_ref, wd_ref, wo_ref, o_ref, wg16_ref, wu16_ref, wd16_ref, wo16_ref,
                s_ref, obuf, bdbuf, wq_buf, u_buf, qk_buf, kd_buf, dec_buf, *, tiles_per_seq):
    tt = GDN_TT
    step = pl.program_id(0)
    for src, dst in ((wg_ref, wg16_ref), (wu_ref, wu16_ref), (wd_ref, wd16_ref), (wo_ref, wo16_ref)):
        dst[...] = src[...].astype(BF16)
    wslot = step % 2
    rslot = 1 - wslot
    starts_sequence = (step - 1) % tiles_per_seq == 0

    @pl.when(step == 0)
    def _():
        for buf in (s_ref, bdbuf, wq_buf, u_buf, qk_buf, kd_buf, dec_buf):
            buf[...] = jnp.zeros_like(buf)

    bd = bd_ref[...]
    q = qkv_ref[0, :, 0:GDN_WIDTH]
    k = qkv_ref[0, :, GDN_WIDTH:2 * GDN_WIDTH]
    v = qkv_ref[0, :, 2 * GDN_WIDTH:]

    ab = ab_ref[0]
    lane = lax.broadcasted_iota(jnp.int32, (1, GATE_PAD), 1)
    xs = ab + dtb_ref[...]
    softplus = jnp.maximum(xs, 0.0) + jnp.log1p(jnp.exp(-jnp.abs(xs)))
    g = jnp.where(lane < GDN_HEADS, -jnp.exp(alog_ref[...]) * softplus, 0.0)
    beta = _sigmoid(ab)
    ltri = ltri_ref[...]
    g1, g2 = _split2(g)
    gcum = _dot(ltri, g1) + _dot(ltri, g2)
    eg = eg_ref[...]
    c1, c2 = _split2(gcum)
    g_x = _dot(c1, eg) + _dot(c2, eg)
    beta_x = _dot(beta.astype(BF16), eb_ref[...])

    r64 = lax.broadcasted_iota(jnp.int32, (CHUNK, GROUP_W), 0)
    l64 = lax.broadcasted_iota(jnp.int32, (CHUNK, GROUP_W), 1) % CHUNK
    causal = r64 >= l64
    eye_x = (r64 == l64).astype(F32)
    strict_x = (r64 > l64).astype(F32)
    same_block = lambda size: (r64 // size) == (l64 // size)
    base_x = same_block(INV_BASE).astype(F32)
    level_x = [(same_block(2 * size) & ~same_block(size)).astype(F32) for size in INV_LEVELS]
    rb = lax.broadcasted_iota(jnp.int32, (GROUP_W, GROUP_W), 0) // HEAD_DIM
    cb = lax.broadcasted_iota(jnp.int32, (GROUP_W, GROUP_W), 1) // HEAD_DIM
    mask_bd = (rb == cb).astype(F32)

    n_groups = GDN_HEADS // HEADS_PER_GROUP
    n_chunks = tt // CHUNK
    units = [(c, gi) for c in range(n_chunks) for gi in range(n_groups)]
    ids = list(range(len(units)))
    tile = lambda a: [a[c * CHUNK:(c + 1) * CHUNK, gi * GROUP_W:(gi + 1) * GROUP_W] for c, gi in units]
    each = lambda f, *lists: [f(*args) for args in zip(*lists)]
    bdr = lambda i, slot, t: _block_diag_rows(bdbuf, i * BD_SLOTS + slot, t)
    mm = lambda a, w16: _dot(a.astype(BF16), w16)
    stack = lambda a, b_: jnp.concatenate([a, b_], axis=0)

    def chunk_parallel_part():
        kc, qc, vc, bx, gx = tile(k), tile(q), tile(v), tile(beta_x), tile(g_x)
        eg_c = each(jnp.exp, gx)
        glast = each(lambda g_: g_[CHUNK - 1:CHUNK, :], gx)
        kb = each(jnp.multiply, kc, bx)
        vb = each(jnp.multiply, vc, bx)
        wr = each(jnp.multiply, kb, eg_c)
        qd = each(jnp.multiply, qc, eg_c)
        for i in ids:
            kd_buf[wslot, i] = (kc[i] * jnp.exp(glast[i] - gx[i])).astype(BF16)
            dec_buf[wslot, i] = jnp.broadcast_to(jnp.exp(glast[i]), (8, GROUP_W))
        dm = each(lambda g_: jnp.exp(jnp.where(causal, g_ - jnp.sum(g_ * eye_x, axis=0, keepdims=True), NEG_BIG)),
                  gx)
        yield
        aq = each(lambda i, a, b_, k_: _dot_nt(stack(a, b_).astype(BF16), bdr(i, 1, k_)), ids, kb, qc, kc)
        xm = each(lambda a, d_: -(a[0:CHUNK] * d_ * strict_x), aq, dm)
        for i in ids:
            qk_buf[wslot, i] = (aq[i][CHUNK:] * dm[i]).astype(BF16)
        xd = each(lambda x_: x_ * base_x, xm)
        tinv = each(lambda x_: eye_x + x_, xd)
        yield
        pw = each(lambda i, x_: mm(x_, bdr(i, 0, x_)), ids, xd)
        yield
        for _ in range(INV_BASE.bit_length() - 3):
            r = each(lambda i, t_, p_: mm(stack(t_, p_), bdr(i, 0, p_)), ids, tinv, pw)
            tinv = each(lambda t_, r_: t_ + r_[0:CHUNK], tinv, r)
            pw = each(lambda r_: r_[CHUNK:], r)
            yield
        tinv = each(lambda i, t_, p_: t_ + mm(t_, bdr(i, 0, p_)), ids, tinv, pw)
        yield
        for lm in level_x:
            e = each(lambda i, x_, t_: mm(x_ * lm, bdr(i, 0, t_)), ids, xm, tinv)
            yield
            tinv = each(lambda i, t_, e_: t_ + mm(t_, bdr(i, 0, e_)), ids, tinv, e)
            yield
        for i in ids:
            u_buf[wslot, i] = mm(tinv[i], bdr(i, 2, vb[i]))
            wq_buf[wslot, i, 0:CHUNK] = mm(tinv[i], bdr(i, 3, wr[i])).astype(BF16)
            wq_buf[wslot, i, CHUNK:] = qd[i].astype(BF16)

    def sequential_part():
        s_state = [jnp.where(starts_sequence, 0.0, s_ref[gi]) for gi in range(n_groups)]
        for c in range(n_chunks):
            cids = [c * n_groups + gi for gi in range(n_groups)]
            r2 = [_dot(wq_buf[rslot, i], s_state[gi].astype(BF16)) for gi, i in enumerate(cids)]
            yield
            vn = [u_buf[rslot, i] - r2[gi][0:CHUNK] for gi, i in enumerate(cids)]
            o = [r2[gi][CHUNK:] + _dot(qk_buf[rslot, i], bdr(GDN_UNITS, gi, vn[gi])) for gi, i in enumerate(cids)]
            s_state = [s_state[gi] * dec_buf[rslot, i][0:1] + mask_bd * _dot_tn(kd_buf[rslot, i], vn[gi].astype(BF16))
                       for gi, i in enumerate(cids)]
            for gi in range(n_groups):
                obuf[c * CHUNK:(c + 1) * CHUNK, gi * GROUP_W:(gi + 1) * GROUP_W] = o[gi]
            yield
        for gi in range(n_groups):
            s_ref[gi] = s_state[gi]

    parts = [chunk_parallel_part(), sequential_part()]
    while parts:
        parts = [p for p in parts if next(p, _DONE) is not _DONE]

    o = obuf[...]
    ms = _dot((o * o).astype(BF16), bd) * (1.0 / HEAD_DIM)
    zz = z_ref[0]
    o_ref[0] = (o * lax.rsqrt(ms + EPS) * gnw_ref[...] * (zz * _sigmoid(zz))).astype(BF16)


def _gdn_consts():
    h = np.arange(GDN_WIDTH) // HEAD_DIM
    eg = np.zeros((GATE_PAD, GDN_WIDTH), np.float32)
    eb = np.zeros((GATE_PAD, GDN_WIDTH), np.float32)
    eg[h, np.arange(GDN_WIDTH)] = 1.0
    eb[GDN_HEADS + h, np.arange(GDN_WIDTH)] = 1.0
    t = np.arange(GDN_TT)
    ltri = ((t[:, None] // CHUNK == t[None, :] // CHUNK) & (t[:, None] >= t[None, :])).astype(np.float32)
    return (_head_sum_matrix(GDN_WIDTH), jnp.asarray(eg, BF16), jnp.asarray(eb, BF16), jnp.asarray(ltri, BF16))


BF16_ROWS = 16


def _swa_head_source(i):
    j = i - GDN_HEADS
    return jnp.where(i < GDN_HEADS, i, GDN_HEADS + (j % 2) * SWA_GROUP + j // 2)


def _cast_specs(weight, unit, steps, source=None):
    rows, cols = weight.shape
    block_rows = next(r for r in range(unit, rows + 1, unit) if rows % r == 0 and rows // r <= steps)
    if source is None:
        source = lambda i: i
    else:
        assert block_rows == unit, "row regrouping needs one unit per block"
    last = rows // block_rows - 1
    src = pl.BlockSpec((block_rows, cols), lambda n: (source(jnp.minimum(n, last)), 0))
    dst = pl.BlockSpec((block_rows, cols), lambda n: (jnp.minimum(n, last), 0))
    return src, dst, jax.ShapeDtypeStruct((rows, cols), BF16)


def _gdn_call(qkv, z, ab, alog_pad, dtb_pad, gnw_x, w_gate, w_up, w_down, w_out):
    b, t, _ = qkv.shape
    tt = GDN_TT
    bd, eg, eb, ltri = _gdn_consts()
    n_tiles = t // tt
    n_groups = GDN_HEADS // HEADS_PER_GROUP
    total = b * n_tiles

    def tile_block(width, lag):
        def index(n):
            m = jnp.clip(n - lag, 0, total - 1)
            return (m // n_tiles, m % n_tiles, 0)
        return pl.BlockSpec((1, tt, width), index)

    ahead = lambda width: tile_block(width, 0)
    behind = lambda width: tile_block(width, 1)
    const = lambda shape: pl.BlockSpec(shape, lambda n: (0,) * len(shape))
    per_unit = lambda rows, dtype: pltpu.VMEM((2, GDN_UNITS, rows, GROUP_W), dtype)
    steps = total + 1
    casts = [
        _cast_specs(w_gate, BF16_ROWS, steps),
        _cast_specs(w_up, BF16_ROWS, steps),
        _cast_specs(w_down, BF16_ROWS, steps),
        _cast_specs(w_out, HEAD_DIM, steps, _swa_head_source),
    ]
    return pl.pallas_call(
        functools.partial(_gdn_kernel, tiles_per_seq=n_tiles),
        grid=(steps,),
        in_specs=[
            ahead(QKV_W), behind(GDN_WIDTH), ahead(GATE_PAD),
            const((1, GATE_PAD)), const((1, GATE_PAD)), const((1, GDN_WIDTH)),
            const((GDN_WIDTH, GDN_WIDTH)), const((GATE_PAD, GDN_WIDTH)), const((GATE_PAD, GDN_WIDTH)),
            const((tt, tt)),
        ] + [c[0] for c in casts],
        out_specs=[behind(GDN_WIDTH)] + [c[1] for c in casts],
        out_shape=[jax.ShapeDtypeStruct((b, t, GDN_WIDTH), BF16)] + [c[2] for c in casts],
        scratch_shapes=[
            pltpu.VMEM((n_groups, GROUP_W, GROUP_W), F32),
            pltpu.VMEM((tt, GDN_WIDTH), F32),
            pltpu.VMEM((GDN_UNITS * BD_SLOTS + n_groups, GROUP_W, GROUP_W), BF16),
            per_unit(2 * CHUNK, BF16), per_unit(CHUNK, F32), per_unit(CHUNK, BF16), per_unit(CHUNK, BF16),
            per_unit(8, F32),
        ],
        compiler_params=pltpu.CompilerParams(
            dimension_semantics=("arbitrary",), vmem_limit_bytes=VMEM_LIMIT),
        name="gdn",
    )(qkv, z, ab, alog_pad, dtb_pad, gnw_x, bd, eg, eb, ltri, w_gate, w_up, w_down, w_out)


SWA_PAIRS = SWA_Q_HEADS // 2
SWA_ROWS = SWA_Q_HEADS * WINDOW
LOG2E = 1.4426950408889634


def _swa_kernel(q_ref, kvc_ref, kvp_ref, qw_ref, kw_ref, bdq_ref, bdk_ref, bias_ref, sink_ref, o_ref):
    tq = SWA_TQ
    ti = pl.program_id(1)
    q = q_ref[0]
    q = q * lax.rsqrt(_dot((q * q).astype(BF16), bdq_ref[...]) * (1.0 / HEAD_DIM) + EPS)
    q = q * (qw_ref[...] * (HEAD_DIM ** -0.5 * LOG2E))
    kv = jnp.concatenate([kvp_ref[0], kvc_ref[0]], axis=0)
    k = kv[:, 0:SWA_KV_WIDTH]
    v = kv[:, SWA_KV_WIDTH:]
    k = k * lax.rsqrt(_dot((k * k).astype(BF16), bdk_ref[...]) * (1.0 / HEAD_DIM) + EPS) * kw_ref[...]
    k16 = k.astype(BF16)
    lo = lax.broadcasted_iota(jnp.int32, (1, LANES), 1) < HEAD_DIM
    qcol = lax.broadcasted_iota(jnp.int32, (WINDOW, SWA_ROWS), 1) % WINDOW
    from_prev = lax.broadcasted_iota(jnp.int32, (WINDOW, SWA_ROWS), 0) > qcol
    first = jnp.where(ti == 0, 1, 0)
    sink = sink_ref[...]
    vt16 = v.T.astype(BF16)
    zero = jnp.zeros((), BF16)
    for j in range(tq // WINDOW):
        qj = q[j * WINDOW:(j + 1) * WINDOW]
        parts = []
        for p in range(SWA_PAIRS):
            qp = qj[:, p * LANES:(p + 1) * LANES]
            parts += [jnp.where(lo, qp, 0.0), jnp.where(lo, 0.0, qp)]
        qs = jnp.concatenate(parts, axis=0).astype(BF16)
        st = _dot_nt(k16[j * WINDOW:(j + 2) * WINDOW], qs)
        bias = bias_ref[first] if j == 0 else bias_ref[0]
        sm = jnp.where(from_prev, st[0:WINDOW], st[WINDOW:]) + bias
        m = jnp.max(sm, axis=0, keepdims=True)
        pe = jnp.exp2(sm - m)
        den = jnp.sum(pe, axis=0, keepdims=True) + jnp.exp2(sink - m)
        pb = pe.astype(BF16)
        pt2 = jnp.concatenate([jnp.where(from_prev, pb, zero), jnp.where(from_prev, zero, pb)], axis=0)
        ot = _dot(vt16[:, j * WINDOW:(j + 2) * WINDOW], pt2) * (1.0 / den)
        for p in range(SWA_PAIRS):
            c0 = p * 2 * WINDOW
            pair_t = jnp.concatenate([ot[0:HEAD_DIM, c0:c0 + WINDOW],
                                      ot[HEAD_DIM:, c0 + WINDOW:c0 + 2 * WINDOW]], axis=0)
            o_ref[0, j * WINDOW:(j + 1) * WINDOW, p * LANES:(p + 1) * LANES] = pair_t.T.astype(BF16)


def _swa_consts():
    qi = np.arange(WINDOW)[:, None]
    kj = np.arange(WINDOW)[None, :]
    from_prev = kj > qi
    dist = np.where(from_prev, qi + WINDOW - kj, qi - kj).astype(np.float32)
    slopes = 2.0 ** (-8.0 * (np.arange(SWA_Q_HEADS, dtype=np.float32) + 1.0) / SWA_Q_HEADS)
    bias = np.zeros((2, SWA_PAIRS, 2, WINDOW, WINDOW), np.float32)
    for first in range(2):
        for p in range(SWA_PAIRS):
            for half, head in enumerate((p, SWA_GROUP + p)):
                b = (-slopes[head].astype(np.float32) * dist) * np.float32(LOG2E)
                bias[first, p, half] = np.where(from_prev & bool(first), np.float32(NEG_BIG), b)
    bias = bias.reshape(2, SWA_ROWS, WINDOW).transpose(0, 2, 1)
    h = np.arange(SWA_WIDTH) // HEAD_DIM
    bdq = (h[:, None] == h[None, :]).astype(np.float32)
    hk = np.arange(SWA_KV_WIDTH) // HEAD_DIM
    bdk = (hk[:, None] == hk[None, :]).astype(np.float32)
    return jnp.asarray(bias), jnp.asarray(bdq, BF16), jnp.asarray(bdk, BF16)


def _swa_call(sq, skv, qw_x, kw_x, sink_col):
    b, t, _ = sq.shape
    tq = SWA_TQ
    nb = tq // WINDOW
    bias, bdq, bdk = _swa_consts()
    const = lambda shape: pl.BlockSpec(shape, lambda bi, ti: (0,) * len(shape))
    return pl.pallas_call(
        _swa_kernel,
        grid=(b, t // tq),
        in_specs=[
            pl.BlockSpec((1, tq, SWA_WIDTH), lambda bi, ti: (bi, ti, 0)),
            pl.BlockSpec((1, tq, 2 * SWA_KV_WIDTH), lambda bi, ti: (bi, ti, 0)),
            pl.BlockSpec((1, WINDOW, 2 * SWA_KV_WIDTH), lambda bi, ti: (bi, jnp.maximum(ti * nb - 1, 0), 0)),
            const((1, SWA_WIDTH)), const((1, SWA_KV_WIDTH)),
            const((SWA_WIDTH, SWA_WIDTH)), const((SWA_KV_WIDTH, SWA_KV_WIDTH)),
            const((2, WINDOW, SWA_ROWS)),
            const((1, SWA_ROWS)),
        ],
        out_specs=pl.BlockSpec((1, tq, SWA_WIDTH), lambda bi, ti: (bi, ti, 0)),
        out_shape=jax.ShapeDtypeStruct((b, t, SWA_WIDTH), BF16),
        compiler_params=pltpu.CompilerParams(
            dimension_semantics=("arbitrary", "arbitrary"), vmem_limit_bytes=VMEM_LIMIT),
        name="swa",
    )(sq, skv, skv, qw_x, kw_x, bdq, bdk, bias, sink_col)


def _ffn_kernel(x_ref, og_ref, os_ref, mod_ref, nw_ref, wo_ref, wg_ref, wu_ref, wd_ref, o_ref):
    x = x_ref[0]
    mod = mod_ref[0]
    gate1 = mod[:, 2 * D_MODEL:3 * D_MODEL]
    shift2 = mod[:, 3 * D_MODEL:4 * D_MODEL]
    scale2 = mod[:, 4 * D_MODEL:5 * D_MODEL]
    gate2 = mod[:, 5 * D_MODEL:]
    mixed = _dot(og_ref[0], wo_ref[0:GDN_WIDTH, :]) + _dot(os_ref[0], wo_ref[GDN_WIDTH:, :])
    x1 = x + gate1 * mixed
    ms = jnp.mean(x1 * x1, axis=-1, keepdims=True)
    hb = ((x1 * lax.rsqrt(ms + EPS) * nw_ref[...]) * (1.0 + scale2) + shift2).astype(BF16)
    gt = _dot(hb, wg_ref[...])
    up = _dot(hb, wu_ref[...])
    act = ((gt * _sigmoid(gt)) * up).astype(BF16)
    o_ref[0] = x1 + gate2 * _dot(act, wd_ref[...])


def _ffn_call(x, og, osw, mod3, norm_w, wo, wg, wu, wd):
    b, t, _ = x.shape
    tm = FFN_TM
    row = lambda width: pl.BlockSpec((1, tm, width), lambda bi, ti: (bi, ti, 0))
    const = lambda shape: pl.BlockSpec(shape, lambda bi, ti: (0,) * len(shape), pipeline_mode=pl.Buffered(1))
    return pl.pallas_call(
        _ffn_kernel,
        grid=(b, t // tm),
        in_specs=[
            row(D_MODEL), row(GDN_WIDTH), row(SWA_WIDTH),
            pl.BlockSpec((1, 1, 6 * D_MODEL), lambda bi, ti: (bi, 0, 0)),
            const((1, D_MODEL)),
            const((GDN_WIDTH + SWA_WIDTH, D_MODEL)),
            const((D_MODEL, D_FF)), const((D_MODEL, D_FF)), const((D_FF, D_MODEL)),
        ],
        out_specs=row(D_MODEL),
        out_shape=jax.ShapeDtypeStruct((b, t, D_MODEL), F32),
        compiler_params=pltpu.CompilerParams(
            dimension_semantics=("arbitrary", "arbitrary"), vmem_limit_bytes=VMEM_LIMIT),
        name="ffn",
    )(x, og, osw, mod3, norm_w, wo, wg, wu, wd)


def _layer(x, c_pad, w_ada, b_ada, norm1_w, w_in, conv_w, a_log, dt_bias, gdn_norm_w, q_norm_w, k_norm_w, sinks,
           w_out, norm2_w, w_gate, w_up, w_down):
    b = x.shape[0]
    mod, w_aligned, w_regrouped = _ada_call(c_pad, w_ada, b_ada.reshape(1, -1), w_in.T)
    mod3 = mod[:b].reshape(b, 1, 6 * D_MODEL)
    qkv, z, sq, skv, ab = _proj_call(x, mod3, norm1_w.reshape(1, D_MODEL), w_aligned, w_regrouped,
                                     conv_w.reshape(CONV_WIDTH, QKV_W))

    alog_pad = jnp.pad(a_log.reshape(1, GDN_HEADS), ((0, 0), (0, GATE_PAD - GDN_HEADS)))
    dtb_pad = jnp.pad(dt_bias.reshape(1, GDN_HEADS), ((0, 0), (0, GATE_PAD - GDN_HEADS)))
    gnw_x = jnp.tile(gdn_norm_w.reshape(1, HEAD_DIM), (1, GDN_HEADS))
    og, wg16, wu16, wd16, wo16 = _gdn_call(qkv, z, ab, alog_pad, dtb_pad, gnw_x, w_gate, w_up, w_down, w_out)

    qw_x = jnp.tile(q_norm_w.reshape(1, HEAD_DIM), (1, SWA_Q_HEADS))
    kw_x = jnp.tile(k_norm_w.reshape(1, HEAD_DIM), (1, SWA_KV_HEADS))
    sink_pairs = jnp.stack([sinks[:SWA_GROUP], sinks[SWA_GROUP:]], axis=1)
    sink_col = (jnp.repeat(sink_pairs, WINDOW, axis=1) * LOG2E).reshape(1, SWA_ROWS)
    osw = _swa_call(sq, skv, qw_x, kw_x, sink_col)

    return _ffn_call(x, og, osw, mod3, norm2_w.reshape(1, D_MODEL), wo16, wg16, wu16, wd16)


def kernel(x, c, w_ada, b_ada, norm1_w, w_in, conv_w, a_log, dt_bias, gdn_norm_w, q_norm_w, k_norm_w, sinks,
           w_out, norm2_w, w_gate, w_up, w_down):
    depth = w_ada.shape[0]
    b = c.shape[0]
    c_pad = jnp.pad(c, ((0, 8 - b), (0, 0)))
    for l in range(depth):
        x = _layer(x, c_pad, w_ada[l], b_ada[l], norm1_w[l], w_in[l], conv_w[l], a_log[l], dt_bias[l],
                   gdn_norm_w[l], q_norm_w[l], k_norm_w[l], sinks[l], w_out[l], norm2_w[l], w_gate[l], w_up[l],
                   w_down[l])
    return x
```

```python
import functools

import numpy as np
import jax
import jax.numpy as jnp
from jax import lax
from jax.experimental import pallas as pl
from jax.experimental.pallas import tpu as pltpu

F32 = jnp.float32
BF16 = jnp.bfloat16

D_MODEL = 1024
HEAD_DIM = 64
GDN_HEADS = 8
GDN_WIDTH = GDN_HEADS * HEAD_DIM
SWA_Q_HEADS = 8
SWA_KV_HEADS = 2
SWA_GROUP = SWA_Q_HEADS // SWA_KV_HEADS
SWA_WIDTH = SWA_Q_HEADS * HEAD_DIM
SWA_KV_WIDTH = SWA_KV_HEADS * HEAD_DIM
WINDOW = 128
CONV_WIDTH = 4
CHUNK = 64
D_FF = 2816
EPS = 1e-6
LANES = 128
GATE_PAD = LANES
HEADS_PER_GROUP = 2
GROUP_W = HEADS_PER_GROUP * HEAD_DIM
INV_BASE = 16
INV_LEVELS = (16, 32)
NEG_BIG = -1e30
VMEM_LIMIT = 56 * 1024 * 1024

PROJ_TM = 512
GDN_TT = 256
SWA_TQ = 1024
FFN_TM = 512
ADA_TN = 1536


def _sigmoid(x):
    return 1.0 / (1.0 + jnp.exp(-x))


def _dot(a, b):
    return jnp.dot(a, b, preferred_element_type=F32)


def _dot_nt(a, b):
    return lax.dot_general(a, b, (((1,), (1,)), ((), ())), preferred_element_type=F32)


def _dot_tn(a, b):
    return lax.dot_general(a, b, (((0,), (0,)), ((), ())), preferred_element_type=F32)


def _split2(x):
    hi = x.astype(BF16)
    lo = (x - hi.astype(F32)).astype(BF16)
    return hi, lo


def _ada_kernel(c_ref, w_ref, b_ref, win_ref, o_ref, wa16_ref, wb16_ref):
    c = c_ref[...]
    ca = c * _sigmoid(c)
    o_ref[...] = _dot(ca.astype(BF16), w_ref[...].astype(BF16)) + b_ref[...]

    @pl.when(pl.program_id(0) == 0)
    def _():
        wa16_ref[...] = win_ref[0:PROJ_ALIGNED, :].T.astype(BF16)
        o1 = PROJ_ALIGNED + 2 * GDN_HEADS
        o2 = o1 + SWA_WIDTH
        head = lambda h: win_ref[o1 + h * HEAD_DIM:o1 + (h + 1) * HEAD_DIM, :]
        pieces = [head(half * SWA_GROUP + p) for p in range(SWA_PAIRS) for half in range(2)]
        pieces.append(win_ref[o2:o2 + 2 * SWA_KV_WIDTH, :])
        pieces.append(win_ref[PROJ_ALIGNED:o1, :])
        pieces.append(jnp.zeros((GATE_PAD - 2 * GDN_HEADS, D_MODEL), F32))
        wb16_ref[...] = jnp.concatenate(pieces, axis=0).T.astype(BF16)


def _ada_call(c_pad, w_ada, b_ada, w_in):
    n = w_ada.shape[1]
    whole = lambda shape: pl.BlockSpec(shape, lambda j: (0, 0))
    return pl.pallas_call(
        _ada_kernel,
        grid=(n // ADA_TN,),
        in_specs=[
            whole((8, D_MODEL)),
            pl.BlockSpec((D_MODEL, ADA_TN), lambda j: (0, j)),
            pl.BlockSpec((1, ADA_TN), lambda j: (0, j)),
            pl.BlockSpec(w_in.shape, lambda j: (0, 0), pipeline_mode=pl.Buffered(1)),
        ],
        out_specs=[pl.BlockSpec((8, ADA_TN), lambda j: (0, j)),
                   whole((D_MODEL, PROJ_ALIGNED)), whole((D_MODEL, PROJ_REGROUPED))],
        out_shape=[jax.ShapeDtypeStruct((8, n), F32),
                   jax.ShapeDtypeStruct((D_MODEL, PROJ_ALIGNED), BF16),
                   jax.ShapeDtypeStruct((D_MODEL, PROJ_REGROUPED), BF16)],
        compiler_params=pltpu.CompilerParams(
            dimension_semantics=("arbitrary",), vmem_limit_bytes=VMEM_LIMIT),
        name="ada",
    )(c_pad, w_ada, b_ada, w_in)


QKV_W = 3 * GDN_WIDTH
PROJ_SPLITS = (QKV_W, GDN_WIDTH, SWA_WIDTH, 2 * SWA_KV_WIDTH, GATE_PAD)
PROJ_ALIGNED = QKV_W + GDN_WIDTH
PROJ_REGROUPED = sum(PROJ_SPLITS) - PROJ_ALIGNED


def _proj_kernel(x_ref, mod_ref, nw_ref, wa_ref, wb_ref, cw_ref, bd_ref, qkv_ref, z_ref, sq_ref, skv_ref, ab_ref,
                 qbuf, kbuf, vbuf):
    tm = PROJ_TM
    bufs = (qbuf, kbuf, vbuf)

    @pl.when(pl.program_id(1) == 0)
    def _():
        for buf in bufs:
            buf[0:8, :] = jnp.zeros((8, GDN_WIDTH), F32)

    x = x_ref[0]
    ms = jnp.mean(x * x, axis=-1, keepdims=True)
    y = x * lax.rsqrt(ms + EPS) * nw_ref[...]
    mod = mod_ref[0]
    shift = mod[:, 0:D_MODEL]
    scale = mod[:, D_MODEL:2 * D_MODEL]
    hb = (y * (1.0 + scale) + shift).astype(BF16)

    def project(s):
        bufs[s][8:8 + tm, :] = _dot(hb, wa_ref[:, s * GDN_WIDTH:(s + 1) * GDN_WIDTH])

    def conv_silu(s):
        buf = bufs[s]
        cw = lambda j: cw_ref[j:j + 1, s * GDN_WIDTH:(s + 1) * GDN_WIDTH]
        acc = cw(CONV_WIDTH - 1) * buf[8:8 + tm, :]
        for j in range(CONV_WIDTH - 1):
            off = 8 - (CONV_WIDTH - 1) + j
            acc = acc + cw(j) * buf[off:off + tm, :]
        buf[0:8, :] = buf[tm:tm + 8, :]
        return acc * _sigmoid(acc)

    def l2_normed(a, scale):
        return a * lax.rsqrt(_dot((a * a).astype(BF16), bd_ref[...]) + EPS) * scale

    project(0)
    project(1)
    qkv_ref[0, :, 0:GDN_WIDTH] = l2_normed(conv_silu(0), HEAD_DIM ** -0.5)
    project(2)
    qkv_ref[0, :, GDN_WIDTH:2 * GDN_WIDTH] = l2_normed(conv_silu(1), 1.0)
    z_ref[0] = _dot(hb, wa_ref[:, QKV_W:])
    qkv_ref[0, :, 2 * GDN_WIDTH:] = conv_silu(2)
    col = 0
    for ref in (sq_ref, skv_ref, ab_ref):
        ref[0] = _dot(hb, wb_ref[:, col:col + ref.shape[-1]])
        col += ref.shape[-1]


def _head_sum_matrix(width):
    h = np.arange(width) // HEAD_DIM
    return jnp.asarray((h[:, None] == h[None, :]).astype(np.float32), BF16)


def _proj_call(x, mod3, norm_w, w_aligned, w_regrouped, conv_w):
    b, t, _ = x.shape
    tm = PROJ_TM
    row = lambda width: pl.BlockSpec((1, tm, width), lambda bi, ti: (bi, ti, 0))
    const = lambda shape: pl.BlockSpec(shape, lambda bi, ti: (0,) * len(shape))
    return pl.pallas_call(
        _proj_kernel,
        grid=(b, t // tm),
        in_specs=[
            row(D_MODEL),
            pl.BlockSpec((1, 1, 6 * D_MODEL), lambda bi, ti: (bi, 0, 0)),
            const((1, D_MODEL)),
            const((D_MODEL, PROJ_ALIGNED)),
            const((D_MODEL, PROJ_REGROUPED)),
            const((CONV_WIDTH, QKV_W)),
            const((GDN_WIDTH, GDN_WIDTH)),
        ],
        out_specs=[row(w) for w in PROJ_SPLITS],
        out_shape=[jax.ShapeDtypeStruct((b, t, w), F32) for w in PROJ_SPLITS],
        scratch_shapes=[pltpu.VMEM((8 + tm, GDN_WIDTH), F32)] * 3,
        compiler_params=pltpu.CompilerParams(
            dimension_semantics=("arbitrary", "arbitrary"), vmem_limit_bytes=VMEM_LIMIT),
        name="proj",
    )(x, mod3, norm_w, w_aligned, w_regrouped, conv_w, _head_sum_matrix(GDN_WIDTH))


GDN_UNITS = (GDN_TT // CHUNK) * (GDN_HEADS // HEADS_PER_GROUP)


def _block_diag_rows(p, same_head):
    stacked = jnp.concatenate([p.astype(BF16)] * HEADS_PER_GROUP, axis=0)
    return jnp.where(same_head, stacked, jnp.zeros((), BF16))


_DONE = object()


def _gdn_kernel(qkv_ref, z_ref, ab_ref, alog_ref, dtb_ref, gnw_ref, bd_ref, eg_ref, eb_ref, ltri_ref,
                wg_ref, wu_ref, wd_ref, wo_ref, o_ref, wg16_ref, wu16_ref, wd16_ref, wo16_ref,
                s_ref, obuf, wq_buf, u_buf, qk_buf, kd_buf, dec_buf, *, tiles_per_seq):
    tt = GDN_TT
    step = pl.program_id(0)
    for src, dst in ((wg_ref, wg16_ref), (wu_ref, wu16_ref), (wd_ref, wd16_ref), (wo_ref, wo16_ref)):
        dst[...] = src[...].astype(BF16)
    wslot = step % 2
    rslot = 1 - wslot
    starts_sequence = (step - 1) % tiles_per_seq == 0

    @pl.when(step == 0)
    def _():
        for buf in (s_ref, wq_buf, u_buf, qk_buf, kd_buf, dec_buf):
            buf[...] = jnp.zeros_like(buf)

    bd = bd_ref[...]
    q = qkv_ref[0, :, 0:GDN_WIDTH]
    k = qkv_ref[0, :, GDN_WIDTH:2 * GDN_WIDTH]
    v = qkv_ref[0, :, 2 * GDN_WIDTH:]

    ab = ab_ref[0]
    lane = lax.broadcasted_iota(jnp.int32, (1, GATE_PAD), 1)
    xs = ab + dtb_ref[...]
    softplus = jnp.maximum(xs, 0.0) + jnp.log1p(jnp.exp(-jnp.abs(xs)))
    g = jnp.where(lane < GDN_HEADS, -jnp.exp(alog_ref[...]) * softplus, 0.0)
    beta = _sigmoid(ab)
    ltri = ltri_ref[...]
    g1, g2 = _split2(g)
    gcum = _dot(ltri, g1) + _dot(ltri, g2)
    eg = eg_ref[...]
    c1, c2 = _split2(gcum)
    g_x = _dot(c1, eg) + _dot(c2, eg)
    beta_x = _dot(beta.astype(BF16), eb_ref[...])

    r64 = lax.broadcasted_iota(jnp.int32, (CHUNK, GROUP_W), 0)
    l64 = lax.broadcasted_iota(jnp.int32, (CHUNK, GROUP_W), 1) % CHUNK
    causal = r64 >= l64
    eye_x = (r64 == l64).astype(F32)
    strict_x = (r64 > l64).astype(F32)
    same_block = lambda size: (r64 // size) == (l64 // size)
    base_x = same_block(INV_BASE).astype(F32)
    level_x = [(same_block(2 * size) & ~same_block(size)).astype(F32) for size in INV_LEVELS]
    rb = lax.broadcasted_iota(jnp.int32, (GROUP_W, GROUP_W), 0) // HEAD_DIM
    cb = lax.broadcasted_iota(jnp.int32, (GROUP_W, GROUP_W), 1) // HEAD_DIM
    same_head = rb == cb
    mask_bd = same_head.astype(F32)

    n_groups = GDN_HEADS // HEADS_PER_GROUP
    n_chunks = tt // CHUNK
    units = [(c, gi) for c in range(n_chunks) for gi in range(n_groups)]
    ids = list(range(len(units)))
    tile = lambda a: [a[c * CHUNK:(c + 1) * CHUNK, gi * GROUP_W:(gi + 1) * GROUP_W] for c, gi in units]
    each = lambda f, *lists: [f(*args) for args in zip(*lists)]
    bdr = lambda t: _block_diag_rows(t, same_head)
    mm = lambda a, w16: _dot(a.astype(BF16), w16)
    stack = lambda a, b_: jnp.concatenate([a, b_], axis=0)

    def chunk_parallel_part():
        kc, qc, vc, bx, gx = tile(k), tile(q), tile(v), tile(beta_x), tile(g_x)
        eg_c = each(jnp.exp, gx)
        glast = each(lambda g_: g_[CHUNK - 1:CHUNK, :], gx)
        kb = each(jnp.multiply, kc, bx)
        vb = each(jnp.multiply, vc, bx)
        wr = each(jnp.multiply, kb, eg_c)
        qd = each(jnp.multiply, qc, eg_c)
        for i in ids:
            kd_buf[wslot, i] = (kc[i] * jnp.exp(glast[i] - gx[i])).astype(BF16)
            dec_buf[wslot, i] = jnp.broadcast_to(jnp.exp(glast[i]), (8, GROUP_W))
        dm = each(lambda g_: jnp.exp(jnp.where(causal, g_ - jnp.sum(g_ * eye_x, axis=0, keepdims=True), NEG_BIG)),
                  gx)
        yield
        aq = each(lambda i, a, b_, k_: _dot_nt(stack(a, b_).astype(BF16), bdr(k_)), ids, kb, qc, kc)
        xm = each(lambda a, d_: -(a[0:CHUNK] * d_ * strict_x), aq, dm)
        for i in ids:
            qk_buf[wslot, i] = (aq[i][CHUNK:] * dm[i]).astype(BF16)
        xd = each(lambda x_: x_ * base_x, xm)
        tinv = each(lambda x_: eye_x + x_, xd)
        yield
        pw = each(lambda i, x_: mm(x_, bdr(x_)), ids, xd)
        yield
        for _ in range(INV_BASE.bit_length() - 3):
            r = each(lambda i, t_, p_: mm(stack(t_, p_), bdr(p_)), ids, tinv, pw)
            tinv = each(lambda t_, r_: t_ + r_[0:CHUNK], tinv, r)
            pw = each(lambda r_: r_[CHUNK:], r)
            yield
        tinv = each(lambda i, t_, p_: t_ + mm(t_, bdr(p_)), ids, tinv, pw)
        yield
        for lm in level_x:
            e = each(lambda i, x_, t_: mm(x_ * lm, bdr(t_)), ids, xm, tinv)
            yield
            tinv = each(lambda i, t_, e_: t_ + mm(t_, bdr(e_)), ids, tinv, e)
            yield
        for i in ids:
            u_buf[wslot, i] = mm(tinv[i], bdr(vb[i]))
            wq_buf[wslot, i, 0:CHUNK] = mm(tinv[i], bdr(wr[i])).astype(BF16)
            wq_buf[wslot, i, CHUNK:] = qd[i].astype(BF16)

    def sequential_part():
        s_state = [jnp.where(starts_sequence, 0.0, s_ref[gi]) for gi in range(n_groups)]
        for c in range(n_chunks):
            cids = [c * n_groups + gi for gi in range(n_groups)]
            r2 = [_dot(wq_buf[rslot, i], s_state[gi].astype(BF16)) for gi, i in enumerate(cids)]
            yield
            vn = [u_buf[rslot, i] - r2[gi][0:CHUNK] for gi, i in enumerate(cids)]
            o = [r2[gi][CHUNK:] + _dot(qk_buf[rslot, i], bdr(vn[gi])) for gi, i in enumerate(cids)]
            s_state = [s_state[gi] * dec_buf[rslot, i][0:1] + mask_bd * _dot_tn(kd_buf[rslot, i], vn[gi].astype(BF16))
                       for gi, i in enumerate(cids)]
            for gi in range(n_groups):
                obuf[c * CHUNK:(c + 1) * CHUNK, gi * GROUP_W:(gi + 1) * GROUP_W] = o[gi]
            yield
        for gi in range(n_groups):
            s_ref[gi] = s_state[gi]

    parts = [chunk_parallel_part(), sequential_part()]
    while parts:
        parts = [p for p in parts if next(p, _DONE) is not _DONE]

    o = obuf[...]
    ms = _dot((o * o).astype(BF16), bd) * (1.0 / HEAD_DIM)
    zz = z_ref[0]
    o_ref[0] = (o * lax.rsqrt(ms + EPS) * gnw_ref[...] * (zz * _sigmoid(zz))).astype(BF16)


def _gdn_consts():
    h = np.arange(GDN_WIDTH) // HEAD_DIM
    eg = np.zeros((GATE_PAD, GDN_WIDTH), np.float32)
    eb = np.zeros((GATE_PAD, GDN_WIDTH), np.float32)
    eg[h, np.arange(GDN_WIDTH)] = 1.0
    eb[GDN_HEADS + h, np.arange(GDN_WIDTH)] = 1.0
    t = np.arange(GDN_TT)
    ltri = ((t[:, None] // CHUNK == t[None, :] // CHUNK) & (t[:, None] >= t[None, :])).astype(np.float32)
    return (_head_sum_matrix(GDN_WIDTH), jnp.asarray(eg, BF16), jnp.asarray(eb, BF16), jnp.asarray(ltri, BF16))


BF16_ROWS = 16


def _swa_head_source(i):
    j = i - GDN_HEADS
    return jnp.where(i < GDN_HEADS, i, GDN_HEADS + (j % 2) * SWA_GROUP + j // 2)


def _cast_specs(weight, unit, steps, source=None):
    rows, cols = weight.shape
    block_rows = next(r for r in range(unit, rows + 1, unit) if rows % r == 0 and rows // r <= steps)
    if source is None:
        source = lambda i: i
    else:
        assert block_rows == unit, "row regrouping needs one unit per block"
    last = rows // block_rows - 1
    src = pl.BlockSpec((block_rows, cols), lambda n: (source(jnp.minimum(n, last)), 0))
    dst = pl.BlockSpec((block_rows, cols), lambda n: (jnp.minimum(n, last), 0))
    return src, dst, jax.ShapeDtypeStruct((rows, cols), BF16)


def _gdn_call(qkv, z, ab, alog_pad, dtb_pad, gnw_x, w_gate, w_up, w_down, w_out):
    b, t, _ = qkv.shape
    tt = GDN_TT
    bd, eg, eb, ltri = _gdn_consts()
    n_tiles = t // tt
    n_groups = GDN_HEADS // HEADS_PER_GROUP
    total = b * n_tiles

    def tile_block(width, lag):
        def index(n):
            m = jnp.clip(n - lag, 0, total - 1)
            return (m // n_tiles, m % n_tiles, 0)
        return pl.BlockSpec((1, tt, width), index)

    ahead = lambda width: tile_block(width, 0)
    behind = lambda width: tile_block(width, 1)
    const = lambda shape: pl.BlockSpec(shape, lambda n: (0,) * len(shape))
    per_unit = lambda rows, dtype: pltpu.VMEM((2, GDN_UNITS, rows, GROUP_W), dtype)
    steps = total + 1
    casts = [
        _cast_specs(w_gate, BF16_ROWS, steps),
        _cast_specs(w_up, BF16_ROWS, steps),
        _cast_specs(w_down, BF16_ROWS, steps),
        _cast_specs(w_out, HEAD_DIM, steps, _swa_head_source),
    ]
    return pl.pallas_call(
        functools.partial(_gdn_kernel, tiles_per_seq=n_tiles),
        grid=(steps,),
        in_specs=[
            ahead(QKV_W), behind(GDN_WIDTH), ahead(GATE_PAD),
            const((1, GATE_PAD)), const((1, GATE_PAD)), const((1, GDN_WIDTH)),
            const((GDN_WIDTH, GDN_WIDTH)), const((GATE_PAD, GDN_WIDTH)), const((GATE_PAD, GDN_WIDTH)),
            const((tt, tt)),
        ] + [c[0] for c in casts],
        out_specs=[behind(GDN_WIDTH)] + [c[1] for c in casts],
        out_shape=[jax.ShapeDtypeStruct((b, t, GDN_WIDTH), BF16)] + [c[2] for c in casts],
        scratch_shapes=[
            pltpu.VMEM((n_groups, GROUP_W, GROUP_W), F32),
            pltpu.VMEM((tt, GDN_WIDTH), F32),
            per_unit(2 * CHUNK, BF16), per_unit(CHUNK, F32), per_unit(CHUNK, BF16), per_unit(CHUNK, BF16),
            per_unit(8, F32),
        ],
        compiler_params=pltpu.CompilerParams(
            dimension_semantics=("arbitrary",), vmem_limit_bytes=VMEM_LIMIT),
        name="gdn",
    )(qkv, z, ab, alog_pad, dtb_pad, gnw_x, bd, eg, eb, ltri, w_gate, w_up, w_down, w_out)


SWA_PAIRS = SWA_Q_HEADS // 2
SWA_ROWS = SWA_Q_HEADS * WINDOW
LOG2E = 1.4426950408889634


def _swa_kernel(q_ref, kvc_ref, kvp_ref, qw_ref, kw_ref, bdq_ref, bdk_ref, bias_ref, sink_ref, o_ref):
    tq = SWA_TQ
    ti = pl.program_id(1)
    q = q_ref[0]
    q = q * lax.rsqrt(_dot((q * q).astype(BF16), bdq_ref[...]) * (1.0 / HEAD_DIM) + EPS)
    q = q * (qw_ref[...] * (HEAD_DIM ** -0.5 * LOG2E))
    kv = jnp.concatenate([kvp_ref[0], kvc_ref[0]], axis=0)
    k = kv[:, 0:SWA_KV_WIDTH]
    v = kv[:, SWA_KV_WIDTH:]
    k = k * lax.rsqrt(_dot((k * k).astype(BF16), bdk_ref[...]) * (1.0 / HEAD_DIM) + EPS) * kw_ref[...]
    k16 = k.astype(BF16)
    lo = lax.broadcasted_iota(jnp.int32, (1, LANES), 1) < HEAD_DIM
    qcol = lax.broadcasted_iota(jnp.int32, (WINDOW, SWA_ROWS), 1) % WINDOW
    from_prev = lax.broadcasted_iota(jnp.int32, (WINDOW, SWA_ROWS), 0) > qcol
    first = jnp.where(ti == 0, 1, 0)
    sink = sink_ref[...]
    vt16 = v.T.astype(BF16)
    zero = jnp.zeros((), BF16)
    for j in range(tq // WINDOW):
        qj = q[j * WINDOW:(j + 1) * WINDOW]
        parts = []
        for p in range(SWA_PAIRS):
            qp = qj[:, p * LANES:(p + 1) * LANES]
            parts += [jnp.where(lo, qp, 0.0), jnp.where(lo, 0.0, qp)]
        qs = jnp.concatenate(parts, axis=0).astype(BF16)
        st = _dot_nt(k16[j * WINDOW:(j + 2) * WINDOW], qs)
        bias = bias_ref[first] if j == 0 else bias_ref[0]
        sm = jnp.where(from_prev, st[0:WINDOW], st[WINDOW:]) + bias
        m = jnp.max(sm, axis=0, keepdims=True)
        pe = jnp.exp2(sm - m)
        den = jnp.sum(pe, axis=0, keepdims=True) + jnp.exp2(sink - m)
        pb = pe.astype(BF16)
        pt2 = jnp.concatenate([jnp.where(from_prev, pb, zero), jnp.where(from_prev, zero, pb)], axis=0)
        ot = _dot(vt16[:, j * WINDOW:(j + 2) * WINDOW], pt2) * (1.0 / den)
        for p in range(SWA_PAIRS):
            c0 = p * 2 * WINDOW
            pair_t = jnp.concatenate([ot[0:HEAD_DIM, c0:c0 + WINDOW],
                                      ot[HEAD_DIM:, c0 + WINDOW:c0 + 2 * WINDOW]], axis=0)
            o_ref[0, j * WINDOW:(j + 1) * WINDOW, p * LANES:(p + 1) * LANES] = pair_t.T.astype(BF16)


def _swa_consts():
    qi = np.arange(WINDOW)[:, None]
    kj = np.arange(WINDOW)[None, :]
    from_prev = kj > qi
    dist = np.where(from_prev, qi + WINDOW - kj, qi - kj).astype(np.float32)
    slopes = 2.0 ** (-8.0 * (np.arange(SWA_Q_HEADS, dtype=np.float32) + 1.0) / SWA_Q_HEADS)
    bias = np.zeros((2, SWA_PAIRS, 2, WINDOW, WINDOW), np.float32)
    for first in range(2):
        for p in range(SWA_PAIRS):
            for half, head in enumerate((p, SWA_GROUP + p)):
                b = (-slopes[head].astype(np.float32) * dist) * np.float32(LOG2E)
                bias[first, p, half] = np.where(from_prev & bool(first), np.float32(NEG_BIG), b)
    bias = bias.reshape(2, SWA_ROWS, WINDOW).transpose(0, 2, 1)
    h = np.arange(SWA_WIDTH) // HEAD_DIM
    bdq = (h[:, None] == h[None, :]).astype(np.float32)
    hk = np.arange(SWA_KV_WIDTH) // HEAD_DIM
    bdk = (hk[:, None] == hk[None, :]).astype(np.float32)
    return jnp.asarray(bias), jnp.asarray(bdq, BF16), jnp.asarray(bdk, BF16)


def _swa_call(sq, skv, qw_x, kw_x, sink_col):
    b, t, _ = sq.shape
    tq = SWA_TQ
    nb = tq // WINDOW
    bias, bdq, bdk = _swa_consts()
    const = lambda shape: pl.BlockSpec(shape, lambda bi, ti: (0,) * len(shape))
    return pl.pallas_call(
        _swa_kernel,
        grid=(b, t // tq),
        in_specs=[
            pl.BlockSpec((1, tq, SWA_WIDTH), lambda bi, ti: (bi, ti, 0)),
            pl.BlockSpec((1, tq, 2 * SWA_KV_WIDTH), lambda bi, ti: (bi, ti, 0)),
            pl.BlockSpec((1, WINDOW, 2 * SWA_KV_WIDTH), lambda bi, ti: (bi, jnp.maximum(ti * nb - 1, 0), 0)),
            const((1, SWA_WIDTH)), const((1, SWA_KV_WIDTH)),
            const((SWA_WIDTH, SWA_WIDTH)), const((SWA_KV_WIDTH, SWA_KV_WIDTH)),
            const((2, WINDOW, SWA_ROWS)),
            const((1, SWA_ROWS)),
        ],
        out_specs=pl.BlockSpec((1, tq, SWA_WIDTH), lambda bi, ti: (bi, ti, 0)),
        out_shape=jax.ShapeDtypeStruct((b, t, SWA_WIDTH), BF16),
        compiler_params=pltpu.CompilerParams(
            dimension_semantics=("arbitrary", "arbitrary"), vmem_limit_bytes=VMEM_LIMIT),
        name="swa",
    )(sq, skv, skv, qw_x, kw_x, bdq, bdk, bias, sink_col)


def _ffn_kernel(x_ref, og_ref, os_ref, mod_ref, nw_ref, wo_ref, wg_ref, wu_ref, wd_ref, o_ref):
    x = x_ref[0]
    mod = mod_ref[0]
    gate1 = mod[:, 2 * D_MODEL:3 * D_MODEL]
    shift2 = mod[:, 3 * D_MODEL:4 * D_MODEL]
    scale2 = mod[:, 4 * D_MODEL:5 * D_MODEL]
    gate2 = mod[:, 5 * D_MODEL:]
    mixed = _dot(og_ref[0], wo_ref[0:GDN_WIDTH, :]) + _dot(os_ref[0], wo_ref[GDN_WIDTH:, :])
    x1 = x + gate1 * mixed
    ms = jnp.mean(x1 * x1, axis=-1, keepdims=True)
    hb = ((x1 * lax.rsqrt(ms + EPS) * nw_ref[...]) * (1.0 + scale2) + shift2).astype(BF16)
    gt = _dot(hb, wg_ref[...])
    up = _dot(hb, wu_ref[...])
    act = ((gt * _sigmoid(gt)) * up).astype(BF16)
    o_ref[0] = x1 + gate2 * _dot(act, wd_ref[...])


def _ffn_call(x, og, osw, mod3, norm_w, wo, wg, wu, wd):
    b, t, _ = x.shape
    tm = FFN_TM
    row = lambda width: pl.BlockSpec((1, tm, width), lambda bi, ti: (bi, ti, 0))
    const = lambda shape: pl.BlockSpec(shape, lambda bi, ti: (0,) * len(shape), pipeline_mode=pl.Buffered(1))
    return pl.pallas_call(
        _ffn_kernel,
        grid=(b, t // tm),
        in_specs=[
            row(D_MODEL), row(GDN_WIDTH), row(SWA_WIDTH),
            pl.BlockSpec((1, 1, 6 * D_MODEL), lambda bi, ti: (bi, 0, 0)),
            const((1, D_MODEL)),
            const((GDN_WIDTH + SWA_WIDTH, D_MODEL)),
            const((D_MODEL, D_FF)), const((D_MODEL, D_FF)), const((D_FF, D_MODEL)),
        ],
        out_specs=row(D_MODEL),
        out_shape=jax.ShapeDtypeStruct((b, t, D_MODEL), F32),
        compiler_params=pltpu.CompilerParams(
            dimension_semantics=("arbitrary", "arbitrary"), vmem_limit_bytes=VMEM_LIMIT),
        name="ffn",
    )(x, og, osw, mod3, norm_w, wo, wg, wu, wd)


def _layer(x, c_pad, w_ada, b_ada, norm1_w, w_in, conv_w, a_log, dt_bias, gdn_norm_w, q_norm_w, k_norm_w, sinks,
           w_out, norm2_w, w_gate, w_up, w_down):
    b = x.shape[0]
    mod, w_aligned, w_regrouped = _ada_call(c_pad, w_ada, b_ada.reshape(1, -1), w_in.T)
    mod3 = mod[:b].reshape(b, 1, 6 * D_MODEL)
    qkv, z, sq, skv, ab = _proj_call(x, mod3, norm1_w.reshape(1, D_MODEL), w_aligned, w_regrouped,
                                     conv_w.reshape(CONV_WIDTH, QKV_W))

    alog_pad = jnp.pad(a_log.reshape(1, GDN_HEADS), ((0, 0), (0, GATE_PAD - GDN_HEADS)))
    dtb_pad = jnp.pad(dt_bias.reshape(1, GDN_HEADS), ((0, 0), (0, GATE_PAD - GDN_HEADS)))
    gnw_x = jnp.tile(gdn_norm_w.reshape(1, HEAD_DIM), (1, GDN_HEADS))
    og, wg16, wu16, wd16, wo16 = _gdn_call(qkv, z, ab, alog_pad, dtb_pad, gnw_x, w_gate, w_up, w_down, w_out)

    qw_x = jnp.tile(q_norm_w.reshape(1, HEAD_DIM), (1, SWA_Q_HEADS))
    kw_x = jnp.tile(k_norm_w.reshape(1, HEAD_DIM), (1, SWA_KV_HEADS))
    sink_pairs = jnp.stack([sinks[:SWA_GROUP], sinks[SWA_GROUP:]], axis=1)
    sink_col = (jnp.repeat(sink_pairs, WINDOW, axis=1) * LOG2E).reshape(1, SWA_ROWS)
    osw = _swa_call(sq, skv, qw_x, kw_x, sink_col)

    return _ffn_call(x, og, osw, mod3, norm2_w.reshape(1, D_MODEL), wo16, wg16, wu16, wd16)


def kernel(x, c, w_ada, b_ada, norm1_w, w_in, conv_w, a_log, dt_bias, gdn_norm_w, q_norm_w, k_norm_w, sinks,
           w_out, norm2_w, w_gate, w_up, w_down):
    depth = w_ada.shape[0]
    b = c.shape[0]
    c_pad = jnp.pad(c, ((0, 8 - b), (0, 0)))
    for l in range(depth):
        x = _layer(x, c_pad, w_ada[l], b_ada[l], norm1_w[l], w_in[l], conv_w[l], a_log[l], dt_bias[l],
                   gdn_norm_w[l], q_norm_w[l], k_norm_w[l], sinks[l], w_out[l], norm2_w[l], w_gate[l], w_up[l],
                   w_down[l])
    return x
```

```python
import functools

import numpy as np
import jax
import jax.numpy as jnp
from jax import lax
from jax.experimental import pallas as pl
from jax.experimental.pallas import tpu as pltpu

F32 = jnp.float32
BF16 = jnp.bfloat16

D_MODEL = 1024
HEAD_DIM = 64
GDN_HEADS = 8
GDN_WIDTH = GDN_HEADS * HEAD_DIM
SWA_Q_HEADS = 8
SWA_KV_HEADS = 2
SWA_GROUP = SWA_Q_HEADS // SWA_KV_HEADS
SWA_WIDTH = SWA_Q_HEADS * HEAD_DIM
SWA_KV_WIDTH = SWA_KV_HEADS * HEAD_DIM
WINDOW = 128
CONV_WIDTH = 4
CHUNK = 64
D_FF = 2816
EPS = 1e-6
LANES = 128
GATE_PAD = LANES
HEADS_PER_GROUP = 2
GROUP_W = HEADS_PER_GROUP * HEAD_DIM
INV_BASE = 16
INV_LEVELS = (16, 32)
NEG_BIG = -1e30
VMEM_LIMIT = 56 * 1024 * 1024

PROJ_TM = 512
GDN_TT = 256
SWA_TQ = 1024
FFN_TM = 512
ADA_TN = 1536


def _sigmoid(x):
    return 1.0 / (1.0 + jnp.exp(-x))


def _dot(a, b):
    return jnp.dot(a, b, preferred_element_type=F32)


def _dot_nt(a, b):
    return lax.dot_general(a, b, (((1,), (1,)), ((), ())), preferred_element_type=F32)


def _dot_tn(a, b):
    return lax.dot_general(a, b, (((0,), (0,)), ((), ())), preferred_element_type=F32)


def _split2(x):
    hi = x.astype(BF16)
    lo = (x - hi.astype(F32)).astype(BF16)
    return hi, lo


def _ada_kernel(c_ref, w_ref, b_ref, win_ref, o_ref, wa16_ref, wb16_ref):
    c = c_ref[...]
    ca = c * _sigmoid(c)
    o_ref[...] = _dot(ca.astype(BF16), w_ref[...].astype(BF16)) + b_ref[...]

    @pl.when(pl.program_id(0) == 0)
    def _():
        wa16_ref[...] = win_ref[0:PROJ_ALIGNED, :].T.astype(BF16)
        o1 = PROJ_ALIGNED + 2 * GDN_HEADS
        o2 = o1 + SWA_WIDTH
        head = lambda h: win_ref[o1 + h * HEAD_DIM:o1 + (h + 1) * HEAD_DIM, :]
        pieces = [head(half * SWA_GROUP + p) for p in range(SWA_PAIRS) for half in range(2)]
        pieces.append(win_ref[o2:o2 + 2 * SWA_KV_WIDTH, :])
        pieces.append(win_ref[PROJ_ALIGNED:o1, :])
        pieces.append(jnp.zeros((GATE_PAD - 2 * GDN_HEADS, D_MODEL), F32))
        wb16_ref[...] = jnp.concatenate(pieces, axis=0).T.astype(BF16)


def _ada_call(c_pad, w_ada, b_ada, w_in):
    n = w_ada.shape[1]
    whole = lambda shape: pl.BlockSpec(shape, lambda j: (0, 0))
    return pl.pallas_call(
        _ada_kernel,
        grid=(n // ADA_TN,),
        in_specs=[
            whole((8, D_MODEL)),
            pl.BlockSpec((D_MODEL, ADA_TN), lambda j: (0, j)),
            pl.BlockSpec((1, ADA_TN), lambda j: (0, j)),
            pl.BlockSpec(w_in.shape, lambda j: (0, 0), pipeline_mode=pl.Buffered(1)),
        ],
        out_specs=[pl.BlockSpec((8, ADA_TN), lambda j: (0, j)),
                   whole((D_MODEL, PROJ_ALIGNED)), whole((D_MODEL, PROJ_REGROUPED))],
        out_shape=[jax.ShapeDtypeStruct((8, n), F32),
                   jax.ShapeDtypeStruct((D_MODEL, PROJ_ALIGNED), BF16),
                   jax.ShapeDtypeStruct((D_MODEL, PROJ_REGROUPED), BF16)],
        compiler_params=pltpu.CompilerParams(
            dimension_semantics=("arbitrary",), vmem_limit_bytes=VMEM_LIMIT),
        name="ada",
    )(c_pad, w_ada, b_ada, w_in)


QKV_W = 3 * GDN_WIDTH
PROJ_SPLITS = (QKV_W, GDN_WIDTH, SWA_WIDTH, 2 * SWA_KV_WIDTH, GATE_PAD)
PROJ_ALIGNED = QKV_W + GDN_WIDTH
PROJ_REGROUPED = sum(PROJ_SPLITS) - PROJ_ALIGNED


def _proj_kernel(x_ref, mod_ref, nw_ref, wa_ref, wb_ref, cw_ref, bd_ref, qkv_ref, z_ref, sq_ref, skv_ref, ab_ref,
                 qbuf, kbuf, vbuf):
    tm = PROJ_TM
    bufs = (qbuf, kbuf, vbuf)

    @pl.when(pl.program_id(1) == 0)
    def _():
        for buf in bufs:
            buf[0:8, :] = jnp.zeros((8, GDN_WIDTH), F32)

    x = x_ref[0]
    ms = jnp.mean(x * x, axis=-1, keepdims=True)
    y = x * lax.rsqrt(ms + EPS) * nw_ref[...]
    mod = mod_ref[0]
    shift = mod[:, 0:D_MODEL]
    scale = mod[:, D_MODEL:2 * D_MODEL]
    hb = (y * (1.0 + scale) + shift).astype(BF16)

    def project(s):
        bufs[s][8:8 + tm, :] = _dot(hb, wa_ref[:, s * GDN_WIDTH:(s + 1) * GDN_WIDTH])

    def conv_silu(s):
        buf = bufs[s]
        cw = lambda j: cw_ref[j:j + 1, s * GDN_WIDTH:(s + 1) * GDN_WIDTH]
        acc = cw(CONV_WIDTH - 1) * buf[8:8 + tm, :]
        for j in range(CONV_WIDTH - 1):
            off = 8 - (CONV_WIDTH - 1) + j
            acc = acc + cw(j) * buf[off:off + tm, :]
        buf[0:8, :] = buf[tm:tm + 8, :]
        return acc * _sigmoid(acc)

    def l2_normed(a, scale):
        return a * lax.rsqrt(_dot((a * a).astype(BF16), bd_ref[...]) + EPS) * scale

    project(0)
    project(1)
    qkv_ref[0, :, 0:GDN_WIDTH] = l2_normed(conv_silu(0), HEAD_DIM ** -0.5)
    project(2)
    qkv_ref[0, :, GDN_WIDTH:2 * GDN_WIDTH] = l2_normed(conv_silu(1), 1.0)
    z_ref[0] = _dot(hb, wa_ref[:, QKV_W:])
    qkv_ref[0, :, 2 * GDN_WIDTH:] = conv_silu(2)
    col = 0
    for ref in (sq_ref, skv_ref, ab_ref):
        ref[0] = _dot(hb, wb_ref[:, col:col + ref.shape[-1]])
        col += ref.shape[-1]


def _head_sum_matrix(width):
    h = np.arange(width) // HEAD_DIM
    return jnp.asarray((h[:, None] == h[None, :]).astype(np.float32), BF16)


def _proj_call(x, mod3, norm_w, w_aligned, w_regrouped, conv_w):
    b, t, _ = x.shape
    tm = PROJ_TM
    row = lambda width: pl.BlockSpec((1, tm, width), lambda bi, ti: (bi, ti, 0))
    const = lambda shape: pl.BlockSpec(shape, lambda bi, ti: (0,) * len(shape))
    return pl.pallas_call(
        _proj_kernel,
        grid=(b, t // tm),
        in_specs=[
            row(D_MODEL),
            pl.BlockSpec((1, 1, 6 * D_MODEL), lambda bi, ti: (bi, 0, 0)),
            const((1, D_MODEL)),
            const((D_MODEL, PROJ_ALIGNED)),
            const((D_MODEL, PROJ_REGROUPED)),
            const((CONV_WIDTH, QKV_W)),
            const((GDN_WIDTH, GDN_WIDTH)),
        ],
        out_specs=[row(w) for w in PROJ_SPLITS],
        out_shape=[jax.ShapeDtypeStruct((b, t, w), F32) for w in PROJ_SPLITS],
        scratch_shapes=[pltpu.VMEM((8 + tm, GDN_WIDTH), F32)] * 3,
        compiler_params=pltpu.CompilerParams(
            dimension_semantics=("arbitrary", "arbitrary"), vmem_limit_bytes=VMEM_LIMIT),
        name="proj",
    )(x, mod3, norm_w, w_aligned, w_regrouped, conv_w, _head_sum_matrix(GDN_WIDTH))


GDN_UNITS = (GDN_TT // CHUNK) * (GDN_HEADS // HEADS_PER_GROUP)


def _block_diag_rows(p, same_head):
    stacked = jnp.concatenate([p.astype(BF16)] * HEADS_PER_GROUP, axis=0)
    return jnp.where(same_head, stacked, jnp.zeros((), BF16))


_DONE = object()


def _gdn_kernel(qkv_ref, z_ref, ab_ref, alog_ref, dtb_ref, gnw_ref, bd_ref, eg_ref, eb_ref, ltri_ref,
                wg_ref, wu_ref, wd_ref, wo_ref, o_ref, wg16_ref, wu16_ref, wd16_ref, wo16_ref,
                s_ref, obuf, wq_buf, u_buf, qk_buf, kd_buf, dec_buf, *, tiles_per_seq):
    tt = GDN_TT
    step = pl.program_id(0)
    @pl.when(step < CAST_STEPS)
    def _():
        for src, dst in ((wg_ref, wg16_ref), (wu_ref, wu16_ref), (wd_ref, wd16_ref), (wo_ref, wo16_ref)):
            dst[...] = src[...].astype(BF16)

    wslot = step % 2
    rslot = 1 - wslot
    starts_sequence = (step - 1) % tiles_per_seq == 0

    @pl.when(step == 0)
    def _():
        for buf in (s_ref, wq_buf, u_buf, qk_buf, kd_buf, dec_buf):
            buf[...] = jnp.zeros_like(buf)

    bd = bd_ref[...]
    q = qkv_ref[0, :, 0:GDN_WIDTH]
    k = qkv_ref[0, :, GDN_WIDTH:2 * GDN_WIDTH]
    v = qkv_ref[0, :, 2 * GDN_WIDTH:]

    ab = ab_ref[0]
    lane = lax.broadcasted_iota(jnp.int32, (1, GATE_PAD), 1)
    xs = ab + dtb_ref[...]
    softplus = jnp.maximum(xs, 0.0) + jnp.log1p(jnp.exp(-jnp.abs(xs)))
    g = jnp.where(lane < GDN_HEADS, -jnp.exp(alog_ref[...]) * softplus, 0.0)
    beta = _sigmoid(ab)
    ltri = ltri_ref[...]
    g1, g2 = _split2(g)
    gcum = _dot(ltri, g1) + _dot(ltri, g2)
    eg = eg_ref[...]
    c1, c2 = _split2(gcum)
    g_x = _dot(c1, eg) + _dot(c2, eg)
    beta_x = _dot(beta.astype(BF16), eb_ref[...])

    r64 = lax.broadcasted_iota(jnp.int32, (CHUNK, GROUP_W), 0)
    l64 = lax.broadcasted_iota(jnp.int32, (CHUNK, GROUP_W), 1) % CHUNK
    causal = r64 >= l64
    eye_x = (r64 == l64).astype(F32)
    strict_x = (r64 > l64).astype(F32)
    same_block = lambda size: (r64 // size) == (l64 // size)
    base_x = same_block(INV_BASE).astype(F32)
    level_x = [(same_block(2 * size) & ~same_block(size)).astype(F32) for size in INV_LEVELS]
    rb = lax.broadcasted_iota(jnp.int32, (GROUP_W, GROUP_W), 0) // HEAD_DIM
    cb = lax.broadcasted_iota(jnp.int32, (GROUP_W, GROUP_W), 1) // HEAD_DIM
    same_head = rb == cb
    mask_bd = same_head.astype(F32)

    n_groups = GDN_HEADS // HEADS_PER_GROUP
    n_chunks = tt // CHUNK
    units = [(c, gi) for c in range(n_chunks) for gi in range(n_groups)]
    ids = list(range(len(units)))
    tile = lambda a: [a[c * CHUNK:(c + 1) * CHUNK, gi * GROUP_W:(gi + 1) * GROUP_W] for c, gi in units]
    each = lambda f, *lists: [f(*args) for args in zip(*lists)]
    bdr = lambda t: _block_diag_rows(t, same_head)
    mm = lambda a, w16: _dot(a.astype(BF16), w16)
    stack = lambda a, b_: jnp.concatenate([a, b_], axis=0)

    def chunk_parallel_part():
        kc, qc, vc, bx, gx = tile(k), tile(q), tile(v), tile(beta_x), tile(g_x)
        eg_c = each(jnp.exp, gx)
        glast = each(lambda g_: g_[CHUNK - 1:CHUNK, :], gx)
        kb = each(jnp.multiply, kc, bx)
        vb = each(jnp.multiply, vc, bx)
        wr = each(jnp.multiply, kb, eg_c)
        qd = each(jnp.multiply, qc, eg_c)
        for i in ids:
            kd_buf[wslot, i] = (kc[i] * jnp.exp(glast[i] - gx[i])).astype(BF16)
            dec_buf[wslot, i] = jnp.broadcast_to(jnp.exp(glast[i]), (8, GROUP_W))
        dm = each(lambda g_: jnp.exp(jnp.where(causal, g_ - jnp.sum(g_ * eye_x, axis=0, keepdims=True), NEG_BIG)),
                  gx)
        yield
        aq = each(lambda i, a, b_, k_: _dot_nt(stack(a, b_).astype(BF16), bdr(k_)), ids, kb, qc, kc)
        xm = each(lambda a, d_: -(a[0:CHUNK] * d_ * strict_x), aq, dm)
        for i in ids:
            qk_buf[wslot, i] = (aq[i][CHUNK:] * dm[i]).astype(BF16)
        xd = each(lambda x_: x_ * base_x, xm)
        tinv = each(lambda x_: eye_x + x_, xd)
        yield
        pw = each(lambda i, x_: mm(x_, bdr(x_)), ids, xd)
        yield
        for _ in range(INV_BASE.bit_length() - 3):
            r = each(lambda i, t_, p_: mm(stack(t_, p_), bdr(p_)), ids, tinv, pw)
            tinv = each(lambda t_, r_: t_ + r_[0:CHUNK], tinv, r)
            pw = each(lambda r_: r_[CHUNK:], r)
            yield
        tinv = each(lambda i, t_, p_: t_ + mm(t_, bdr(p_)), ids, tinv, pw)
        yield
        for lm in level_x:
            e = each(lambda i, x_, t_: mm(x_ * lm, bdr(t_)), ids, xm, tinv)
            yield
            tinv = each(lambda i, t_, e_: t_ + mm(t_, bdr(e_)), ids, tinv, e)
            yield
        for i in ids:
            uw = mm(tinv[i], jnp.concatenate([bdr(vb[i]), bdr(wr[i])], axis=1))
            u_buf[wslot, i] = uw[:, 0:GROUP_W]
            wq_buf[wslot, i, 0:CHUNK] = uw[:, GROUP_W:].astype(BF16)
            wq_buf[wslot, i, CHUNK:] = qd[i].astype(BF16)

    def sequential_part():
        s_state = [jnp.where(starts_sequence, 0.0, s_ref[gi]) for gi in range(n_groups)]
        for c in range(n_chunks):
            cids = [c * n_groups + gi for gi in range(n_groups)]
            r2 = [_dot(wq_buf[rslot, i], s_state[gi].astype(BF16)) for gi, i in enumerate(cids)]
            yield
            vn = [u_buf[rslot, i] - r2[gi][0:CHUNK] for gi, i in enumerate(cids)]
            o = [r2[gi][CHUNK:] + _dot(qk_buf[rslot, i], bdr(vn[gi])) for gi, i in enumerate(cids)]
            s_state = [s_state[gi] * dec_buf[rslot, i][0:1] + mask_bd * _dot_tn(kd_buf[rslot, i], vn[gi].astype(BF16))
                       for gi, i in enumerate(cids)]
            for gi in range(n_groups):
                obuf[c * CHUNK:(c + 1) * CHUNK, gi * GROUP_W:(gi + 1) * GROUP_W] = o[gi]
            yield
        for gi in range(n_groups):
            s_ref[gi] = s_state[gi]

    parts = [chunk_parallel_part(), sequential_part()]
    while parts:
        parts = [p for p in parts if next(p, _DONE) is not _DONE]

    o = obuf[...]
    ms = _dot((o * o).astype(BF16), bd) * (1.0 / HEAD_DIM)
    zz = z_ref[0]
    o_ref[0] = (o * lax.rsqrt(ms + EPS) * gnw_ref[...] * (zz * _sigmoid(zz))).astype(BF16)


def _gdn_consts():
    h = np.arange(GDN_WIDTH) // HEAD_DIM
    eg = np.zeros((GATE_PAD, GDN_WIDTH), np.float32)
    eb = np.zeros((GATE_PAD, GDN_WIDTH), np.float32)
    eg[h, np.arange(GDN_WIDTH)] = 1.0
    eb[GDN_HEADS + h, np.arange(GDN_WIDTH)] = 1.0
    t = np.arange(GDN_TT)
    ltri = ((t[:, None] // CHUNK == t[None, :] // CHUNK) & (t[:, None] >= t[None, :])).astype(np.float32)
    return (_head_sum_matrix(GDN_WIDTH), jnp.asarray(eg, BF16), jnp.asarray(eb, BF16), jnp.asarray(ltri, BF16))


BF16_ROWS = 16
CAST_STEPS = 2 * GDN_HEADS


def _swa_head_source(i):
    j = i - GDN_HEADS
    return jnp.where(i < GDN_HEADS, i, GDN_HEADS + (j % 2) * SWA_GROUP + j // 2)


def _cast_specs(weight, unit, steps, source=None):
    rows, cols = weight.shape
    block_rows = next(r for r in range(unit, rows + 1, unit) if rows % r == 0 and rows // r <= steps)
    if source is None:
        source = lambda i: i
    else:
        assert block_rows == unit, "row regrouping needs one unit per block"
    last = rows // block_rows - 1
    src = pl.BlockSpec((block_rows, cols), lambda n: (source(jnp.minimum(n, last)), 0))
    dst = pl.BlockSpec((block_rows, cols), lambda n: (jnp.minimum(n, last), 0))
    return src, dst, jax.ShapeDtypeStruct((rows, cols), BF16)


def _gdn_call(qkv, z, ab, alog_pad, dtb_pad, gnw_x, w_gate, w_up, w_down, w_out):
    b, t, _ = qkv.shape
    tt = GDN_TT
    bd, eg, eb, ltri = _gdn_consts()
    n_tiles = t // tt
    n_groups = GDN_HEADS // HEADS_PER_GROUP
    total = b * n_tiles

    def tile_block(width, lag):
        def index(n):
            m = jnp.clip(n - lag, 0, total - 1)
            return (m // n_tiles, m % n_tiles, 0)
        return pl.BlockSpec((1, tt, width), index)

    ahead = lambda width: tile_block(width, 0)
    behind = lambda width: tile_block(width, 1)
    const = lambda shape: pl.BlockSpec(shape, lambda n: (0,) * len(shape))
    per_unit = lambda rows, dtype: pltpu.VMEM((2, GDN_UNITS, rows, GROUP_W), dtype)
    steps = total + 1
    assert steps >= CAST_STEPS
    casts = [
        _cast_specs(w_gate, BF16_ROWS, CAST_STEPS),
        _cast_specs(w_up, BF16_ROWS, CAST_STEPS),
        _cast_specs(w_down, BF16_ROWS, CAST_STEPS),
        _cast_specs(w_out, HEAD_DIM, CAST_STEPS, _swa_head_source),
    ]
    return pl.pallas_call(
        functools.partial(_gdn_kernel, tiles_per_seq=n_tiles),
        grid=(steps,),
        in_specs=[
            ahead(QKV_W), behind(GDN_WIDTH), ahead(GATE_PAD),
            const((1, GATE_PAD)), const((1, GATE_PAD)), const((1, GDN_WIDTH)),
            const((GDN_WIDTH, GDN_WIDTH)), const((GATE_PAD, GDN_WIDTH)), const((GATE_PAD, GDN_WIDTH)),
            const((tt, tt)),
        ] + [c[0] for c in casts],
        out_specs=[behind(GDN_WIDTH)] + [c[1] for c in casts],
        out_shape=[jax.ShapeDtypeStruct((b, t, GDN_WIDTH), BF16)] + [c[2] for c in casts],
        scratch_shapes=[
            pltpu.VMEM((n_groups, GROUP_W, GROUP_W), F32),
            pltpu.VMEM((tt, GDN_WIDTH), F32),
            per_unit(2 * CHUNK, BF16), per_unit(CHUNK, F32), per_unit(CHUNK, BF16), per_unit(CHUNK, BF16),
            per_unit(8, F32),
        ],
        compiler_params=pltpu.CompilerParams(
            dimension_semantics=("arbitrary",), vmem_limit_bytes=VMEM_LIMIT),
        name="gdn",
    )(qkv, z, ab, alog_pad, dtb_pad, gnw_x, bd, eg, eb, ltri, w_gate, w_up, w_down, w_out)


SWA_PAIRS = SWA_Q_HEADS // 2
SWA_ROWS = SWA_Q_HEADS * WINDOW
LOG2E = 1.4426950408889634


def _swa_kernel(q_ref, kvc_ref, kvp_ref, qw_ref, kw_ref, bdq_ref, bdk_ref, bias_ref, sink_ref, o_ref):
    tq = SWA_TQ
    ti = pl.program_id(1)
    q = q_ref[0]
    q = q * lax.rsqrt(_dot((q * q).astype(BF16), bdq_ref[...]) * (1.0 / HEAD_DIM) + EPS)
    q = q * (qw_ref[...] * (HEAD_DIM ** -0.5 * LOG2E))
    kv = jnp.concatenate([kvp_ref[0], kvc_ref[0]], axis=0)
    k = kv[:, 0:SWA_KV_WIDTH]
    v = kv[:, SWA_KV_WIDTH:]
    k = k * lax.rsqrt(_dot((k * k).astype(BF16), bdk_ref[...]) * (1.0 / HEAD_DIM) + EPS) * kw_ref[...]
    k16 = k.astype(BF16)
    lo = lax.broadcasted_iota(jnp.int32, (1, LANES), 1) < HEAD_DIM
    qcol = lax.broadcasted_iota(jnp.int32, (WINDOW, SWA_ROWS), 1) % WINDOW
    from_prev = lax.broadcasted_iota(jnp.int32, (WINDOW, SWA_ROWS), 0) > qcol
    first = jnp.where(ti == 0, 1, 0)
    sink = sink_ref[...]
    vt16 = v.T.astype(BF16)
    zero = jnp.zeros((), BF16)
    for j in range(tq // WINDOW):
        qj = q[j * WINDOW:(j + 1) * WINDOW]
        parts = []
        for p in range(SWA_PAIRS):
            qp = qj[:, p * LANES:(p + 1) * LANES]
            parts += [jnp.where(lo, qp, 0.0), jnp.where(lo, 0.0, qp)]
        qs = jnp.concatenate(parts, axis=0).astype(BF16)
        st = _dot_nt(k16[j * WINDOW:(j + 2) * WINDOW], qs)
        bias = bias_ref[first] if j == 0 else bias_ref[0]
        sm = jnp.where(from_prev, st[0:WINDOW], st[WINDOW:]) + bias
        m = jnp.max(sm, axis=0, keepdims=True)
        pe = jnp.exp2(sm - m)
        den = jnp.sum(pe, axis=0, keepdims=True) + jnp.exp2(sink - m)
        pb = pe.astype(BF16)
        pt2 = jnp.concatenate([jnp.where(from_prev, pb, zero), jnp.where(from_prev, zero, pb)], axis=0)
        ot = _dot(vt16[:, j * WINDOW:(j + 2) * WINDOW], pt2) * (1.0 / den)
        for p in range(SWA_PAIRS):
            c0 = p * 2 * WINDOW
            pair_t = jnp.concatenate([ot[0:HEAD_DIM, c0:c0 + WINDOW],
                                      ot[HEAD_DIM:, c0 + WINDOW:c0 + 2 * WINDOW]], axis=0)
            o_ref[0, j * WINDOW:(j + 1) * WINDOW, p * LANES:(p + 1) * LANES] = pair_t.T.astype(BF16)


def _swa_consts():
    qi = np.arange(WINDOW)[:, None]
    kj = np.arange(WINDOW)[None, :]
    from_prev = kj > qi
    dist = np.where(from_prev, qi + WINDOW - kj, qi - kj).astype(np.float32)
    slopes = 2.0 ** (-8.0 * (np.arange(SWA_Q_HEADS, dtype=np.float32) + 1.0) / SWA_Q_HEADS)
    bias = np.zeros((2, SWA_PAIRS, 2, WINDOW, WINDOW), np.float32)
    for first in range(2):
        for p in range(SWA_PAIRS):
            for half, head in enumerate((p, SWA_GROUP + p)):
                b = (-slopes[head].astype(np.float32) * dist) * np.float32(LOG2E)
                bias[first, p, half] = np.where(from_prev & bool(first), np.float32(NEG_BIG), b)
    bias = bias.reshape(2, SWA_ROWS, WINDOW).transpose(0, 2, 1)
    h = np.arange(SWA_WIDTH) // HEAD_DIM
    bdq = (h[:, None] == h[None, :]).astype(np.float32)
    hk = np.arange(SWA_KV_WIDTH) // HEAD_DIM
    bdk = (hk[:, None] == hk[None, :]).astype(np.float32)
    return jnp.asarray(bias), jnp.asarray(bdq, BF16), jnp.asarray(bdk, BF16)


def _swa_call(sq, skv, qw_x, kw_x, sink_col):
    b, t, _ = sq.shape
    tq = SWA_TQ
    nb = tq // WINDOW
    bias, bdq, bdk = _swa_consts()
    const = lambda shape: pl.BlockSpec(shape, lambda bi, ti: (0,) * len(shape))
    return pl.pallas_call(
        _swa_kernel,
        grid=(b, t // tq),
        in_specs=[
            pl.BlockSpec((1, tq, SWA_WIDTH), lambda bi, ti: (bi, ti, 0)),
            pl.BlockSpec((1, tq, 2 * SWA_KV_WIDTH), lambda bi, ti: (bi, ti, 0)),
            pl.BlockSpec((1, WINDOW, 2 * SWA_KV_WIDTH), lambda bi, ti: (bi, jnp.maximum(ti * nb - 1, 0), 0)),
            const((1, SWA_WIDTH)), const((1, SWA_KV_WIDTH)),
            const((SWA_WIDTH, SWA_WIDTH)), const((SWA_KV_WIDTH, SWA_KV_WIDTH)),
            const((2, WINDOW, SWA_ROWS)),
            const((1, SWA_ROWS)),
        ],
        out_specs=pl.BlockSpec((1, tq, SWA_WIDTH), lambda bi, ti: (bi, ti, 0)),
        out_shape=jax.ShapeDtypeStruct((b, t, SWA_WIDTH), BF16),
        compiler_params=pltpu.CompilerParams(
            dimension_semantics=("arbitrary", "arbitrary"), vmem_limit_bytes=VMEM_LIMIT),
        name="swa",
    )(sq, skv, skv, qw_x, kw_x, bdq, bdk, bias, sink_col)


def _ffn_kernel(x_ref, og_ref, os_ref, mod_ref, nw_ref, wo_ref, wg_ref, wu_ref, wd_ref, o_ref):
    x = x_ref[0]
    mod = mod_ref[0]
    gate1 = mod[:, 2 * D_MODEL:3 * D_MODEL]
    shift2 = mod[:, 3 * D_MODEL:4 * D_MODEL]
    scale2 = mod[:, 4 * D_MODEL:5 * D_MODEL]
    gate2 = mod[:, 5 * D_MODEL:]
    mixed = _dot(og_ref[0], wo_ref[0:GDN_WIDTH, :]) + _dot(os_ref[0], wo_ref[GDN_WIDTH:, :])
    x1 = x + gate1 * mixed
    ms = jnp.mean(x1 * x1, axis=-1, keepdims=True)
    hb = ((x1 * lax.rsqrt(ms + EPS) * nw_ref[...]) * (1.0 + scale2) + shift2).astype(BF16)
    gt = _dot(hb, wg_ref[...])
    up = _dot(hb, wu_ref[...])
    act = ((gt * _sigmoid(gt)) * up).astype(BF16)
    o_ref[0] = x1 + gate2 * _dot(act, wd_ref[...])


def _ffn_call(x, og, osw, mod3, norm_w, wo, wg, wu, wd):
    b, t, _ = x.shape
    tm = FFN_TM
    row = lambda width: pl.BlockSpec((1, tm, width), lambda bi, ti: (bi, ti, 0))
    const = lambda shape: pl.BlockSpec(shape, lambda bi, ti: (0,) * len(shape), pipeline_mode=pl.Buffered(1))
    return pl.pallas_call(
        _ffn_kernel,
        grid=(b, t // tm),
        in_specs=[
            row(D_MODEL), row(GDN_WIDTH), row(SWA_WIDTH),
            pl.BlockSpec((1, 1, 6 * D_MODEL), lambda bi, ti: (bi, 0, 0)),
            const((1, D_MODEL)),
            const((GDN_WIDTH + SWA_WIDTH, D_MODEL)),
            const((D_MODEL, D_FF)), const((D_MODEL, D_FF)), const((D_FF, D_MODEL)),
        ],
        out_specs=row(D_MODEL),
        out_shape=jax.ShapeDtypeStruct((b, t, D_MODEL), F32),
        compiler_params=pltpu.CompilerParams(
            dimension_semantics=("arbitrary", "arbitrary"), vmem_limit_bytes=VMEM_LIMIT),
        name="ffn",
    )(x, og, osw, mod3, norm_w, wo, wg, wu, wd)


def _layer(x, c_pad, w_ada, b_ada, norm1_w, w_in, conv_w, a_log, dt_bias, gdn_norm_w, q_norm_w, k_norm_w, sinks,
           w_out, norm2_w, w_gate, w_up, w_down):
    b = x.shape[0]
    mod, w_aligned, w_regrouped = _ada_call(c_pad, w_ada, b_ada.reshape(1, -1), w_in.T)
    mod3 = mod[:b].reshape(b, 1, 6 * D_MODEL)
    qkv, z, sq, skv, ab = _proj_call(x, mod3, norm1_w.reshape(1, D_MODEL), w_aligned, w_regrouped,
                                     conv_w.reshape(CONV_WIDTH, QKV_W))

    alog_pad = jnp.pad(a_log.reshape(1, GDN_HEADS), ((0, 0), (0, GATE_PAD - GDN_HEADS)))
    dtb_pad = jnp.pad(dt_bias.reshape(1, GDN_HEADS), ((0, 0), (0, GATE_PAD - GDN_HEADS)))
    gnw_x = jnp.tile(gdn_norm_w.reshape(1, HEAD_DIM), (1, GDN_HEADS))
    og, wg16, wu16, wd16, wo16 = _gdn_call(qkv, z, ab, alog_pad, dtb_pad, gnw_x, w_gate, w_up, w_down, w_out)

    qw_x = jnp.tile(q_norm_w.reshape(1, HEAD_DIM), (1, SWA_Q_HEADS))
    kw_x = jnp.tile(k_norm_w.reshape(1, HEAD_DIM), (1, SWA_KV_HEADS))
    sink_pairs = jnp.stack([sinks[:SWA_GROUP], sinks[SWA_GROUP:]], axis=1)
    sink_col = (jnp.repeat(sink_pairs, WINDOW, axis=1) * LOG2E).reshape(1, SWA_ROWS)
    osw = _swa_call(sq, skv, qw_x, kw_x, sink_col)

    return _ffn_call(x, og, osw, mod3, norm2_w.reshape(1, D_MODEL), wo16, wg16, wu16, wd16)


def kernel(x, c, w_ada, b_ada, norm1_w, w_in, conv_w, a_log, dt_bias, gdn_norm_w, q_norm_w, k_norm_w, sinks,
           w_out, norm2_w, w_gate, w_up, w_down):
    depth = w_ada.shape[0]
    b = c.shape[0]
    c_pad = jnp.pad(c, ((0, 8 - b), (0, 0)))
    for l in range(depth):
        x = _layer(x, c_pad, w_ada[l], b_ada[l], norm1_w[l], w_in[l], conv_w[l], a_log[l], dt_bias[l],
                   gdn_norm_w[l], q_norm_w[l], k_norm_w[l], sinks[l], w_out[l], norm2_w[l], w_gate[l], w_up[l],
                   w_down[l])
    return x
```

```python
import functools

import numpy as np
import jax
import jax.numpy as jnp
from jax import lax
from jax.experimental import pallas as pl
from jax.experimental.pallas import tpu as pltpu

F32 = jnp.float32
BF16 = jnp.bfloat16

D_MODEL = 1024
HEAD_DIM = 64
GDN_HEADS = 8
GDN_WIDTH = GDN_HEADS * HEAD_DIM
SWA_Q_HEADS = 8
SWA_KV_HEADS = 2
SWA_GROUP = SWA_Q_HEADS // SWA_KV_HEADS
SWA_WIDTH = SWA_Q_HEADS * HEAD_DIM
SWA_KV_WIDTH = SWA_KV_HEADS * HEAD_DIM
WINDOW = 128
CONV_WIDTH = 4
CHUNK = 64
D_FF = 2816
EPS = 1e-6
LANES = 128
GATE_PAD = LANES
HEADS_PER_GROUP = 2
GROUP_W = HEADS_PER_GROUP * HEAD_DIM
INV_BASE = 16
INV_LEVELS = (16, 32)
NEG_BIG = -1e30
VMEM_LIMIT = 56 * 1024 * 1024

PROJ_TM = 512
GDN_TT = 256
SWA_TQ = 1024
FFN_TM = 512
ADA_TN = 1536


def _sigmoid(x):
    return 1.0 / (1.0 + jnp.exp(-x))


def _dot(a, b):
    return jnp.dot(a, b, preferred_element_type=F32)


def _dot_nt(a, b):
    return lax.dot_general(a, b, (((1,), (1,)), ((), ())), preferred_element_type=F32)


def _dot_tn(a, b):
    return lax.dot_general(a, b, (((0,), (0,)), ((), ())), preferred_element_type=F32)


def _split2(x):
    hi = x.astype(BF16)
    lo = (x - hi.astype(F32)).astype(BF16)
    return hi, lo


def _ada_kernel(c_ref, w_ref, b_ref, win_ref, o_ref, wa16_ref, wb16_ref):
    c = c_ref[...]
    ca = c * _sigmoid(c)
    o_ref[...] = _dot(ca.astype(BF16), w_ref[...].astype(BF16)) + b_ref[...]

    @pl.when(pl.program_id(0) == 0)
    def _():
        wa16_ref[...] = win_ref[0:PROJ_ALIGNED, :].T.astype(BF16)
        o1 = PROJ_ALIGNED + 2 * GDN_HEADS
        o2 = o1 + SWA_WIDTH
        head = lambda h: win_ref[o1 + h * HEAD_DIM:o1 + (h + 1) * HEAD_DIM, :]
        pieces = [head(half * SWA_GROUP + p) for p in range(SWA_PAIRS) for half in range(2)]
        pieces.append(win_ref[o2:o2 + 2 * SWA_KV_WIDTH, :])
        pieces.append(win_ref[PROJ_ALIGNED:o1, :])
        pieces.append(jnp.zeros((GATE_PAD - 2 * GDN_HEADS, D_MODEL), F32))
        wb16_ref[...] = jnp.concatenate(pieces, axis=0).T.astype(BF16)


def _ada_call(c_pad, w_ada, b_ada, w_in):
    n = w_ada.shape[1]
    whole = lambda shape: pl.BlockSpec(shape, lambda j: (0, 0))
    return pl.pallas_call(
        _ada_kernel,
        grid=(n // ADA_TN,),
        in_specs=[
            whole((8, D_MODEL)),
            pl.BlockSpec((D_MODEL, ADA_TN), lambda j: (0, j)),
            pl.BlockSpec((1, ADA_TN), lambda j: (0, j)),
            pl.BlockSpec(w_in.shape, lambda j: (0, 0), pipeline_mode=pl.Buffered(1)),
        ],
        out_specs=[pl.BlockSpec((8, ADA_TN), lambda j: (0, j)),
                   whole((D_MODEL, PROJ_ALIGNED)), whole((D_MODEL, PROJ_REGROUPED))],
        out_shape=[jax.ShapeDtypeStruct((8, n), F32),
                   jax.ShapeDtypeStruct((D_MODEL, PROJ_ALIGNED), BF16),
                   jax.ShapeDtypeStruct((D_MODEL, PROJ_REGROUPED), BF16)],
        compiler_params=pltpu.CompilerParams(
            dimension_semantics=("arbitrary",), vmem_limit_bytes=VMEM_LIMIT),
        name="ada",
    )(c_pad, w_ada, b_ada, w_in)


QKV_W = 3 * GDN_WIDTH
PROJ_SPLITS = (QKV_W, GDN_WIDTH, SWA_WIDTH, 2 * SWA_KV_WIDTH, GATE_PAD)
PROJ_ALIGNED = QKV_W + GDN_WIDTH
PROJ_REGROUPED = sum(PROJ_SPLITS) - PROJ_ALIGNED


def _proj_kernel(x_ref, mod_ref, nw_ref, wa_ref, wb_ref, cw_ref, bd_ref, qkv_ref, z_ref, sq_ref, skv_ref, ab_ref,
                 qbuf, kbuf, vbuf):
    tm = PROJ_TM
    bufs = (qbuf, kbuf, vbuf)

    @pl.when(pl.program_id(1) == 0)
    def _():
        for buf in bufs:
            buf[0:8, :] = jnp.zeros((8, GDN_WIDTH), F32)

    x = x_ref[0]
    ms = jnp.mean(x * x, axis=-1, keepdims=True)
    y = x * lax.rsqrt(ms + EPS) * nw_ref[...]
    mod = mod_ref[0]
    shift = mod[:, 0:D_MODEL]
    scale = mod[:, D_MODEL:2 * D_MODEL]
    hb = (y * (1.0 + scale) + shift).astype(BF16)

    def project(s):
        bufs[s][8:8 + tm, :] = _dot(hb, wa_ref[:, s * GDN_WIDTH:(s + 1) * GDN_WIDTH])

    def conv_silu(s):
        buf = bufs[s]
        cw = lambda j: cw_ref[j:j + 1, s * GDN_WIDTH:(s + 1) * GDN_WIDTH]
        acc = cw(CONV_WIDTH - 1) * buf[8:8 + tm, :]
        for j in range(CONV_WIDTH - 1):
            off = 8 - (CONV_WIDTH - 1) + j
            acc = acc + cw(j) * buf[off:off + tm, :]
        buf[0:8, :] = buf[tm:tm + 8, :]
        return acc * _sigmoid(acc)

    def l2_normed(a, scale):
        return a * lax.rsqrt(_dot((a * a).astype(BF16), bd_ref[...]) + EPS) * scale

    project(0)
    project(1)
    qkv_ref[0, :, 0:GDN_WIDTH] = l2_normed(conv_silu(0), HEAD_DIM ** -0.5)
    project(2)
    qkv_ref[0, :, GDN_WIDTH:2 * GDN_WIDTH] = l2_normed(conv_silu(1), 1.0)
    z_ref[0] = _dot(hb, wa_ref[:, QKV_W:])
    qkv_ref[0, :, 2 * GDN_WIDTH:] = conv_silu(2)
    col = 0
    for ref in (sq_ref, skv_ref, ab_ref):
        ref[0] = _dot(hb, wb_ref[:, col:col + ref.shape[-1]])
        col += ref.shape[-1]


def _head_sum_matrix(width):
    h = np.arange(width) // HEAD_DIM
    return jnp.asarray((h[:, None] == h[None, :]).astype(np.float32), BF16)


def _proj_call(x, mod3, norm_w, w_aligned, w_regrouped, conv_w):
    b, t, _ = x.shape
    tm = PROJ_TM
    row = lambda width: pl.BlockSpec((1, tm, width), lambda bi, ti: (bi, ti, 0))
    const = lambda shape: pl.BlockSpec(shape, lambda bi, ti: (0,) * len(shape))
    return pl.pallas_call(
        _proj_kernel,
        grid=(b, t // tm),
        in_specs=[
            row(D_MODEL),
            pl.BlockSpec((1, 1, 6 * D_MODEL), lambda bi, ti: (bi, 0, 0)),
            const((1, D_MODEL)),
            const((D_MODEL, PROJ_ALIGNED)),
            const((D_MODEL, PROJ_REGROUPED)),
            const((CONV_WIDTH, QKV_W)),
            const((GDN_WIDTH, GDN_WIDTH)),
        ],
        out_specs=[row(w) for w in PROJ_SPLITS],
        out_shape=[jax.ShapeDtypeStruct((b, t, w), F32) for w in PROJ_SPLITS],
        scratch_shapes=[pltpu.VMEM((8 + tm, GDN_WIDTH), F32)] * 3,
        compiler_params=pltpu.CompilerParams(
            dimension_semantics=("arbitrary", "arbitrary"), vmem_limit_bytes=VMEM_LIMIT),
        name="proj",
    )(x, mod3, norm_w, w_aligned, w_regrouped, conv_w, _head_sum_matrix(GDN_WIDTH))


GDN_UNITS = (GDN_TT // CHUNK) * (GDN_HEADS // HEADS_PER_GROUP)


def _block_diag_rows(p, same_head):
    stacked = jnp.concatenate([p.astype(BF16)] * HEADS_PER_GROUP, axis=0)
    return jnp.where(same_head, stacked, jnp.zeros((), BF16))


_DONE = object()


def _gdn_kernel(qkv_ref, z_ref, ab_ref, alog_ref, dtb_ref, gnw_ref, bd_ref, eg_ref, eb_ref, ltri_ref,
                wg_ref, wu_ref, wd_ref, wo_ref, o_ref, wg16_ref, wu16_ref, wd16_ref, wo16_ref,
                s_ref, obuf, wq_buf, u_buf, qk_buf, kd_buf, dec_buf, *, tiles_per_seq):
    tt = GDN_TT
    step = pl.program_id(0)
    @pl.when(step < CAST_STEPS)
    def _():
        for src, dst in ((wg_ref, wg16_ref), (wu_ref, wu16_ref), (wd_ref, wd16_ref), (wo_ref, wo16_ref)):
            dst[...] = src[...].astype(BF16)

    wslot = step % 2
    rslot = 1 - wslot
    starts_sequence = (step - 1) % tiles_per_seq == 0

    @pl.when(step == 0)
    def _():
        for buf in (s_ref, wq_buf, u_buf, qk_buf, kd_buf, dec_buf):
            buf[...] = jnp.zeros_like(buf)

    bd = bd_ref[...]
    q = qkv_ref[0, :, 0:GDN_WIDTH]
    k = qkv_ref[0, :, GDN_WIDTH:2 * GDN_WIDTH]
    v = qkv_ref[0, :, 2 * GDN_WIDTH:]

    ab = ab_ref[0]
    lane = lax.broadcasted_iota(jnp.int32, (1, GATE_PAD), 1)
    xs = ab + dtb_ref[...]
    softplus = jnp.maximum(xs, 0.0) + jnp.log1p(jnp.exp(-jnp.abs(xs)))
    g = jnp.where(lane < GDN_HEADS, -jnp.exp(alog_ref[...]) * softplus, 0.0)
    beta = _sigmoid(ab)
    ltri = ltri_ref[...]
    g1, g2 = _split2(g)
    gcum = _dot(ltri, g1) + _dot(ltri, g2)
    eg = eg_ref[...]
    c1, c2 = _split2(gcum)
    g_x = _dot(c1, eg) + _dot(c2, eg)
    beta_x = _dot(beta.astype(BF16), eb_ref[...])

    r64 = lax.broadcasted_iota(jnp.int32, (CHUNK, GROUP_W), 0)
    l64 = lax.broadcasted_iota(jnp.int32, (CHUNK, GROUP_W), 1) % CHUNK
    causal = r64 >= l64
    eye_x = (r64 == l64).astype(F32)
    strict_x = (r64 > l64).astype(F32)
    same_block = lambda size: (r64 // size) == (l64 // size)
    base_x = same_block(INV_BASE).astype(F32)
    level_x = [(same_block(2 * size) & ~same_block(size)).astype(F32) for size in INV_LEVELS]
    rb = lax.broadcasted_iota(jnp.int32, (GROUP_W, GROUP_W), 0) // HEAD_DIM
    cb = lax.broadcasted_iota(jnp.int32, (GROUP_W, GROUP_W), 1) // HEAD_DIM
    same_head = rb == cb
    mask_bd = same_head.astype(F32)

    n_groups = GDN_HEADS // HEADS_PER_GROUP
    n_chunks = tt // CHUNK
    units = [(c, gi) for c in range(n_chunks) for gi in range(n_groups)]
    ids = list(range(len(units)))
    tile = lambda a: [a[c * CHUNK:(c + 1) * CHUNK, gi * GROUP_W:(gi + 1) * GROUP_W] for c, gi in units]
    each = lambda f, *lists: [f(*args) for args in zip(*lists)]
    bdr = lambda t: _block_diag_rows(t, same_head)
    mm = lambda a, w16: _dot(a.astype(BF16), w16)
    stack = lambda a, b_: jnp.concatenate([a, b_], axis=0)

    def chunk_parallel_part():
        kc, qc, vc, bx, gx = tile(k), tile(q), tile(v), tile(beta_x), tile(g_x)
        eg_c = each(jnp.exp, gx)
        glast = each(lambda g_: g_[CHUNK - 1:CHUNK, :], gx)
        kb = each(jnp.multiply, kc, bx)
        vb = each(jnp.multiply, vc, bx)
        wr = each(jnp.multiply, kb, eg_c)
        qd = each(jnp.multiply, qc, eg_c)
        for i in ids:
            kd_buf[wslot, i] = (kc[i] * jnp.exp(glast[i] - gx[i])).astype(BF16)
            dec_buf[wslot, i] = jnp.broadcast_to(jnp.exp(glast[i]), (8, GROUP_W))
        dm = each(lambda g_: jnp.exp(jnp.where(causal, g_ - jnp.sum(g_ * eye_x, axis=0, keepdims=True), NEG_BIG)),
                  gx)
        yield
        aq = each(lambda i, a, b_, k_: _dot_nt(stack(a, b_).astype(BF16), bdr(k_)), ids, kb, qc, kc)
        xm = each(lambda a, d_: -(a[0:CHUNK] * d_ * strict_x), aq, dm)
        for i in ids:
            qk_buf[wslot, i] = (aq[i][CHUNK:] * dm[i]).astype(BF16)
        xd = each(lambda x_: x_ * base_x, xm)
        tinv = each(lambda x_: eye_x + x_, xd)
        yield
        pw = each(lambda i, x_: mm(x_, bdr(x_)), ids, xd)
        yield
        for _ in range(INV_BASE.bit_length() - 3):
            r = each(lambda i, t_, p_: mm(stack(t_, p_), bdr(p_)), ids, tinv, pw)
            tinv = each(lambda t_, r_: t_ + r_[0:CHUNK], tinv, r)
            pw = each(lambda r_: r_[CHUNK:], r)
            yield
        tinv = each(lambda i, t_, p_: t_ + mm(t_, bdr(p_)), ids, tinv, pw)
        yield
        for lm in level_x:
            e = each(lambda i, x_, t_: mm(x_ * lm, bdr(t_)), ids, xm, tinv)
            yield
            tinv = each(lambda i, t_, e_: t_ + mm(t_, bdr(e_)), ids, tinv, e)
            yield
        for i in ids:
            uw = mm(tinv[i], jnp.concatenate([bdr(vb[i]), bdr(wr[i])], axis=1))
            u_buf[wslot, i] = uw[:, 0:GROUP_W]
            wq_buf[wslot, i, 0:CHUNK] = uw[:, GROUP_W:].astype(BF16)
            wq_buf[wslot, i, CHUNK:] = qd[i].astype(BF16)

    def sequential_part():
        s_state = [jnp.where(starts_sequence, 0.0, s_ref[gi]) for gi in range(n_groups)]
        for c in range(n_chunks):
            cids = [c * n_groups + gi for gi in range(n_groups)]
            r2 = [_dot(wq_buf[rslot, i], s_state[gi].astype(BF16)) for gi, i in enumerate(cids)]
            yield
            vn = [u_buf[rslot, i] - r2[gi][0:CHUNK] for gi, i in enumerate(cids)]
            o = [r2[gi][CHUNK:] + _dot(qk_buf[rslot, i], bdr(vn[gi])) for gi, i in enumerate(cids)]
            s_state = [s_state[gi] * dec_buf[rslot, i][0:1] + mask_bd * _dot_tn(kd_buf[rslot, i], vn[gi].astype(BF16))
                       for gi, i in enumerate(cids)]
            for gi in range(n_groups):
                obuf[c * CHUNK:(c + 1) * CHUNK, gi * GROUP_W:(gi + 1) * GROUP_W] = o[gi]
            yield
        for gi in range(n_groups):
            s_ref[gi] = s_state[gi]

    parts = [chunk_parallel_part(), sequential_part()]
    while parts:
        parts = [p for p in parts if next(p, _DONE) is not _DONE]

    o = obuf[...]
    ms = _dot((o * o).astype(BF16), bd) * (1.0 / HEAD_DIM)
    zz = z_ref[0]
    o_ref[0] = (o * lax.rsqrt(ms + EPS) * gnw_ref[...] * (zz * _sigmoid(zz))).astype(BF16)


def _gdn_consts():
    h = np.arange(GDN_WIDTH) // HEAD_DIM
    eg = np.zeros((GATE_PAD, GDN_WIDTH), np.float32)
    eb = np.zeros((GATE_PAD, GDN_WIDTH), np.float32)
    eg[h, np.arange(GDN_WIDTH)] = 1.0
    eb[GDN_HEADS + h, np.arange(GDN_WIDTH)] = 1.0
    t = np.arange(GDN_TT)
    ltri = ((t[:, None] // CHUNK == t[None, :] // CHUNK) & (t[:, None] >= t[None, :])).astype(np.float32)
    return (_head_sum_matrix(GDN_WIDTH), jnp.asarray(eg, BF16), jnp.asarray(eb, BF16), jnp.asarray(ltri, BF16))


BF16_ROWS = 16
CAST_STEPS = 2 * GDN_HEADS


def _swa_head_source(i):
    j = i - GDN_HEADS
    return jnp.where(i < GDN_HEADS, i, GDN_HEADS + (j % 2) * SWA_GROUP + j // 2)


def _cast_specs(weight, unit, steps, source=None):
    rows, cols = weight.shape
    block_rows = next(r for r in range(unit, rows + 1, unit) if rows % r == 0 and rows // r <= steps)
    if source is None:
        source = lambda i: i
    else:
        assert block_rows == unit, "row regrouping needs one unit per block"
    last = rows // block_rows - 1
    src = pl.BlockSpec((block_rows, cols), lambda n: (source(jnp.minimum(n, last)), 0))
    dst = pl.BlockSpec((block_rows, cols), lambda n: (jnp.minimum(n, last), 0))
    return src, dst, jax.ShapeDtypeStruct((rows, cols), BF16)


def _gdn_call(qkv, z, ab, alog_pad, dtb_pad, gnw_x, w_gate, w_up, w_down, w_out):
    b, t, _ = qkv.shape
    tt = GDN_TT
    bd, eg, eb, ltri = _gdn_consts()
    n_tiles = t // tt
    n_groups = GDN_HEADS // HEADS_PER_GROUP
    total = b * n_tiles

    def tile_block(width, lag):
        def index(n):
            m = jnp.clip(n - lag, 0, total - 1)
            return (m // n_tiles, m % n_tiles, 0)
        return pl.BlockSpec((1, tt, width), index)

    ahead = lambda width: tile_block(width, 0)
    behind = lambda width: tile_block(width, 1)
    const = lambda shape: pl.BlockSpec(shape, lambda n: (0,) * len(shape))
    per_unit = lambda rows, dtype: pltpu.VMEM((2, GDN_UNITS, rows, GROUP_W), dtype)
    steps = total + 1
    assert steps >= CAST_STEPS
    casts = [
        _cast_specs(w_gate, BF16_ROWS, CAST_STEPS),
        _cast_specs(w_up, BF16_ROWS, CAST_STEPS),
        _cast_specs(w_down, BF16_ROWS, CAST_STEPS),
        _cast_specs(w_out, HEAD_DIM, CAST_STEPS, _swa_head_source),
    ]
    return pl.pallas_call(
        functools.partial(_gdn_kernel, tiles_per_seq=n_tiles),
        grid=(steps,),
        in_specs=[
            ahead(QKV_W), behind(GDN_WIDTH), ahead(GATE_PAD),
            const((1, GATE_PAD)), const((1, GATE_PAD)), const((1, GDN_WIDTH)),
            const((GDN_WIDTH, GDN_WIDTH)), const((GATE_PAD, GDN_WIDTH)), const((GATE_PAD, GDN_WIDTH)),
            const((tt, tt)),
        ] + [c[0] for c in casts],
        out_specs=[behind(GDN_WIDTH)] + [c[1] for c in casts],
        out_shape=[jax.ShapeDtypeStruct((b, t, GDN_WIDTH), BF16)] + [c[2] for c in casts],
        scratch_shapes=[
            pltpu.VMEM((n_groups, GROUP_W, GROUP_W), F32),
            pltpu.VMEM((tt, GDN_WIDTH), F32),
            per_unit(2 * CHUNK, BF16), per_unit(CHUNK, F32), per_unit(CHUNK, BF16), per_unit(CHUNK, BF16),
            per_unit(8, F32),
        ],
        compiler_params=pltpu.CompilerParams(
            dimension_semantics=("arbitrary",), vmem_limit_bytes=VMEM_LIMIT),
        name="gdn",
    )(qkv, z, ab, alog_pad, dtb_pad, gnw_x, bd, eg, eb, ltri, w_gate, w_up, w_down, w_out)


SWA_PAIRS = SWA_Q_HEADS // 2
SWA_ROWS = SWA_Q_HEADS * WINDOW
LOG2E = 1.4426950408889634
SWA_INTERLEAVE = 4


def _swa_kernel(q_ref, kvc_ref, kvp_ref, qw_ref, kw_ref, bdq_ref, bdk_ref, bias_ref, sink_ref, o_ref):
    tq = SWA_TQ
    ti = pl.program_id(1)
    q = q_ref[0]
    q = q * lax.rsqrt(_dot((q * q).astype(BF16), bdq_ref[...]) * (1.0 / HEAD_DIM) + EPS)
    q = q * (qw_ref[...] * (HEAD_DIM ** -0.5 * LOG2E))
    kv = jnp.concatenate([kvp_ref[0], kvc_ref[0]], axis=0)
    k = kv[:, 0:SWA_KV_WIDTH]
    v = kv[:, SWA_KV_WIDTH:]
    k = k * lax.rsqrt(_dot((k * k).astype(BF16), bdk_ref[...]) * (1.0 / HEAD_DIM) + EPS) * kw_ref[...]
    k16 = k.astype(BF16)
    lo = lax.broadcasted_iota(jnp.int32, (1, LANES), 1) < HEAD_DIM
    qcol = lax.broadcasted_iota(jnp.int32, (WINDOW, SWA_ROWS), 1) % WINDOW
    from_prev = lax.broadcasted_iota(jnp.int32, (WINDOW, SWA_ROWS), 0) > qcol
    first = jnp.where(ti == 0, 1, 0)
    sink = sink_ref[...]
    vt16 = v.T.astype(BF16)
    zero = jnp.zeros((), BF16)
    def block(j):
        qj = q[j * WINDOW:(j + 1) * WINDOW]
        parts = []
        for p in range(SWA_PAIRS):
            qp = qj[:, p * LANES:(p + 1) * LANES]
            parts += [jnp.where(lo, qp, 0.0), jnp.where(lo, 0.0, qp)]
        qs = jnp.concatenate(parts, axis=0).astype(BF16)
        st = _dot_nt(k16[j * WINDOW:(j + 2) * WINDOW], qs)
        yield
        bias = bias_ref[first] if j == 0 else bias_ref[0]
        sm = jnp.where(from_prev, st[0:WINDOW], st[WINDOW:]) + bias
        m = jnp.max(sm, axis=0, keepdims=True)
        yield
        pe = jnp.exp2(sm - m)
        den = jnp.sum(pe, axis=0, keepdims=True) + jnp.exp2(sink - m)
        pb = pe.astype(BF16)
        pt2 = jnp.concatenate([jnp.where(from_prev, pb, zero), jnp.where(from_prev, zero, pb)], axis=0)
        yield
        ot = _dot(vt16[:, j * WINDOW:(j + 2) * WINDOW], pt2) * (1.0 / den)
        yield
        for p in range(SWA_PAIRS):
            c0 = p * 2 * WINDOW
            pair_t = jnp.concatenate([ot[0:HEAD_DIM, c0:c0 + WINDOW],
                                      ot[HEAD_DIM:, c0 + WINDOW:c0 + 2 * WINDOW]], axis=0)
            o_ref[0, j * WINDOW:(j + 1) * WINDOW, p * LANES:(p + 1) * LANES] = pair_t.T.astype(BF16)

    n_blocks = tq // WINDOW
    for j0 in range(0, n_blocks, SWA_INTERLEAVE):
        parts = [block(j) for j in range(j0, min(j0 + SWA_INTERLEAVE, n_blocks))]
        while parts:
            parts = [p for p in parts if next(p, _DONE) is not _DONE]


def _swa_consts():
    qi = np.arange(WINDOW)[:, None]
    kj = np.arange(WINDOW)[None, :]
    from_prev = kj > qi
    dist = np.where(from_prev, qi + WINDOW - kj, qi - kj).astype(np.float32)
    slopes = 2.0 ** (-8.0 * (np.arange(SWA_Q_HEADS, dtype=np.float32) + 1.0) / SWA_Q_HEADS)
    bias = np.zeros((2, SWA_PAIRS, 2, WINDOW, WINDOW), np.float32)
    for first in range(2):
        for p in range(SWA_PAIRS):
            for half, head in enumerate((p, SWA_GROUP + p)):
                b = (-slopes[head].astype(np.float32) * dist) * np.float32(LOG2E)
                bias[first, p, half] = np.where(from_prev & bool(first), np.float32(NEG_BIG), b)
    bias = bias.reshape(2, SWA_ROWS, WINDOW).transpose(0, 2, 1)
    h = np.arange(SWA_WIDTH) // HEAD_DIM
    bdq = (h[:, None] == h[None, :]).astype(np.float32)
    hk = np.arange(SWA_KV_WIDTH) // HEAD_DIM
    bdk = (hk[:, None] == hk[None, :]).astype(np.float32)
    return jnp.asarray(bias), jnp.asarray(bdq, BF16), jnp.asarray(bdk, BF16)


def _swa_call(sq, skv, qw_x, kw_x, sink_col):
    b, t, _ = sq.shape
    tq = SWA_TQ
    nb = tq // WINDOW
    bias, bdq, bdk = _swa_consts()
    const = lambda shape: pl.BlockSpec(shape, lambda bi, ti: (0,) * len(shape))
    return pl.pallas_call(
        _swa_kernel,
        grid=(b, t // tq),
        in_specs=[
            pl.BlockSpec((1, tq, SWA_WIDTH), lambda bi, ti: (bi, ti, 0)),
            pl.BlockSpec((1, tq, 2 * SWA_KV_WIDTH), lambda bi, ti: (bi, ti, 0)),
            pl.BlockSpec((1, WINDOW, 2 * SWA_KV_WIDTH), lambda bi, ti: (bi, jnp.maximum(ti * nb - 1, 0), 0)),
            const((1, SWA_WIDTH)), const((1, SWA_KV_WIDTH)),
            const((SWA_WIDTH, SWA_WIDTH)), const((SWA_KV_WIDTH, SWA_KV_WIDTH)),
            const((2, WINDOW, SWA_ROWS)),
            const((1, SWA_ROWS)),
        ],
        out_specs=pl.BlockSpec((1, tq, SWA_WIDTH), lambda bi, ti: (bi, ti, 0)),
        out_shape=jax.ShapeDtypeStruct((b, t, SWA_WIDTH), BF16),
        compiler_params=pltpu.CompilerParams(
            dimension_semantics=("arbitrary", "arbitrary"), vmem_limit_bytes=VMEM_LIMIT),
        name="swa",
    )(sq, skv, skv, qw_x, kw_x, bdq, bdk, bias, sink_col)


def _ffn_kernel(x_ref, og_ref, os_ref, mod_ref, nw_ref, wo_ref, wg_ref, wu_ref, wd_ref, o_ref):
    x = x_ref[0]
    mod = mod_ref[0]
    gate1 = mod[:, 2 * D_MODEL:3 * D_MODEL]
    shift2 = mod[:, 3 * D_MODEL:4 * D_MODEL]
    scale2 = mod[:, 4 * D_MODEL:5 * D_MODEL]
    gate2 = mod[:, 5 * D_MODEL:]
    mixed = _dot(og_ref[0], wo_ref[0:GDN_WIDTH, :]) + _dot(os_ref[0], wo_ref[GDN_WIDTH:, :])
    x1 = x + gate1 * mixed
    ms = jnp.mean(x1 * x1, axis=-1, keepdims=True)
    hb = ((x1 * lax.rsqrt(ms + EPS) * nw_ref[...]) * (1.0 + scale2) + shift2).astype(BF16)
    gt = _dot(hb, wg_ref[...])
    up = _dot(hb, wu_ref[...])
    act = ((gt * _sigmoid(gt)) * up).astype(BF16)
    o_ref[0] = x1 + gate2 * _dot(act, wd_ref[...])


def _ffn_call(x, og, osw, mod3, norm_w, wo, wg, wu, wd):
    b, t, _ = x.shape
    tm = FFN_TM
    row = lambda width: pl.BlockSpec((1, tm, width), lambda bi, ti: (bi, ti, 0))
    const = lambda shape: pl.BlockSpec(shape, lambda bi, ti: (0,) * len(shape), pipeline_mode=pl.Buffered(1))
    return pl.pallas_call(
        _ffn_kernel,
        grid=(b, t // tm),
        in_specs=[
            row(D_MODEL), row(GDN_WIDTH), row(SWA_WIDTH),
            pl.BlockSpec((1, 1, 6 * D_MODEL), lambda bi, ti: (bi, 0, 0)),
            const((1, D_MODEL)),
            const((GDN_WIDTH + SWA_WIDTH, D_MODEL)),
            const((D_MODEL, D_FF)), const((D_MODEL, D_FF)), const((D_FF, D_MODEL)),
        ],
        out_specs=row(D_MODEL),
        out_shape=jax.ShapeDtypeStruct((b, t, D_MODEL), F32),
        compiler_params=pltpu.CompilerParams(
            dimension_semantics=("arbitrary", "arbitrary"), vmem_limit_bytes=VMEM_LIMIT),
        name="ffn",
    )(x, og, osw, mod3, norm_w, wo, wg, wu, wd)


def _layer(x, c_pad, w_ada, b_ada, norm1_w, w_in, conv_w, a_log, dt_bias, gdn_norm_w, q_norm_w, k_norm_w, sinks,
           w_out, norm2_w, w_gate, w_up, w_down):
    b = x.shape[0]
    mod, w_aligned, w_regrouped = _ada_call(c_pad, w_ada, b_ada.reshape(1, -1), w_in.T)
    mod3 = mod[:b].reshape(b, 1, 6 * D_MODEL)
    qkv, z, sq, skv, ab = _proj_call(x, mod3, norm1_w.reshape(1, D_MODEL), w_aligned, w_regrouped,
                                     conv_w.reshape(CONV_WIDTH, QKV_W))

    alog_pad = jnp.pad(a_log.reshape(1, GDN_HEADS), ((0, 0), (0, GATE_PAD - GDN_HEADS)))
    dtb_pad = jnp.pad(dt_bias.reshape(1, GDN_HEADS), ((0, 0), (0, GATE_PAD - GDN_HEADS)))
    gnw_x = jnp.tile(gdn_norm_w.reshape(1, HEAD_DIM), (1, GDN_HEADS))
    og, wg16, wu16, wd16, wo16 = _gdn_call(qkv, z, ab, alog_pad, dtb_pad, gnw_x, w_gate, w_up, w_down, w_out)

    qw_x = jnp.tile(q_norm_w.reshape(1, HEAD_DIM), (1, SWA_Q_HEADS))
    kw_x = jnp.tile(k_norm_w.reshape(1, HEAD_DIM), (1, SWA_KV_HEADS))
    sink_pairs = jnp.stack([sinks[:SWA_GROUP], sinks[SWA_GROUP:]], axis=1)
    sink_col = (jnp.repeat(sink_pairs, WINDOW, axis=1) * LOG2E).reshape(1, SWA_ROWS)
    osw = _swa_call(sq, skv, qw_x, kw_x, sink_col)

    return _ffn_call(x, og, osw, mod3, norm2_w.reshape(1, D_MODEL), wo16, wg16, wu16, wd16)


def kernel(x, c, w_ada, b_ada, norm1_w, w_in, conv_w, a_log, dt_bias, gdn_norm_w, q_norm_w, k_norm_w, sinks,
           w_out, norm2_w, w_gate, w_up, w_down):
    depth = w_ada.shape[0]
    b = c.shape[0]
    c_pad = jnp.pad(c, ((0, 8 - b), (0, 0)))
    for l in range(depth):
        x = _layer(x, c_pad, w_ada[l], b_ada[l], norm1_w[l], w_in[l], conv_w[l], a_log[l], dt_bias[l],
                   gdn_norm_w[l], q_norm_w[l], k_norm_w[l], sinks[l], w_out[l], norm2_w[l], w_gate[l], w_up[l],
                   w_down[l])
    return x
```

```python
import functools

import numpy as np
import jax
import jax.numpy as jnp
from jax import lax
from jax.experimental import pallas as pl
from jax.experimental.pallas import tpu as pltpu

F32 = jnp.float32
BF16 = jnp.bfloat16

D_MODEL = 1024
HEAD_DIM = 64
GDN_HEADS = 8
GDN_WIDTH = GDN_HEADS * HEAD_DIM
SWA_Q_HEADS = 8
SWA_KV_HEADS = 2
SWA_GROUP = SWA_Q_HEADS // SWA_KV_HEADS
SWA_WIDTH = SWA_Q_HEADS * HEAD_DIM
SWA_KV_WIDTH = SWA_KV_HEADS * HEAD_DIM
WINDOW = 128
CONV_WIDTH = 4
CHUNK = 64
D_FF = 2816
EPS = 1e-6
LANES = 128
GATE_PAD = LANES
HEADS_PER_GROUP = 2
GROUP_W = HEADS_PER_GROUP * HEAD_DIM
INV_BASE = 16
INV_LEVELS = (16, 32)
NEG_BIG = -1e30
VMEM_LIMIT = 56 * 1024 * 1024

PROJ_TM = 512
GDN_TT = 256
SWA_TQ = 1024
FFN_TM = 512
ADA_TN = 1536


def _sigmoid(x):
    return 1.0 / (1.0 + jnp.exp(-x))


def _dot(a, b):
    return jnp.dot(a, b, preferred_element_type=F32)


def _dot_nt(a, b):
    return lax.dot_general(a, b, (((1,), (1,)), ((), ())), preferred_element_type=F32)


def _dot_tn(a, b):
    return lax.dot_general(a, b, (((0,), (0,)), ((), ())), preferred_element_type=F32)


def _split2(x):
    hi = x.astype(BF16)
    lo = (x - hi.astype(F32)).astype(BF16)
    return hi, lo


def _ada_kernel(c_ref, w_ref, b_ref, win_ref, o_ref, wa16_ref, wb16_ref):
    c = c_ref[...]
    ca = c * _sigmoid(c)
    o_ref[...] = _dot(ca.astype(BF16), w_ref[...].astype(BF16)) + b_ref[...]

    @pl.when(pl.program_id(0) == 0)
    def _():
        wa16_ref[...] = win_ref[0:PROJ_ALIGNED, :].T.astype(BF16)
        o1 = PROJ_ALIGNED + 2 * GDN_HEADS
        o2 = o1 + SWA_WIDTH
        head = lambda h: win_ref[o1 + h * HEAD_DIM:o1 + (h + 1) * HEAD_DIM, :]
        pieces = [head(half * SWA_GROUP + p) for p in range(SWA_PAIRS) for half in range(2)]
        pieces.append(win_ref[o2:o2 + 2 * SWA_KV_WIDTH, :])
        pieces.append(win_ref[PROJ_ALIGNED:o1, :])
        pieces.append(jnp.zeros((GATE_PAD - 2 * GDN_HEADS, D_MODEL), F32))
        wb16_ref[...] = jnp.concatenate(pieces, axis=0).T.astype(BF16)


def _ada_call(c_pad, w_ada, b_ada, w_in):
    n = w_ada.shape[1]
    whole = lambda shape: pl.BlockSpec(shape, lambda j: (0, 0))
    return pl.pallas_call(
        _ada_kernel,
        grid=(n // ADA_TN,),
        in_specs=[
            whole((8, D_MODEL)),
            pl.BlockSpec((D_MODEL, ADA_TN), lambda j: (0, j)),
            pl.BlockSpec((1, ADA_TN), lambda j: (0, j)),
            pl.BlockSpec(w_in.shape, lambda j: (0, 0), pipeline_mode=pl.Buffered(1)),
        ],
        out_specs=[pl.BlockSpec((8, ADA_TN), lambda j: (0, j)),
                   whole((D_MODEL, PROJ_ALIGNED)), whole((D_MODEL, PROJ_REGROUPED))],
        out_shape=[jax.ShapeDtypeStruct((8, n), F32),
                   jax.ShapeDtypeStruct((D_MODEL, PROJ_ALIGNED), BF16),
                   jax.ShapeDtypeStruct((D_MODEL, PROJ_REGROUPED), BF16)],
        compiler_params=pltpu.CompilerParams(
            dimension_semantics=("arbitrary",), vmem_limit_bytes=VMEM_LIMIT),
        name="ada",
    )(c_pad, w_ada, b_ada, w_in)


QKV_W = 3 * GDN_WIDTH
PROJ_SPLITS = (QKV_W, GDN_WIDTH, SWA_WIDTH, 2 * SWA_KV_WIDTH, GATE_PAD)
PROJ_ALIGNED = QKV_W + GDN_WIDTH
PROJ_REGROUPED = sum(PROJ_SPLITS) - PROJ_ALIGNED


def _proj_kernel(x_ref, mod_ref, nw_ref, wa_ref, wb_ref, cw_ref, bd_ref, qkv_ref, z_ref, sq_ref, skv_ref, ab_ref,
                 qbuf, kbuf, vbuf):
    tm = PROJ_TM
    bufs = (qbuf, kbuf, vbuf)

    @pl.when(pl.program_id(1) == 0)
    def _():
        for buf in bufs:
            buf[0:8, :] = jnp.zeros((8, GDN_WIDTH), F32)

    x = x_ref[0]
    ms = jnp.mean(x * x, axis=-1, keepdims=True)
    mod = mod_ref[0]
    shift = mod[:, 0:D_MODEL]
    gain = nw_ref[...] * (1.0 + mod[:, D_MODEL:2 * D_MODEL])
    hb = (x * lax.rsqrt(ms + EPS) * gain + shift).astype(BF16)

    def project(s):
        bufs[s][8:8 + tm, :] = _dot(hb, wa_ref[:, s * GDN_WIDTH:(s + 1) * GDN_WIDTH])

    def conv_silu(s):
        buf = bufs[s]
        cw = lambda j: cw_ref[j:j + 1, s * GDN_WIDTH:(s + 1) * GDN_WIDTH]
        acc = cw(CONV_WIDTH - 1) * buf[8:8 + tm, :]
        for j in range(CONV_WIDTH - 1):
            off = 8 - (CONV_WIDTH - 1) + j
            acc = acc + cw(j) * buf[off:off + tm, :]
        buf[0:8, :] = buf[tm:tm + 8, :]
        return acc * _sigmoid(acc)

    def l2_normed(a, scale):
        return a * lax.rsqrt(_dot((a * a).astype(BF16), bd_ref[...]) + EPS) * scale

    project(0)
    project(1)
    qkv_ref[0, :, 0:GDN_WIDTH] = l2_normed(conv_silu(0), HEAD_DIM ** -0.5)
    project(2)
    qkv_ref[0, :, GDN_WIDTH:2 * GDN_WIDTH] = l2_normed(conv_silu(1), 1.0)
    z_ref[0] = _dot(hb, wa_ref[:, QKV_W:])
    qkv_ref[0, :, 2 * GDN_WIDTH:] = conv_silu(2)
    col = 0
    for ref in (sq_ref, skv_ref, ab_ref):
        ref[0] = _dot(hb, wb_ref[:, col:col + ref.shape[-1]])
        col += ref.shape[-1]


def _head_sum_matrix(width):
    h = np.arange(width) // HEAD_DIM
    return jnp.asarray((h[:, None] == h[None, :]).astype(np.float32), BF16)


def _proj_call(x, mod3, norm_w, w_aligned, w_regrouped, conv_w):
    b, t, _ = x.shape
    tm = PROJ_TM
    row = lambda width: pl.BlockSpec((1, tm, width), lambda bi, ti: (bi, ti, 0))
    const = lambda shape: pl.BlockSpec(shape, lambda bi, ti: (0,) * len(shape))
    return pl.pallas_call(
        _proj_kernel,
        grid=(b, t // tm),
        in_specs=[
            row(D_MODEL),
            pl.BlockSpec((1, 1, 6 * D_MODEL), lambda bi, ti: (bi, 0, 0)),
            const((1, D_MODEL)),
            const((D_MODEL, PROJ_ALIGNED)),
            const((D_MODEL, PROJ_REGROUPED)),
            const((CONV_WIDTH, QKV_W)),
            const((GDN_WIDTH, GDN_WIDTH)),
        ],
        out_specs=[row(w) for w in PROJ_SPLITS],
        out_shape=[jax.ShapeDtypeStruct((b, t, w), F32) for w in PROJ_SPLITS],
        scratch_shapes=[pltpu.VMEM((8 + tm, GDN_WIDTH), F32)] * 3,
        compiler_params=pltpu.CompilerParams(
            dimension_semantics=("arbitrary", "arbitrary"), vmem_limit_bytes=VMEM_LIMIT),
        name="proj",
    )(x, mod3, norm_w, w_aligned, w_regrouped, conv_w, _head_sum_matrix(GDN_WIDTH))


GDN_UNITS = (GDN_TT // CHUNK) * (GDN_HEADS // HEADS_PER_GROUP)


def _block_diag_rows(p, same_head):
    stacked = jnp.concatenate([p.astype(BF16)] * HEADS_PER_GROUP, axis=0)
    return jnp.where(same_head, stacked, jnp.zeros((), BF16))


_DONE = object()


def _gdn_kernel(qkv_ref, z_ref, ab_ref, alog_ref, dtb_ref, gnw_ref, bd_ref, eg_ref, eb_ref, ltri_ref,
                wg_ref, wu_ref, wd_ref, wo_ref, o_ref, wg16_ref, wu16_ref, wd16_ref, wo16_ref,
                s_ref, obuf, wq_buf, u_buf, qk_buf, kd_buf, dec_buf, *, tiles_per_seq):
    tt = GDN_TT
    step = pl.program_id(0)
    @pl.when(step < CAST_STEPS)
    def _():
        for src, dst in ((wg_ref, wg16_ref), (wu_ref, wu16_ref), (wd_ref, wd16_ref), (wo_ref, wo16_ref)):
            dst[...] = src[...].astype(BF16)

    wslot = step % 2
    rslot = 1 - wslot
    starts_sequence = (step - 1) % tiles_per_seq == 0

    @pl.when(step == 0)
    def _():
        for buf in (s_ref, wq_buf, u_buf, qk_buf, kd_buf, dec_buf):
            buf[...] = jnp.zeros_like(buf)

    bd = bd_ref[...]
    q = qkv_ref[0, :, 0:GDN_WIDTH]
    k = qkv_ref[0, :, GDN_WIDTH:2 * GDN_WIDTH]
    v = qkv_ref[0, :, 2 * GDN_WIDTH:]

    ab = ab_ref[0]
    lane = lax.broadcasted_iota(jnp.int32, (1, GATE_PAD), 1)
    xs = ab + dtb_ref[...]
    softplus = jnp.maximum(xs, 0.0) + jnp.log1p(jnp.exp(-jnp.abs(xs)))
    g = jnp.where(lane < GDN_HEADS, -jnp.exp(alog_ref[...]) * softplus, 0.0)
    beta = _sigmoid(ab)
    ltri = ltri_ref[...]
    g1, g2 = _split2(g)
    gcum = _dot(ltri, g1) + _dot(ltri, g2)
    eg = eg_ref[...]
    c1, c2 = _split2(gcum)
    g_x = _dot(c1, eg) + _dot(c2, eg)
    beta_x = _dot(beta.astype(BF16), eb_ref[...])

    r64 = lax.broadcasted_iota(jnp.int32, (CHUNK, GROUP_W), 0)
    l64 = lax.broadcasted_iota(jnp.int32, (CHUNK, GROUP_W), 1) % CHUNK
    causal = r64 >= l64
    eye_x = (r64 == l64).astype(F32)
    strict_x = (r64 > l64).astype(F32)
    same_block = lambda size: (r64 // size) == (l64 // size)
    base_x = same_block(INV_BASE).astype(F32)
    level_x = [(same_block(2 * size) & ~same_block(size)).astype(F32) for size in INV_LEVELS]
    rb = lax.broadcasted_iota(jnp.int32, (GROUP_W, GROUP_W), 0) // HEAD_DIM
    cb = lax.broadcasted_iota(jnp.int32, (GROUP_W, GROUP_W), 1) // HEAD_DIM
    same_head = rb == cb
    mask_bd = same_head.astype(F32)

    n_groups = GDN_HEADS // HEADS_PER_GROUP
    n_chunks = tt // CHUNK
    units = [(c, gi) for c in range(n_chunks) for gi in range(n_groups)]
    ids = list(range(len(units)))
    tile = lambda a: [a[c * CHUNK:(c + 1) * CHUNK, gi * GROUP_W:(gi + 1) * GROUP_W] for c, gi in units]
    each = lambda f, *lists: [f(*args) for args in zip(*lists)]
    bdr = lambda t: _block_diag_rows(t, same_head)
    mm = lambda a, w16: _dot(a.astype(BF16), w16)
    stack = lambda a, b_: jnp.concatenate([a, b_], axis=0)

    def chunk_parallel_part():
        kc, qc, vc, bx, gx = tile(k), tile(q), tile(v), tile(beta_x), tile(g_x)
        eg_c = each(jnp.exp, gx)
        glast = each(lambda g_: g_[CHUNK - 1:CHUNK, :], gx)
        kb = each(jnp.multiply, kc, bx)
        vb = each(jnp.multiply, vc, bx)
        wr = each(jnp.multiply, kb, eg_c)
        qd = each(jnp.multiply, qc, eg_c)
        for i in ids:
            kd_buf[wslot, i] = (kc[i] * jnp.exp(glast[i] - gx[i])).astype(BF16)
            dec_buf[wslot, i] = jnp.broadcast_to(jnp.exp(glast[i]), (8, GROUP_W))
        dm = each(lambda g_: jnp.exp(jnp.where(causal, g_ - jnp.sum(g_ * eye_x, axis=0, keepdims=True), NEG_BIG)),
                  gx)
        yield
        aq = each(lambda i, a, b_, k_: _dot_nt(stack(a, b_).astype(BF16), bdr(k_)), ids, kb, qc, kc)
        xm = each(lambda a, d_: -(a[0:CHUNK] * d_ * strict_x), aq, dm)
        for i in ids:
            qk_buf[wslot, i] = (aq[i][CHUNK:] * dm[i]).astype(BF16)
        xd = each(lambda x_: x_ * base_x, xm)
        tinv = each(lambda x_: eye_x + x_, xd)
        yield
        pw = each(lambda i, x_: mm(x_, bdr(x_)), ids, xd)
        yield
        for _ in range(INV_BASE.bit_length() - 3):
            r = each(lambda i, t_, p_: mm(stack(t_, p_), bdr(p_)), ids, tinv, pw)
            tinv = each(lambda t_, r_: t_ + r_[0:CHUNK], tinv, r)
            pw = each(lambda r_: r_[CHUNK:], r)
            yield
        tinv = each(lambda i, t_, p_: t_ + mm(t_, bdr(p_)), ids, tinv, pw)
        yield
        for lm in level_x:
            e = each(lambda i, x_, t_: mm(x_ * lm, bdr(t_)), ids, xm, tinv)
            yield
            tinv = each(lambda i, t_, e_: t_ + mm(t_, bdr(e_)), ids, tinv, e)
            yield
        for i in ids:
            uw = mm(tinv[i], jnp.concatenate([bdr(vb[i]), bdr(wr[i])], axis=1))
            u_buf[wslot, i] = uw[:, 0:GROUP_W]
            wq_buf[wslot, i, 0:CHUNK] = uw[:, GROUP_W:].astype(BF16)
            wq_buf[wslot, i, CHUNK:] = qd[i].astype(BF16)

    def sequential_part():
        s_state = [jnp.where(starts_sequence, 0.0, s_ref[gi]) for gi in range(n_groups)]
        for c in range(n_chunks):
            cids = [c * n_groups + gi for gi in range(n_groups)]
            r2 = [_dot(wq_buf[rslot, i], s_state[gi].astype(BF16)) for gi, i in enumerate(cids)]
            yield
            vn = [u_buf[rslot, i] - r2[gi][0:CHUNK] for gi, i in enumerate(cids)]
            o = [r2[gi][CHUNK:] + _dot(qk_buf[rslot, i], bdr(vn[gi])) for gi, i in enumerate(cids)]
            s_state = [s_state[gi] * dec_buf[rslot, i][0:1] + mask_bd * _dot_tn(kd_buf[rslot, i], vn[gi].astype(BF16))
                       for gi, i in enumerate(cids)]
            for gi in range(n_groups):
                obuf[c * CHUNK:(c + 1) * CHUNK, gi * GROUP_W:(gi + 1) * GROUP_W] = o[gi]
            yield
        for gi in range(n_groups):
            s_ref[gi] = s_state[gi]

    parts = [chunk_parallel_part(), sequential_part()]
    while parts:
        parts = [p for p in parts if next(p, _DONE) is not _DONE]

    o = obuf[...]
    ms = _dot((o * o).astype(BF16), bd) * (1.0 / HEAD_DIM)
    zz = z_ref[0]
    o_ref[0] = (o * lax.rsqrt(ms + EPS) * gnw_ref[...] * (zz * _sigmoid(zz))).astype(BF16)


def _gdn_consts():
    h = np.arange(GDN_WIDTH) // HEAD_DIM
    eg = np.zeros((GATE_PAD, GDN_WIDTH), np.float32)
    eb = np.zeros((GATE_PAD, GDN_WIDTH), np.float32)
    eg[h, np.arange(GDN_WIDTH)] = 1.0
    eb[GDN_HEADS + h, np.arange(GDN_WIDTH)] = 1.0
    t = np.arange(GDN_TT)
    ltri = ((t[:, None] // CHUNK == t[None, :] // CHUNK) & (t[:, None] >= t[None, :])).astype(np.float32)
    return (_head_sum_matrix(GDN_WIDTH), jnp.asarray(eg, BF16), jnp.asarray(eb, BF16), jnp.asarray(ltri, BF16))


BF16_ROWS = 16
CAST_STEPS = 2 * GDN_HEADS


def _swa_head_source(i):
    j = i - GDN_HEADS
    return jnp.where(i < GDN_HEADS, i, GDN_HEADS + (j % 2) * SWA_GROUP + j // 2)


def _cast_specs(weight, unit, steps, source=None):
    rows, cols = weight.shape
    block_rows = next(r for r in range(unit, rows + 1, unit) if rows % r == 0 and rows // r <= steps)
    if source is None:
        source = lambda i: i
    else:
        assert block_rows == unit, "row regrouping needs one unit per block"
    last = rows // block_rows - 1
    src = pl.BlockSpec((block_rows, cols), lambda n: (source(jnp.minimum(n, last)), 0))
    dst = pl.BlockSpec((block_rows, cols), lambda n: (jnp.minimum(n, last), 0))
    return src, dst, jax.ShapeDtypeStruct((rows, cols), BF16)


def _gdn_call(qkv, z, ab, alog_pad, dtb_pad, gnw_x, w_gate, w_up, w_down, w_out):
    b, t, _ = qkv.shape
    tt = GDN_TT
    bd, eg, eb, ltri = _gdn_consts()
    n_tiles = t // tt
    n_groups = GDN_HEADS // HEADS_PER_GROUP
    total = b * n_tiles

    def tile_block(width, lag):
        def index(n):
            m = jnp.clip(n - lag, 0, total - 1)
            return (m // n_tiles, m % n_tiles, 0)
        return pl.BlockSpec((1, tt, width), index)

    ahead = lambda width: tile_block(width, 0)
    behind = lambda width: tile_block(width, 1)
    const = lambda shape: pl.BlockSpec(shape, lambda n: (0,) * len(shape))
    per_unit = lambda rows, dtype: pltpu.VMEM((2, GDN_UNITS, rows, GROUP_W), dtype)
    steps = total + 1
    assert steps >= CAST_STEPS
    casts = [
        _cast_specs(w_gate, BF16_ROWS, CAST_STEPS),
        _cast_specs(w_up, BF16_ROWS, CAST_STEPS),
        _cast_specs(w_down, BF16_ROWS, CAST_STEPS),
        _cast_specs(w_out, HEAD_DIM, CAST_STEPS, _swa_head_source),
    ]
    return pl.pallas_call(
        functools.partial(_gdn_kernel, tiles_per_seq=n_tiles),
        grid=(steps,),
        in_specs=[
            ahead(QKV_W), behind(GDN_WIDTH), ahead(GATE_PAD),
            const((1, GATE_PAD)), const((1, GATE_PAD)), const((1, GDN_WIDTH)),
            const((GDN_WIDTH, GDN_WIDTH)), const((GATE_PAD, GDN_WIDTH)), const((GATE_PAD, GDN_WIDTH)),
            const((tt, tt)),
        ] + [c[0] for c in casts],
        out_specs=[behind(GDN_WIDTH)] + [c[1] for c in casts],
        out_shape=[jax.ShapeDtypeStruct((b, t, GDN_WIDTH), BF16)] + [c[2] for c in casts],
        scratch_shapes=[
            pltpu.VMEM((n_groups, GROUP_W, GROUP_W), F32),
            pltpu.VMEM((tt, GDN_WIDTH), F32),
            per_unit(2 * CHUNK, BF16), per_unit(CHUNK, F32), per_unit(CHUNK, BF16), per_unit(CHUNK, BF16),
            per_unit(8, F32),
        ],
        compiler_params=pltpu.CompilerParams(
            dimension_semantics=("arbitrary",), vmem_limit_bytes=VMEM_LIMIT),
        name="gdn",
    )(qkv, z, ab, alog_pad, dtb_pad, gnw_x, bd, eg, eb, ltri, w_gate, w_up, w_down, w_out)


SWA_PAIRS = SWA_Q_HEADS // 2
SWA_ROWS = SWA_Q_HEADS * WINDOW
LOG2E = 1.4426950408889634
SWA_INTERLEAVE = 8


def _swa_kernel(q_ref, kvc_ref, kvp_ref, qw_ref, kw_ref, bdq_ref, bdk_ref, bias_ref, sink_ref, o_ref):
    tq = SWA_TQ
    ti = pl.program_id(1)
    q = q_ref[0]
    q = q * lax.rsqrt(_dot((q * q).astype(BF16), bdq_ref[...]) * (1.0 / HEAD_DIM) + EPS)
    q = q * (qw_ref[...] * (HEAD_DIM ** -0.5 * LOG2E))
    kv = jnp.concatenate([kvp_ref[0], kvc_ref[0]], axis=0)
    k = kv[:, 0:SWA_KV_WIDTH]
    v = kv[:, SWA_KV_WIDTH:]
    k = k * lax.rsqrt(_dot((k * k).astype(BF16), bdk_ref[...]) * (1.0 / HEAD_DIM) + EPS) * kw_ref[...]
    k16 = k.astype(BF16)
    lo = lax.broadcasted_iota(jnp.int32, (1, LANES), 1) < HEAD_DIM
    qcol = lax.broadcasted_iota(jnp.int32, (WINDOW, SWA_ROWS), 1) % WINDOW
    from_prev = lax.broadcasted_iota(jnp.int32, (WINDOW, SWA_ROWS), 0) > qcol
    first = jnp.where(ti == 0, 1, 0)
    sink = sink_ref[...]
    vt16 = v.T.astype(BF16)
    zero = jnp.zeros((), BF16)
    def block(j):
        qj = q[j * WINDOW:(j + 1) * WINDOW]
        parts = []
        for p in range(SWA_PAIRS):
            qp = qj[:, p * LANES:(p + 1) * LANES]
            parts += [jnp.where(lo, qp, 0.0), jnp.where(lo, 0.0, qp)]
        qs = jnp.concatenate(parts, axis=0).astype(BF16)
        st = _dot_nt(k16[j * WINDOW:(j + 2) * WINDOW], qs)
        yield
        bias = bias_ref[first] if j == 0 else bias_ref[0]
        sm = jnp.where(from_prev, st[0:WINDOW], st[WINDOW:]) + bias
        m = jnp.max(sm, axis=0, keepdims=True)
        yield
        pe = jnp.exp2(sm - m)
        den = jnp.sum(pe, axis=0, keepdims=True) + jnp.exp2(sink - m)
        pb = pe.astype(BF16)
        pt2 = jnp.concatenate([jnp.where(from_prev, pb, zero), jnp.where(from_prev, zero, pb)], axis=0)
        yield
        ot = _dot(vt16[:, j * WINDOW:(j + 2) * WINDOW], pt2) * (1.0 / den)
        yield
        for p in range(SWA_PAIRS):
            c0 = p * 2 * WINDOW
            pair_t = jnp.concatenate([ot[0:HEAD_DIM, c0:c0 + WINDOW],
                                      ot[HEAD_DIM:, c0 + WINDOW:c0 + 2 * WINDOW]], axis=0)
            o_ref[0, j * WINDOW:(j + 1) * WINDOW, p * LANES:(p + 1) * LANES] = pair_t.T.astype(BF16)

    n_blocks = tq // WINDOW
    for j0 in range(0, n_blocks, SWA_INTERLEAVE):
        parts = [block(j) for j in range(j0, min(j0 + SWA_INTERLEAVE, n_blocks))]
        while parts:
            parts = [p for p in parts if next(p, _DONE) is not _DONE]


def _swa_consts():
    qi = np.arange(WINDOW)[:, None]
    kj = np.arange(WINDOW)[None, :]
    from_prev = kj > qi
    dist = np.where(from_prev, qi + WINDOW - kj, qi - kj).astype(np.float32)
    slopes = 2.0 ** (-8.0 * (np.arange(SWA_Q_HEADS, dtype=np.float32) + 1.0) / SWA_Q_HEADS)
    bias = np.zeros((2, SWA_PAIRS, 2, WINDOW, WINDOW), np.float32)
    for first in range(2):
        for p in range(SWA_PAIRS):
            for half, head in enumerate((p, SWA_GROUP + p)):
                b = (-slopes[head].astype(np.float32) * dist) * np.float32(LOG2E)
                bias[first, p, half] = np.where(from_prev & bool(first), np.float32(NEG_BIG), b)
    bias = bias.reshape(2, SWA_ROWS, WINDOW).transpose(0, 2, 1)
    h = np.arange(SWA_WIDTH) // HEAD_DIM
    bdq = (h[:, None] == h[None, :]).astype(np.float32)
    hk = np.arange(SWA_KV_WIDTH) // HEAD_DIM
    bdk = (hk[:, None] == hk[None, :]).astype(np.float32)
    return jnp.asarray(bias), jnp.asarray(bdq, BF16), jnp.asarray(bdk, BF16)


def _swa_call(sq, skv, qw_x, kw_x, sink_col):
    b, t, _ = sq.shape
    tq = SWA_TQ
    nb = tq // WINDOW
    bias, bdq, bdk = _swa_consts()
    const = lambda shape: pl.BlockSpec(shape, lambda bi, ti: (0,) * len(shape))
    return pl.pallas_call(
        _swa_kernel,
        grid=(b, t // tq),
        in_specs=[
            pl.BlockSpec((1, tq, SWA_WIDTH), lambda bi, ti: (bi, ti, 0)),
            pl.BlockSpec((1, tq, 2 * SWA_KV_WIDTH), lambda bi, ti: (bi, ti, 0)),
            pl.BlockSpec((1, WINDOW, 2 * SWA_KV_WIDTH), lambda bi, ti: (bi, jnp.maximum(ti * nb - 1, 0), 0)),
            const((1, SWA_WIDTH)), const((1, SWA_KV_WIDTH)),
            const((SWA_WIDTH, SWA_WIDTH)), const((SWA_KV_WIDTH, SWA_KV_WIDTH)),
            const((2, WINDOW, SWA_ROWS)),
            const((1, SWA_ROWS)),
        ],
        out_specs=pl.BlockSpec((1, tq, SWA_WIDTH), lambda bi, ti: (bi, ti, 0)),
        out_shape=jax.ShapeDtypeStruct((b, t, SWA_WIDTH), BF16),
        compiler_params=pltpu.CompilerParams(
            dimension_semantics=("arbitrary", "arbitrary"), vmem_limit_bytes=VMEM_LIMIT),
        name="swa",
    )(sq, skv, skv, qw_x, kw_x, bdq, bdk, bias, sink_col)


def _ffn_kernel(x_ref, og_ref, os_ref, mod_ref, nw_ref, wo_ref, wg_ref, wu_ref, wd_ref, o_ref):
    x = x_ref[0]
    mod = mod_ref[0]
    gate1 = mod[:, 2 * D_MODEL:3 * D_MODEL]
    shift2 = mod[:, 3 * D_MODEL:4 * D_MODEL]
    scale2 = mod[:, 4 * D_MODEL:5 * D_MODEL]
    gate2 = mod[:, 5 * D_MODEL:]
    mixed = _dot(og_ref[0], wo_ref[0:GDN_WIDTH, :]) + _dot(os_ref[0], wo_ref[GDN_WIDTH:, :])
    x1 = x + gate1 * mixed
    ms = jnp.mean(x1 * x1, axis=-1, keepdims=True)
    gain = nw_ref[...] * (1.0 + scale2)
    hb = (x1 * lax.rsqrt(ms + EPS) * gain + shift2).astype(BF16)
    gt = _dot(hb, wg_ref[...])
    up = _dot(hb, wu_ref[...])
    act = ((gt * _sigmoid(gt)) * up).astype(BF16)
    o_ref[0] = x1 + gate2 * _dot(act, wd_ref[...])


def _ffn_call(x, og, osw, mod3, norm_w, wo, wg, wu, wd):
    b, t, _ = x.shape
    tm = FFN_TM
    row = lambda width: pl.BlockSpec((1, tm, width), lambda bi, ti: (bi, ti, 0))
    const = lambda shape: pl.BlockSpec(shape, lambda bi, ti: (0,) * len(shape), pipeline_mode=pl.Buffered(1))
    return pl.pallas_call(
        _ffn_kernel,
        grid=(b, t // tm),
        in_specs=[
            row(D_MODEL), row(GDN_WIDTH), row(SWA_WIDTH),
            pl.BlockSpec((1, 1, 6 * D_MODEL), lambda bi, ti: (bi, 0, 0)),
            const((1, D_MODEL)),
            const((GDN_WIDTH + SWA_WIDTH, D_MODEL)),
            const((D_MODEL, D_FF)), const((D_MODEL, D_FF)), const((D_FF, D_MODEL)),
        ],
        out_specs=row(D_MODEL),
        out_shape=jax.ShapeDtypeStruct((b, t, D_MODEL), F32),
        compiler_params=pltpu.CompilerParams(
            dimension_semantics=("arbitrary", "arbitrary"), vmem_limit_bytes=VMEM_LIMIT),
        name="ffn",
    )(x, og, osw, mod3, norm_w, wo, wg, wu, wd)


def _layer(x, c_pad, w_ada, b_ada, norm1_w, w_in, conv_w, a_log, dt_bias, gdn_norm_w, q_norm_w, k_norm_w, sinks,
           w_out, norm2_w, w_gate, w_up, w_down):
    b = x.shape[0]
    mod, w_aligned, w_regrouped = _ada_call(c_pad, w_ada, b_ada.reshape(1, -1), w_in.T)
    mod3 = mod[:b].reshape(b, 1, 6 * D_MODEL)
    qkv, z, sq, skv, ab = _proj_call(x, mod3, norm1_w.reshape(1, D_MODEL), w_aligned, w_regrouped,
                                     conv_w.reshape(CONV_WIDTH, QKV_W))

    alog_pad = jnp.pad(a_log.reshape(1, GDN_HEADS), ((0, 0), (0, GATE_PAD - GDN_HEADS)))
    dtb_pad = jnp.pad(dt_bias.reshape(1, GDN_HEADS), ((0, 0), (0, GATE_PAD - GDN_HEADS)))
    gnw_x = jnp.tile(gdn_norm_w.reshape(1, HEAD_DIM), (1, GDN_HEADS))
    og, wg16, wu16, wd16, wo16 = _gdn_call(qkv, z, ab, alog_pad, dtb_pad, gnw_x, w_gate, w_up, w_down, w_out)

    qw_x = jnp.tile(q_norm_w.reshape(1, HEAD_DIM), (1, SWA_Q_HEADS))
    kw_x = jnp.tile(k_norm_w.reshape(1, HEAD_DIM), (1, SWA_KV_HEADS))
    sink_pairs = jnp.stack([sinks[:SWA_GROUP], sinks[SWA_GROUP:]], axis=1)
    sink_col = (jnp.repeat(sink_pairs, WINDOW, axis=1) * LOG2E).reshape(1, SWA_ROWS)
    osw = _swa_call(sq, skv, qw_x, kw_x, sink_col)

    return _ffn_call(x, og, osw, mod3, norm2_w.reshape(1, D_MODEL), wo16, wg16, wu16, wd16)


def kernel(x, c, w_ada, b_ada, norm1_w, w_in, conv_w, a_log, dt_bias, gdn_norm_w, q_norm_w, k_norm_w, sinks,
           w_out, norm2_w, w_gate, w_up, w_down):
    depth = w_ada.shape[0]
    b = c.shape[0]
    c_pad = jnp.pad(c, ((0, 8 - b), (0, 0)))
    for l in range(depth):
        x = _layer(x, c_pad, w_ada[l], b_ada[l], norm1_w[l], w_in[l], conv_w[l], a_log[l], dt_bias[l],
                   gdn_norm_w[l], q_norm_w[l], k_norm_w[l], sinks[l], w_out[l], norm2_w[l], w_gate[l], w_up[l],
                   w_down[l])
    return x
```

```python
import functools

import numpy as np
import jax
import jax.numpy as jnp
from jax import lax
from jax.experimental import pallas as pl
from jax.experimental.pallas import tpu as pltpu

F32 = jnp.float32
BF16 = jnp.bfloat16

D_MODEL = 1024
HEAD_DIM = 64
GDN_HEADS = 8
GDN_WIDTH = GDN_HEADS * HEAD_DIM
SWA_Q_HEADS = 8
SWA_KV_HEADS = 2
SWA_GROUP = SWA_Q_HEADS // SWA_KV_HEADS
SWA_WIDTH = SWA_Q_HEADS * HEAD_DIM
SWA_KV_WIDTH = SWA_KV_HEADS * HEAD_DIM
WINDOW = 128
CONV_WIDTH = 4
CHUNK = 64
D_FF = 2816
EPS = 1e-6
LANES = 128
GATE_PAD = LANES
HEADS_PER_GROUP = 2
GROUP_W = HEADS_PER_GROUP * HEAD_DIM
INV_BASE = 16
INV_LEVELS = (16, 32)
NEG_BIG = -1e30
VMEM_LIMIT = 56 * 1024 * 1024

PROJ_TM = 512
GDN_TT = 256
SWA_TQ = 1024
FFN_TM = 512
ADA_TN = 1536


def _sigmoid(x):
    return 1.0 / (1.0 + jnp.exp(-x))


def _dot(a, b):
    return jnp.dot(a, b, preferred_element_type=F32)


def _dot_nt(a, b):
    return lax.dot_general(a, b, (((1,), (1,)), ((), ())), preferred_element_type=F32)


def _dot_tn(a, b):
    return lax.dot_general(a, b, (((0,), (0,)), ((), ())), preferred_element_type=F32)


def _split2(x):
    hi = x.astype(BF16)
    lo = (x - hi.astype(F32)).astype(BF16)
    return hi, lo


def _ada_kernel(c_ref, w_ref, b_ref, win_ref, o_ref, wa16_ref, wb16_ref):
    c = c_ref[...]
    ca = c * _sigmoid(c)
    o_ref[...] = _dot(ca.astype(BF16), w_ref[...].astype(BF16)) + b_ref[...]

    @pl.when(pl.program_id(0) == 0)
    def _():
        wa16_ref[...] = win_ref[0:PROJ_ALIGNED, :].T.astype(BF16)
        o1 = PROJ_ALIGNED + 2 * GDN_HEADS
        o2 = o1 + SWA_WIDTH
        head = lambda h: win_ref[o1 + h * HEAD_DIM:o1 + (h + 1) * HEAD_DIM, :]
        pieces = [head(half * SWA_GROUP + p) for p in range(SWA_PAIRS) for half in range(2)]
        pieces.append(win_ref[o2:o2 + 2 * SWA_KV_WIDTH, :])
        pieces.append(win_ref[PROJ_ALIGNED:o1, :])
        pieces.append(jnp.zeros((GATE_PAD - 2 * GDN_HEADS, D_MODEL), F32))
        wb16_ref[...] = jnp.concatenate(pieces, axis=0).T.astype(BF16)


def _ada_call(c_pad, w_ada, b_ada, w_in):
    n = w_ada.shape[1]
    whole = lambda shape: pl.BlockSpec(shape, lambda j: (0, 0))
    return pl.pallas_call(
        _ada_kernel,
        grid=(n // ADA_TN,),
        in_specs=[
            whole((8, D_MODEL)),
            pl.BlockSpec((D_MODEL, ADA_TN), lambda j: (0, j)),
            pl.BlockSpec((1, ADA_TN), lambda j: (0, j)),
            pl.BlockSpec(w_in.shape, lambda j: (0, 0), pipeline_mode=pl.Buffered(1)),
        ],
        out_specs=[pl.BlockSpec((8, ADA_TN), lambda j: (0, j)),
                   whole((D_MODEL, PROJ_ALIGNED)), whole((D_MODEL, PROJ_REGROUPED))],
        out_shape=[jax.ShapeDtypeStruct((8, n), F32),
                   jax.ShapeDtypeStruct((D_MODEL, PROJ_ALIGNED), BF16),
                   jax.ShapeDtypeStruct((D_MODEL, PROJ_REGROUPED), BF16)],
        compiler_params=pltpu.CompilerParams(
            dimension_semantics=("arbitrary",), vmem_limit_bytes=VMEM_LIMIT),
        name="ada",
    )(c_pad, w_ada, b_ada, w_in)


QKV_W = 3 * GDN_WIDTH
PROJ_SPLITS = (QKV_W, GDN_WIDTH, SWA_WIDTH, 2 * SWA_KV_WIDTH, GATE_PAD)
PROJ_ALIGNED = QKV_W + GDN_WIDTH
PROJ_REGROUPED = sum(PROJ_SPLITS) - PROJ_ALIGNED


def _proj_kernel(x_ref, mod_ref, nw_ref, wa_ref, wb_ref, cw_ref, bd_ref, qkv_ref, z_ref, sq_ref, skv_ref, ab_ref,
                 qbuf, kbuf, vbuf):
    tm = PROJ_TM
    bufs = (qbuf, kbuf, vbuf)

    @pl.when(pl.program_id(1) == 0)
    def _():
        for buf in bufs:
            buf[0:8, :] = jnp.zeros((8, GDN_WIDTH), F32)

    x = x_ref[0]
    ms = jnp.mean(x * x, axis=-1, keepdims=True)
    mod = mod_ref[0]
    shift = mod[:, 0:D_MODEL]
    gain = nw_ref[...] * (1.0 + mod[:, D_MODEL:2 * D_MODEL])
    hb = (x * lax.rsqrt(ms + EPS) * gain + shift).astype(BF16)

    def project(s):
        bufs[s][8:8 + tm, :] = _dot(hb, wa_ref[:, s * GDN_WIDTH:(s + 1) * GDN_WIDTH])

    def conv_silu(s):
        buf = bufs[s]
        cw = lambda j: cw_ref[j:j + 1, s * GDN_WIDTH:(s + 1) * GDN_WIDTH]
        acc = cw(CONV_WIDTH - 1) * buf[8:8 + tm, :]
        for j in range(CONV_WIDTH - 1):
            off = 8 - (CONV_WIDTH - 1) + j
            acc = acc + cw(j) * buf[off:off + tm, :]
        buf[0:8, :] = buf[tm:tm + 8, :]
        return acc * _sigmoid(acc)

    def l2_normed(a, scale):
        return a * lax.rsqrt(_dot((a * a).astype(BF16), bd_ref[...]) + EPS) * scale

    def rest(ref, w_ref, c0):
        ref[0] = _dot(hb, w_ref[:, c0:c0 + ref.shape[-1]])

    project(0)
    project(1)
    rest(z_ref, wa_ref, QKV_W)
    qkv_ref[0, :, 0:GDN_WIDTH] = l2_normed(conv_silu(0), HEAD_DIM ** -0.5)
    project(2)
    rest(sq_ref, wb_ref, 0)
    qkv_ref[0, :, GDN_WIDTH:2 * GDN_WIDTH] = l2_normed(conv_silu(1), 1.0)
    rest(skv_ref, wb_ref, SWA_WIDTH)
    rest(ab_ref, wb_ref, SWA_WIDTH + 2 * SWA_KV_WIDTH)
    qkv_ref[0, :, 2 * GDN_WIDTH:] = conv_silu(2)


def _head_sum_matrix(width):
    h = np.arange(width) // HEAD_DIM
    return jnp.asarray((h[:, None] == h[None, :]).astype(np.float32), BF16)


def _proj_call(x, mod3, norm_w, w_aligned, w_regrouped, conv_w):
    b, t, _ = x.shape
    tm = PROJ_TM
    row = lambda width: pl.BlockSpec((1, tm, width), lambda bi, ti: (bi, ti, 0))
    const = lambda shape: pl.BlockSpec(shape, lambda bi, ti: (0,) * len(shape))
    return pl.pallas_call(
        _proj_kernel,
        grid=(b, t // tm),
        in_specs=[
            row(D_MODEL),
            pl.BlockSpec((1, 1, 6 * D_MODEL), lambda bi, ti: (bi, 0, 0)),
            const((1, D_MODEL)),
            const((D_MODEL, PROJ_ALIGNED)),
            const((D_MODEL, PROJ_REGROUPED)),
            const((CONV_WIDTH, QKV_W)),
            const((GDN_WIDTH, GDN_WIDTH)),
        ],
        out_specs=[row(w) for w in PROJ_SPLITS],
        out_shape=[jax.ShapeDtypeStruct((b, t, w), F32) for w in PROJ_SPLITS],
        scratch_shapes=[pltpu.VMEM((8 + tm, GDN_WIDTH), F32)] * 3,
        compiler_params=pltpu.CompilerParams(
            dimension_semantics=("arbitrary", "arbitrary"), vmem_limit_bytes=VMEM_LIMIT),
        name="proj",
    )(x, mod3, norm_w, w_aligned, w_regrouped, conv_w, _head_sum_matrix(GDN_WIDTH))


GDN_UNITS = (GDN_TT // CHUNK) * (GDN_HEADS // HEADS_PER_GROUP)


def _block_diag_rows(p, same_head):
    stacked = jnp.concatenate([p.astype(BF16)] * HEADS_PER_GROUP, axis=0)
    return jnp.where(same_head, stacked, jnp.zeros((), BF16))


_DONE = object()


def _gdn_kernel(qkv_ref, z_ref, ab_ref, alog_ref, dtb_ref, gnw_ref, bd_ref, eg_ref, eb_ref, ltri_ref,
                wg_ref, wu_ref, wd_ref, wo_ref, o_ref, wg16_ref, wu16_ref, wd16_ref, wo16_ref,
                s_ref, obuf, wq_buf, u_buf, qk_buf, kd_buf, dec_buf, *, tiles_per_seq):
    tt = GDN_TT
    step = pl.program_id(0)
    @pl.when(step < CAST_STEPS)
    def _():
        for src, dst in ((wg_ref, wg16_ref), (wu_ref, wu16_ref), (wd_ref, wd16_ref), (wo_ref, wo16_ref)):
            dst[...] = src[...].astype(BF16)

    wslot = step % 2
    rslot = 1 - wslot
    starts_sequence = (step - 1) % tiles_per_seq == 0

    @pl.when(step == 0)
    def _():
        for buf in (s_ref, wq_buf, u_buf, qk_buf, kd_buf, dec_buf):
            buf[...] = jnp.zeros_like(buf)

    bd = bd_ref[...]
    q = qkv_ref[0, :, 0:GDN_WIDTH]
    k = qkv_ref[0, :, GDN_WIDTH:2 * GDN_WIDTH]
    v = qkv_ref[0, :, 2 * GDN_WIDTH:]

    ab = ab_ref[0]
    lane = lax.broadcasted_iota(jnp.int32, (1, GATE_PAD), 1)
    xs = ab + dtb_ref[...]
    softplus = jnp.maximum(xs, 0.0) + jnp.log1p(jnp.exp(-jnp.abs(xs)))
    g = jnp.where(lane < GDN_HEADS, -jnp.exp(alog_ref[...]) * softplus, 0.0)
    beta = _sigmoid(ab)
    ltri = ltri_ref[...]
    g1, g2 = _split2(g)
    gcum = _dot(ltri, g1) + _dot(ltri, g2)
    eg = eg_ref[...]
    c1, c2 = _split2(gcum)
    g_x = _dot(c1, eg) + _dot(c2, eg)
    beta_x = _dot(beta.astype(BF16), eb_ref[...])

    r64 = lax.broadcasted_iota(jnp.int32, (CHUNK, GROUP_W), 0)
    l64 = lax.broadcasted_iota(jnp.int32, (CHUNK, GROUP_W), 1) % CHUNK
    causal = r64 >= l64
    eye_x = (r64 == l64).astype(F32)
    strict_x = (r64 > l64).astype(F32)
    same_block = lambda size: (r64 // size) == (l64 // size)
    base_x = same_block(INV_BASE).astype(F32)
    level_x = [(same_block(2 * size) & ~same_block(size)).astype(F32) for size in INV_LEVELS]
    rb = lax.broadcasted_iota(jnp.int32, (GROUP_W, GROUP_W), 0) // HEAD_DIM
    cb = lax.broadcasted_iota(jnp.int32, (GROUP_W, GROUP_W), 1) // HEAD_DIM
    same_head = rb == cb
    mask_bd = same_head.astype(F32)

    n_groups = GDN_HEADS // HEADS_PER_GROUP
    n_chunks = tt // CHUNK
    units = [(c, gi) for c in range(n_chunks) for gi in range(n_groups)]
    ids = list(range(len(units)))
    tile = lambda a: [a[c * CHUNK:(c + 1) * CHUNK, gi * GROUP_W:(gi + 1) * GROUP_W] for c, gi in units]
    each = lambda f, *lists: [f(*args) for args in zip(*lists)]
    bdr = lambda t: _block_diag_rows(t, same_head)
    mm = lambda a, w16: _dot(a.astype(BF16), w16)
    stack = lambda a, b_: jnp.concatenate([a, b_], axis=0)

    def chunk_parallel_part():
        kc, qc, vc, bx, gx = tile(k), tile(q), tile(v), tile(beta_x), tile(g_x)
        eg_c = each(jnp.exp, gx)
        glast = each(lambda g_: g_[CHUNK - 1:CHUNK, :], gx)
        kb = each(jnp.multiply, kc, bx)
        vb = each(jnp.multiply, vc, bx)
        wr = each(jnp.multiply, kb, eg_c)
        qd = each(jnp.multiply, qc, eg_c)
        for i in ids:
            kd_buf[wslot, i] = (kc[i] * jnp.exp(glast[i] - gx[i])).astype(BF16)
            dec_buf[wslot, i] = jnp.broadcast_to(jnp.exp(glast[i]), (8, GROUP_W))
        dm = each(lambda g_: jnp.exp(jnp.where(causal, g_ - jnp.sum(g_ * eye_x, axis=0, keepdims=True), NEG_BIG)),
                  gx)
        yield
        aq = each(lambda i, a, b_, k_: _dot_nt(stack(a, b_).astype(BF16), bdr(k_)), ids, kb, qc, kc)
        xm = each(lambda a, d_: -(a[0:CHUNK] * d_ * strict_x), aq, dm)
        for i in ids:
            qk_buf[wslot, i] = (aq[i][CHUNK:] * dm[i]).astype(BF16)
        xd = each(lambda x_: x_ * base_x, xm)
        tinv = each(lambda x_: eye_x + x_, xd)
        yield
        pw = each(lambda i, x_: mm(x_, bdr(x_)), ids, xd)
        yield
        for _ in range(INV_BASE.bit_length() - 3):
            r = each(lambda i, t_, p_: mm(stack(t_, p_), bdr(p_)), ids, tinv, pw)
            tinv = each(lambda t_, r_: t_ + r_[0:CHUNK], tinv, r)
            pw = each(lambda r_: r_[CHUNK:], r)
            yield
        tinv = each(lambda i, t_, p_: t_ + mm(t_, bdr(p_)), ids, tinv, pw)
        yield
        for lm in level_x:
            e = each(lambda i, x_, t_: mm(x_ * lm, bdr(t_)), ids, xm, tinv)
            yield
            tinv = each(lambda i, t_, e_: t_ + mm(t_, bdr(e_)), ids, tinv, e)
            yield
        for i in ids:
            uw = mm(tinv[i], jnp.concatenate([bdr(vb[i]), bdr(wr[i])], axis=1))
            u_buf[wslot, i] = uw[:, 0:GROUP_W]
            wq_buf[wslot, i, 0:CHUNK] = uw[:, GROUP_W:].astype(BF16)
            wq_buf[wslot, i, CHUNK:] = qd[i].astype(BF16)

    def sequential_part():
        s_state = [jnp.where(starts_sequence, 0.0, s_ref[gi]) for gi in range(n_groups)]
        for c in range(n_chunks):
            cids = [c * n_groups + gi for gi in range(n_groups)]
            r2 = [_dot(wq_buf[rslot, i], s_state[gi].astype(BF16)) for gi, i in enumerate(cids)]
            yield
            vn = [u_buf[rslot, i] - r2[gi][0:CHUNK] for gi, i in enumerate(cids)]
            o = [r2[gi][CHUNK:] + _dot(qk_buf[rslot, i], bdr(vn[gi])) for gi, i in enumerate(cids)]
            s_state = [s_state[gi] * dec_buf[rslot, i][0:1] + mask_bd * _dot_tn(kd_buf[rslot, i], vn[gi].astype(BF16))
                       for gi, i in enumerate(cids)]
            for gi in range(n_groups):
                obuf[c * CHUNK:(c + 1) * CHUNK, gi * GROUP_W:(gi + 1) * GROUP_W] = o[gi]
            yield
        for gi in range(n_groups):
            s_ref[gi] = s_state[gi]

    parts = [chunk_parallel_part(), sequential_part()]
    while parts:
        parts = [p for p in parts if next(p, _DONE) is not _DONE]

    o = obuf[...]
    ms = _dot((o * o).astype(BF16), bd) * (1.0 / HEAD_DIM)
    zz = z_ref[0]
    o_ref[0] = (o * lax.rsqrt(ms + EPS) * gnw_ref[...] * (zz * _sigmoid(zz))).astype(BF16)


def _gdn_consts():
    h = np.arange(GDN_WIDTH) // HEAD_DIM
    eg = np.zeros((GATE_PAD, GDN_WIDTH), np.float32)
    eb = np.zeros((GATE_PAD, GDN_WIDTH), np.float32)
    eg[h, np.arange(GDN_WIDTH)] = 1.0
    eb[GDN_HEADS + h, np.arange(GDN_WIDTH)] = 1.0
    t = np.arange(GDN_TT)
    ltri = ((t[:, None] // CHUNK == t[None, :] // CHUNK) & (t[:, None] >= t[None, :])).astype(np.float32)
    return (_head_sum_matrix(GDN_WIDTH), jnp.asarray(eg, BF16), jnp.asarray(eb, BF16), jnp.asarray(ltri, BF16))


BF16_ROWS = 16
CAST_STEPS = 2 * GDN_HEADS


def _swa_head_source(i):
    j = i - GDN_HEADS
    return jnp.where(i < GDN_HEADS, i, GDN_HEADS + (j % 2) * SWA_GROUP + j // 2)


def _cast_specs(weight, unit, steps, source=None):
    rows, cols = weight.shape
    block_rows = next(r for r in range(unit, rows + 1, unit) if rows % r == 0 and rows // r <= steps)
    if source is None:
        source = lambda i: i
    else:
        assert block_rows == unit, "row regrouping needs one unit per block"
    last = rows // block_rows - 1
    src = pl.BlockSpec((block_rows, cols), lambda n: (source(jnp.minimum(n, last)), 0))
    dst = pl.BlockSpec((block_rows, cols), lambda n: (jnp.minimum(n, last), 0))
    return src, dst, jax.ShapeDtypeStruct((rows, cols), BF16)


def _gdn_call(qkv, z, ab, alog_pad, dtb_pad, gnw_x, w_gate, w_up, w_down, w_out):
    b, t, _ = qkv.shape
    tt = GDN_TT
    bd, eg, eb, ltri = _gdn_consts()
    n_tiles = t // tt
    n_groups = GDN_HEADS // HEADS_PER_GROUP
    total = b * n_tiles

    def tile_block(width, lag):
        def index(n):
            m = jnp.clip(n - lag, 0, total - 1)
            return (m // n_tiles, m % n_tiles, 0)
        return pl.BlockSpec((1, tt, width), index)

    ahead = lambda width: tile_block(width, 0)
    behind = lambda width: tile_block(width, 1)
    const = lambda shape: pl.BlockSpec(shape, lambda n: (0,) * len(shape))
    per_unit = lambda rows, dtype: pltpu.VMEM((2, GDN_UNITS, rows, GROUP_W), dtype)
    steps = total + 1
    assert steps >= CAST_STEPS
    casts = [
        _cast_specs(w_gate, BF16_ROWS, CAST_STEPS),
        _cast_specs(w_up, BF16_ROWS, CAST_STEPS),
        _cast_specs(w_down, BF16_ROWS, CAST_STEPS),
        _cast_specs(w_out, HEAD_DIM, CAST_STEPS, _swa_head_source),
    ]
    return pl.pallas_call(
        functools.partial(_gdn_kernel, tiles_per_seq=n_tiles),
        grid=(steps,),
        in_specs=[
            ahead(QKV_W), behind(GDN_WIDTH), ahead(GATE_PAD),
            const((1, GATE_PAD)), const((1, GATE_PAD)), const((1, GDN_WIDTH)),
            const((GDN_WIDTH, GDN_WIDTH)), const((GATE_PAD, GDN_WIDTH)), const((GATE_PAD, GDN_WIDTH)),
            const((tt, tt)),
        ] + [c[0] for c in casts],
        out_specs=[behind(GDN_WIDTH)] + [c[1] for c in casts],
        out_shape=[jax.ShapeDtypeStruct((b, t, GDN_WIDTH), BF16)] + [c[2] for c in casts],
        scratch_shapes=[
            pltpu.VMEM((n_groups, GROUP_W, GROUP_W), F32),
            pltpu.VMEM((tt, GDN_WIDTH), F32),
            per_unit(2 * CHUNK, BF16), per_unit(CHUNK, F32), per_unit(CHUNK, BF16), per_unit(CHUNK, BF16),
            per_unit(8, F32),
        ],
        compiler_params=pltpu.CompilerParams(
            dimension_semantics=("arbitrary",), vmem_limit_bytes=VMEM_LIMIT),
        name="gdn",
    )(qkv, z, ab, alog_pad, dtb_pad, gnw_x, bd, eg, eb, ltri, w_gate, w_up, w_down, w_out)


SWA_PAIRS = SWA_Q_HEADS // 2
SWA_ROWS = SWA_Q_HEADS * WINDOW
LOG2E = 1.4426950408889634
SWA_INTERLEAVE = 8


def _swa_kernel(q_ref, kvc_ref, kvp_ref, qw_ref, kw_ref, bdq_ref, bdk_ref, bias_ref, sink_ref, o_ref):
    tq = SWA_TQ
    ti = pl.program_id(1)
    q = q_ref[0]
    q = q * lax.rsqrt(_dot((q * q).astype(BF16), bdq_ref[...]) * (1.0 / HEAD_DIM) + EPS)
    q = q * (qw_ref[...] * (HEAD_DIM ** -0.5 * LOG2E))
    kv = jnp.concatenate([kvp_ref[0], kvc_ref[0]], axis=0)
    k = kv[:, 0:SWA_KV_WIDTH]
    v = kv[:, SWA_KV_WIDTH:]
    k = k * lax.rsqrt(_dot((k * k).astype(BF16), bdk_ref[...]) * (1.0 / HEAD_DIM) + EPS) * kw_ref[...]
    k16 = k.astype(BF16)
    lo = lax.broadcasted_iota(jnp.int32, (1, LANES), 1) < HEAD_DIM
    qcol = lax.broadcasted_iota(jnp.int32, (WINDOW, SWA_ROWS), 1) % WINDOW
    from_prev = lax.broadcasted_iota(jnp.int32, (WINDOW, SWA_ROWS), 0) > qcol
    first = jnp.where(ti == 0, 1, 0)
    sink = sink_ref[...]
    vt16 = v.T.astype(BF16)
    zero = jnp.zeros((), BF16)
    def block(j):
        qj = q[j * WINDOW:(j + 1) * WINDOW]
        parts = []
        for p in range(SWA_PAIRS):
            qp = qj[:, p * LANES:(p + 1) * LANES]
            parts += [jnp.where(lo, qp, 0.0), jnp.where(lo, 0.0, qp)]
        qs = jnp.concatenate(parts, axis=0).astype(BF16)
        st = _dot_nt(k16[j * WINDOW:(j + 2) * WINDOW], qs)
        yield
        bias = bias_ref[first] if j == 0 else bias_ref[0]
        sm = jnp.where(from_prev, st[0:WINDOW], st[WINDOW:]) + bias
        m = jnp.max(sm, axis=0, keepdims=True)
        yield
        pe = jnp.exp2(sm - m)
        den = jnp.sum(pe, axis=0, keepdims=True) + jnp.exp2(sink - m)
        pb = pe.astype(BF16)
        pt2 = jnp.concatenate([jnp.where(from_prev, pb, zero), jnp.where(from_prev, zero, pb)], axis=0)
        yield
        ot = _dot(vt16[:, j * WINDOW:(j + 2) * WINDOW], pt2) * (1.0 / den)
        yield
        for p in range(SWA_PAIRS):
            c0 = p * 2 * WINDOW
            pair_t = jnp.concatenate([ot[0:HEAD_DIM, c0:c0 + WINDOW],
                                      ot[HEAD_DIM:, c0 + WINDOW:c0 + 2 * WINDOW]], axis=0)
            o_ref[0, j * WINDOW:(j + 1) * WINDOW, p * LANES:(p + 1) * LANES] = pair_t.T.astype(BF16)

    n_blocks = tq // WINDOW
    for j0 in range(0, n_blocks, SWA_INTERLEAVE):
        parts = [block(j) for j in range(j0, min(j0 + SWA_INTERLEAVE, n_blocks))]
        while parts:
            parts = [p for p in parts if next(p, _DONE) is not _DONE]


def _swa_consts():
    qi = np.arange(WINDOW)[:, None]
    kj = np.arange(WINDOW)[None, :]
    from_prev = kj > qi
    dist = np.where(from_prev, qi + WINDOW - kj, qi - kj).astype(np.float32)
    slopes = 2.0 ** (-8.0 * (np.arange(SWA_Q_HEADS, dtype=np.float32) + 1.0) / SWA_Q_HEADS)
    bias = np.zeros((2, SWA_PAIRS, 2, WINDOW, WINDOW), np.float32)
    for first in range(2):
        for p in range(SWA_PAIRS):
            for half, head in enumerate((p, SWA_GROUP + p)):
                b = (-slopes[head].astype(np.float32) * dist) * np.float32(LOG2E)
                bias[first, p, half] = np.where(from_prev & bool(first), np.float32(NEG_BIG), b)
    bias = bias.reshape(2, SWA_ROWS, WINDOW).transpose(0, 2, 1)
    h = np.arange(SWA_WIDTH) // HEAD_DIM
    bdq = (h[:, None] == h[None, :]).astype(np.float32)
    hk = np.arange(SWA_KV_WIDTH) // HEAD_DIM
    bdk = (hk[:, None] == hk[None, :]).astype(np.float32)
    return jnp.asarray(bias), jnp.asarray(bdq, BF16), jnp.asarray(bdk, BF16)


def _swa_call(sq, skv, qw_x, kw_x, sink_col):
    b, t, _ = sq.shape
    tq = SWA_TQ
    nb = tq // WINDOW
    bias, bdq, bdk = _swa_consts()
    const = lambda shape: pl.BlockSpec(shape, lambda bi, ti: (0,) * len(shape))
    return pl.pallas_call(
        _swa_kernel,
        grid=(b, t // tq),
        in_specs=[
            pl.BlockSpec((1, tq, SWA_WIDTH), lambda bi, ti: (bi, ti, 0)),
            pl.BlockSpec((1, tq, 2 * SWA_KV_WIDTH), lambda bi, ti: (bi, ti, 0)),
            pl.BlockSpec((1, WINDOW, 2 * SWA_KV_WIDTH), lambda bi, ti: (bi, jnp.maximum(ti * nb - 1, 0), 0)),
            const((1, SWA_WIDTH)), const((1, SWA_KV_WIDTH)),
            const((SWA_WIDTH, SWA_WIDTH)), const((SWA_KV_WIDTH, SWA_KV_WIDTH)),
            const((2, WINDOW, SWA_ROWS)),
            const((1, SWA_ROWS)),
        ],
        out_specs=pl.BlockSpec((1, tq, SWA_WIDTH), lambda bi, ti: (bi, ti, 0)),
        out_shape=jax.ShapeDtypeStruct((b, t, SWA_WIDTH), BF16),
        compiler_params=pltpu.CompilerParams(
            dimension_semantics=("arbitrary", "arbitrary"), vmem_limit_bytes=VMEM_LIMIT),
        name="swa",
    )(sq, skv, skv, qw_x, kw_x, bdq, bdk, bias, sink_col)


def _ffn_kernel(x_ref, og_ref, os_ref, mod_ref, nw_ref, wo_ref, wg_ref, wu_ref, wd_ref, o_ref):
    x = x_ref[0]
    mod = mod_ref[0]
    gate1 = mod[:, 2 * D_MODEL:3 * D_MODEL]
    shift2 = mod[:, 3 * D_MODEL:4 * D_MODEL]
    scale2 = mod[:, 4 * D_MODEL:5 * D_MODEL]
    gate2 = mod[:, 5 * D_MODEL:]
    mixed = _dot(og_ref[0], wo_ref[0:GDN_WIDTH, :]) + _dot(os_ref[0], wo_ref[GDN_WIDTH:, :])
    x1 = x + gate1 * mixed
    ms = jnp.mean(x1 * x1, axis=-1, keepdims=True)
    gain = nw_ref[...] * (1.0 + scale2)
    hb = (x1 * lax.rsqrt(ms + EPS) * gain + shift2).astype(BF16)
    gt = _dot(hb, wg_ref[...])
    up = _dot(hb, wu_ref[...])
    act = ((gt * _sigmoid(gt)) * up).astype(BF16)
    o_ref[0] = x1 + gate2 * _dot(act, wd_ref[...])


def _ffn_call(x, og, osw, mod3, norm_w, wo, wg, wu, wd):
    b, t, _ = x.shape
    tm = FFN_TM
    row = lambda width: pl.BlockSpec((1, tm, width), lambda bi, ti: (bi, ti, 0))
    const = lambda shape: pl.BlockSpec(shape, lambda bi, ti: (0,) * len(shape), pipeline_mode=pl.Buffered(1))
    return pl.pallas_call(
        _ffn_kernel,
        grid=(b, t // tm),
        in_specs=[
            row(D_MODEL), row(GDN_WIDTH), row(SWA_WIDTH),
            pl.BlockSpec((1, 1, 6 * D_MODEL), lambda bi, ti: (bi, 0, 0)),
            const((1, D_MODEL)),
            const((GDN_WIDTH + SWA_WIDTH, D_MODEL)),
            const((D_MODEL, D_FF)), const((D_MODEL, D_FF)), const((D_FF, D_MODEL)),
        ],
        out_specs=row(D_MODEL),
        out_shape=jax.ShapeDtypeStruct((b, t, D_MODEL), F32),
        compiler_params=pltpu.CompilerParams(
            dimension_semantics=("arbitrary", "arbitrary"), vmem_limit_bytes=VMEM_LIMIT),
        name="ffn",
    )(x, og, osw, mod3, norm_w, wo, wg, wu, wd)


def _layer(x, c_pad, w_ada, b_ada, norm1_w, w_in, conv_w, a_log, dt_bias, gdn_norm_w, q_norm_w, k_norm_w, sinks,
           w_out, norm2_w, w_gate, w_up, w_down):
    b = x.shape[0]
    mod, w_aligned, w_regrouped = _ada_call(c_pad, w_ada, b_ada.reshape(1, -1), w_in.T)
    mod3 = mod[:b].reshape(b, 1, 6 * D_MODEL)
    qkv, z, sq, skv, ab = _proj_call(x, mod3, norm1_w.reshape(1, D_MODEL), w_aligned, w_regrouped,
                                     conv_w.reshape(CONV_WIDTH, QKV_W))

    alog_pad = jnp.pad(a_log.reshape(1, GDN_HEADS), ((0, 0), (0, GATE_PAD - GDN_HEADS)))
    dtb_pad = jnp.pad(dt_bias.reshape(1, GDN_HEADS), ((0, 0), (0, GATE_PAD - GDN_HEADS)))
    gnw_x = jnp.tile(gdn_norm_w.reshape(1, HEAD_DIM), (1, GDN_HEADS))
    og, wg16, wu16, wd16, wo16 = _gdn_call(qkv, z, ab, alog_pad, dtb_pad, gnw_x, w_gate, w_up, w_down, w_out)

    qw_x = jnp.tile(q_norm_w.reshape(1, HEAD_DIM), (1, SWA_Q_HEADS))
    kw_x = jnp.tile(k_norm_w.reshape(1, HEAD_DIM), (1, SWA_KV_HEADS))
    sink_pairs = jnp.stack([sinks[:SWA_GROUP], sinks[SWA_GROUP:]], axis=1)
    sink_col = (jnp.repeat(sink_pairs, WINDOW, axis=1) * LOG2E).reshape(1, SWA_ROWS)
    osw = _swa_call(sq, skv, qw_x, kw_x, sink_col)

    return _ffn_call(x, og, osw, mod3, norm2_w.reshape(1, D_MODEL), wo16, wg16, wu16, wd16)


def kernel(x, c, w_ada, b_ada, norm1_w, w_in, conv_w, a_log, dt_bias, gdn_norm_w, q_norm_w, k_norm_w, sinks,
           w_out, norm2_w, w_gate, w_up, w_down):
    depth = w_ada.shape[0]
    b = c.shape[0]
    c_pad = jnp.pad(c, ((0, 8 - b), (0, 0)))
    for l in range(depth):
        x = _layer(x, c_pad, w_ada[l], b_ada[l], norm1_w[l], w_in[l], conv_w[l], a_log[l], dt_bias[l],
                   gdn_norm_w[l], q_norm_w[l], k_norm_w[l], sinks[l], w_out[l], norm2_w[l], w_gate[l], w_up[l],
                   w_down[l])
    return x
```

```python
import functools

import numpy as np
import jax
import jax.numpy as jnp
from jax import lax
from jax.experimental import pallas as pl
from jax.experimental.pallas import tpu as pltpu

F32 = jnp.float32
BF16 = jnp.bfloat16

D_MODEL = 1024
HEAD_DIM = 64
GDN_HEADS = 8
GDN_WIDTH = GDN_HEADS * HEAD_DIM
SWA_Q_HEADS = 8
SWA_KV_HEADS = 2
SWA_GROUP = SWA_Q_HEADS // SWA_KV_HEADS
SWA_WIDTH = SWA_Q_HEADS * HEAD_DIM
SWA_KV_WIDTH = SWA_KV_HEADS * HEAD_DIM
WINDOW = 128
CONV_WIDTH = 4
CHUNK = 64
D_FF = 2816
EPS = 1e-6
LANES = 128
GATE_PAD = LANES
HEADS_PER_GROUP = 2
GROUP_W = HEADS_PER_GROUP * HEAD_DIM
INV_BASE = 16
INV_LEVELS = (16, 32)
NEG_BIG = -1e30
VMEM_LIMIT = 56 * 1024 * 1024

PROJ_TM = 512
GDN_TT = 256
SWA_TQ = 1024
FFN_TM = 512
ADA_TN = 1536


def _sigmoid(x):
    return 1.0 / (1.0 + jnp.exp(-x))


def _dot(a, b):
    return jnp.dot(a, b, preferred_element_type=F32)


def _dot_nt(a, b):
    return lax.dot_general(a, b, (((1,), (1,)), ((), ())), preferred_element_type=F32)


def _dot_tn(a, b):
    return lax.dot_general(a, b, (((0,), (0,)), ((), ())), preferred_element_type=F32)


def _split2(x):
    hi = x.astype(BF16)
    lo = (x - hi.astype(F32)).astype(BF16)
    return hi, lo


def _ada_kernel(c_ref, w_ref, b_ref, win_ref, o_ref, wa16_ref, wb16_ref):
    c = c_ref[...]
    ca = c * _sigmoid(c)
    o_ref[...] = _dot(ca.astype(BF16), w_ref[...].astype(BF16)) + b_ref[...]

    @pl.when(pl.program_id(0) == 0)
    def _():
        wa16_ref[...] = win_ref[0:PROJ_ALIGNED, :].T.astype(BF16)
        o1 = PROJ_ALIGNED + 2 * GDN_HEADS
        o2 = o1 + SWA_WIDTH
        head = lambda h: win_ref[o1 + h * HEAD_DIM:o1 + (h + 1) * HEAD_DIM, :]
        pieces = [head(half * SWA_GROUP + p) for p in range(SWA_PAIRS) for half in range(2)]
        pieces.append(win_ref[o2:o2 + 2 * SWA_KV_WIDTH, :])
        pieces.append(win_ref[PROJ_ALIGNED:o1, :])
        pieces.append(jnp.zeros((GATE_PAD - 2 * GDN_HEADS, D_MODEL), F32))
        wb16_ref[...] = jnp.concatenate(pieces, axis=0).T.astype(BF16)


def _ada_call(c_pad, w_ada, b_ada, w_in):
    n = w_ada.shape[1]
    whole = lambda shape: pl.BlockSpec(shape, lambda j: (0, 0))
    return pl.pallas_call(
        _ada_kernel,
        grid=(n // ADA_TN,),
        in_specs=[
            whole((8, D_MODEL)),
            pl.BlockSpec((D_MODEL, ADA_TN), lambda j: (0, j)),
            pl.BlockSpec((1, ADA_TN), lambda j: (0, j)),
            pl.BlockSpec(w_in.shape, lambda j: (0, 0), pipeline_mode=pl.Buffered(1)),
        ],
        out_specs=[pl.BlockSpec((8, ADA_TN), lambda j: (0, j)),
                   whole((D_MODEL, PROJ_ALIGNED)), whole((D_MODEL, PROJ_REGROUPED))],
        out_shape=[jax.ShapeDtypeStruct((8, n), F32),
                   jax.ShapeDtypeStruct((D_MODEL, PROJ_ALIGNED), BF16),
                   jax.ShapeDtypeStruct((D_MODEL, PROJ_REGROUPED), BF16)],
        compiler_params=pltpu.CompilerParams(
            dimension_semantics=("arbitrary",), vmem_limit_bytes=VMEM_LIMIT),
        name="ada",
    )(c_pad, w_ada, b_ada, w_in)


QKV_W = 3 * GDN_WIDTH
PROJ_SPLITS = (QKV_W, GDN_WIDTH, SWA_WIDTH, 2 * SWA_KV_WIDTH, GATE_PAD)
PROJ_ALIGNED = QKV_W + GDN_WIDTH
PROJ_REGROUPED = sum(PROJ_SPLITS) - PROJ_ALIGNED


def _proj_kernel(x_ref, mod_ref, nw_ref, wa_ref, wb_ref, cw_ref, bd_ref, qkv_ref, z_ref, sq_ref, skv_ref, ab_ref,
                 qbuf, kbuf, vbuf):
    tm = PROJ_TM
    bufs = (qbuf, kbuf, vbuf)

    @pl.when(pl.program_id(1) == 0)
    def _():
        for buf in bufs:
            buf[0:8, :] = jnp.zeros((8, GDN_WIDTH), F32)

    x = x_ref[0]
    ms = jnp.mean(x * x, axis=-1, keepdims=True)
    mod = mod_ref[0]
    shift = mod[:, 0:D_MODEL]
    gain = nw_ref[...] * (1.0 + mod[:, D_MODEL:2 * D_MODEL])
    hb = (x * lax.rsqrt(ms + EPS) * gain + shift).astype(BF16)

    def project(s):
        bufs[s][8:8 + tm, :] = _dot(hb, wa_ref[:, s * GDN_WIDTH:(s + 1) * GDN_WIDTH])

    def conv_silu(s):
        buf = bufs[s]
        cw = lambda j: cw_ref[j:j + 1, s * GDN_WIDTH:(s + 1) * GDN_WIDTH]
        acc = cw(CONV_WIDTH - 1) * buf[8:8 + tm, :]
        for j in range(CONV_WIDTH - 1):
            off = 8 - (CONV_WIDTH - 1) + j
            acc = acc + cw(j) * buf[off:off + tm, :]
        buf[0:8, :] = buf[tm:tm + 8, :]
        return acc * _sigmoid(acc)

    def l2_normed(a, scale):
        return a * lax.rsqrt(_dot((a * a).astype(BF16), bd_ref[...]) + EPS) * scale

    def rest(ref, w_ref, c0):
        ref[0] = _dot(hb, w_ref[:, c0:c0 + ref.shape[-1]])

    project(0)
    rest(z_ref, wa_ref, QKV_W)
    project(1)
    qkv_ref[0, :, 0:GDN_WIDTH] = l2_normed(conv_silu(0), HEAD_DIM ** -0.5)
    rest(sq_ref, wb_ref, 0)
    project(2)
    rest(skv_ref, wb_ref, SWA_WIDTH)
    qkv_ref[0, :, GDN_WIDTH:2 * GDN_WIDTH] = l2_normed(conv_silu(1), 1.0)
    rest(ab_ref, wb_ref, SWA_WIDTH + 2 * SWA_KV_WIDTH)
    qkv_ref[0, :, 2 * GDN_WIDTH:] = conv_silu(2)


def _head_sum_matrix(width):
    h = np.arange(width) // HEAD_DIM
    return jnp.asarray((h[:, None] == h[None, :]).astype(np.float32), BF16)


def _proj_call(x, mod3, norm_w, w_aligned, w_regrouped, conv_w):
    b, t, _ = x.shape
    tm = PROJ_TM
    row = lambda width: pl.BlockSpec((1, tm, width), lambda bi, ti: (bi, ti, 0))
    const = lambda shape: pl.BlockSpec(shape, lambda bi, ti: (0,) * len(shape))
    return pl.pallas_call(
        _proj_kernel,
        grid=(b, t // tm),
        in_specs=[
            row(D_MODEL),
            pl.BlockSpec((1, 1, 6 * D_MODEL), lambda bi, ti: (bi, 0, 0)),
            const((1, D_MODEL)),
            const((D_MODEL, PROJ_ALIGNED)),
            const((D_MODEL, PROJ_REGROUPED)),
            const((CONV_WIDTH, QKV_W)),
            const((GDN_WIDTH, GDN_WIDTH)),
        ],
        out_specs=[row(w) for w in PROJ_SPLITS],
        out_shape=[jax.ShapeDtypeStruct((b, t, w), F32) for w in PROJ_SPLITS],
        scratch_shapes=[pltpu.VMEM((8 + tm, GDN_WIDTH), F32)] * 3,
        compiler_params=pltpu.CompilerParams(
            dimension_semantics=("arbitrary", "arbitrary"), vmem_limit_bytes=VMEM_LIMIT),
        name="proj",
    )(x, mod3, norm_w, w_aligned, w_regrouped, conv_w, _head_sum_matrix(GDN_WIDTH))


GDN_UNITS = (GDN_TT // CHUNK) * (GDN_HEADS // HEADS_PER_GROUP)


def _block_diag_rows(p, same_head):
    stacked = jnp.concatenate([p.astype(BF16)] * HEADS_PER_GROUP, axis=0)
    return jnp.where(same_head, stacked, jnp.zeros((), BF16))


_DONE = object()


def _gdn_kernel(qkv_ref, z_ref, ab_ref, alog_ref, dtb_ref, gnw_ref, bd_ref, eg_ref, eb_ref, ltri_ref,
                wg_ref, wu_ref, wd_ref, wo_ref, o_ref, wg16_ref, wu16_ref, wd16_ref, wo16_ref,
                s_ref, obuf, wq_buf, u_buf, qk_buf, kd_buf, dec_buf, *, tiles_per_seq):
    tt = GDN_TT
    step = pl.program_id(0)
    @pl.when(step < CAST_STEPS)
    def _():
        for src, dst in ((wg_ref, wg16_ref), (wu_ref, wu16_ref), (wd_ref, wd16_ref), (wo_ref, wo16_ref)):
            dst[...] = src[...].astype(BF16)

    wslot = step % 2
    rslot = 1 - wslot
    starts_sequence = (step - 1) % tiles_per_seq == 0

    @pl.when(step == 0)
    def _():
        for buf in (s_ref, wq_buf, u_buf, qk_buf, kd_buf, dec_buf):
            buf[...] = jnp.zeros_like(buf)

    bd = bd_ref[...]
    q = qkv_ref[0, :, 0:GDN_WIDTH]
    k = qkv_ref[0, :, GDN_WIDTH:2 * GDN_WIDTH]
    v = qkv_ref[0, :, 2 * GDN_WIDTH:]

    ab = ab_ref[0]
    lane = lax.broadcasted_iota(jnp.int32, (1, GATE_PAD), 1)
    xs = ab + dtb_ref[...]
    softplus = jnp.maximum(xs, 0.0) + jnp.log1p(jnp.exp(-jnp.abs(xs)))
    g = jnp.where(lane < GDN_HEADS, -jnp.exp(alog_ref[...]) * softplus, 0.0)
    beta = _sigmoid(ab)
    ltri = ltri_ref[...]
    g1, g2 = _split2(g)
    gcum = _dot(ltri, g1) + _dot(ltri, g2)
    eg = eg_ref[...]
    c1, c2 = _split2(gcum)
    g_x = _dot(c1, eg) + _dot(c2, eg)
    beta_x = _dot(beta.astype(BF16), eb_ref[...])

    r64 = lax.broadcasted_iota(jnp.int32, (CHUNK, GROUP_W), 0)
    l64 = lax.broadcasted_iota(jnp.int32, (CHUNK, GROUP_W), 1) % CHUNK
    causal = r64 >= l64
    eye_x = (r64 == l64).astype(F32)
    strict_x = (r64 > l64).astype(F32)
    same_block = lambda size: (r64 // size) == (l64 // size)
    base_x = same_block(INV_BASE).astype(F32)
    level_x = [(same_block(2 * size) & ~same_block(size)).astype(F32) for size in INV_LEVELS]
    rb = lax.broadcasted_iota(jnp.int32, (GROUP_W, GROUP_W), 0) // HEAD_DIM
    cb = lax.broadcasted_iota(jnp.int32, (GROUP_W, GROUP_W), 1) // HEAD_DIM
    same_head = rb == cb
    mask_bd = same_head.astype(F32)

    n_groups = GDN_HEADS // HEADS_PER_GROUP
    n_chunks = tt // CHUNK
    units = [(c, gi) for c in range(n_chunks) for gi in range(n_groups)]
    ids = list(range(len(units)))
    tile = lambda a: [a[c * CHUNK:(c + 1) * CHUNK, gi * GROUP_W:(gi + 1) * GROUP_W] for c, gi in units]
    each = lambda f, *lists: [f(*args) for args in zip(*lists)]
    bdr = lambda t: _block_diag_rows(t, same_head)
    mm = lambda a, w16: _dot(a.astype(BF16), w16)
    stack = lambda a, b_: jnp.concatenate([a, b_], axis=0)

    def chunk_parallel_part():
        kc, qc, vc, bx, gx = tile(k), tile(q), tile(v), tile(beta_x), tile(g_x)
        eg_c = each(jnp.exp, gx)
        glast = each(lambda g_: g_[CHUNK - 1:CHUNK, :], gx)
        kb = each(jnp.multiply, kc, bx)
        vb = each(jnp.multiply, vc, bx)
        wr = each(jnp.multiply, kb, eg_c)
        qd = each(jnp.multiply, qc, eg_c)
        for i in ids:
            kd_buf[wslot, i] = (kc[i] * jnp.exp(glast[i] - gx[i])).astype(BF16)
            dec_buf[wslot, i] = jnp.broadcast_to(jnp.exp(glast[i]), (8, GROUP_W))
        dm = each(lambda g_: jnp.exp(jnp.where(causal, g_ - jnp.sum(g_ * eye_x, axis=0, keepdims=True), NEG_BIG)),
                  gx)
        yield
        aq = each(lambda i, a, b_, k_: _dot_nt(stack(a, b_).astype(BF16), bdr(k_)), ids, kb, qc, kc)
        xm = each(lambda a, d_: -(a[0:CHUNK] * d_ * strict_x), aq, dm)
        for i in ids:
            qk_buf[wslot, i] = (aq[i][CHUNK:] * dm[i]).astype(BF16)
        xd = each(lambda x_: x_ * base_x, xm)
        tinv = each(lambda x_: eye_x + x_, xd)
        yield
        pw = each(lambda i, x_: mm(x_, bdr(x_)), ids, xd)
        yield
        for _ in range(INV_BASE.bit_length() - 3):
            r = each(lambda i, t_, p_: mm(stack(t_, p_), bdr(p_)), ids, tinv, pw)
            tinv = each(lambda t_, r_: t_ + r_[0:CHUNK], tinv, r)
            pw = each(lambda r_: r_[CHUNK:], r)
            yield
        tinv = each(lambda i, t_, p_: t_ + mm(t_, bdr(p_)), ids, tinv, pw)
        yield
        for lm in level_x:
            e = each(lambda i, x_, t_: mm(x_ * lm, bdr(t_)), ids, xm, tinv)
            yield
            tinv = each(lambda i, t_, e_: t_ + mm(t_, bdr(e_)), ids, tinv, e)
            yield
        for i in ids:
            uw = mm(tinv[i], jnp.concatenate([bdr(vb[i]), bdr(wr[i])], axis=1))
            u_buf[wslot, i] = uw[:, 0:GROUP_W]
            wq_buf[wslot, i, 0:CHUNK] = uw[:, GROUP_W:].astype(BF16)
            wq_buf[wslot, i, CHUNK:] = qd[i].astype(BF16)

    def sequential_part():
        s_state = [jnp.where(starts_sequence, 0.0, s_ref[gi]) for gi in range(n_groups)]
        for c in range(n_chunks):
            cids = [c * n_groups + gi for gi in range(n_groups)]
            r2 = [_dot(wq_buf[rslot, i], s_state[gi].astype(BF16)) for gi, i in enumerate(cids)]
            yield
            vn = [u_buf[rslot, i] - r2[gi][0:CHUNK] for gi, i in enumerate(cids)]
            o = [r2[gi][CHUNK:] + _dot(qk_buf[rslot, i], bdr(vn[gi])) for gi, i in enumerate(cids)]
            s_state = [s_state[gi] * dec_buf[rslot, i][0:1] + mask_bd * _dot_tn(kd_buf[rslot, i], vn[gi].astype(BF16))
                       for gi, i in enumerate(cids)]
            for gi in range(n_groups):
                obuf[c * CHUNK:(c + 1) * CHUNK, gi * GROUP_W:(gi + 1) * GROUP_W] = o[gi]
            yield
        for gi in range(n_groups):
            s_ref[gi] = s_state[gi]

    parts = [chunk_parallel_part(), sequential_part()]
    while parts:
        parts = [p for p in parts if next(p, _DONE) is not _DONE]

    o = obuf[...]
    ms = _dot((o * o).astype(BF16), bd) * (1.0 / HEAD_DIM)
    zz = z_ref[0]
    o_ref[0] = (o * lax.rsqrt(ms + EPS) * gnw_ref[...] * (zz * _sigmoid(zz))).astype(BF16)


def _gdn_consts():
    h = np.arange(GDN_WIDTH) // HEAD_DIM
    eg = np.zeros((GATE_PAD, GDN_WIDTH), np.float32)
    eb = np.zeros((GATE_PAD, GDN_WIDTH), np.float32)
    eg[h, np.arange(GDN_WIDTH)] = 1.0
    eb[GDN_HEADS + h, np.arange(GDN_WIDTH)] = 1.0
    t = np.arange(GDN_TT)
    ltri = ((t[:, None] // CHUNK == t[None, :] // CHUNK) & (t[:, None] >= t[None, :])).astype(np.float32)
    return (_head_sum_matrix(GDN_WIDTH), jnp.asarray(eg, BF16), jnp.asarray(eb, BF16), jnp.asarray(ltri, BF16))


BF16_ROWS = 16
CAST_STEPS = 2 * GDN_HEADS


def _swa_head_source(i):
    j = i - GDN_HEADS
    return jnp.where(i < GDN_HEADS, i, GDN_HEADS + (j % 2) * SWA_GROUP + j // 2)


def _cast_specs(weight, unit, steps, source=None):
    rows, cols = weight.shape
    block_rows = next(r for r in range(unit, rows + 1, unit) if rows % r == 0 and rows // r <= steps)
    if source is None:
        source = lambda i: i
    else:
        assert block_rows == unit, "row regrouping needs one unit per block"
    last = rows // block_rows - 1
    src = pl.BlockSpec((block_rows, cols), lambda n: (source(jnp.minimum(n, last)), 0))
    dst = pl.BlockSpec((block_rows, cols), lambda n: (jnp.minimum(n, last), 0))
    return src, dst, jax.ShapeDtypeStruct((rows, cols), BF16)


def _gdn_call(qkv, z, ab, alog_pad, dtb_pad, gnw_x, w_gate, w_up, w_down, w_out):
    b, t, _ = qkv.shape
    tt = GDN_TT
    bd, eg, eb, ltri = _gdn_consts()
    n_tiles = t // tt
    n_groups = GDN_HEADS // HEADS_PER_GROUP
    total = b * n_tiles

    def tile_block(width, lag):
        def index(n):
            m = jnp.clip(n - lag, 0, total - 1)
            return (m // n_tiles, m % n_tiles, 0)
        return pl.BlockSpec((1, tt, width), index)

    ahead = lambda width: tile_block(width, 0)
    behind = lambda width: tile_block(width, 1)
    const = lambda shape: pl.BlockSpec(shape, lambda n: (0,) * len(shape))
    per_unit = lambda rows, dtype: pltpu.VMEM((2, GDN_UNITS, rows, GROUP_W), dtype)
    steps = total + 1
    assert steps >= CAST_STEPS
    casts = [
        _cast_specs(w_gate, BF16_ROWS, CAST_STEPS),
        _cast_specs(w_up, BF16_ROWS, CAST_STEPS),
        _cast_specs(w_down, BF16_ROWS, CAST_STEPS),
        _cast_specs(w_out, HEAD_DIM, CAST_STEPS, _swa_head_source),
    ]
    return pl.pallas_call(
        functools.partial(_gdn_kernel, tiles_per_seq=n_tiles),
        grid=(steps,),
        in_specs=[
            ahead(QKV_W), behind(GDN_WIDTH), ahead(GATE_PAD),
            const((1, GATE_PAD)), const((1, GATE_PAD)), const((1, GDN_WIDTH)),
            const((GDN_WIDTH, GDN_WIDTH)), const((GATE_PAD, GDN_WIDTH)), const((GATE_PAD, GDN_WIDTH)),
            const((tt, tt)),
        ] + [c[0] for c in casts],
        out_specs=[behind(GDN_WIDTH)] + [c[1] for c in casts],
        out_shape=[jax.ShapeDtypeStruct((b, t, GDN_WIDTH), BF16)] + [c[2] for c in casts],
        scratch_shapes=[
            pltpu.VMEM((n_groups, GROUP_W, GROUP_W), F32),
            pltpu.VMEM((tt, GDN_WIDTH), F32),
            per_unit(2 * CHUNK, BF16), per_unit(CHUNK, F32), per_unit(CHUNK, BF16), per_unit(CHUNK, BF16),
            per_unit(8, F32),
        ],
        compiler_params=pltpu.CompilerParams(
            dimension_semantics=("arbitrary",), vmem_limit_bytes=VMEM_LIMIT),
        name="gdn",
    )(qkv, z, ab, alog_pad, dtb_pad, gnw_x, bd, eg, eb, ltri, w_gate, w_up, w_down, w_out)


SWA_PAIRS = SWA_Q_HEADS // 2
SWA_ROWS = SWA_Q_HEADS * WINDOW
LOG2E = 1.4426950408889634
SWA_INTERLEAVE = 8


def _swa_kernel(q_ref, kvc_ref, kvp_ref, qw_ref, kw_ref, bdq_ref, bdk_ref, bias_ref, sink_ref, o_ref):
    tq = SWA_TQ
    ti = pl.program_id(1)
    q = q_ref[0]
    q = q * lax.rsqrt(_dot((q * q).astype(BF16), bdq_ref[...]) * (1.0 / HEAD_DIM) + EPS)
    q = q * (qw_ref[...] * (HEAD_DIM ** -0.5 * LOG2E))
    kv = jnp.concatenate([kvp_ref[0], kvc_ref[0]], axis=0)
    k = kv[:, 0:SWA_KV_WIDTH]
    v = kv[:, SWA_KV_WIDTH:]
    k = k * lax.rsqrt(_dot((k * k).astype(BF16), bdk_ref[...]) * (1.0 / HEAD_DIM) + EPS) * kw_ref[...]
    k16 = k.astype(BF16)
    lo = lax.broadcasted_iota(jnp.int32, (1, LANES), 1) < HEAD_DIM
    qcol = lax.broadcasted_iota(jnp.int32, (WINDOW, SWA_ROWS), 1) % WINDOW
    from_prev = lax.broadcasted_iota(jnp.int32, (WINDOW, SWA_ROWS), 0) > qcol
    first = jnp.where(ti == 0, 1, 0)
    sink = sink_ref[...]
    vt16 = v.T.astype(BF16)
    zero = jnp.zeros((), BF16)
    def block(j):
        qj = q[j * WINDOW:(j + 1) * WINDOW]
        parts = []
        for p in range(SWA_PAIRS):
            qp = qj[:, p * LANES:(p + 1) * LANES]
            parts += [jnp.where(lo, qp, 0.0), jnp.where(lo, 0.0, qp)]
        qs = jnp.concatenate(parts, axis=0).astype(BF16)
        st = _dot_nt(k16[j * WINDOW:(j + 2) * WINDOW], qs)
        yield
        bias = bias_ref[first] if j == 0 else bias_ref[0]
        sm = jnp.where(from_prev, st[0:WINDOW], st[WINDOW:]) + bias
        m = jnp.max(sm, axis=0, keepdims=True)
        yield
        pe = jnp.exp2(sm - m)
        den = jnp.sum(pe, axis=0, keepdims=True) + jnp.exp2(sink - m)
        pb = pe.astype(BF16)
        pt2 = jnp.concatenate([jnp.where(from_prev, pb, zero), jnp.where(from_prev, zero, pb)], axis=0)
        yield
        ot = _dot(vt16[:, j * WINDOW:(j + 2) * WINDOW], pt2) * (1.0 / den)
        yield
        for p in range(SWA_PAIRS):
            c0 = p * 2 * WINDOW
            pair_t = jnp.concatenate([ot[0:HEAD_DIM, c0:c0 + WINDOW],
                                      ot[HEAD_DIM:, c0 + WINDOW:c0 + 2 * WINDOW]], axis=0)
            o_ref[0, j * WINDOW:(j + 1) * WINDOW, p * LANES:(p + 1) * LANES] = pair_t.T.astype(BF16)

    n_blocks = tq // WINDOW
    for j0 in range(0, n_blocks, SWA_INTERLEAVE):
        parts = [block(j) for j in range(j0, min(j0 + SWA_INTERLEAVE, n_blocks))]
        while parts:
            parts = [p for p in parts if next(p, _DONE) is not _DONE]


def _swa_consts():
    qi = np.arange(WINDOW)[:, None]
    kj = np.arange(WINDOW)[None, :]
    from_prev = kj > qi
    dist = np.where(from_prev, qi + WINDOW - kj, qi - kj).astype(np.float32)
    slopes = 2.0 ** (-8.0 * (np.arange(SWA_Q_HEADS, dtype=np.float32) + 1.0) / SWA_Q_HEADS)
    bias = np.zeros((2, SWA_PAIRS, 2, WINDOW, WINDOW), np.float32)
    for first in range(2):
        for p in range(SWA_PAIRS):
            for half, head in enumerate((p, SWA_GROUP + p)):
                b = (-slopes[head].astype(np.float32) * dist) * np.float32(LOG2E)
                bias[first, p, half] = np.where(from_prev & bool(first), np.float32(NEG_BIG), b)
    bias = bias.reshape(2, SWA_ROWS, WINDOW).transpose(0, 2, 1)
    h = np.arange(SWA_WIDTH) // HEAD_DIM
    bdq = (h[:, None] == h[None, :]).astype(np.float32)
    hk = np.arange(SWA_KV_WIDTH) // HEAD_DIM
    bdk = (hk[:, None] == hk[None, :]).astype(np.float32)
    return jnp.asarray(bias), jnp.asarray(bdq, BF16), jnp.asarray(bdk, BF16)


def _swa_call(sq, skv, qw_x, kw_x, sink_col):
    b, t, _ = sq.shape
    tq = SWA_TQ
    nb = tq // WINDOW
    bias, bdq, bdk = _swa_consts()
    const = lambda shape: pl.BlockSpec(shape, lambda bi, ti: (0,) * len(shape))
    return pl.pallas_call(
        _swa_kernel,
        grid=(b, t // tq),
        in_specs=[
            pl.BlockSpec((1, tq, SWA_WIDTH), lambda bi, ti: (bi, ti, 0)),
            pl.BlockSpec((1, tq, 2 * SWA_KV_WIDTH), lambda bi, ti: (bi, ti, 0)),
            pl.BlockSpec((1, WINDOW, 2 * SWA_KV_WIDTH), lambda bi, ti: (bi, jnp.maximum(ti * nb - 1, 0), 0)),
            const((1, SWA_WIDTH)), const((1, SWA_KV_WIDTH)),
            const((SWA_WIDTH, SWA_WIDTH)), const((SWA_KV_WIDTH, SWA_KV_WIDTH)),
            const((2, WINDOW, SWA_ROWS)),
            const((1, SWA_ROWS)),
        ],
        out_specs=pl.BlockSpec((1, tq, SWA_WIDTH), lambda bi, ti: (bi, ti, 0)),
        out_shape=jax.ShapeDtypeStruct((b, t, SWA_WIDTH), BF16),
        compiler_params=pltpu.CompilerParams(
            dimension_semantics=("arbitrary", "arbitrary"), vmem_limit_bytes=VMEM_LIMIT),
        name="swa",
    )(sq, skv, skv, qw_x, kw_x, bdq, bdk, bias, sink_col)


def _ffn_kernel(x_ref, og_ref, os_ref, mod_ref, nw_ref, wo_ref, wg_ref, wu_ref, wd_ref, o_ref):
    x = x_ref[0]
    mod = mod_ref[0]
    gate1 = mod[:, 2 * D_MODEL:3 * D_MODEL]
    shift2 = mod[:, 3 * D_MODEL:4 * D_MODEL]
    scale2 = mod[:, 4 * D_MODEL:5 * D_MODEL]
    gate2 = mod[:, 5 * D_MODEL:]
    mixed = _dot(og_ref[0], wo_ref[0:GDN_WIDTH, :]) + _dot(os_ref[0], wo_ref[GDN_WIDTH:, :])
    x1 = x + gate1 * mixed
    ms = jnp.mean(x1 * x1, axis=-1, keepdims=True)
    gain = nw_ref[...] * (1.0 + scale2)
    hb = (x1 * lax.rsqrt(ms + EPS) * gain + shift2).astype(BF16)
    gt = _dot(hb, wg_ref[...])
    up = _dot(hb, wu_ref[...])
    act = ((gt * _sigmoid(gt)) * up).astype(BF16)
    o_ref[0] = x1 + gate2 * _dot(act, wd_ref[...])


def _ffn_call(x, og, osw, mod3, norm_w, wo, wg, wu, wd):
    b, t, _ = x.shape
    tm = FFN_TM
    row = lambda width: pl.BlockSpec((1, tm, width), lambda bi, ti: (bi, ti, 0))
    const = lambda shape: pl.BlockSpec(shape, lambda bi, ti: (0,) * len(shape), pipeline_mode=pl.Buffered(1))
    return pl.pallas_call(
        _ffn_kernel,
        grid=(b, t // tm),
        in_specs=[
            row(D_MODEL), row(GDN_WIDTH), row(SWA_WIDTH),
            pl.BlockSpec((1, 1, 6 * D_MODEL), lambda bi, ti: (bi, 0, 0)),
            const((1, D_MODEL)),
            const((GDN_WIDTH + SWA_WIDTH, D_MODEL)),
            const((D_MODEL, D_FF)), const((D_MODEL, D_FF)), const((D_FF, D_MODEL)),
        ],
        out_specs=row(D_MODEL),
        out_shape=jax.ShapeDtypeStruct((b, t, D_MODEL), F32),
        compiler_params=pltpu.CompilerParams(
            dimension_semantics=("arbitrary", "arbitrary"), vmem_limit_bytes=VMEM_LIMIT),
        name="ffn",
    )(x, og, osw, mod3, norm_w, wo, wg, wu, wd)


def _layer(x, c_pad, w_ada, b_ada, norm1_w, w_in, conv_w, a_log, dt_bias, gdn_norm_w, q_norm_w, k_norm_w, sinks,
           w_out, norm2_w, w_gate, w_up, w_down):
    b = x.shape[0]
    mod, w_aligned, w_regrouped = _ada_call(c_pad, w_ada, b_ada.reshape(1, -1), w_in.T)
    mod3 = mod[:b].reshape(b, 1, 6 * D_MODEL)
    qkv, z, sq, skv, ab = _proj_call(x, mod3, norm1_w.reshape(1, D_MODEL), w_aligned, w_regrouped,
                                     conv_w.reshape(CONV_WIDTH, QKV_W))

    alog_pad = jnp.pad(a_log.reshape(1, GDN_HEADS), ((0, 0), (0, GATE_PAD - GDN_HEADS)))
    dtb_pad = jnp.pad(dt_bias.reshape(1, GDN_HEADS), ((0, 0), (0, GATE_PAD - GDN_HEADS)))
    gnw_x = jnp.tile(gdn_norm_w.reshape(1, HEAD_DIM), (1, GDN_HEADS))
    og, wg16, wu16, wd16, wo16 = _gdn_call(qkv, z, ab, alog_pad, dtb_pad, gnw_x, w_gate, w_up, w_down, w_out)

    qw_x = jnp.tile(q_norm_w.reshape(1, HEAD_DIM), (1, SWA_Q_HEADS))
    kw_x = jnp.tile(k_norm_w.reshape(1, HEAD_DIM), (1, SWA_KV_HEADS))
    sink_pairs = jnp.stack([sinks[:SWA_GROUP], sinks[SWA_GROUP:]], axis=1)
    sink_col = (jnp.repeat(sink_pairs, WINDOW, axis=1) * LOG2E).reshape(1, SWA_ROWS)
    osw = _swa_call(sq, skv, qw_x, kw_x, sink_col)

    return _ffn_call(x, og, osw, mod3, norm2_w.reshape(1, D_MODEL), wo16, wg16, wu16, wd16)


def kernel(x, c, w_ada, b_ada, norm1_w, w_in, conv_w, a_log, dt_bias, gdn_norm_w, q_norm_w, k_norm_w, sinks,
           w_out, norm2_w, w_gate, w_up, w_down):
    depth = w_ada.shape[0]
    b = c.shape[0]
    c_pad = jnp.pad(c, ((0, 8 - b), (0, 0)))
    for l in range(depth):
        x = _layer(x, c_pad, w_ada[l], b_ada[l], norm1_w[l], w_in[l], conv_w[l], a_log[l], dt_bias[l],
                   gdn_norm_w[l], q_norm_w[l], k_norm_w[l], sinks[l], w_out[l], norm2_w[l], w_gate[l], w_up[l],
                   w_down[l])
    return x
```

```python
import functools

import numpy as np
import jax
import jax.numpy as jnp
from jax import lax
from jax.experimental import pallas as pl
from jax.experimental.pallas import tpu as pltpu

F32 = jnp.float32
BF16 = jnp.bfloat16

D_MODEL = 1024
HEAD_DIM = 64
GDN_HEADS = 8
GDN_WIDTH = GDN_HEADS * HEAD_DIM
SWA_Q_HEADS = 8
SWA_KV_HEADS = 2
SWA_GROUP = SWA_Q_HEADS // SWA_KV_HEADS
SWA_WIDTH = SWA_Q_HEADS * HEAD_DIM
SWA_KV_WIDTH = SWA_KV_HEADS * HEAD_DIM
WINDOW = 128
CONV_WIDTH = 4
CHUNK = 64
D_FF = 2816
EPS = 1e-6
LANES = 128
GATE_PAD = LANES
HEADS_PER_GROUP = 2
GROUP_W = HEADS_PER_GROUP * HEAD_DIM
INV_BASE = 16
INV_LEVELS = (16, 32)
NEG_BIG = -1e30
VMEM_LIMIT = 56 * 1024 * 1024

PROJ_TM = 512
GDN_TT = 256
SWA_TQ = 1024
FFN_TM = 512
ADA_TN = 1536


def _sigmoid(x):
    return 1.0 / (1.0 + jnp.exp(-x))


def _dot(a, b):
    return jnp.dot(a, b, preferred_element_type=F32)


def _dot_nt(a, b):
    return lax.dot_general(a, b, (((1,), (1,)), ((), ())), preferred_element_type=F32)


def _dot_tn(a, b):
    return lax.dot_general(a, b, (((0,), (0,)), ((), ())), preferred_element_type=F32)


def _split2(x):
    hi = x.astype(BF16)
    lo = (x - hi.astype(F32)).astype(BF16)
    return hi, lo


def _ada_kernel(c_ref, w_ref, b_ref, win_ref, o_ref, wa16_ref, wb16_ref):
    c = c_ref[...]
    ca = c * _sigmoid(c)
    o_ref[...] = _dot(ca.astype(BF16), w_ref[...].astype(BF16)) + b_ref[...]

    @pl.when(pl.program_id(0) == 0)
    def _():
        wa16_ref[...] = win_ref[0:PROJ_ALIGNED, :].T.astype(BF16)
        o1 = PROJ_ALIGNED + 2 * GDN_HEADS
        o2 = o1 + SWA_WIDTH
        head = lambda h: win_ref[o1 + h * HEAD_DIM:o1 + (h + 1) * HEAD_DIM, :]
        pieces = [head(half * SWA_GROUP + p) for p in range(SWA_PAIRS) for half in range(2)]
        pieces.append(win_ref[o2:o2 + 2 * SWA_KV_WIDTH, :])
        pieces.append(win_ref[PROJ_ALIGNED:o1, :])
        pieces.append(jnp.zeros((GATE_PAD - 2 * GDN_HEADS, D_MODEL), F32))
        wb16_ref[...] = jnp.concatenate(pieces, axis=0).T.astype(BF16)


def _ada_call(c_pad, w_ada, b_ada, w_in):
    n = w_ada.shape[1]
    whole = lambda shape: pl.BlockSpec(shape, lambda j: (0, 0))
    return pl.pallas_call(
        _ada_kernel,
        grid=(n // ADA_TN,),
        in_specs=[
            whole((8, D_MODEL)),
            pl.BlockSpec((D_MODEL, ADA_TN), lambda j: (0, j)),
            pl.BlockSpec((1, ADA_TN), lambda j: (0, j)),
            pl.BlockSpec(w_in.shape, lambda j: (0, 0), pipeline_mode=pl.Buffered(1)),
        ],
        out_specs=[pl.BlockSpec((8, ADA_TN), lambda j: (0, j)),
                   whole((D_MODEL, PROJ_ALIGNED)), whole((D_MODEL, PROJ_REGROUPED))],
        out_shape=[jax.ShapeDtypeStruct((8, n), F32),
                   jax.ShapeDtypeStruct((D_MODEL, PROJ_ALIGNED), BF16),
                   jax.ShapeDtypeStruct((D_MODEL, PROJ_REGROUPED), BF16)],
        compiler_params=pltpu.CompilerParams(
            dimension_semantics=("arbitrary",), vmem_limit_bytes=VMEM_LIMIT),
        name="ada",
    )(c_pad, w_ada, b_ada, w_in)


QKV_W = 3 * GDN_WIDTH
PROJ_SPLITS = (QKV_W, GDN_WIDTH, SWA_WIDTH, 2 * SWA_KV_WIDTH, GATE_PAD)
PROJ_ALIGNED = QKV_W + GDN_WIDTH
PROJ_REGROUPED = sum(PROJ_SPLITS) - PROJ_ALIGNED


def _proj_kernel(x_ref, mod_ref, nw_ref, wa_ref, wb_ref, cw_ref, bd_ref, qkv_ref, z_ref, sq_ref, skv_ref, ab_ref,
                 qbuf, kbuf, vbuf):
    tm = PROJ_TM
    bufs = (qbuf, kbuf, vbuf)

    @pl.when(pl.program_id(1) == 0)
    def _():
        for buf in bufs:
            buf[0:8, :] = jnp.zeros((8, GDN_WIDTH), F32)

    x = x_ref[0]
    ms = jnp.mean(x * x, axis=-1, keepdims=True)
    mod = mod_ref[0]
    shift = mod[:, 0:D_MODEL]
    gain = nw_ref[...] * (1.0 + mod[:, D_MODEL:2 * D_MODEL])
    hb = (x * lax.rsqrt(ms + EPS) * gain + shift).astype(BF16)

    def project(s):
        bufs[s][8:8 + tm, :] = _dot(hb, wa_ref[:, s * GDN_WIDTH:(s + 1) * GDN_WIDTH])

    def conv_silu(s):
        buf = bufs[s]
        cw = lambda j: cw_ref[j:j + 1, s * GDN_WIDTH:(s + 1) * GDN_WIDTH]
        acc = cw(CONV_WIDTH - 1) * buf[8:8 + tm, :]
        for j in range(CONV_WIDTH - 1):
            off = 8 - (CONV_WIDTH - 1) + j
            acc = acc + cw(j) * buf[off:off + tm, :]
        buf[0:8, :] = buf[tm:tm + 8, :]
        return acc * _sigmoid(acc)

    def l2_normed(a, scale):
        return a * lax.rsqrt(_dot((a * a).astype(BF16), bd_ref[...]) + EPS) * scale

    def rest(ref, w_ref, c0):
        ref[0] = _dot(hb, w_ref[:, c0:c0 + ref.shape[-1]])

    project(0)
    rest(z_ref, wa_ref, QKV_W)
    project(1)
    qkv_ref[0, :, 0:GDN_WIDTH] = l2_normed(conv_silu(0), HEAD_DIM ** -0.5)
    rest(sq_ref, wb_ref, 0)
    project(2)
    rest(skv_ref, wb_ref, SWA_WIDTH)
    qkv_ref[0, :, GDN_WIDTH:2 * GDN_WIDTH] = l2_normed(conv_silu(1), 1.0)
    rest(ab_ref, wb_ref, SWA_WIDTH + 2 * SWA_KV_WIDTH)
    qkv_ref[0, :, 2 * GDN_WIDTH:] = conv_silu(2)


def _head_sum_matrix(width):
    h = np.arange(width) // HEAD_DIM
    return jnp.asarray((h[:, None] == h[None, :]).astype(np.float32), BF16)


def _proj_call(x, mod3, norm_w, w_aligned, w_regrouped, conv_w):
    b, t, _ = x.shape
    tm = PROJ_TM
    row = lambda width: pl.BlockSpec((1, tm, width), lambda bi, ti: (bi, ti, 0))
    const = lambda shape: pl.BlockSpec(shape, lambda bi, ti: (0,) * len(shape))
    return pl.pallas_call(
        _proj_kernel,
        grid=(b, t // tm),
        in_specs=[
            row(D_MODEL),
            pl.BlockSpec((1, 1, 6 * D_MODEL), lambda bi, ti: (bi, 0, 0)),
            const((1, D_MODEL)),
            const((D_MODEL, PROJ_ALIGNED)),
            const((D_MODEL, PROJ_REGROUPED)),
            const((CONV_WIDTH, QKV_W)),
            const((GDN_WIDTH, GDN_WIDTH)),
        ],
        out_specs=[row(w) for w in PROJ_SPLITS],
        out_shape=[jax.ShapeDtypeStruct((b, t, w), F32) for w in PROJ_SPLITS],
        scratch_shapes=[pltpu.VMEM((8 + tm, GDN_WIDTH), F32)] * 3,
        compiler_params=pltpu.CompilerParams(
            dimension_semantics=("arbitrary", "arbitrary"), vmem_limit_bytes=VMEM_LIMIT),
        name="proj",
    )(x, mod3, norm_w, w_aligned, w_regrouped, conv_w, _head_sum_matrix(GDN_WIDTH))


GDN_UNITS = (GDN_TT // CHUNK) * (GDN_HEADS // HEADS_PER_GROUP)


def _block_diag_rows(p, same_head):
    stacked = jnp.concatenate([p.astype(BF16)] * HEADS_PER_GROUP, axis=0)
    return jnp.where(same_head, stacked, jnp.zeros((), BF16))


_DONE = object()


def _gdn_kernel(qkv_ref, z_ref, ab_ref, alog_ref, dtb_ref, gnw_ref, bd_ref, eg_ref, eb_ref, ltri_ref,
                wg_ref, wu_ref, wd_ref, wo_ref, o_ref, wg16_ref, wu16_ref, wd16_ref, wo16_ref,
                s_ref, obuf, vw_buf, wq_buf, u_buf, qk_buf, kd_buf, dec_buf, *, tiles_per_seq):
    tt = GDN_TT
    step = pl.program_id(0)
    @pl.when(step < CAST_STEPS)
    def _():
        for src, dst in ((wg_ref, wg16_ref), (wu_ref, wu16_ref), (wd_ref, wd16_ref), (wo_ref, wo16_ref)):
            dst[...] = src[...].astype(BF16)

    wslot = step % 2
    rslot = 1 - wslot
    starts_sequence = (step - 1) % tiles_per_seq == 0

    @pl.when(step == 0)
    def _():
        for buf in (s_ref, wq_buf, u_buf, qk_buf, kd_buf, dec_buf):
            buf[...] = jnp.zeros_like(buf)

    bd = bd_ref[...]
    q = qkv_ref[0, :, 0:GDN_WIDTH]
    k = qkv_ref[0, :, GDN_WIDTH:2 * GDN_WIDTH]
    v = qkv_ref[0, :, 2 * GDN_WIDTH:]

    ab = ab_ref[0]
    lane = lax.broadcasted_iota(jnp.int32, (1, GATE_PAD), 1)
    xs = ab + dtb_ref[...]
    softplus = jnp.maximum(xs, 0.0) + jnp.log1p(jnp.exp(-jnp.abs(xs)))
    g = jnp.where(lane < GDN_HEADS, -jnp.exp(alog_ref[...]) * softplus, 0.0)
    beta = _sigmoid(ab)
    ltri = ltri_ref[...]
    g1, g2 = _split2(g)
    gcum = _dot(ltri, g1) + _dot(ltri, g2)
    eg = eg_ref[...]
    c1, c2 = _split2(gcum)
    g_x = _dot(c1, eg) + _dot(c2, eg)
    beta_x = _dot(beta.astype(BF16), eb_ref[...])

    r64 = lax.broadcasted_iota(jnp.int32, (CHUNK, GROUP_W), 0)
    l64 = lax.broadcasted_iota(jnp.int32, (CHUNK, GROUP_W), 1) % CHUNK
    causal = r64 >= l64
    eye_x = (r64 == l64).astype(F32)
    strict_x = (r64 > l64).astype(F32)
    same_block = lambda size: (r64 // size) == (l64 // size)
    base_x = same_block(INV_BASE).astype(F32)
    level_x = [(same_block(2 * size) & ~same_block(size)).astype(F32) for size in INV_LEVELS]
    rb = lax.broadcasted_iota(jnp.int32, (GROUP_W, GROUP_W), 0) // HEAD_DIM
    cb = lax.broadcasted_iota(jnp.int32, (GROUP_W, GROUP_W), 1) // HEAD_DIM
    same_head = rb == cb
    mask_bd = same_head.astype(F32)

    n_groups = GDN_HEADS // HEADS_PER_GROUP
    n_chunks = tt // CHUNK
    units = [(c, gi) for c in range(n_chunks) for gi in range(n_groups)]
    ids = list(range(len(units)))
    tile = lambda a: [a[c * CHUNK:(c + 1) * CHUNK, gi * GROUP_W:(gi + 1) * GROUP_W] for c, gi in units]
    each = lambda f, *lists: [f(*args) for args in zip(*lists)]
    bdr = lambda t: _block_diag_rows(t, same_head)
    mm = lambda a, w16: _dot(a.astype(BF16), w16)
    stack = lambda a, b_: jnp.concatenate([a, b_], axis=0)

    def chunk_parallel_part():
        kc, qc, vc, bx, gx = tile(k), tile(q), tile(v), tile(beta_x), tile(g_x)
        eg_c = each(jnp.exp, gx)
        glast = each(lambda g_: g_[CHUNK - 1:CHUNK, :], gx)
        kb = each(jnp.multiply, kc, bx)
        for i in ids:
            vw_buf[i, :, 0:GROUP_W] = (vc[i] * bx[i]).astype(BF16)
            vw_buf[i, :, GROUP_W:] = (kb[i] * eg_c[i]).astype(BF16)
            wq_buf[wslot, i, CHUNK:] = (qc[i] * eg_c[i]).astype(BF16)
            kd_buf[wslot, i] = (kc[i] * jnp.exp(glast[i] - gx[i])).astype(BF16)
            dec_buf[wslot, i] = jnp.broadcast_to(jnp.exp(glast[i]), (8, GROUP_W))
        dm = each(lambda g_: jnp.exp(jnp.where(causal, g_ - jnp.sum(g_ * eye_x, axis=0, keepdims=True), NEG_BIG)),
                  gx)
        yield
        aq = each(lambda i, a, b_, k_: _dot_nt(stack(a, b_).astype(BF16), bdr(k_)), ids, kb, qc, kc)
        xm = each(lambda a, d_: -(a[0:CHUNK] * d_ * strict_x), aq, dm)
        for i in ids:
            qk_buf[wslot, i] = (aq[i][CHUNK:] * dm[i]).astype(BF16)
        xd = each(lambda x_: x_ * base_x, xm)
        tinv = each(lambda x_: eye_x + x_, xd)
        yield
        pw = each(lambda i, x_: mm(x_, bdr(x_)), ids, xd)
        yield
        for _ in range(INV_BASE.bit_length() - 3):
            r = each(lambda i, t_, p_: mm(stack(t_, p_), bdr(p_)), ids, tinv, pw)
            tinv = each(lambda t_, r_: t_ + r_[0:CHUNK], tinv, r)
            pw = each(lambda r_: r_[CHUNK:], r)
            yield
        tinv = each(lambda i, t_, p_: t_ + mm(t_, bdr(p_)), ids, tinv, pw)
        yield
        for lm in level_x:
            e = each(lambda i, x_, t_: mm(x_ * lm, bdr(t_)), ids, xm, tinv)
            yield
            tinv = each(lambda i, t_, e_: t_ + mm(t_, bdr(e_)), ids, tinv, e)
            yield
        for i in ids:
            operands = jnp.concatenate([bdr(vw_buf[i, :, 0:GROUP_W]), bdr(vw_buf[i, :, GROUP_W:])], axis=1)
            uw = mm(tinv[i], operands)
            u_buf[wslot, i] = uw[:, 0:GROUP_W]
            wq_buf[wslot, i, 0:CHUNK] = uw[:, GROUP_W:].astype(BF16)

    def sequential_part():
        s_state = [jnp.where(starts_sequence, 0.0, s_ref[gi]) for gi in range(n_groups)]
        for c in range(n_chunks):
            cids = [c * n_groups + gi for gi in range(n_groups)]
            r2 = [_dot(wq_buf[rslot, i], s_state[gi].astype(BF16)) for gi, i in enumerate(cids)]
            yield
            vn = [u_buf[rslot, i] - r2[gi][0:CHUNK] for gi, i in enumerate(cids)]
            o = [r2[gi][CHUNK:] + _dot(qk_buf[rslot, i], bdr(vn[gi])) for gi, i in enumerate(cids)]
            s_state = [s_state[gi] * dec_buf[rslot, i][0:1] + mask_bd * _dot_tn(kd_buf[rslot, i], vn[gi].astype(BF16))
                       for gi, i in enumerate(cids)]
            for gi in range(n_groups):
                obuf[c * CHUNK:(c + 1) * CHUNK, gi * GROUP_W:(gi + 1) * GROUP_W] = o[gi]
            yield
        for gi in range(n_groups):
            s_ref[gi] = s_state[gi]

    parts = [chunk_parallel_part(), sequential_part()]
    while parts:
        parts = [p for p in parts if next(p, _DONE) is not _DONE]

    o = obuf[...]
    ms = _dot((o * o).astype(BF16), bd) * (1.0 / HEAD_DIM)
    zz = z_ref[0]
    o_ref[0] = (o * lax.rsqrt(ms + EPS) * gnw_ref[...] * (zz * _sigmoid(zz))).astype(BF16)


def _gdn_consts():
    h = np.arange(GDN_WIDTH) // HEAD_DIM
    eg = np.zeros((GATE_PAD, GDN_WIDTH), np.float32)
    eb = np.zeros((GATE_PAD, GDN_WIDTH), np.float32)
    eg[h, np.arange(GDN_WIDTH)] = 1.0
    eb[GDN_HEADS + h, np.arange(GDN_WIDTH)] = 1.0
    t = np.arange(GDN_TT)
    ltri = ((t[:, None] // CHUNK == t[None, :] // CHUNK) & (t[:, None] >= t[None, :])).astype(np.float32)
    return (_head_sum_matrix(GDN_WIDTH), jnp.asarray(eg, BF16), jnp.asarray(eb, BF16), jnp.asarray(ltri, BF16))


BF16_ROWS = 16
CAST_STEPS = 2 * GDN_HEADS


def _swa_head_source(i):
    j = i - GDN_HEADS
    return jnp.where(i < GDN_HEADS, i, GDN_HEADS + (j % 2) * SWA_GROUP + j // 2)


def _cast_specs(weight, unit, steps, source=None):
    rows, cols = weight.shape
    block_rows = next(r for r in range(unit, rows + 1, unit) if rows % r == 0 and rows // r <= steps)
    if source is None:
        source = lambda i: i
    else:
        assert block_rows == unit, "row regrouping needs one unit per block"
    last = rows // block_rows - 1
    src = pl.BlockSpec((block_rows, cols), lambda n: (source(jnp.minimum(n, last)), 0))
    dst = pl.BlockSpec((block_rows, cols), lambda n: (jnp.minimum(n, last), 0))
    return src, dst, jax.ShapeDtypeStruct((rows, cols), BF16)


def _gdn_call(qkv, z, ab, alog_pad, dtb_pad, gnw_x, w_gate, w_up, w_down, w_out):
    b, t, _ = qkv.shape
    tt = GDN_TT
    bd, eg, eb, ltri = _gdn_consts()
    n_tiles = t // tt
    n_groups = GDN_HEADS // HEADS_PER_GROUP
    total = b * n_tiles

    def tile_block(width, lag):
        def index(n):
            m = jnp.clip(n - lag, 0, total - 1)
            return (m // n_tiles, m % n_tiles, 0)
        return pl.BlockSpec((1, tt, width), index)

    ahead = lambda width: tile_block(width, 0)
    behind = lambda width: tile_block(width, 1)
    const = lambda shape: pl.BlockSpec(shape, lambda n: (0,) * len(shape))
    per_unit = lambda rows, dtype: pltpu.VMEM((2, GDN_UNITS, rows, GROUP_W), dtype)
    steps = total + 1
    assert steps >= CAST_STEPS
    casts = [
        _cast_specs(w_gate, BF16_ROWS, CAST_STEPS),
        _cast_specs(w_up, BF16_ROWS, CAST_STEPS),
        _cast_specs(w_down, BF16_ROWS, CAST_STEPS),
        _cast_specs(w_out, HEAD_DIM, CAST_STEPS, _swa_head_source),
    ]
    return pl.pallas_call(
        functools.partial(_gdn_kernel, tiles_per_seq=n_tiles),
        grid=(steps,),
        in_specs=[
            ahead(QKV_W), behind(GDN_WIDTH), ahead(GATE_PAD),
            const((1, GATE_PAD)), const((1, GATE_PAD)), const((1, GDN_WIDTH)),
            const((GDN_WIDTH, GDN_WIDTH)), const((GATE_PAD, GDN_WIDTH)), const((GATE_PAD, GDN_WIDTH)),
            const((tt, tt)),
        ] + [c[0] for c in casts],
        out_specs=[behind(GDN_WIDTH)] + [c[1] for c in casts],
        out_shape=[jax.ShapeDtypeStruct((b, t, GDN_WIDTH), BF16)] + [c[2] for c in casts],
        scratch_shapes=[
            pltpu.VMEM((n_groups, GROUP_W, GROUP_W), F32),
            pltpu.VMEM((tt, GDN_WIDTH), F32),
            pltpu.VMEM((GDN_UNITS, CHUNK, 2 * GROUP_W), BF16),
            per_unit(2 * CHUNK, BF16), per_unit(CHUNK, F32), per_unit(CHUNK, BF16), per_unit(CHUNK, BF16),
            per_unit(8, F32),
        ],
        compiler_params=pltpu.CompilerParams(
            dimension_semantics=("arbitrary",), vmem_limit_bytes=VMEM_LIMIT),
        name="gdn",
    )(qkv, z, ab, alog_pad, dtb_pad, gnw_x, bd, eg, eb, ltri, w_gate, w_up, w_down, w_out)


SWA_PAIRS = SWA_Q_HEADS // 2
SWA_ROWS = SWA_Q_HEADS * WINDOW
LOG2E = 1.4426950408889634
SWA_INTERLEAVE = 8


def _swa_kernel(q_ref, kvc_ref, kvp_ref, qw_ref, kw_ref, bdq_ref, bdk_ref, bias_ref, sink_ref, o_ref):
    tq = SWA_TQ
    ti = pl.program_id(1)
    q = q_ref[0]
    q = q * lax.rsqrt(_dot((q * q).astype(BF16), bdq_ref[...]) * (1.0 / HEAD_DIM) + EPS)
    q = q * (qw_ref[...] * (HEAD_DIM ** -0.5 * LOG2E))
    kv = jnp.concatenate([kvp_ref[0], kvc_ref[0]], axis=0)
    k = kv[:, 0:SWA_KV_WIDTH]
    v = kv[:, SWA_KV_WIDTH:]
    k = k * lax.rsqrt(_dot((k * k).astype(BF16), bdk_ref[...]) * (1.0 / HEAD_DIM) + EPS) * kw_ref[...]
    k16 = k.astype(BF16)
    lo = lax.broadcasted_iota(jnp.int32, (1, LANES), 1) < HEAD_DIM
    qcol = lax.broadcasted_iota(jnp.int32, (WINDOW, SWA_ROWS), 1) % WINDOW
    from_prev = lax.broadcasted_iota(jnp.int32, (WINDOW, SWA_ROWS), 0) > qcol
    first = jnp.where(ti == 0, 1, 0)
    sink = sink_ref[...]
    vt16 = v.T.astype(BF16)
    zero = jnp.zeros((), BF16)
    def block(j):
        qj = q[j * WINDOW:(j + 1) * WINDOW]
        parts = []
        for p in range(SWA_PAIRS):
            qp = qj[:, p * LANES:(p + 1) * LANES]
            parts += [jnp.where(lo, qp, 0.0), jnp.where(lo, 0.0, qp)]
        qs = jnp.concatenate(parts, axis=0).astype(BF16)
        st = _dot_nt(k16[j * WINDOW:(j + 2) * WINDOW], qs)
        yield
        bias = bias_ref[first] if j == 0 else bias_ref[0]
        sm = jnp.where(from_prev, st[0:WINDOW], st[WINDOW:]) + bias
        m = jnp.max(sm, axis=0, keepdims=True)
        yield
        pe = jnp.exp2(sm - m)
        den = jnp.sum(pe, axis=0, keepdims=True) + jnp.exp2(sink - m)
        pb = pe.astype(BF16)
        pt2 = jnp.concatenate([jnp.where(from_prev, pb, zero), jnp.where(from_prev, zero, pb)], axis=0)
        yield
        ot = _dot(vt16[:, j * WINDOW:(j + 2) * WINDOW], pt2) * (1.0 / den)
        yield
        for p in range(SWA_PAIRS):
            c0 = p * 2 * WINDOW
            pair_t = jnp.concatenate([ot[0:HEAD_DIM, c0:c0 + WINDOW],
                                      ot[HEAD_DIM:, c0 + WINDOW:c0 + 2 * WINDOW]], axis=0)
            o_ref[0, j * WINDOW:(j + 1) * WINDOW, p * LANES:(p + 1) * LANES] = pair_t.T.astype(BF16)

    n_blocks = tq // WINDOW
    for j0 in range(0, n_blocks, SWA_INTERLEAVE):
        parts = [block(j) for j in range(j0, min(j0 + SWA_INTERLEAVE, n_blocks))]
        while parts:
            parts = [p for p in parts if next(p, _DONE) is not _DONE]


def _swa_consts():
    qi = np.arange(WINDOW)[:, None]
    kj = np.arange(WINDOW)[None, :]
    from_prev = kj > qi
    dist = np.where(from_prev, qi + WINDOW - kj, qi - kj).astype(np.float32)
    slopes = 2.0 ** (-8.0 * (np.arange(SWA_Q_HEADS, dtype=np.float32) + 1.0) / SWA_Q_HEADS)
    bias = np.zeros((2, SWA_PAIRS, 2, WINDOW, WINDOW), np.float32)
    for first in range(2):
        for p in range(SWA_PAIRS):
            for half, head in enumerate((p, SWA_GROUP + p)):
                b = (-slopes[head].astype(np.float32) * dist) * np.float32(LOG2E)
                bias[first, p, half] = np.where(from_prev & bool(first), np.float32(NEG_BIG), b)
    bias = bias.reshape(2, SWA_ROWS, WINDOW).transpose(0, 2, 1)
    h = np.arange(SWA_WIDTH) // HEAD_DIM
    bdq = (h[:, None] == h[None, :]).astype(np.float32)
    hk = np.arange(SWA_KV_WIDTH) // HEAD_DIM
    bdk = (hk[:, None] == hk[None, :]).astype(np.float32)
    return jnp.asarray(bias), jnp.asarray(bdq, BF16), jnp.asarray(bdk, BF16)


def _swa_call(sq, skv, qw_x, kw_x, sink_col):
    b, t, _ = sq.shape
    tq = SWA_TQ
    nb = tq // WINDOW
    bias, bdq, bdk = _swa_consts()
    const = lambda shape: pl.BlockSpec(shape, lambda bi, ti: (0,) * len(shape))
    return pl.pallas_call(
        _swa_kernel,
        grid=(b, t // tq),
        in_specs=[
            pl.BlockSpec((1, tq, SWA_WIDTH), lambda bi, ti: (bi, ti, 0)),
            pl.BlockSpec((1, tq, 2 * SWA_KV_WIDTH), lambda bi, ti: (bi, ti, 0)),
            pl.BlockSpec((1, WINDOW, 2 * SWA_KV_WIDTH), lambda bi, ti: (bi, jnp.maximum(ti * nb - 1, 0), 0)),
            const((1, SWA_WIDTH)), const((1, SWA_KV_WIDTH)),
            const((SWA_WIDTH, SWA_WIDTH)), const((SWA_KV_WIDTH, SWA_KV_WIDTH)),
            const((2, WINDOW, SWA_ROWS)),
            const((1, SWA_ROWS)),
        ],
        out_specs=pl.BlockSpec((1, tq, SWA_WIDTH), lambda bi, ti: (bi, ti, 0)),
        out_shape=jax.ShapeDtypeStruct((b, t, SWA_WIDTH), BF16),
        compiler_params=pltpu.CompilerParams(
            dimension_semantics=("arbitrary", "arbitrary"), vmem_limit_bytes=VMEM_LIMIT),
        name="swa",
    )(sq, skv, skv, qw_x, kw_x, bdq, bdk, bias, sink_col)


def _ffn_kernel(x_ref, og_ref, os_ref, mod_ref, nw_ref, wo_ref, wg_ref, wu_ref, wd_ref, o_ref):
    x = x_ref[0]
    mod = mod_ref[0]
    gate1 = mod[:, 2 * D_MODEL:3 * D_MODEL]
    shift2 = mod[:, 3 * D_MODEL:4 * D_MODEL]
    scale2 = mod[:, 4 * D_MODEL:5 * D_MODEL]
    gate2 = mod[:, 5 * D_MODEL:]
    mixed = _dot(og_ref[0], wo_ref[0:GDN_WIDTH, :]) + _dot(os_ref[0], wo_ref[GDN_WIDTH:, :])
    x1 = x + gate1 * mixed
    ms = jnp.mean(x1 * x1, axis=-1, keepdims=True)
    gain = nw_ref[...] * (1.0 + scale2)
    hb = (x1 * lax.rsqrt(ms + EPS) * gain + shift2).astype(BF16)
    gt = _dot(hb, wg_ref[...])
    up = _dot(hb, wu_ref[...])
    act = ((gt * _sigmoid(gt)) * up).astype(BF16)
    o_ref[0] = x1 + gate2 * _dot(act, wd_ref[...])


def _ffn_call(x, og, osw, mod3, norm_w, wo, wg, wu, wd):
    b, t, _ = x.shape
    tm = FFN_TM
    row = lambda width: pl.BlockSpec((1, tm, width), lambda bi, ti: (bi, ti, 0))
    const = lambda shape: pl.BlockSpec(shape, lambda bi, ti: (0,) * len(shape), pipeline_mode=pl.Buffered(1))
    return pl.pallas_call(
        _ffn_kernel,
        grid=(b, t // tm),
        in_specs=[
            row(D_MODEL), row(GDN_WIDTH), row(SWA_WIDTH),
            pl.BlockSpec((1, 1, 6 * D_MODEL), lambda bi, ti: (bi, 0, 0)),
            const((1, D_MODEL)),
            const((GDN_WIDTH + SWA_WIDTH, D_MODEL)),
            const((D_MODEL, D_FF)), const((D_MODEL, D_FF)), const((D_FF, D_MODEL)),
        ],
        out_specs=row(D_MODEL),
        out_shape=jax.ShapeDtypeStruct((b, t, D_MODEL), F32),
        compiler_params=pltpu.CompilerParams(
            dimension_semantics=("arbitrary", "arbitrary"), vmem_limit_bytes=VMEM_LIMIT),
        name="ffn",
    )(x, og, osw, mod3, norm_w, wo, wg, wu, wd)


def _layer(x, c_pad, w_ada, b_ada, norm1_w, w_in, conv_w, a_log, dt_bias, gdn_norm_w, q_norm_w, k_norm_w, sinks,
           w_out, norm2_w, w_gate, w_up, w_down):
    b = x.shape[0]
    mod, w_aligned, w_regrouped = _ada_call(c_pad, w_ada, b_ada.reshape(1, -1), w_in.T)
    mod3 = mod[:b].reshape(b, 1, 6 * D_MODEL)
    qkv, z, sq, skv, ab = _proj_call(x, mod3, norm1_w.reshape(1, D_MODEL), w_aligned, w_regrouped,
                                     conv_w.reshape(CONV_WIDTH, QKV_W))

    alog_pad = jnp.pad(a_log.reshape(1, GDN_HEADS), ((0, 0), (0, GATE_PAD - GDN_HEADS)))
    dtb_pad = jnp.pad(dt_bias.reshape(1, GDN_HEADS), ((0, 0), (0, GATE_PAD - GDN_HEADS)))
    gnw_x = jnp.tile(gdn_norm_w.reshape(1, HEAD_DIM), (1, GDN_HEADS))
    og, wg16, wu16, wd16, wo16 = _gdn_call(qkv, z, ab, alog_pad, dtb_pad, gnw_x, w_gate, w_up, w_down, w_out)

    qw_x = jnp.tile(q_norm_w.reshape(1, HEAD_DIM), (1, SWA_Q_HEADS))
    kw_x = jnp.tile(k_norm_w.reshape(1, HEAD_DIM), (1, SWA_KV_HEADS))
    sink_pairs = jnp.stack([sinks[:SWA_GROUP], sinks[SWA_GROUP:]], axis=1)
    sink_col = (jnp.repeat(sink_pairs, WINDOW, axis=1) * LOG2E).reshape(1, SWA_ROWS)
    osw = _swa_call(sq, skv, qw_x, kw_x, sink_col)

    return _ffn_call(x, og, osw, mod3, norm2_w.reshape(1, D_MODEL), wo16, wg16, wu16, wd16)


def kernel(x, c, w_ada, b_ada, norm1_w, w_in, conv_w, a_log, dt_bias, gdn_norm_w, q_norm_w, k_norm_w, sinks,
           w_out, norm2_w, w_gate, w_up, w_down):
    depth = w_ada.shape[0]
    b = c.shape[0]
    c_pad = jnp.pad(c, ((0, 8 - b), (0, 0)))
    for l in range(depth):
        x = _layer(x, c_pad, w_ada[l], b_ada[l], norm1_w[l], w_in[l], conv_w[l], a_log[l], dt_bias[l],
                   gdn_norm_w[l], q_norm_w[l], k_norm_w[l], sinks[l], w_out[l], norm2_w[l], w_gate[l], w_up[l],
                   w_down[l])
    return x
```

```python
import functools

import numpy as np
import jax
import jax.numpy as jnp
from jax import lax
from jax.experimental import pallas as pl
from jax.experimental.pallas import tpu as pltpu

F32 = jnp.float32
BF16 = jnp.bfloat16

D_MODEL = 1024
HEAD_DIM = 64
GDN_HEADS = 8
GDN_WIDTH = GDN_HEADS * HEAD_DIM
SWA_Q_HEADS = 8
SWA_KV_HEADS = 2
SWA_GROUP = SWA_Q_HEADS // SWA_KV_HEADS
SWA_WIDTH = SWA_Q_HEADS * HEAD_DIM
SWA_KV_WIDTH = SWA_KV_HEADS * HEAD_DIM
WINDOW = 128
CONV_WIDTH = 4
CHUNK = 64
D_FF = 2816
EPS = 1e-6
LANES = 128
GATE_PAD = LANES
HEADS_PER_GROUP = 2
GROUP_W = HEADS_PER_GROUP * HEAD_DIM
INV_BASE = 16
INV_LEVELS = (16, 32)
NEG_BIG = -1e30
VMEM_LIMIT = 56 * 1024 * 1024

PROJ_TM = 512
GDN_TT = 256
SWA_TQ = 1024
FFN_TM = 512
FFN_HEAD_ROWS = 256
ADA_TN = 1536


def _sigmoid(x):
    return 1.0 / (1.0 + jnp.exp(-x))


def _dot(a, b):
    return jnp.dot(a, b, preferred_element_type=F32)


def _dot_nt(a, b):
    return lax.dot_general(a, b, (((1,), (1,)), ((), ())), preferred_element_type=F32)


def _dot_tn(a, b):
    return lax.dot_general(a, b, (((0,), (0,)), ((), ())), preferred_element_type=F32)


def _split2(x):
    hi = x.astype(BF16)
    lo = (x - hi.astype(F32)).astype(BF16)
    return hi, lo


def _ada_kernel(c_ref, w_ref, b_ref, win_ref, o_ref, wa16_ref, wb16_ref):
    c = c_ref[...]
    ca = c * _sigmoid(c)
    o_ref[...] = _dot(ca.astype(BF16), w_ref[...].astype(BF16)) + b_ref[...]

    @pl.when(pl.program_id(0) == 0)
    def _():
        wa16_ref[...] = win_ref[0:PROJ_ALIGNED, :].T.astype(BF16)
        o1 = PROJ_ALIGNED + 2 * GDN_HEADS
        o2 = o1 + SWA_WIDTH
        head = lambda h: win_ref[o1 + h * HEAD_DIM:o1 + (h + 1) * HEAD_DIM, :]
        pieces = [head(half * SWA_GROUP + p) for p in range(SWA_PAIRS) for half in range(2)]
        pieces.append(win_ref[o2:o2 + 2 * SWA_KV_WIDTH, :])
        pieces.append(win_ref[PROJ_ALIGNED:o1, :])
        pieces.append(jnp.zeros((GATE_PAD - 2 * GDN_HEADS, D_MODEL), F32))
        wb16_ref[...] = jnp.concatenate(pieces, axis=0).T.astype(BF16)


def _ada_call(c_pad, w_ada, b_ada, w_in):
    n = w_ada.shape[1]
    whole = lambda shape: pl.BlockSpec(shape, lambda j: (0, 0))
    return pl.pallas_call(
        _ada_kernel,
        grid=(n // ADA_TN,),
        in_specs=[
            whole((8, D_MODEL)),
            pl.BlockSpec((D_MODEL, ADA_TN), lambda j: (0, j)),
            pl.BlockSpec((1, ADA_TN), lambda j: (0, j)),
            pl.BlockSpec(w_in.shape, lambda j: (0, 0), pipeline_mode=pl.Buffered(1)),
        ],
        out_specs=[pl.BlockSpec((8, ADA_TN), lambda j: (0, j)),
                   whole((D_MODEL, PROJ_ALIGNED)), whole((D_MODEL, PROJ_REGROUPED))],
        out_shape=[jax.ShapeDtypeStruct((8, n), F32),
                   jax.ShapeDtypeStruct((D_MODEL, PROJ_ALIGNED), BF16),
                   jax.ShapeDtypeStruct((D_MODEL, PROJ_REGROUPED), BF16)],
        compiler_params=pltpu.CompilerParams(
            dimension_semantics=("arbitrary",), vmem_limit_bytes=VMEM_LIMIT),
        name="ada",
    )(c_pad, w_ada, b_ada, w_in)


QKV_W = 3 * GDN_WIDTH
PROJ_SPLITS = (QKV_W, GDN_WIDTH, SWA_WIDTH, 2 * SWA_KV_WIDTH, GATE_PAD)
PROJ_ALIGNED = QKV_W + GDN_WIDTH
PROJ_REGROUPED = sum(PROJ_SPLITS) - PROJ_ALIGNED


def _proj_kernel(x_ref, mod_ref, nw_ref, wa_ref, wb_ref, cw_ref, bd_ref, qkv_ref, z_ref, sq_ref, skv_ref, ab_ref,
                 qbuf, kbuf, vbuf):
    tm = PROJ_TM
    bufs = (qbuf, kbuf, vbuf)

    @pl.when(pl.program_id(1) == 0)
    def _():
        for buf in bufs:
            buf[0:8, :] = jnp.zeros((8, GDN_WIDTH), F32)

    x = x_ref[0]
    ms = jnp.mean(x * x, axis=-1, keepdims=True)
    mod = mod_ref[0]
    shift = mod[:, 0:D_MODEL]
    gain = nw_ref[...] * (1.0 + mod[:, D_MODEL:2 * D_MODEL])
    hb = (x * lax.rsqrt(ms + EPS) * gain + shift).astype(BF16)

    def project(s):
        bufs[s][8:8 + tm, :] = _dot(hb, wa_ref[:, s * GDN_WIDTH:(s + 1) * GDN_WIDTH])

    def conv_silu(s):
        buf = bufs[s]
        cw = lambda j: cw_ref[j:j + 1, s * GDN_WIDTH:(s + 1) * GDN_WIDTH]
        acc = cw(CONV_WIDTH - 1) * buf[8:8 + tm, :]
        for j in range(CONV_WIDTH - 1):
            off = 8 - (CONV_WIDTH - 1) + j
            acc = acc + cw(j) * buf[off:off + tm, :]
        buf[0:8, :] = buf[tm:tm + 8, :]
        return acc * _sigmoid(acc)

    def l2_normed(a, scale):
        return a * lax.rsqrt(_dot((a * a).astype(BF16), bd_ref[...]) + EPS) * scale

    def rest(ref, w_ref, c0):
        ref[0] = _dot(hb, w_ref[:, c0:c0 + ref.shape[-1]])

    project(0)
    rest(z_ref, wa_ref, QKV_W)
    project(1)
    qkv_ref[0, :, 0:GDN_WIDTH] = l2_normed(conv_silu(0), HEAD_DIM ** -0.5)
    rest(sq_ref, wb_ref, 0)
    project(2)
    rest(skv_ref, wb_ref, SWA_WIDTH)
    qkv_ref[0, :, GDN_WIDTH:2 * GDN_WIDTH] = l2_normed(conv_silu(1), 1.0)
    rest(ab_ref, wb_ref, SWA_WIDTH + 2 * SWA_KV_WIDTH)
    qkv_ref[0, :, 2 * GDN_WIDTH:] = conv_silu(2)


def _head_sum_matrix(width):
    h = np.arange(width) // HEAD_DIM
    return jnp.asarray((h[:, None] == h[None, :]).astype(np.float32), BF16)


def _proj_call(x, mod3, norm_w, w_aligned, w_regrouped, conv_w):
    b, t, _ = x.shape
    tm = PROJ_TM
    row = lambda width: pl.BlockSpec((1, tm, width), lambda bi, ti: (bi, ti, 0))
    const = lambda shape: pl.BlockSpec(shape, lambda bi, ti: (0,) * len(shape))
    return pl.pallas_call(
        _proj_kernel,
        grid=(b, t // tm),
        in_specs=[
            row(D_MODEL),
            pl.BlockSpec((1, 1, 6 * D_MODEL), lambda bi, ti: (bi, 0, 0)),
            const((1, D_MODEL)),
            const((D_MODEL, PROJ_ALIGNED)),
            const((D_MODEL, PROJ_REGROUPED)),
            const((CONV_WIDTH, QKV_W)),
            const((GDN_WIDTH, GDN_WIDTH)),
        ],
        out_specs=[row(w) for w in PROJ_SPLITS],
        out_shape=[jax.ShapeDtypeStruct((b, t, w), F32) for w in PROJ_SPLITS],
        scratch_shapes=[pltpu.VMEM((8 + tm, GDN_WIDTH), F32)] * 3,
        compiler_params=pltpu.CompilerParams(
            dimension_semantics=("arbitrary", "arbitrary"), vmem_limit_bytes=VMEM_LIMIT),
        name="proj",
    )(x, mod3, norm_w, w_aligned, w_regrouped, conv_w, _head_sum_matrix(GDN_WIDTH))


GDN_UNITS = (GDN_TT // CHUNK) * (GDN_HEADS // HEADS_PER_GROUP)


def _block_diag_rows(p, same_head):
    stacked = jnp.concatenate([p.astype(BF16)] * HEADS_PER_GROUP, axis=0)
    return jnp.where(same_head, stacked, jnp.zeros((), BF16))


_DONE = object()


def _gdn_kernel(qkv_ref, z_ref, ab_ref, alog_ref, dtb_ref, gnw_ref, bd_ref, eg_ref, eb_ref, ltri_ref,
                wg_ref, wu_ref, wd_ref, wo_ref, o_ref, wg16_ref, wu16_ref, wd16_ref, wo16_ref,
                s_ref, obuf, vw_buf, wq_buf, u_buf, qk_buf, kd_buf, dec_buf, *, tiles_per_seq):
    tt = GDN_TT
    step = pl.program_id(0)
    @pl.when(step < CAST_STEPS)
    def _():
        for src, dst in ((wg_ref, wg16_ref), (wu_ref, wu16_ref), (wd_ref, wd16_ref), (wo_ref, wo16_ref)):
            dst[...] = src[...].astype(BF16)

    wslot = step % 2
    rslot = 1 - wslot
    starts_sequence = (step - 1) % tiles_per_seq == 0

    @pl.when(step == 0)
    def _():
        for buf in (s_ref, wq_buf, u_buf, qk_buf, kd_buf, dec_buf):
            buf[...] = jnp.zeros_like(buf)

    bd = bd_ref[...]
    q = qkv_ref[0, :, 0:GDN_WIDTH]
    k = qkv_ref[0, :, GDN_WIDTH:2 * GDN_WIDTH]
    v = qkv_ref[0, :, 2 * GDN_WIDTH:]

    ab = ab_ref[0]
    lane = lax.broadcasted_iota(jnp.int32, (1, GATE_PAD), 1)
    xs = ab + dtb_ref[...]
    softplus = jnp.maximum(xs, 0.0) + jnp.log1p(jnp.exp(-jnp.abs(xs)))
    g = jnp.where(lane < GDN_HEADS, -jnp.exp(alog_ref[...]) * softplus, 0.0)
    beta = _sigmoid(ab)
    ltri = ltri_ref[...]
    g1, g2 = _split2(g)
    gcum = _dot(ltri, g1) + _dot(ltri, g2)
    eg = eg_ref[...]
    c1, c2 = _split2(gcum)
    g_x = _dot(c1, eg) + _dot(c2, eg)
    beta_x = _dot(beta.astype(BF16), eb_ref[...])

    r64 = lax.broadcasted_iota(jnp.int32, (CHUNK, GROUP_W), 0)
    l64 = lax.broadcasted_iota(jnp.int32, (CHUNK, GROUP_W), 1) % CHUNK
    causal = r64 >= l64
    eye_x = (r64 == l64).astype(F32)
    strict_x = (r64 > l64).astype(F32)
    same_block = lambda size: (r64 // size) == (l64 // size)
    base_x = same_block(INV_BASE).astype(F32)
    level_x = [(same_block(2 * size) & ~same_block(size)).astype(F32) for size in INV_LEVELS]
    rb = lax.broadcasted_iota(jnp.int32, (GROUP_W, GROUP_W), 0) // HEAD_DIM
    cb = lax.broadcasted_iota(jnp.int32, (GROUP_W, GROUP_W), 1) // HEAD_DIM
    same_head = rb == cb
    mask_bd = same_head.astype(F32)

    n_groups = GDN_HEADS // HEADS_PER_GROUP
    n_chunks = tt // CHUNK
    units = [(c, gi) for c in range(n_chunks) for gi in range(n_groups)]
    ids = list(range(len(units)))
    tile = lambda a: [a[c * CHUNK:(c + 1) * CHUNK, gi * GROUP_W:(gi + 1) * GROUP_W] for c, gi in units]
    each = lambda f, *lists: [f(*args) for args in zip(*lists)]
    bdr = lambda t: _block_diag_rows(t, same_head)
    mm = lambda a, w16: _dot(a.astype(BF16), w16)
    stack = lambda a, b_: jnp.concatenate([a, b_], axis=0)

    def chunk_parallel_part():
        kc, qc, vc, bx, gx = tile(k), tile(q), tile(v), tile(beta_x), tile(g_x)
        eg_c = each(jnp.exp, gx)
        glast = each(lambda g_: g_[CHUNK - 1:CHUNK, :], gx)
        kb = each(jnp.multiply, kc, bx)
        for i in ids:
            vw_buf[i, :, 0:GROUP_W] = (vc[i] * bx[i]).astype(BF16)
            vw_buf[i, :, GROUP_W:] = (kb[i] * eg_c[i]).astype(BF16)
            wq_buf[wslot, i, CHUNK:] = (qc[i] * eg_c[i]).astype(BF16)
            kd_buf[wslot, i] = (kc[i] * jnp.exp(glast[i] - gx[i])).astype(BF16)
            dec_buf[wslot, i] = jnp.broadcast_to(jnp.exp(glast[i]), (8, GROUP_W))
        dm = each(lambda g_: jnp.exp(jnp.where(causal, g_ - jnp.sum(g_ * eye_x, axis=0, keepdims=True), NEG_BIG)),
                  gx)
        yield
        aq = each(lambda i, a, b_, k_: _dot_nt(stack(a, b_).astype(BF16), bdr(k_)), ids, kb, qc, kc)
        xm = each(lambda a, d_: -(a[0:CHUNK] * d_ * strict_x), aq, dm)
        for i in ids:
            qk_buf[wslot, i] = (aq[i][CHUNK:] * dm[i]).astype(BF16)
        xd = each(lambda x_: x_ * base_x, xm)
        tinv = each(lambda x_: eye_x + x_, xd)
        yield
        pw = each(lambda i, x_: mm(x_, bdr(x_)), ids, xd)
        yield
        for _ in range(INV_BASE.bit_length() - 3):
            r = each(lambda i, t_, p_: mm(stack(t_, p_), bdr(p_)), ids, tinv, pw)
            tinv = each(lambda t_, r_: t_ + r_[0:CHUNK], tinv, r)
            pw = each(lambda r_: r_[CHUNK:], r)
            yield
        tinv = each(lambda i, t_, p_: t_ + mm(t_, bdr(p_)), ids, tinv, pw)
        yield
        for lm in level_x:
            e = each(lambda i, x_, t_: mm(x_ * lm, bdr(t_)), ids, xm, tinv)
            yield
            tinv = each(lambda i, t_, e_: t_ + mm(t_, bdr(e_)), ids, tinv, e)
            yield
        for i in ids:
            operands = jnp.concatenate([bdr(vw_buf[i, :, 0:GROUP_W]), bdr(vw_buf[i, :, GROUP_W:])], axis=1)
            uw = mm(tinv[i], operands)
            u_buf[wslot, i] = uw[:, 0:GROUP_W]
            wq_buf[wslot, i, 0:CHUNK] = uw[:, GROUP_W:].astype(BF16)

    def sequential_part():
        s_state = [jnp.where(starts_sequence, 0.0, s_ref[gi]) for gi in range(n_groups)]
        for c in range(n_chunks):
            cids = [c * n_groups + gi for gi in range(n_groups)]
            r2 = [_dot(wq_buf[rslot, i], s_state[gi].astype(BF16)) for gi, i in enumerate(cids)]
            yield
            vn = [u_buf[rslot, i] - r2[gi][0:CHUNK] for gi, i in enumerate(cids)]
            o = [r2[gi][CHUNK:] + _dot(qk_buf[rslot, i], bdr(vn[gi])) for gi, i in enumerate(cids)]
            s_state = [s_state[gi] * dec_buf[rslot, i][0:1] + mask_bd * _dot_tn(kd_buf[rslot, i], vn[gi].astype(BF16))
                       for gi, i in enumerate(cids)]
            for gi in range(n_groups):
                obuf[c * CHUNK:(c + 1) * CHUNK, gi * GROUP_W:(gi + 1) * GROUP_W] = o[gi]
            yield
        for gi in range(n_groups):
            s_ref[gi] = s_state[gi]

    parts = [chunk_parallel_part(), sequential_part()]
    while parts:
        parts = [p for p in parts if next(p, _DONE) is not _DONE]

    o = obuf[...]
    ms = _dot((o * o).astype(BF16), bd) * (1.0 / HEAD_DIM)
    zz = z_ref[0]
    o_ref[0] = (o * lax.rsqrt(ms + EPS) * gnw_ref[...] * (zz * _sigmoid(zz))).astype(BF16)


def _gdn_consts():
    h = np.arange(GDN_WIDTH) // HEAD_DIM
    eg = np.zeros((GATE_PAD, GDN_WIDTH), np.float32)
    eb = np.zeros((GATE_PAD, GDN_WIDTH), np.float32)
    eg[h, np.arange(GDN_WIDTH)] = 1.0
    eb[GDN_HEADS + h, np.arange(GDN_WIDTH)] = 1.0
    t = np.arange(GDN_TT)
    ltri = ((t[:, None] // CHUNK == t[None, :] // CHUNK) & (t[:, None] >= t[None, :])).astype(np.float32)
    return (_head_sum_matrix(GDN_WIDTH), jnp.asarray(eg, BF16), jnp.asarray(eb, BF16), jnp.asarray(ltri, BF16))


BF16_ROWS = 16
CAST_STEPS = 2 * GDN_HEADS


def _swa_head_source(i):
    j = i - GDN_HEADS
    return jnp.where(i < GDN_HEADS, i, GDN_HEADS + (j % 2) * SWA_GROUP + j // 2)


def _cast_specs(weight, unit, steps, source=None):
    rows, cols = weight.shape
    block_rows = next(r for r in range(unit, rows + 1, unit) if rows % r == 0 and rows // r <= steps)
    if source is None:
        source = lambda i: i
    else:
        assert block_rows == unit, "row regrouping needs one unit per block"
    last = rows // block_rows - 1
    src = pl.BlockSpec((block_rows, cols), lambda n: (source(jnp.minimum(n, last)), 0))
    dst = pl.BlockSpec((block_rows, cols), lambda n: (jnp.minimum(n, last), 0))
    return src, dst, jax.ShapeDtypeStruct((rows, cols), BF16)


def _gdn_call(qkv, z, ab, alog_pad, dtb_pad, gnw_x, w_gate, w_up, w_down, w_out):
    b, t, _ = qkv.shape
    tt = GDN_TT
    bd, eg, eb, ltri = _gdn_consts()
    n_tiles = t // tt
    n_groups = GDN_HEADS // HEADS_PER_GROUP
    total = b * n_tiles

    def tile_block(width, lag):
        def index(n):
            m = jnp.clip(n - lag, 0, total - 1)
            return (m // n_tiles, m % n_tiles, 0)
        return pl.BlockSpec((1, tt, width), index)

    ahead = lambda width: tile_block(width, 0)
    behind = lambda width: tile_block(width, 1)
    const = lambda shape: pl.BlockSpec(shape, lambda n: (0,) * len(shape))
    per_unit = lambda rows, dtype: pltpu.VMEM((2, GDN_UNITS, rows, GROUP_W), dtype)
    steps = total + 1
    assert steps >= CAST_STEPS
    casts = [
        _cast_specs(w_gate, BF16_ROWS, CAST_STEPS),
        _cast_specs(w_up, BF16_ROWS, CAST_STEPS),
        _cast_specs(w_down, BF16_ROWS, CAST_STEPS),
        _cast_specs(w_out, HEAD_DIM, CAST_STEPS, _swa_head_source),
    ]
    return pl.pallas_call(
        functools.partial(_gdn_kernel, tiles_per_seq=n_tiles),
        grid=(steps,),
        in_specs=[
            ahead(QKV_W), behind(GDN_WIDTH), ahead(GATE_PAD),
            const((1, GATE_PAD)), const((1, GATE_PAD)), const((1, GDN_WIDTH)),
            const((GDN_WIDTH, GDN_WIDTH)), const((GATE_PAD, GDN_WIDTH)), const((GATE_PAD, GDN_WIDTH)),
            const((tt, tt)),
        ] + [c[0] for c in casts],
        out_specs=[behind(GDN_WIDTH)] + [c[1] for c in casts],
        out_shape=[jax.ShapeDtypeStruct((b, t, GDN_WIDTH), BF16)] + [c[2] for c in casts],
        scratch_shapes=[
            pltpu.VMEM((n_groups, GROUP_W, GROUP_W), F32),
            pltpu.VMEM((tt, GDN_WIDTH), F32),
            pltpu.VMEM((GDN_UNITS, CHUNK, 2 * GROUP_W), BF16),
            per_unit(2 * CHUNK, BF16), per_unit(CHUNK, F32), per_unit(CHUNK, BF16), per_unit(CHUNK, BF16),
            per_unit(8, F32),
        ],
        compiler_params=pltpu.CompilerParams(
            dimension_semantics=("arbitrary",), vmem_limit_bytes=VMEM_LIMIT),
        name="gdn",
    )(qkv, z, ab, alog_pad, dtb_pad, gnw_x, bd, eg, eb, ltri, w_gate, w_up, w_down, w_out)


SWA_PAIRS = SWA_Q_HEADS // 2
SWA_ROWS = SWA_Q_HEADS * WINDOW
LOG2E = 1.4426950408889634
SWA_INTERLEAVE = 8


def _swa_kernel(q_ref, kvc_ref, kvp_ref, qw_ref, kw_ref, bdq_ref, bdk_ref, bias_ref, sink_ref, o_ref):
    tq = SWA_TQ
    ti = pl.program_id(1)
    q = q_ref[0]
    q = q * lax.rsqrt(_dot((q * q).astype(BF16), bdq_ref[...]) * (1.0 / HEAD_DIM) + EPS)
    q = q * (qw_ref[...] * (HEAD_DIM ** -0.5 * LOG2E))
    kv = jnp.concatenate([kvp_ref[0], kvc_ref[0]], axis=0)
    k = kv[:, 0:SWA_KV_WIDTH]
    v = kv[:, SWA_KV_WIDTH:]
    k = k * lax.rsqrt(_dot((k * k).astype(BF16), bdk_ref[...]) * (1.0 / HEAD_DIM) + EPS) * kw_ref[...]
    k16 = k.astype(BF16)
    lo = lax.broadcasted_iota(jnp.int32, (1, LANES), 1) < HEAD_DIM
    qcol = lax.broadcasted_iota(jnp.int32, (WINDOW, SWA_ROWS), 1) % WINDOW
    from_prev = lax.broadcasted_iota(jnp.int32, (WINDOW, SWA_ROWS), 0) > qcol
    first = jnp.where(ti == 0, 1, 0)
    sink = sink_ref[...]
    vt16 = v.T.astype(BF16)
    zero = jnp.zeros((), BF16)
    def block(j):
        qj = q[j * WINDOW:(j + 1) * WINDOW]
        parts = []
        for p in range(SWA_PAIRS):
            qp = qj[:, p * LANES:(p + 1) * LANES]
            parts += [jnp.where(lo, qp, 0.0), jnp.where(lo, 0.0, qp)]
        qs = jnp.concatenate(parts, axis=0).astype(BF16)
        st = _dot_nt(k16[j * WINDOW:(j + 2) * WINDOW], qs)
        yield
        bias = bias_ref[first] if j == 0 else bias_ref[0]
        sm = jnp.where(from_prev, st[0:WINDOW], st[WINDOW:]) + bias
        m = jnp.max(sm, axis=0, keepdims=True)
        yield
        pe = jnp.exp2(sm - m)
        den = jnp.sum(pe, axis=0, keepdims=True) + jnp.exp2(sink - m)
        pb = pe.astype(BF16)
        pt2 = jnp.concatenate([jnp.where(from_prev, pb, zero), jnp.where(from_prev, zero, pb)], axis=0)
        yield
        ot = _dot(vt16[:, j * WINDOW:(j + 2) * WINDOW], pt2) * (1.0 / den)
        yield
        for p in range(SWA_PAIRS):
            c0 = p * 2 * WINDOW
            pair_t = jnp.concatenate([ot[0:HEAD_DIM, c0:c0 + WINDOW],
                                      ot[HEAD_DIM:, c0 + WINDOW:c0 + 2 * WINDOW]], axis=0)
            o_ref[0, j * WINDOW:(j + 1) * WINDOW, p * LANES:(p + 1) * LANES] = pair_t.T.astype(BF16)

    n_blocks = tq // WINDOW
    for j0 in range(0, n_blocks, SWA_INTERLEAVE):
        parts = [block(j) for j in range(j0, min(j0 + SWA_INTERLEAVE, n_blocks))]
        while parts:
            parts = [p for p in parts if next(p, _DONE) is not _DONE]


def _swa_consts():
    qi = np.arange(WINDOW)[:, None]
    kj = np.arange(WINDOW)[None, :]
    from_prev = kj > qi
    dist = np.where(from_prev, qi + WINDOW - kj, qi - kj).astype(np.float32)
    slopes = 2.0 ** (-8.0 * (np.arange(SWA_Q_HEADS, dtype=np.float32) + 1.0) / SWA_Q_HEADS)
    bias = np.zeros((2, SWA_PAIRS, 2, WINDOW, WINDOW), np.float32)
    for first in range(2):
        for p in range(SWA_PAIRS):
            for half, head in enumerate((p, SWA_GROUP + p)):
                b = (-slopes[head].astype(np.float32) * dist) * np.float32(LOG2E)
                bias[first, p, half] = np.where(from_prev & bool(first), np.float32(NEG_BIG), b)
    bias = bias.reshape(2, SWA_ROWS, WINDOW).transpose(0, 2, 1)
    h = np.arange(SWA_WIDTH) // HEAD_DIM
    bdq = (h[:, None] == h[None, :]).astype(np.float32)
    hk = np.arange(SWA_KV_WIDTH) // HEAD_DIM
    bdk = (hk[:, None] == hk[None, :]).astype(np.float32)
    return jnp.asarray(bias), jnp.asarray(bdq, BF16), jnp.asarray(bdk, BF16)


def _swa_call(sq, skv, qw_x, kw_x, sink_col):
    b, t, _ = sq.shape
    tq = SWA_TQ
    nb = tq // WINDOW
    bias, bdq, bdk = _swa_consts()
    const = lambda shape: pl.BlockSpec(shape, lambda bi, ti: (0,) * len(shape))
    return pl.pallas_call(
        _swa_kernel,
        grid=(b, t // tq),
        in_specs=[
            pl.BlockSpec((1, tq, SWA_WIDTH), lambda bi, ti: (bi, ti, 0)),
            pl.BlockSpec((1, tq, 2 * SWA_KV_WIDTH), lambda bi, ti: (bi, ti, 0)),
            pl.BlockSpec((1, WINDOW, 2 * SWA_KV_WIDTH), lambda bi, ti: (bi, jnp.maximum(ti * nb - 1, 0), 0)),
            const((1, SWA_WIDTH)), const((1, SWA_KV_WIDTH)),
            const((SWA_WIDTH, SWA_WIDTH)), const((SWA_KV_WIDTH, SWA_KV_WIDTH)),
            const((2, WINDOW, SWA_ROWS)),
            const((1, SWA_ROWS)),
        ],
        out_specs=pl.BlockSpec((1, tq, SWA_WIDTH), lambda bi, ti: (bi, ti, 0)),
        out_shape=jax.ShapeDtypeStruct((b, t, SWA_WIDTH), BF16),
        compiler_params=pltpu.CompilerParams(
            dimension_semantics=("arbitrary", "arbitrary"), vmem_limit_bytes=VMEM_LIMIT),
        name="swa",
    )(sq, skv, skv, qw_x, kw_x, bdq, bdk, bias, sink_col)


def _ffn_kernel(x_ref, og_ref, os_ref, mod_ref, nw_ref, wo_ref, wg_ref, wu_ref, wd_ref, o_ref):
    x = x_ref[0]
    mod = mod_ref[0]
    gate1 = mod[:, 2 * D_MODEL:3 * D_MODEL]
    shift2 = mod[:, 3 * D_MODEL:4 * D_MODEL]
    scale2 = mod[:, 4 * D_MODEL:5 * D_MODEL]
    gate2 = mod[:, 5 * D_MODEL:]
    gain = nw_ref[...] * (1.0 + scale2)
    blocks = [slice(r * FFN_HEAD_ROWS, (r + 1) * FFN_HEAD_ROWS) for r in range(FFN_TM // FFN_HEAD_ROWS)]
    mixed = [_dot(og_ref[0, rows, :], wo_ref[0:GDN_WIDTH, :]) + _dot(os_ref[0, rows, :], wo_ref[GDN_WIDTH:, :])
             for rows in blocks]
    x1_blocks, hb_blocks = [], []
    for rows, m_ in zip(blocks, mixed):
        x1_ = x[rows] + gate1 * m_
        ms = jnp.mean(x1_ * x1_, axis=-1, keepdims=True)
        x1_blocks.append(x1_)
        hb_blocks.append((x1_ * lax.rsqrt(ms + EPS) * gain + shift2).astype(BF16))
    x1 = jnp.concatenate(x1_blocks, axis=0)
    hb = jnp.concatenate(hb_blocks, axis=0)
    gt = _dot(hb, wg_ref[...])
    up = _dot(hb, wu_ref[...])
    act = ((gt * _sigmoid(gt)) * up).astype(BF16)
    o_ref[0] = x1 + gate2 * _dot(act, wd_ref[...])


def _ffn_call(x, og, osw, mod3, norm_w, wo, wg, wu, wd):
    b, t, _ = x.shape
    tm = FFN_TM
    row = lambda width: pl.BlockSpec((1, tm, width), lambda bi, ti: (bi, ti, 0))
    const = lambda shape: pl.BlockSpec(shape, lambda bi, ti: (0,) * len(shape), pipeline_mode=pl.Buffered(1))
    return pl.pallas_call(
        _ffn_kernel,
        grid=(b, t // tm),
        in_specs=[
            row(D_MODEL), row(GDN_WIDTH), row(SWA_WIDTH),
            pl.BlockSpec((1, 1, 6 * D_MODEL), lambda bi, ti: (bi, 0, 0)),
            const((1, D_MODEL)),
            const((GDN_WIDTH + SWA_WIDTH, D_MODEL)),
            const((D_MODEL, D_FF)), const((D_MODEL, D_FF)), const((D_FF, D_MODEL)),
        ],
        out_specs=row(D_MODEL),
        out_shape=jax.ShapeDtypeStruct((b, t, D_MODEL), F32),
        compiler_params=pltpu.CompilerParams(
            dimension_semantics=("arbitrary", "arbitrary"), vmem_limit_bytes=VMEM_LIMIT),
        name="ffn",
    )(x, og, osw, mod3, norm_w, wo, wg, wu, wd)


def _layer(x, c_pad, w_ada, b_ada, norm1_w, w_in, conv_w, a_log, dt_bias, gdn_norm_w, q_norm_w, k_norm_w, sinks,
           w_out, norm2_w, w_gate, w_up, w_down):
    b = x.shape[0]
    mod, w_aligned, w_regrouped = _ada_call(c_pad, w_ada, b_ada.reshape(1, -1), w_in.T)
    mod3 = mod[:b].reshape(b, 1, 6 * D_MODEL)
    qkv, z, sq, skv, ab = _proj_call(x, mod3, norm1_w.reshape(1, D_MODEL), w_aligned, w_regrouped,
                                     conv_w.reshape(CONV_WIDTH, QKV_W))

    alog_pad = jnp.pad(a_log.reshape(1, GDN_HEADS), ((0, 0), (0, GATE_PAD - GDN_HEADS)))
    dtb_pad = jnp.pad(dt_bias.reshape(1, GDN_HEADS), ((0, 0), (0, GATE_PAD - GDN_HEADS)))
    gnw_x = jnp.tile(gdn_norm_w.reshape(1, HEAD_DIM), (1, GDN_HEADS))
    og, wg16, wu16, wd16, wo16 = _gdn_call(qkv, z, ab, alog_pad, dtb_pad, gnw_x, w_gate, w_up, w_down, w_out)

    qw_x = jnp.tile(q_norm_w.reshape(1, HEAD_DIM), (1, SWA_Q_HEADS))
    kw_x = jnp.tile(k_norm_w.reshape(1, HEAD_DIM), (1, SWA_KV_HEADS))
    sink_pairs = jnp.stack([sinks[:SWA_GROUP], sinks[SWA_GROUP:]], axis=1)
    sink_col = (jnp.repeat(sink_pairs, WINDOW, axis=1) * LOG2E).reshape(1, SWA_ROWS)
    osw = _swa_call(sq, skv, qw_x, kw_x, sink_col)

    return _ffn_call(x, og, osw, mod3, norm2_w.reshape(1, D_MODEL), wo16, wg16, wu16, wd16)


def kernel(x, c, w_ada, b_ada, norm1_w, w_in, conv_w, a_log, dt_bias, gdn_norm_w, q_norm_w, k_norm_w, sinks,
           w_out, norm2_w, w_gate, w_up, w_down):
    depth = w_ada.shape[0]
    b = c.shape[0]
    c_pad = jnp.pad(c, ((0, 8 - b), (0, 0)))
    for l in range(depth):
        x = _layer(x, c_pad, w_ada[l], b_ada[l], norm1_w[l], w_in[l], conv_w[l], a_log[l], dt_bias[l],
                   gdn_norm_w[l], q_norm_w[l], k_norm_w[l], sinks[l], w_out[l], norm2_w[l], w_gate[l], w_up[l],
                   w_down[l])
    return x
```

```python
import functools

import numpy as np
import jax
import jax.numpy as jnp
from jax import lax
from jax.experimental import pallas as pl
from jax.experimental.pallas import tpu as pltpu

F32 = jnp.float32
BF16 = jnp.bfloat16

D_MODEL = 1024
HEAD_DIM = 64
GDN_HEADS = 8
GDN_WIDTH = GDN_HEADS * HEAD_DIM
SWA_Q_HEADS = 8
SWA_KV_HEADS = 2
SWA_GROUP = SWA_Q_HEADS // SWA_KV_HEADS
SWA_WIDTH = SWA_Q_HEADS * HEAD_DIM
SWA_KV_WIDTH = SWA_KV_HEADS * HEAD_DIM
WINDOW = 128
CONV_WIDTH = 4
CHUNK = 64
D_FF = 2816
EPS = 1e-6
LANES = 128
GATE_PAD = LANES
HEADS_PER_GROUP = 2
GROUP_W = HEADS_PER_GROUP * HEAD_DIM
INV_BASE = 16
INV_LEVELS = (16, 32)
NEG_BIG = -1e30
VMEM_LIMIT = 56 * 1024 * 1024

PROJ_TM = 512
GDN_TT = 256
FFN_TM = 512
FFN_HEAD_ROWS = 256
ADA_TN = 1536


def _sigmoid(x):
    return 1.0 / (1.0 + jnp.exp(-x))


def _dot(a, b):
    return jnp.dot(a, b, preferred_element_type=F32)


def _dot_nt(a, b):
    return lax.dot_general(a, b, (((1,), (1,)), ((), ())), preferred_element_type=F32)


def _dot_tn(a, b):
    return lax.dot_general(a, b, (((0,), (0,)), ((), ())), preferred_element_type=F32)


def _split2(x):
    hi = x.astype(BF16)
    lo = (x - hi.astype(F32)).astype(BF16)
    return hi, lo


def _ada_kernel(c_ref, w_ref, b_ref, win_ref, o_ref, wa16_ref, wb16_ref):
    c = c_ref[...]
    ca = c * _sigmoid(c)
    o_ref[...] = _dot(ca.astype(BF16), w_ref[...].astype(BF16)) + b_ref[...]

    @pl.when(pl.program_id(0) == 0)
    def _():
        wa16_ref[...] = win_ref[0:PROJ_ALIGNED, :].T.astype(BF16)
        o1 = PROJ_ALIGNED + 2 * GDN_HEADS
        o2 = o1 + SWA_WIDTH
        head = lambda h: win_ref[o1 + h * HEAD_DIM:o1 + (h + 1) * HEAD_DIM, :]
        pieces = [head(half * SWA_GROUP + p) for p in range(SWA_PAIRS) for half in range(2)]
        pieces.append(win_ref[o2:o2 + 2 * SWA_KV_WIDTH, :])
        pieces.append(win_ref[PROJ_ALIGNED:o1, :])
        pieces.append(jnp.zeros((GATE_PAD - 2 * GDN_HEADS, D_MODEL), F32))
        wb16_ref[...] = jnp.concatenate(pieces, axis=0).T.astype(BF16)


def _ada_call(c_pad, w_ada, b_ada, w_in):
    n = w_ada.shape[1]
    whole = lambda shape: pl.BlockSpec(shape, lambda j: (0, 0))
    return pl.pallas_call(
        _ada_kernel,
        grid=(n // ADA_TN,),
        in_specs=[
            whole((8, D_MODEL)),
            pl.BlockSpec((D_MODEL, ADA_TN), lambda j: (0, j)),
            pl.BlockSpec((1, ADA_TN), lambda j: (0, j)),
            pl.BlockSpec(w_in.shape, lambda j: (0, 0), pipeline_mode=pl.Buffered(1)),
        ],
        out_specs=[pl.BlockSpec((8, ADA_TN), lambda j: (0, j)),
                   whole((D_MODEL, PROJ_ALIGNED)), whole((D_MODEL, PROJ_REGROUPED))],
        out_shape=[jax.ShapeDtypeStruct((8, n), F32),
                   jax.ShapeDtypeStruct((D_MODEL, PROJ_ALIGNED), BF16),
                   jax.ShapeDtypeStruct((D_MODEL, PROJ_REGROUPED), BF16)],
        compiler_params=pltpu.CompilerParams(
            dimension_semantics=("arbitrary",), vmem_limit_bytes=VMEM_LIMIT),
        name="ada",
    )(c_pad, w_ada, b_ada, w_in)


QKV_W = 3 * GDN_WIDTH
PROJ_SPLITS = (QKV_W, GDN_WIDTH, SWA_WIDTH, 2 * SWA_KV_WIDTH, GATE_PAD)
PROJ_ALIGNED = QKV_W + GDN_WIDTH
PROJ_REGROUPED = sum(PROJ_SPLITS) - PROJ_ALIGNED


def _proj_kernel(x_ref, mod_ref, nw_ref, wa_ref, wb_ref, cw_ref, bd_ref, qkv_ref, z_ref, sq_ref, skv_ref, ab_ref,
                 qbuf, kbuf, vbuf):
    tm = PROJ_TM
    bufs = (qbuf, kbuf, vbuf)

    @pl.when(pl.program_id(1) == 0)
    def _():
        for buf in bufs:
            buf[0:8, :] = jnp.zeros((8, GDN_WIDTH), F32)

    x = x_ref[0]
    ms = jnp.mean(x * x, axis=-1, keepdims=True)
    mod = mod_ref[0]
    shift = mod[:, 0:D_MODEL]
    gain = nw_ref[...] * (1.0 + mod[:, D_MODEL:2 * D_MODEL])
    hb = (x * lax.rsqrt(ms + EPS) * gain + shift).astype(BF16)

    def project(s):
        bufs[s][8:8 + tm, :] = _dot(hb, wa_ref[:, s * GDN_WIDTH:(s + 1) * GDN_WIDTH])

    def conv_silu(s):
        buf = bufs[s]
        cw = lambda j: cw_ref[j:j + 1, s * GDN_WIDTH:(s + 1) * GDN_WIDTH]
        acc = cw(CONV_WIDTH - 1) * buf[8:8 + tm, :]
        for j in range(CONV_WIDTH - 1):
            off = 8 - (CONV_WIDTH - 1) + j
            acc = acc + cw(j) * buf[off:off + tm, :]
        buf[0:8, :] = buf[tm:tm + 8, :]
        return acc * _sigmoid(acc)

    def l2_normed(a, scale):
        return a * lax.rsqrt(_dot((a * a).astype(BF16), bd_ref[...]) + EPS) * scale

    def rest(ref, w_ref, c0):
        ref[0] = _dot(hb, w_ref[:, c0:c0 + ref.shape[-1]])

    project(0)
    rest(z_ref, wa_ref, QKV_W)
    project(1)
    qkv_ref[0, :, 0:GDN_WIDTH] = l2_normed(conv_silu(0), HEAD_DIM ** -0.5)
    rest(sq_ref, wb_ref, 0)
    project(2)
    rest(skv_ref, wb_ref, SWA_WIDTH)
    qkv_ref[0, :, GDN_WIDTH:2 * GDN_WIDTH] = l2_normed(conv_silu(1), 1.0)
    rest(ab_ref, wb_ref, SWA_WIDTH + 2 * SWA_KV_WIDTH)
    qkv_ref[0, :, 2 * GDN_WIDTH:] = conv_silu(2)


def _head_sum_matrix(width):
    h = np.arange(width) // HEAD_DIM
    return jnp.asarray((h[:, None] == h[None, :]).astype(np.float32), BF16)


def _proj_call(x, mod3, norm_w, w_aligned, w_regrouped, conv_w):
    b, t, _ = x.shape
    tm = PROJ_TM
    row = lambda width: pl.BlockSpec((1, tm, width), lambda bi, ti: (bi, ti, 0))
    const = lambda shape: pl.BlockSpec(shape, lambda bi, ti: (0,) * len(shape))
    return pl.pallas_call(
        _proj_kernel,
        grid=(b, t // tm),
        in_specs=[
            row(D_MODEL),
            pl.BlockSpec((1, 1, 6 * D_MODEL), lambda bi, ti: (bi, 0, 0)),
            const((1, D_MODEL)),
            const((D_MODEL, PROJ_ALIGNED)),
            const((D_MODEL, PROJ_REGROUPED)),
            const((CONV_WIDTH, QKV_W)),
            const((GDN_WIDTH, GDN_WIDTH)),
        ],
        out_specs=[row(w) for w in PROJ_SPLITS],
        out_shape=[jax.ShapeDtypeStruct((b, t, w), F32) for w in PROJ_SPLITS],
        scratch_shapes=[pltpu.VMEM((8 + tm, GDN_WIDTH), F32)] * 3,
        compiler_params=pltpu.CompilerParams(
            dimension_semantics=("arbitrary", "arbitrary"), vmem_limit_bytes=VMEM_LIMIT),
        name="proj",
    )(x, mod3, norm_w, w_aligned, w_regrouped, conv_w, _head_sum_matrix(GDN_WIDTH))


GDN_UNITS = (GDN_TT // CHUNK) * (GDN_HEADS // HEADS_PER_GROUP)


def _block_diag_rows(p, same_head):
    stacked = jnp.concatenate([p.astype(BF16)] * HEADS_PER_GROUP, axis=0)
    return jnp.where(same_head, stacked, jnp.zeros((), BF16))


_DONE = object()


def _gdn_kernel(qkv_ref, z_ref, ab_ref, alog_ref, dtb_ref, gnw_ref, bd_ref, eg_ref, eb_ref, ltri_ref,
                wg_ref, wu_ref, wd_ref, wo_ref,
                sq_ref, kvc_ref, kvp_ref, qw_ref, kw_ref, bdk_ref, bias_ref, sink_ref,
                o_ref, wg16_ref, wu16_ref, wd16_ref, wo16_ref, os_ref,
                s_ref, obuf, vw_buf, wq_buf, u_buf, qk_buf, kd_buf, dec_buf, *, tiles_per_seq):
    tt = GDN_TT
    step = pl.program_id(0)
    @pl.when(step < CAST_STEPS)
    def _():
        for src, dst in ((wg_ref, wg16_ref), (wu_ref, wu16_ref), (wd_ref, wd16_ref), (wo_ref, wo16_ref)):
            dst[...] = src[...].astype(BF16)

    wslot = step % 2
    rslot = 1 - wslot
    starts_sequence = (step - 1) % tiles_per_seq == 0

    @pl.when(step == 0)
    def _():
        for buf in (s_ref, wq_buf, u_buf, qk_buf, kd_buf, dec_buf):
            buf[...] = jnp.zeros_like(buf)

    bd = bd_ref[...]
    q = qkv_ref[0, :, 0:GDN_WIDTH]
    k = qkv_ref[0, :, GDN_WIDTH:2 * GDN_WIDTH]
    v = qkv_ref[0, :, 2 * GDN_WIDTH:]

    ab = ab_ref[0]
    lane = lax.broadcasted_iota(jnp.int32, (1, GATE_PAD), 1)
    xs = ab + dtb_ref[...]
    softplus = jnp.maximum(xs, 0.0) + jnp.log1p(jnp.exp(-jnp.abs(xs)))
    g = jnp.where(lane < GDN_HEADS, -jnp.exp(alog_ref[...]) * softplus, 0.0)
    beta = _sigmoid(ab)
    ltri = ltri_ref[...]
    g1, g2 = _split2(g)
    gcum = _dot(ltri, g1) + _dot(ltri, g2)
    eg = eg_ref[...]
    c1, c2 = _split2(gcum)
    g_x = _dot(c1, eg) + _dot(c2, eg)
    beta_x = _dot(beta.astype(BF16), eb_ref[...])

    r64 = lax.broadcasted_iota(jnp.int32, (CHUNK, GROUP_W), 0)
    l64 = lax.broadcasted_iota(jnp.int32, (CHUNK, GROUP_W), 1) % CHUNK
    causal = r64 >= l64
    eye_x = (r64 == l64).astype(F32)
    strict_x = (r64 > l64).astype(F32)
    same_block = lambda size: (r64 // size) == (l64 // size)
    base_x = same_block(INV_BASE).astype(F32)
    level_x = [(same_block(2 * size) & ~same_block(size)).astype(F32) for size in INV_LEVELS]
    rb = lax.broadcasted_iota(jnp.int32, (GROUP_W, GROUP_W), 0) // HEAD_DIM
    cb = lax.broadcasted_iota(jnp.int32, (GROUP_W, GROUP_W), 1) // HEAD_DIM
    same_head = rb == cb
    mask_bd = same_head.astype(F32)

    n_groups = GDN_HEADS // HEADS_PER_GROUP
    n_chunks = tt // CHUNK
    units = [(c, gi) for c in range(n_chunks) for gi in range(n_groups)]
    ids = list(range(len(units)))
    tile = lambda a: [a[c * CHUNK:(c + 1) * CHUNK, gi * GROUP_W:(gi + 1) * GROUP_W] for c, gi in units]
    each = lambda f, *lists: [f(*args) for args in zip(*lists)]
    bdr = lambda t: _block_diag_rows(t, same_head)
    mm = lambda a, w16: _dot(a.astype(BF16), w16)
    stack = lambda a, b_: jnp.concatenate([a, b_], axis=0)

    def chunk_parallel_part():
        kc, qc, vc, bx, gx = tile(k), tile(q), tile(v), tile(beta_x), tile(g_x)
        eg_c = each(jnp.exp, gx)
        glast = each(lambda g_: g_[CHUNK - 1:CHUNK, :], gx)
        kb = each(jnp.multiply, kc, bx)
        for i in ids:
            vw_buf[i, :, 0:GROUP_W] = (vc[i] * bx[i]).astype(BF16)
            vw_buf[i, :, GROUP_W:] = (kb[i] * eg_c[i]).astype(BF16)
            wq_buf[wslot, i, CHUNK:] = (qc[i] * eg_c[i]).astype(BF16)
            kd_buf[wslot, i] = (kc[i] * jnp.exp(glast[i] - gx[i])).astype(BF16)
            dec_buf[wslot, i] = jnp.broadcast_to(jnp.exp(glast[i]), (8, GROUP_W))
        dm = each(lambda g_: jnp.exp(jnp.where(causal, g_ - jnp.sum(g_ * eye_x, axis=0, keepdims=True), NEG_BIG)),
                  gx)
        yield
        aq = each(lambda i, a, b_, k_: _dot_nt(stack(a, b_).astype(BF16), bdr(k_)), ids, kb, qc, kc)
        xm = each(lambda a, d_: -(a[0:CHUNK] * d_ * strict_x), aq, dm)
        for i in ids:
            qk_buf[wslot, i] = (aq[i][CHUNK:] * dm[i]).astype(BF16)
        xd = each(lambda x_: x_ * base_x, xm)
        tinv = each(lambda x_: eye_x + x_, xd)
        yield
        pw = each(lambda i, x_: mm(x_, bdr(x_)), ids, xd)
        yield
        for _ in range(INV_BASE.bit_length() - 3):
            r = each(lambda i, t_, p_: mm(stack(t_, p_), bdr(p_)), ids, tinv, pw)
            tinv = each(lambda t_, r_: t_ + r_[0:CHUNK], tinv, r)
            pw = each(lambda r_: r_[CHUNK:], r)
            yield
        tinv = each(lambda i, t_, p_: t_ + mm(t_, bdr(p_)), ids, tinv, pw)
        yield
        for lm in level_x:
            e = each(lambda i, x_, t_: mm(x_ * lm, bdr(t_)), ids, xm, tinv)
            yield
            tinv = each(lambda i, t_, e_: t_ + mm(t_, bdr(e_)), ids, tinv, e)
            yield
        for i in ids:
            operands = jnp.concatenate([bdr(vw_buf[i, :, 0:GROUP_W]), bdr(vw_buf[i, :, GROUP_W:])], axis=1)
            uw = mm(tinv[i], operands)
            u_buf[wslot, i] = uw[:, 0:GROUP_W]
            wq_buf[wslot, i, 0:CHUNK] = uw[:, GROUP_W:].astype(BF16)

    def sequential_part():
        s_state = [jnp.where(starts_sequence, 0.0, s_ref[gi]) for gi in range(n_groups)]
        for c in range(n_chunks):
            cids = [c * n_groups + gi for gi in range(n_groups)]
            r2 = [_dot(wq_buf[rslot, i], s_state[gi].astype(BF16)) for gi, i in enumerate(cids)]
            yield
            vn = [u_buf[rslot, i] - r2[gi][0:CHUNK] for gi, i in enumerate(cids)]
            o = [r2[gi][CHUNK:] + _dot(qk_buf[rslot, i], bdr(vn[gi])) for gi, i in enumerate(cids)]
            s_state = [s_state[gi] * dec_buf[rslot, i][0:1] + mask_bd * _dot_tn(kd_buf[rslot, i], vn[gi].astype(BF16))
                       for gi, i in enumerate(cids)]
            for gi in range(n_groups):
                obuf[c * CHUNK:(c + 1) * CHUNK, gi * GROUP_W:(gi + 1) * GROUP_W] = o[gi]
            yield
        for gi in range(n_groups):
            s_ref[gi] = s_state[gi]

    swa_tile = jnp.minimum(step, pl.num_programs(0) - 2)
    swa_first = jnp.where(swa_tile % tiles_per_seq == 0, 1, 0)
    swa_parts = _swa_block_parts(sq_ref, kvc_ref, kvp_ref, qw_ref, kw_ref, bd, bdk_ref, bias_ref, sink_ref, os_ref,
                                 swa_first)
    parts = [chunk_parallel_part(), sequential_part()] + swa_parts
    while parts:
        parts = [p for p in parts if next(p, _DONE) is not _DONE]

    o = obuf[...]
    ms = _dot((o * o).astype(BF16), bd) * (1.0 / HEAD_DIM)
    zz = z_ref[0]
    o_ref[0] = (o * lax.rsqrt(ms + EPS) * gnw_ref[...] * (zz * _sigmoid(zz))).astype(BF16)


def _gdn_consts():
    h = np.arange(GDN_WIDTH) // HEAD_DIM
    eg = np.zeros((GATE_PAD, GDN_WIDTH), np.float32)
    eb = np.zeros((GATE_PAD, GDN_WIDTH), np.float32)
    eg[h, np.arange(GDN_WIDTH)] = 1.0
    eb[GDN_HEADS + h, np.arange(GDN_WIDTH)] = 1.0
    t = np.arange(GDN_TT)
    ltri = ((t[:, None] // CHUNK == t[None, :] // CHUNK) & (t[:, None] >= t[None, :])).astype(np.float32)
    return (_head_sum_matrix(GDN_WIDTH), jnp.asarray(eg, BF16), jnp.asarray(eb, BF16), jnp.asarray(ltri, BF16))


BF16_ROWS = 16
CAST_STEPS = 2 * GDN_HEADS


def _swa_head_source(i):
    j = i - GDN_HEADS
    return jnp.where(i < GDN_HEADS, i, GDN_HEADS + (j % 2) * SWA_GROUP + j // 2)


def _cast_specs(weight, unit, steps, source=None):
    rows, cols = weight.shape
    block_rows = next(r for r in range(unit, rows + 1, unit) if rows % r == 0 and rows // r <= steps)
    if source is None:
        source = lambda i: i
    else:
        assert block_rows == unit, "row regrouping needs one unit per block"
    last = rows // block_rows - 1
    src = pl.BlockSpec((block_rows, cols), lambda n: (source(jnp.minimum(n, last)), 0))
    dst = pl.BlockSpec((block_rows, cols), lambda n: (jnp.minimum(n, last), 0))
    return src, dst, jax.ShapeDtypeStruct((rows, cols), BF16)


def _gdn_call(qkv, z, ab, alog_pad, dtb_pad, gnw_x, w_gate, w_up, w_down, w_out, sq, skv, qw_x, kw_x, sink_col):
    b, t, _ = qkv.shape
    tt = GDN_TT
    bd, eg, eb, ltri = _gdn_consts()
    bias, bdk = _swa_consts()
    n_tiles = t // tt
    n_groups = GDN_HEADS // HEADS_PER_GROUP
    total = b * n_tiles

    def tile_block(width, lag):
        def index(n):
            m = jnp.clip(n - lag, 0, total - 1)
            return (m // n_tiles, m % n_tiles, 0)
        return pl.BlockSpec((1, tt, width), index)

    def window_before(n):
        m = jnp.clip(n, 0, total - 1)
        return (m // n_tiles, jnp.maximum((m % n_tiles) * (tt // WINDOW) - 1, 0), 0)

    ahead = lambda width: tile_block(width, 0)
    behind = lambda width: tile_block(width, 1)
    const = lambda shape: pl.BlockSpec(shape, lambda n: (0,) * len(shape))
    per_unit = lambda rows, dtype: pltpu.VMEM((2, GDN_UNITS, rows, GROUP_W), dtype)
    steps = total + 1
    assert steps >= CAST_STEPS
    casts = [
        _cast_specs(w_gate, BF16_ROWS, CAST_STEPS),
        _cast_specs(w_up, BF16_ROWS, CAST_STEPS),
        _cast_specs(w_down, BF16_ROWS, CAST_STEPS),
        _cast_specs(w_out, HEAD_DIM, CAST_STEPS, _swa_head_source),
    ]
    return pl.pallas_call(
        functools.partial(_gdn_kernel, tiles_per_seq=n_tiles),
        grid=(steps,),
        in_specs=[
            ahead(QKV_W), behind(GDN_WIDTH), ahead(GATE_PAD),
            const((1, GATE_PAD)), const((1, GATE_PAD)), const((1, GDN_WIDTH)),
            const((GDN_WIDTH, GDN_WIDTH)), const((GATE_PAD, GDN_WIDTH)), const((GATE_PAD, GDN_WIDTH)),
            const((tt, tt)),
        ] + [c[0] for c in casts] + [
            ahead(SWA_WIDTH), ahead(2 * SWA_KV_WIDTH),
            pl.BlockSpec((1, WINDOW, 2 * SWA_KV_WIDTH), window_before),
            const((1, SWA_WIDTH)), const((1, SWA_KV_WIDTH)), const((SWA_KV_WIDTH, SWA_KV_WIDTH)),
            const((2, WINDOW, SWA_ROWS)), const((1, SWA_ROWS)),
        ],
        out_specs=[behind(GDN_WIDTH)] + [c[1] for c in casts] + [ahead(SWA_WIDTH)],
        out_shape=[jax.ShapeDtypeStruct((b, t, GDN_WIDTH), BF16)] + [c[2] for c in casts]
        + [jax.ShapeDtypeStruct((b, t, SWA_WIDTH), BF16)],
        scratch_shapes=[
            pltpu.VMEM((n_groups, GROUP_W, GROUP_W), F32),
            pltpu.VMEM((tt, GDN_WIDTH), F32),
            pltpu.VMEM((GDN_UNITS, CHUNK, 2 * GROUP_W), BF16),
            per_unit(2 * CHUNK, BF16), per_unit(CHUNK, F32), per_unit(CHUNK, BF16), per_unit(CHUNK, BF16),
            per_unit(8, F32),
        ],
        compiler_params=pltpu.CompilerParams(
            dimension_semantics=("arbitrary",), vmem_limit_bytes=VMEM_LIMIT),
        name="gdn",
    )(qkv, z, ab, alog_pad, dtb_pad, gnw_x, bd, eg, eb, ltri, w_gate, w_up, w_down, w_out,
      sq, skv, skv, qw_x, kw_x, bdk, bias, sink_col)


SWA_PAIRS = SWA_Q_HEADS // 2
SWA_ROWS = SWA_Q_HEADS * WINDOW
LOG2E = 1.4426950408889634


def _swa_block_parts(q_ref, kvc_ref, kvp_ref, qw_ref, kw_ref, bdq, bdk_ref, bias_ref, sink_ref, o_ref, first):
    tq = q_ref.shape[1]
    q = q_ref[0]
    q = q * lax.rsqrt(_dot((q * q).astype(BF16), bdq) * (1.0 / HEAD_DIM) + EPS)
    q = q * (qw_ref[...] * (HEAD_DIM ** -0.5 * LOG2E))
    kv = jnp.concatenate([kvp_ref[0], kvc_ref[0]], axis=0)
    k = kv[:, 0:SWA_KV_WIDTH]
    v = kv[:, SWA_KV_WIDTH:]
    k = k * lax.rsqrt(_dot((k * k).astype(BF16), bdk_ref[...]) * (1.0 / HEAD_DIM) + EPS) * kw_ref[...]
    k16 = k.astype(BF16)
    lo = lax.broadcasted_iota(jnp.int32, (1, LANES), 1) < HEAD_DIM
    qcol = lax.broadcasted_iota(jnp.int32, (WINDOW, SWA_ROWS), 1) % WINDOW
    from_prev = lax.broadcasted_iota(jnp.int32, (WINDOW, SWA_ROWS), 0) > qcol
    sink = sink_ref[...]
    vt16 = v.T.astype(BF16)
    zero = jnp.zeros((), BF16)
    def block(j):
        qj = q[j * WINDOW:(j + 1) * WINDOW]
        parts = []
        for p in range(SWA_PAIRS):
            qp = qj[:, p * LANES:(p + 1) * LANES]
            parts += [jnp.where(lo, qp, 0.0), jnp.where(lo, 0.0, qp)]
        qs = jnp.concatenate(parts, axis=0).astype(BF16)
        st = _dot_nt(k16[j * WINDOW:(j + 2) * WINDOW], qs)
        yield
        bias = bias_ref[first] if j == 0 else bias_ref[0]
        sm = jnp.where(from_prev, st[0:WINDOW], st[WINDOW:]) + bias
        m = jnp.max(sm, axis=0, keepdims=True)
        yield
        pe = jnp.exp2(sm - m)
        den = jnp.sum(pe, axis=0, keepdims=True) + jnp.exp2(sink - m)
        pb = pe.astype(BF16)
        pt2 = jnp.concatenate([jnp.where(from_prev, pb, zero), jnp.where(from_prev, zero, pb)], axis=0)
        yield
        ot = _dot(vt16[:, j * WINDOW:(j + 2) * WINDOW], pt2) * (1.0 / den)
        yield
        for p in range(SWA_PAIRS):
            c0 = p * 2 * WINDOW
            pair_t = jnp.concatenate([ot[0:HEAD_DIM, c0:c0 + WINDOW],
                                      ot[HEAD_DIM:, c0 + WINDOW:c0 + 2 * WINDOW]], axis=0)
            o_ref[0, j * WINDOW:(j + 1) * WINDOW, p * LANES:(p + 1) * LANES] = pair_t.T.astype(BF16)

    return [block(j) for j in range(tq // WINDOW)]


def _swa_consts():
    qi = np.arange(WINDOW)[:, None]
    kj = np.arange(WINDOW)[None, :]
    from_prev = kj > qi
    dist = np.where(from_prev, qi + WINDOW - kj, qi - kj).astype(np.float32)
    slopes = 2.0 ** (-8.0 * (np.arange(SWA_Q_HEADS, dtype=np.float32) + 1.0) / SWA_Q_HEADS)
    bias = np.zeros((2, SWA_PAIRS, 2, WINDOW, WINDOW), np.float32)
    for first in range(2):
        for p in range(SWA_PAIRS):
            for half, head in enumerate((p, SWA_GROUP + p)):
                b = (-slopes[head].astype(np.float32) * dist) * np.float32(LOG2E)
                bias[first, p, half] = np.where(from_prev & bool(first), np.float32(NEG_BIG), b)
    bias = bias.reshape(2, SWA_ROWS, WINDOW).transpose(0, 2, 1)
    hk = np.arange(SWA_KV_WIDTH) // HEAD_DIM
    bdk = (hk[:, None] == hk[None, :]).astype(np.float32)
    return jnp.asarray(bias), jnp.asarray(bdk, BF16)


def _ffn_kernel(x_ref, og_ref, os_ref, mod_ref, nw_ref, wo_ref, wg_ref, wu_ref, wd_ref, o_ref):
    x = x_ref[0]
    mod = mod_ref[0]
    gate1 = mod[:, 2 * D_MODEL:3 * D_MODEL]
    shift2 = mod[:, 3 * D_MODEL:4 * D_MODEL]
    scale2 = mod[:, 4 * D_MODEL:5 * D_MODEL]
    gate2 = mod[:, 5 * D_MODEL:]
    gain = nw_ref[...] * (1.0 + scale2)
    blocks = [slice(r * FFN_HEAD_ROWS, (r + 1) * FFN_HEAD_ROWS) for r in range(FFN_TM // FFN_HEAD_ROWS)]
    mixed = [_dot(og_ref[0, rows, :], wo_ref[0:GDN_WIDTH, :]) + _dot(os_ref[0, rows, :], wo_ref[GDN_WIDTH:, :])
             for rows in blocks]
    x1_blocks, hb_blocks = [], []
    for rows, m_ in zip(blocks, mixed):
        x1_ = x[rows] + gate1 * m_
        ms = jnp.mean(x1_ * x1_, axis=-1, keepdims=True)
        x1_blocks.append(x1_)
        hb_blocks.append((x1_ * lax.rsqrt(ms + EPS) * gain + shift2).astype(BF16))
    x1 = jnp.concatenate(x1_blocks, axis=0)
    hb = jnp.concatenate(hb_blocks, axis=0)
    gt = _dot(hb, wg_ref[...])
    up = _dot(hb, wu_ref[...])
    act = ((gt * _sigmoid(gt)) * up).astype(BF16)
    o_ref[0] = x1 + gate2 * _dot(act, wd_ref[...])


def _ffn_call(x, og, osw, mod3, norm_w, wo, wg, wu, wd):
    b, t, _ = x.shape
    tm = FFN_TM
    row = lambda width: pl.BlockSpec((1, tm, width), lambda bi, ti: (bi, ti, 0))
    const = lambda shape: pl.BlockSpec(shape, lambda bi, ti: (0,) * len(shape), pipeline_mode=pl.Buffered(1))
    return pl.pallas_call(
        _ffn_kernel,
        grid=(b, t // tm),
        in_specs=[
            row(D_MODEL), row(GDN_WIDTH), row(SWA_WIDTH),
            pl.BlockSpec((1, 1, 6 * D_MODEL), lambda bi, ti: (bi, 0, 0)),
            const((1, D_MODEL)),
            const((GDN_WIDTH + SWA_WIDTH, D_MODEL)),
            const((D_MODEL, D_FF)), const((D_MODEL, D_FF)), const((D_FF, D_MODEL)),
        ],
        out_specs=row(D_MODEL),
        out_shape=jax.ShapeDtypeStruct((b, t, D_MODEL), F32),
        compiler_params=pltpu.CompilerParams(
            dimension_semantics=("arbitrary", "arbitrary"), vmem_limit_bytes=VMEM_LIMIT),
        name="ffn",
    )(x, og, osw, mod3, norm_w, wo, wg, wu, wd)


def _layer(x, c_pad, w_ada, b_ada, norm1_w, w_in, conv_w, a_log, dt_bias, gdn_norm_w, q_norm_w, k_norm_w, sinks,
           w_out, norm2_w, w_gate, w_up, w_down):
    b = x.shape[0]
    mod, w_aligned, w_regrouped = _ada_call(c_pad, w_ada, b_ada.reshape(1, -1), w_in.T)
    mod3 = mod[:b].reshape(b, 1, 6 * D_MODEL)
    qkv, z, sq, skv, ab = _proj_call(x, mod3, norm1_w.reshape(1, D_MODEL), w_aligned, w_regrouped,
                                     conv_w.reshape(CONV_WIDTH, QKV_W))

    alog_pad = jnp.pad(a_log.reshape(1, GDN_HEADS), ((0, 0), (0, GATE_PAD - GDN_HEADS)))
    dtb_pad = jnp.pad(dt_bias.reshape(1, GDN_HEADS), ((0, 0), (0, GATE_PAD - GDN_HEADS)))
    gnw_x = jnp.tile(gdn_norm_w.reshape(1, HEAD_DIM), (1, GDN_HEADS))
    qw_x = jnp.tile(q_norm_w.reshape(1, HEAD_DIM), (1, SWA_Q_HEADS))
    kw_x = jnp.tile(k_norm_w.reshape(1, HEAD_DIM), (1, SWA_KV_HEADS))
    sink_pairs = jnp.stack([sinks[:SWA_GROUP], sinks[SWA_GROUP:]], axis=1)
    sink_col = (jnp.repeat(sink_pairs, WINDOW, axis=1) * LOG2E).reshape(1, SWA_ROWS)
    og, wg16, wu16, wd16, wo16, osw = _gdn_call(qkv, z, ab, alog_pad, dtb_pad, gnw_x, w_gate, w_up, w_down, w_out,
                                                sq, skv, qw_x, kw_x, sink_col)

    return _ffn_call(x, og, osw, mod3, norm2_w.reshape(1, D_MODEL), wo16, wg16, wu16, wd16)


def kernel(x, c, w_ada, b_ada, norm1_w, w_in, conv_w, a_log, dt_bias, gdn_norm_w, q_norm_w, k_norm_w, sinks,
           w_out, norm2_w, w_gate, w_up, w_down):
    depth = w_ada.shape[0]
    b = c.shape[0]
    c_pad = jnp.pad(c, ((0, 8 - b), (0, 0)))
    for l in range(depth):
        x = _layer(x, c_pad, w_ada[l], b_ada[l], norm1_w[l], w_in[l], conv_w[l], a_log[l], dt_bias[l],
                   gdn_norm_w[l], q_norm_w[l], k_norm_w[l], sinks[l], w_out[l], norm2_w[l], w_gate[l], w_up[l],
                   w_down[l])
    return x
```

```python
import functools

import numpy as np
import jax
import jax.numpy as jnp
from jax import lax
from jax.experimental import pallas as pl
from jax.experimental.pallas import tpu as pltpu

F32 = jnp.float32
BF16 = jnp.bfloat16

D_MODEL = 1024
HEAD_DIM = 64
GDN_HEADS = 8
GDN_WIDTH = GDN_HEADS * HEAD_DIM
SWA_Q_HEADS = 8
SWA_KV_HEADS = 2
SWA_GROUP = SWA_Q_HEADS // SWA_KV_HEADS
SWA_WIDTH = SWA_Q_HEADS * HEAD_DIM
SWA_KV_WIDTH = SWA_KV_HEADS * HEAD_DIM
WINDOW = 128
CONV_WIDTH = 4
CHUNK = 64
D_FF = 2816
EPS = 1e-6
LANES = 128
GATE_PAD = LANES
HEADS_PER_GROUP = 2
GROUP_W = HEADS_PER_GROUP * HEAD_DIM
INV_BASE = 16
INV_LEVELS = (16, 32)
NEG_BIG = -1e30
VMEM_LIMIT = 56 * 1024 * 1024

PROJ_TM = 512
GDN_TT = 256
SWA_TQ = 1024
FFN_TM = 512
FFN_HEAD_ROWS = 256
ADA_TN = 1536


def _sigmoid(x):
    return 1.0 / (1.0 + jnp.exp(-x))


def _dot(a, b):
    return jnp.dot(a, b, preferred_element_type=F32)


def _dot_nt(a, b):
    return lax.dot_general(a, b, (((1,), (1,)), ((), ())), preferred_element_type=F32)


def _dot_tn(a, b):
    return lax.dot_general(a, b, (((0,), (0,)), ((), ())), preferred_element_type=F32)


def _split2(x):
    hi = x.astype(BF16)
    lo = (x - hi.astype(F32)).astype(BF16)
    return hi, lo


def _ada_kernel(c_ref, w_ref, b_ref, win_ref, tail_ref, o_ref, wa16_ref, wb16_ref, *, tail_base):
    c = c_ref[...]
    ca = c * _sigmoid(c)
    o_ref[...] = _dot(ca.astype(BF16), w_ref[...].astype(BF16)) + b_ref[...]

    wa16_ref[...] = win_ref[...].T.astype(BF16)

    @pl.when(pl.program_id(0) == 0)
    def _():
        o1 = tail_base + 2 * GDN_HEADS
        o2 = o1 + SWA_WIDTH
        head = lambda h: tail_ref[o1 + h * HEAD_DIM:o1 + (h + 1) * HEAD_DIM, :]
        pieces = [head(half * SWA_GROUP + p) for p in range(SWA_PAIRS) for half in range(2)]
        pieces.append(tail_ref[o2:o2 + 2 * SWA_KV_WIDTH, :])
        pieces.append(tail_ref[tail_base:o1, :])
        pieces.append(jnp.zeros((GATE_PAD - 2 * GDN_HEADS, D_MODEL), F32))
        wb16_ref[...] = jnp.concatenate(pieces, axis=0).T.astype(BF16)


def _ada_call(c_pad, w_ada, b_ada, w_in):
    n = w_ada.shape[1]
    steps = n // ADA_TN
    rows = w_in.shape[0]
    head_rows = PROJ_ALIGNED // steps
    assert PROJ_ALIGNED % steps == 0 and head_rows % LANES == 0
    tail_rows = next(r for r in range(8, rows + 1, 8) if rows % r == 0 and rows - r <= PROJ_ALIGNED)
    whole = lambda shape: pl.BlockSpec(shape, lambda j: (0, 0))
    return pl.pallas_call(
        functools.partial(_ada_kernel, tail_base=PROJ_ALIGNED - (rows - tail_rows)),
        grid=(steps,),
        in_specs=[
            whole((8, D_MODEL)),
            pl.BlockSpec((D_MODEL, ADA_TN), lambda j: (0, j)),
            pl.BlockSpec((1, ADA_TN), lambda j: (0, j)),
            pl.BlockSpec((head_rows, D_MODEL), lambda j: (j, 0)),
            pl.BlockSpec((tail_rows, D_MODEL), lambda j: (rows // tail_rows - 1, 0), pipeline_mode=pl.Buffered(1)),
        ],
        out_specs=[pl.BlockSpec((8, ADA_TN), lambda j: (0, j)),
                   pl.BlockSpec((D_MODEL, head_rows), lambda j: (0, j)), whole((D_MODEL, PROJ_REGROUPED))],
        out_shape=[jax.ShapeDtypeStruct((8, n), F32),
                   jax.ShapeDtypeStruct((D_MODEL, PROJ_ALIGNED), BF16),
                   jax.ShapeDtypeStruct((D_MODEL, PROJ_REGROUPED), BF16)],
        compiler_params=pltpu.CompilerParams(
            dimension_semantics=("arbitrary",), vmem_limit_bytes=VMEM_LIMIT),
        name="ada",
    )(c_pad, w_ada, b_ada, w_in, w_in)


QKV_W = 3 * GDN_WIDTH
PROJ_SPLITS = (QKV_W, GDN_WIDTH, SWA_WIDTH, 2 * SWA_KV_WIDTH, GATE_PAD)
PROJ_ALIGNED = QKV_W + GDN_WIDTH
PROJ_REGROUPED = sum(PROJ_SPLITS) - PROJ_ALIGNED


def _proj_kernel(x_ref, mod_ref, nw_ref, wa_ref, wb_ref, cw_ref, bd_ref, qkv_ref, z_ref, sq_ref, skv_ref, ab_ref,
                 qbuf, kbuf, vbuf):
    tm = PROJ_TM
    bufs = (qbuf, kbuf, vbuf)

    @pl.when(pl.program_id(1) == 0)
    def _():
        for buf in bufs:
            buf[0:8, :] = jnp.zeros((8, GDN_WIDTH), F32)

    x = x_ref[0]
    ms = jnp.mean(x * x, axis=-1, keepdims=True)
    mod = mod_ref[0]
    shift = mod[:, 0:D_MODEL]
    gain = nw_ref[...] * (1.0 + mod[:, D_MODEL:2 * D_MODEL])
    hb = (x * lax.rsqrt(ms + EPS) * gain + shift).astype(BF16)

    def project(s):
        bufs[s][8:8 + tm, :] = _dot(hb, wa_ref[:, s * GDN_WIDTH:(s + 1) * GDN_WIDTH])

    def conv_silu(s):
        buf = bufs[s]
        cw = lambda j: cw_ref[j:j + 1, s * GDN_WIDTH:(s + 1) * GDN_WIDTH]
        acc = cw(CONV_WIDTH - 1) * buf[8:8 + tm, :]
        for j in range(CONV_WIDTH - 1):
            off = 8 - (CONV_WIDTH - 1) + j
            acc = acc + cw(j) * buf[off:off + tm, :]
        buf[0:8, :] = buf[tm:tm + 8, :]
        return acc * _sigmoid(acc)

    def l2_normed(a, scale):
        return a * lax.rsqrt(_dot((a * a).astype(BF16), bd_ref[...]) + EPS) * scale

    def rest(ref, w_ref, c0):
        ref[0] = _dot(hb, w_ref[:, c0:c0 + ref.shape[-1]])

    project(0)
    rest(z_ref, wa_ref, QKV_W)
    project(1)
    qkv_ref[0, :, 0:GDN_WIDTH] = l2_normed(conv_silu(0), HEAD_DIM ** -0.5)
    rest(sq_ref, wb_ref, 0)
    project(2)
    rest(skv_ref, wb_ref, SWA_WIDTH)
    qkv_ref[0, :, GDN_WIDTH:2 * GDN_WIDTH] = l2_normed(conv_silu(1), 1.0)
    rest(ab_ref, wb_ref, SWA_WIDTH + 2 * SWA_KV_WIDTH)
    qkv_ref[0, :, 2 * GDN_WIDTH:] = conv_silu(2)


def _head_sum_matrix(width):
    h = np.arange(width) // HEAD_DIM
    return jnp.asarray((h[:, None] == h[None, :]).astype(np.float32), BF16)


def _proj_call(x, mod3, norm_w, w_aligned, w_regrouped, conv_w):
    b, t, _ = x.shape
    tm = PROJ_TM
    row = lambda width: pl.BlockSpec((1, tm, width), lambda bi, ti: (bi, ti, 0))
    const = lambda shape: pl.BlockSpec(shape, lambda bi, ti: (0,) * len(shape))
    return pl.pallas_call(
        _proj_kernel,
        grid=(b, t // tm),
        in_specs=[
            row(D_MODEL),
            pl.BlockSpec((1, 1, 6 * D_MODEL), lambda bi, ti: (bi, 0, 0)),
            const((1, D_MODEL)),
            const((D_MODEL, PROJ_ALIGNED)),
            const((D_MODEL, PROJ_REGROUPED)),
            const((CONV_WIDTH, QKV_W)),
            const((GDN_WIDTH, GDN_WIDTH)),
        ],
        out_specs=[row(w) for w in PROJ_SPLITS],
        out_shape=[jax.ShapeDtypeStruct((b, t, w), F32) for w in PROJ_SPLITS],
        scratch_shapes=[pltpu.VMEM((8 + tm, GDN_WIDTH), F32)] * 3,
        compiler_params=pltpu.CompilerParams(
            dimension_semantics=("arbitrary", "arbitrary"), vmem_limit_bytes=VMEM_LIMIT),
        name="proj",
    )(x, mod3, norm_w, w_aligned, w_regrouped, conv_w, _head_sum_matrix(GDN_WIDTH))


GDN_UNITS = (GDN_TT // CHUNK) * (GDN_HEADS // HEADS_PER_GROUP)


def _block_diag_rows(p, same_head):
    stacked = jnp.concatenate([p.astype(BF16)] * HEADS_PER_GROUP, axis=0)
    return jnp.where(same_head, stacked, jnp.zeros((), BF16))


_DONE = object()


def _gdn_kernel(qkv_ref, z_ref, ab_ref, alog_ref, dtb_ref, gnw_ref, bd_ref, eg_ref, eb_ref, ltri_ref,
                wg_ref, wu_ref, wd_ref, wo_ref, o_ref, wg16_ref, wu16_ref, wd16_ref, wo16_ref,
                s_ref, obuf, vw_buf, wq_buf, u_buf, qk_buf, kd_buf, dec_buf, *, tiles_per_seq):
    tt = GDN_TT
    step = pl.program_id(0)
    @pl.when(step < CAST_STEPS)
    def _():
        for src, dst in ((wg_ref, wg16_ref), (wu_ref, wu16_ref), (wd_ref, wd16_ref), (wo_ref, wo16_ref)):
            dst[...] = src[...].astype(BF16)

    wslot = step % 2
    rslot = 1 - wslot
    starts_sequence = (step - 1) % tiles_per_seq == 0

    @pl.when(step == 0)
    def _():
        for buf in (s_ref, wq_buf, u_buf, qk_buf, kd_buf, dec_buf):
            buf[...] = jnp.zeros_like(buf)

    bd = bd_ref[...]
    q = qkv_ref[0, :, 0:GDN_WIDTH]
    k = qkv_ref[0, :, GDN_WIDTH:2 * GDN_WIDTH]
    v = qkv_ref[0, :, 2 * GDN_WIDTH:]

    ab = ab_ref[0]
    lane = lax.broadcasted_iota(jnp.int32, (1, GATE_PAD), 1)
    xs = ab + dtb_ref[...]
    softplus = jnp.maximum(xs, 0.0) + jnp.log1p(jnp.exp(-jnp.abs(xs)))
    g = jnp.where(lane < GDN_HEADS, -jnp.exp(alog_ref[...]) * softplus, 0.0)
    beta = _sigmoid(ab)
    ltri = ltri_ref[...]
    g1, g2 = _split2(g)
    gcum = _dot(ltri, g1) + _dot(ltri, g2)
    eg = eg_ref[...]
    c1, c2 = _split2(gcum)
    g_x = _dot(c1, eg) + _dot(c2, eg)
    beta_x = _dot(beta.astype(BF16), eb_ref[...])

    r64 = lax.broadcasted_iota(jnp.int32, (CHUNK, GROUP_W), 0)
    l64 = lax.broadcasted_iota(jnp.int32, (CHUNK, GROUP_W), 1) % CHUNK
    causal = r64 >= l64
    eye_x = (r64 == l64).astype(F32)
    strict_x = (r64 > l64).astype(F32)
    same_block = lambda size: (r64 // size) == (l64 // size)
    base_x = same_block(INV_BASE).astype(F32)
    level_x = [(same_block(2 * size) & ~same_block(size)).astype(F32) for size in INV_LEVELS]
    rb = lax.broadcasted_iota(jnp.int32, (GROUP_W, GROUP_W), 0) // HEAD_DIM
    cb = lax.broadcasted_iota(jnp.int32, (GROUP_W, GROUP_W), 1) // HEAD_DIM
    same_head = rb == cb
    mask_bd = same_head.astype(F32)

    n_groups = GDN_HEADS // HEADS_PER_GROUP
    n_chunks = tt // CHUNK
    units = [(c, gi) for c in range(n_chunks) for gi in range(n_groups)]
    ids = list(range(len(units)))
    tile = lambda a: [a[c * CHUNK:(c + 1) * CHUNK, gi * GROUP_W:(gi + 1) * GROUP_W] for c, gi in units]
    each = lambda f, *lists: [f(*args) for args in zip(*lists)]
    bdr = lambda t: _block_diag_rows(t, same_head)
    mm = lambda a, w16: _dot(a.astype(BF16), w16)
    stack = lambda a, b_: jnp.concatenate([a, b_], axis=0)

    def chunk_parallel_part():
        kc, qc, vc, bx, gx = tile(k), tile(q), tile(v), tile(beta_x), tile(g_x)
        eg_c = each(jnp.exp, gx)
        glast = each(lambda g_: g_[CHUNK - 1:CHUNK, :], gx)
        kb = each(jnp.multiply, kc, bx)
        for i in ids:
            vw_buf[i, :, 0:GROUP_W] = (vc[i] * bx[i]).astype(BF16)
            vw_buf[i, :, GROUP_W:] = (kb[i] * eg_c[i]).astype(BF16)
            wq_buf[wslot, i, CHUNK:] = (qc[i] * eg_c[i]).astype(BF16)
            kd_buf[wslot, i] = (kc[i] * jnp.exp(glast[i] - gx[i])).astype(BF16)
            dec_buf[wslot, i] = jnp.broadcast_to(jnp.exp(glast[i]), (8, GROUP_W))
        dm = each(lambda g_: jnp.exp(jnp.where(causal, g_ - jnp.sum(g_ * eye_x, axis=0, keepdims=True), NEG_BIG)),
                  gx)
        yield
        aq = each(lambda i, a, b_, k_: _dot_nt(stack(a, b_).astype(BF16), bdr(k_)), ids, kb, qc, kc)
        xm = each(lambda a, d_: -(a[0:CHUNK] * d_ * strict_x), aq, dm)
        for i in ids:
            qk_buf[wslot, i] = (aq[i][CHUNK:] * dm[i]).astype(BF16)
        xd = each(lambda x_: x_ * base_x, xm)
        tinv = each(lambda x_: eye_x + x_, xd)
        yield
        pw = each(lambda i, x_: mm(x_, bdr(x_)), ids, xd)
        yield
        for _ in range(INV_BASE.bit_length() - 3):
            r = each(lambda i, t_, p_: mm(stack(t_, p_), bdr(p_)), ids, tinv, pw)
            tinv = each(lambda t_, r_: t_ + r_[0:CHUNK], tinv, r)
            pw = each(lambda r_: r_[CHUNK:], r)
            yield
        tinv = each(lambda i, t_, p_: t_ + mm(t_, bdr(p_)), ids, tinv, pw)
        yield
        for lm in level_x:
            e = each(lambda i, x_, t_: mm(x_ * lm, bdr(t_)), ids, xm, tinv)
            yield
            tinv = each(lambda i, t_, e_: t_ + mm(t_, bdr(e_)), ids, tinv, e)
            yield
        for i in ids:
            operands = jnp.concatenate([bdr(vw_buf[i, :, 0:GROUP_W]), bdr(vw_buf[i, :, GROUP_W:])], axis=1)
            uw = mm(tinv[i], operands)
            u_buf[wslot, i] = uw[:, 0:GROUP_W]
            wq_buf[wslot, i, 0:CHUNK] = uw[:, GROUP_W:].astype(BF16)

    def sequential_part():
        s_state = [jnp.where(starts_sequence, 0.0, s_ref[gi]) for gi in range(n_groups)]
        for c in range(n_chunks):
            cids = [c * n_groups + gi for gi in range(n_groups)]
            r2 = [_dot(wq_buf[rslot, i], s_state[gi].astype(BF16)) for gi, i in enumerate(cids)]
            yield
            vn = [u_buf[rslot, i] - r2[gi][0:CHUNK] for gi, i in enumerate(cids)]
            o = [r2[gi][CHUNK:] + _dot(qk_buf[rslot, i], bdr(vn[gi])) for gi, i in enumerate(cids)]
            s_state = [s_state[gi] * dec_buf[rslot, i][0:1] + mask_bd * _dot_tn(kd_buf[rslot, i], vn[gi].astype(BF16))
                       for gi, i in enumerate(cids)]
            for gi in range(n_groups):
                obuf[c * CHUNK:(c + 1) * CHUNK, gi * GROUP_W:(gi + 1) * GROUP_W] = o[gi]
            yield
        for gi in range(n_groups):
            s_ref[gi] = s_state[gi]

    parts = [chunk_parallel_part(), sequential_part()]
    while parts:
        parts = [p for p in parts if next(p, _DONE) is not _DONE]

    o = obuf[...]
    ms = _dot((o * o).astype(BF16), bd) * (1.0 / HEAD_DIM)
    zz = z_ref[0]
    o_ref[0] = (o * lax.rsqrt(ms + EPS) * gnw_ref[...] * (zz * _sigmoid(zz))).astype(BF16)


def _gdn_consts():
    h = np.arange(GDN_WIDTH) // HEAD_DIM
    eg = np.zeros((GATE_PAD, GDN_WIDTH), np.float32)
    eb = np.zeros((GATE_PAD, GDN_WIDTH), np.float32)
    eg[h, np.arange(GDN_WIDTH)] = 1.0
    eb[GDN_HEADS + h, np.arange(GDN_WIDTH)] = 1.0
    t = np.arange(GDN_TT)
    ltri = ((t[:, None] // CHUNK == t[None, :] // CHUNK) & (t[:, None] >= t[None, :])).astype(np.float32)
    return (_head_sum_matrix(GDN_WIDTH), jnp.asarray(eg, BF16), jnp.asarray(eb, BF16), jnp.asarray(ltri, BF16))


BF16_ROWS = 16
CAST_STEPS = 2 * GDN_HEADS


def _swa_head_source(i):
    j = i - GDN_HEADS
    return jnp.where(i < GDN_HEADS, i, GDN_HEADS + (j % 2) * SWA_GROUP + j // 2)


def _cast_specs(weight, unit, steps, source=None):
    rows, cols = weight.shape
    block_rows = next(r for r in range(unit, rows + 1, unit) if rows % r == 0 and rows // r <= steps)
    if source is None:
        source = lambda i: i
    else:
        assert block_rows == unit, "row regrouping needs one unit per block"
    last = rows // block_rows - 1
    src = pl.BlockSpec((block_rows, cols), lambda n: (source(jnp.minimum(n, last)), 0))
    dst = pl.BlockSpec((block_rows, cols), lambda n: (jnp.minimum(n, last), 0))
    return src, dst, jax.ShapeDtypeStruct((rows, cols), BF16)


def _gdn_call(qkv, z, ab, alog_pad, dtb_pad, gnw_x, w_gate, w_up, w_down, w_out):
    b, t, _ = qkv.shape
    tt = GDN_TT
    bd, eg, eb, ltri = _gdn_consts()
    n_tiles = t // tt
    n_groups = GDN_HEADS // HEADS_PER_GROUP
    total = b * n_tiles

    def tile_block(width, lag):
        def index(n):
            m = jnp.clip(n - lag, 0, total - 1)
            return (m // n_tiles, m % n_tiles, 0)
        return pl.BlockSpec((1, tt, width), index)

    ahead = lambda width: tile_block(width, 0)
    behind = lambda width: tile_block(width, 1)
    const = lambda shape: pl.BlockSpec(shape, lambda n: (0,) * len(shape))
    per_unit = lambda rows, dtype: pltpu.VMEM((2, GDN_UNITS, rows, GROUP_W), dtype)
    steps = total + 1
    assert steps >= CAST_STEPS
    casts = [
        _cast_specs(w_gate, BF16_ROWS, CAST_STEPS),
        _cast_specs(w_up, BF16_ROWS, CAST_STEPS),
        _cast_specs(w_down, BF16_ROWS, CAST_STEPS),
        _cast_specs(w_out, HEAD_DIM, CAST_STEPS, _swa_head_source),
    ]
    return pl.pallas_call(
        functools.partial(_gdn_kernel, tiles_per_seq=n_tiles),
        grid=(steps,),
        in_specs=[
            ahead(QKV_W), behind(GDN_WIDTH), ahead(GATE_PAD),
            const((1, GATE_PAD)), const((1, GATE_PAD)), const((1, GDN_WIDTH)),
            const((GDN_WIDTH, GDN_WIDTH)), const((GATE_PAD, GDN_WIDTH)), const((GATE_PAD, GDN_WIDTH)),
            const((tt, tt)),
        ] + [c[0] for c in casts],
        out_specs=[behind(GDN_WIDTH)] + [c[1] for c in casts],
        out_shape=[jax.ShapeDtypeStruct((b, t, GDN_WIDTH), BF16)] + [c[2] for c in casts],
        scratch_shapes=[
            pltpu.VMEM((n_groups, GROUP_W, GROUP_W), F32),
            pltpu.VMEM((tt, GDN_WIDTH), F32),
            pltpu.VMEM((GDN_UNITS, CHUNK, 2 * GROUP_W), BF16),
            per_unit(2 * CHUNK, BF16), per_unit(CHUNK, F32), per_unit(CHUNK, BF16), per_unit(CHUNK, BF16),
            per_unit(8, F32),
        ],
        compiler_params=pltpu.CompilerParams(
            dimension_semantics=("arbitrary",), vmem_limit_bytes=VMEM_LIMIT),
        name="gdn",
    )(qkv, z, ab, alog_pad, dtb_pad, gnw_x, bd, eg, eb, ltri, w_gate, w_up, w_down, w_out)


SWA_PAIRS = SWA_Q_HEADS // 2
SWA_ROWS = SWA_Q_HEADS * WINDOW
LOG2E = 1.4426950408889634
SWA_INTERLEAVE = 8


def _swa_kernel(q_ref, kvc_ref, kvp_ref, qw_ref, kw_ref, bdq_ref, bdk_ref, bias_ref, sink_ref, o_ref):
    tq = SWA_TQ
    ti = pl.program_id(1)
    q = q_ref[0]
    q = q * lax.rsqrt(_dot((q * q).astype(BF16), bdq_ref[...]) * (1.0 / HEAD_DIM) + EPS)
    q = q * (qw_ref[...] * (HEAD_DIM ** -0.5 * LOG2E))
    kv = jnp.concatenate([kvp_ref[0], kvc_ref[0]], axis=0)
    k = kv[:, 0:SWA_KV_WIDTH]
    v = kv[:, SWA_KV_WIDTH:]
    k = k * lax.rsqrt(_dot((k * k).astype(BF16), bdk_ref[...]) * (1.0 / HEAD_DIM) + EPS) * kw_ref[...]
    k16 = k.astype(BF16)
    lo = lax.broadcasted_iota(jnp.int32, (1, LANES), 1) < HEAD_DIM
    qcol = lax.broadcasted_iota(jnp.int32, (WINDOW, SWA_ROWS), 1) % WINDOW
    from_prev = lax.broadcasted_iota(jnp.int32, (WINDOW, SWA_ROWS), 0) > qcol
    first = jnp.where(ti == 0, 1, 0)
    sink = sink_ref[...]
    vt16 = v.T.astype(BF16)
    zero = jnp.zeros((), BF16)
    def block(j):
        qj = q[j * WINDOW:(j + 1) * WINDOW]
        parts = []
        for p in range(SWA_PAIRS):
            qp = qj[:, p * LANES:(p + 1) * LANES]
            parts += [jnp.where(lo, qp, 0.0), jnp.where(lo, 0.0, qp)]
        qs = jnp.concatenate(parts, axis=0).astype(BF16)
        st = _dot_nt(k16[j * WINDOW:(j + 2) * WINDOW], qs)
        yield
        bias = bias_ref[first] if j == 0 else bias_ref[0]
        sm = jnp.where(from_prev, st[0:WINDOW], st[WINDOW:]) + bias
        m = jnp.max(sm, axis=0, keepdims=True)
        yield
        pe = jnp.exp2(sm - m)
        den = jnp.sum(pe, axis=0, keepdims=True) + jnp.exp2(sink - m)
        pb = pe.astype(BF16)
        pt2 = jnp.concatenate([jnp.where(from_prev, pb, zero), jnp.where(from_prev, zero, pb)], axis=0)
        yield
        ot = _dot(vt16[:, j * WINDOW:(j + 2) * WINDOW], pt2) * (1.0 / den)
        yield
        for p in range(SWA_PAIRS):
            c0 = p * 2 * WINDOW
            pair_t = jnp.concatenate([ot[0:HEAD_DIM, c0:c0 + WINDOW],
                                      ot[HEAD_DIM:, c0 + WINDOW:c0 + 2 * WINDOW]], axis=0)
            o_ref[0, j * WINDOW:(j + 1) * WINDOW, p * LANES:(p + 1) * LANES] = pair_t.T.astype(BF16)

    n_blocks = tq // WINDOW
    for j0 in range(0, n_blocks, SWA_INTERLEAVE):
        parts = [block(j) for j in range(j0, min(j0 + SWA_INTERLEAVE, n_blocks))]
        while parts:
            parts = [p for p in parts if next(p, _DONE) is not _DONE]


def _swa_consts():
    qi = np.arange(WINDOW)[:, None]
    kj = np.arange(WINDOW)[None, :]
    from_prev = kj > qi
    dist = np.where(from_prev, qi + WINDOW - kj, qi - kj).astype(np.float32)
    slopes = 2.0 ** (-8.0 * (np.arange(SWA_Q_HEADS, dtype=np.float32) + 1.0) / SWA_Q_HEADS)
    bias = np.zeros((2, SWA_PAIRS, 2, WINDOW, WINDOW), np.float32)
    for first in range(2):
        for p in range(SWA_PAIRS):
            for half, head in enumerate((p, SWA_GROUP + p)):
                b = (-slopes[head].astype(np.float32) * dist) * np.float32(LOG2E)
                bias[first, p, half] = np.where(from_prev & bool(first), np.float32(NEG_BIG), b)
    bias = bias.reshape(2, SWA_ROWS, WINDOW).transpose(0, 2, 1)
    h = np.arange(SWA_WIDTH) // HEAD_DIM
    bdq = (h[:, None] == h[None, :]).astype(np.float32)
    hk = np.arange(SWA_KV_WIDTH) // HEAD_DIM
    bdk = (hk[:, None] == hk[None, :]).astype(np.float32)
    return jnp.asarray(bias), jnp.asarray(bdq, BF16), jnp.asarray(bdk, BF16)


def _swa_call(sq, skv, qw_x, kw_x, sink_col):
    b, t, _ = sq.shape
    tq = SWA_TQ
    nb = tq // WINDOW
    bias, bdq, bdk = _swa_consts()
    const = lambda shape: pl.BlockSpec(shape, lambda bi, ti: (0,) * len(shape))
    return pl.pallas_call(
        _swa_kernel,
        grid=(b, t // tq),
        in_specs=[
            pl.BlockSpec((1, tq, SWA_WIDTH), lambda bi, ti: (bi, ti, 0)),
            pl.BlockSpec((1, tq, 2 * SWA_KV_WIDTH), lambda bi, ti: (bi, ti, 0)),
            pl.BlockSpec((1, WINDOW, 2 * SWA_KV_WIDTH), lambda bi, ti: (bi, jnp.maximum(ti * nb - 1, 0), 0)),
            const((1, SWA_WIDTH)), const((1, SWA_KV_WIDTH)),
            const((SWA_WIDTH, SWA_WIDTH)), const((SWA_KV_WIDTH, SWA_KV_WIDTH)),
            const((2, WINDOW, SWA_ROWS)),
            const((1, SWA_ROWS)),
        ],
        out_specs=pl.BlockSpec((1, tq, SWA_WIDTH), lambda bi, ti: (bi, ti, 0)),
        out_shape=jax.ShapeDtypeStruct((b, t, SWA_WIDTH), BF16),
        compiler_params=pltpu.CompilerParams(
            dimension_semantics=("arbitrary", "arbitrary"), vmem_limit_bytes=VMEM_LIMIT),
        name="swa",
    )(sq, skv, skv, qw_x, kw_x, bdq, bdk, bias, sink_col)


def _ffn_kernel(x_ref, og_ref, os_ref, mod_ref, nw_ref, wo_ref, wg_ref, wu_ref, wd_ref, o_ref):
    x = x_ref[0]
    mod = mod_ref[0]
    gate1 = mod[:, 2 * D_MODEL:3 * D_MODEL]
    shift2 = mod[:, 3 * D_MODEL:4 * D_MODEL]
    scale2 = mod[:, 4 * D_MODEL:5 * D_MODEL]
    gate2 = mod[:, 5 * D_MODEL:]
    gain = nw_ref[...] * (1.0 + scale2)
    blocks = [slice(r * FFN_HEAD_ROWS, (r + 1) * FFN_HEAD_ROWS) for r in range(FFN_TM // FFN_HEAD_ROWS)]
    mixed = [_dot(og_ref[0, rows, :], wo_ref[0:GDN_WIDTH, :]) + _dot(os_ref[0, rows, :], wo_ref[GDN_WIDTH:, :])
             for rows in blocks]
    x1_blocks, hb_blocks = [], []
    for rows, m_ in zip(blocks, mixed):
        x1_ = x[rows] + gate1 * m_
        ms = jnp.mean(x1_ * x1_, axis=-1, keepdims=True)
        x1_blocks.append(x1_)
        hb_blocks.append((x1_ * lax.rsqrt(ms + EPS) * gain + shift2).astype(BF16))
    x1 = jnp.concatenate(x1_blocks, axis=0)
    hb = jnp.concatenate(hb_blocks, axis=0)
    gt = _dot(hb, wg_ref[...])
    up = _dot(hb, wu_ref[...])
    act = ((gt * _sigmoid(gt)) * up).astype(BF16)
    o_ref[0] = x1 + gate2 * _dot(act, wd_ref[...])


def _ffn_call(x, og, osw, mod3, norm_w, wo, wg, wu, wd):
    b, t, _ = x.shape
    tm = FFN_TM
    row = lambda width: pl.BlockSpec((1, tm, width), lambda bi, ti: (bi, ti, 0))
    const = lambda shape: pl.BlockSpec(shape, lambda bi, ti: (0,) * len(shape), pipeline_mode=pl.Buffered(1))
    return pl.pallas_call(
        _ffn_kernel,
        grid=(b, t // tm),
        in_specs=[
            row(D_MODEL), row(GDN_WIDTH), row(SWA_WIDTH),
            pl.BlockSpec((1, 1, 6 * D_MODEL), lambda bi, ti: (bi, 0, 0)),
            const((1, D_MODEL)),
            const((GDN_WIDTH + SWA_WIDTH, D_MODEL)),
            const((D_MODEL, D_FF)), const((D_MODEL, D_FF)), const((D_FF, D_MODEL)),
        ],
        out_specs=row(D_MODEL),
        out_shape=jax.ShapeDtypeStruct((b, t, D_MODEL), F32),
        compiler_params=pltpu.CompilerParams(
            dimension_semantics=("arbitrary", "arbitrary"), vmem_limit_bytes=VMEM_LIMIT),
        name="ffn",
    )(x, og, osw, mod3, norm_w, wo, wg, wu, wd)


def _layer(x, c_pad, w_ada, b_ada, norm1_w, w_in, conv_w, a_log, dt_bias, gdn_norm_w, q_norm_w, k_norm_w, sinks,
           w_out, norm2_w, w_gate, w_up, w_down):
    b = x.shape[0]
    mod, w_aligned, w_regrouped = _ada_call(c_pad, w_ada, b_ada.reshape(1, -1), w_in.T)
    mod3 = mod[:b].reshape(b, 1, 6 * D_MODEL)
    qkv, z, sq, skv, ab = _proj_call(x, mod3, norm1_w.reshape(1, D_MODEL), w_aligned, w_regrouped,
                                     conv_w.reshape(CONV_WIDTH, QKV_W))

    alog_pad = jnp.pad(a_log.reshape(1, GDN_HEADS), ((0, 0), (0, GATE_PAD - GDN_HEADS)))
    dtb_pad = jnp.pad(dt_bias.reshape(1, GDN_HEADS), ((0, 0), (0, GATE_PAD - GDN_HEADS)))
    gnw_x = jnp.tile(gdn_norm_w.reshape(1, HEAD_DIM), (1, GDN_HEADS))
    og, wg16, wu16, wd16, wo16 = _gdn_call(qkv, z, ab, alog_pad, dtb_pad, gnw_x, w_gate, w_up, w_down, w_out)

    qw_x = jnp.tile(q_norm_w.reshape(1, HEAD_DIM), (1, SWA_Q_HEADS))
    kw_x = jnp.tile(k_norm_w.reshape(1, HEAD_DIM), (1, SWA_KV_HEADS))
    sink_pairs = jnp.stack([sinks[:SWA_GROUP], sinks[SWA_GROUP:]], axis=1)
    sink_col = (jnp.repeat(sink_pairs, WINDOW, axis=1) * LOG2E).reshape(1, SWA_ROWS)
    osw = _swa_call(sq, skv, qw_x, kw_x, sink_col)

    return _ffn_call(x, og, osw, mod3, norm2_w.reshape(1, D_MODEL), wo16, wg16, wu16, wd16)


def kernel(x, c, w_ada, b_ada, norm1_w, w_in, conv_w, a_log, dt_bias, gdn_norm_w, q_norm_w, k_norm_w, sinks,
           w_out, norm2_w, w_gate, w_up, w_down):
    depth = w_ada.shape[0]
    b = c.shape[0]
    c_pad = jnp.pad(c, ((0, 8 - b), (0, 0)))
    for l in range(depth):
        x = _layer(x, c_pad, w_ada[l], b_ada[l], norm1_w[l], w_in[l], conv_w[l], a_log[l], dt_bias[l],
                   gdn_norm_w[l], q_norm_w[l], k_norm_w[l], sinks[l], w_out[l], norm2_w[l], w_gate[l], w_up[l],
                   w_down[l])
    return x
```
